```python
import math
import jax, jax.numpy as jnp
from jax import lax
import numpy as np

D_MODEL = 1024
BATCH = 8
SEQ = 4096
DEPTH = 2

D_MIX = D_MODEL
N_GROUPS = 4
GROUP = D_MIX // N_GROUPS
HEAD_DIM = 64
LRU_BLOCKS = GROUP // HEAD_DIM
LRU_BLOCK_DIM = GROUP // LRU_BLOCKS
LRU_CONV = 4
LRU_C = 8.0
SC_CONV = 3
FOX_HEADS = GROUP // HEAD_DIM
NSA_HEADS = GROUP // HEAD_DIM
CMP_LEN = 32
CMP_STRIDE = 16
CMP_HIDDEN = 128
SLC_BLOCK = 64
SLC_TOPK = 16
WINDOW = 512
Q_BLOCK = 128
REL_BUCKETS = 32
REL_MAX_DIST = 128
D_FF = ((8 * D_MODEL // 3 + 255) // 256) * 256
NEG_INF = -1e30
BIG = 1e30
RMS_EPS = 1e-6

C_LRU_X = 0
C_LRU_G = C_LRU_X + GROUP
C_SC_B = C_LRU_G + GROUP
C_SC_C = C_SC_B + GROUP
C_SC_X = C_SC_C + GROUP
C_FOX_Q = C_SC_X + GROUP
C_FOX_K = C_FOX_Q + GROUP
C_FOX_V = C_FOX_K + GROUP
C_FOX_F = C_FOX_V + GROUP
C_NSA_Q = C_FOX_F + FOX_HEADS
C_NSA_KV = C_NSA_Q + GROUP
C_NSA_G = C_NSA_KV + 6 * HEAD_DIM
N_IN = C_NSA_G + 3 * NSA_HEADS

kernel_name = "hymba_style_hybrid_lru_conv_fox_nsa"


def rms_norm(x, g):
    xf = x.astype(jnp.float32)
    y = xf * lax.rsqrt(jnp.mean(xf * xf, axis=-1, keepdims=True) + RMS_EPS)
    return (y * g.astype(jnp.float32)).astype(x.dtype)


def cols(z, start, width):
    return z[..., start:start + width]


def causal_dwconv(x, w):
    k_len, ch = w.shape
    return lax.conv_general_dilated(x, w[:, None, :], window_strides=(1,), padding=[(k_len - 1, 0)],
                                    dimension_numbers=('NWC', 'WIO', 'NWC'), feature_group_count=ch)


def masked_softmax(s, mask):
    p = jax.nn.softmax(jnp.where(mask, s, NEG_INF), axis=-1)
    return jnp.where(mask, p, 0.0)


def rel_bucket(dist):
    max_exact = REL_BUCKETS // 2
    d = jnp.maximum(dist, 0)
    large = max_exact + (jnp.log(jnp.maximum(d, 1).astype(jnp.float32) / max_exact)
                         / math.log(REL_MAX_DIST / max_exact) * (REL_BUCKETS - max_exact)).astype(jnp.int32)
    large = jnp.minimum(large, REL_BUCKETS - 1)
    return jnp.where(d < max_exact, d, large)


def rg_lru_mixer(xr, gate, conv_w, conv_b, w_gates, b_gates, lam):
    bsz, s_len, _ = xr.shape
    xc = causal_dwconv(xr, conv_w) + conv_b
    xb = xc.reshape(bsz, s_len, LRU_BLOCKS, LRU_BLOCK_DIM)
    r = jax.nn.sigmoid(jnp.einsum('bshi,hij->bshj', xb, w_gates[0]).reshape(bsz, s_len, GROUP) + b_gates[0])
    i = jax.nn.sigmoid(jnp.einsum('bshi,hij->bshj', xb, w_gates[1]).reshape(bsz, s_len, GROUP) + b_gates[1])
    log_a = (-LRU_C * jax.nn.softplus(-lam.astype(jnp.float32))) * r.astype(jnp.float32)
    a = jnp.exp(log_a)
    b = jnp.sqrt(-jnp.expm1(2.0 * log_a)) * (i * xc).astype(jnp.float32)

    def combine(left, right):
        a1, b1 = left
        a2, b2 = right
        return a1 * a2, a2 * b1 + b2

    _, h = lax.associative_scan(combine, (a, b), axis=1)
    return h.astype(xr.dtype) * jax.nn.gelu(gate)


def short_conv_mixer(bg, cg, xs, conv_w):
    return bg * causal_dwconv(cg * xs, conv_w)


def fox_mixer(q, k, v, f_logit, f_bias, qk_gain):
    bsz, s_len, _ = q.shape
    h, d = FOX_HEADS, HEAD_DIM
    q = rms_norm(q.reshape(bsz, s_len, h, d), qk_gain[0])
    k = rms_norm(k.reshape(bsz, s_len, h, d), qk_gain[1])
    v = v.reshape(bsz, s_len, h, d)
    logf = jax.nn.log_sigmoid((f_logit + f_bias).astype(jnp.float32))
    c = lax.cumsum(logf, axis=1).transpose(0, 2, 1)
    nb = s_len // Q_BLOCK
    qb = q.reshape(bsz, nb, Q_BLOCK, h, d).swapaxes(0, 1)
    cb = c.reshape(bsz, h, nb, Q_BLOCK).transpose(2, 0, 1, 3)
    kpos = jnp.arange(s_len)
    scale = HEAD_DIM ** -0.5

    def block(args):
        qi, ci, bi = args
        t = bi * Q_BLOCK + jnp.arange(Q_BLOCK)
        s = jnp.einsum('bqhd,bkhd->bhqk', qi, k).astype(jnp.float32) * scale + (ci[..., :, None] - c[..., None, :])
        p = masked_softmax(s, kpos[None, :] <= t[:, None])
        return jnp.einsum('bhqk,bkhd->bqhd', p.astype(v.dtype), v)

    o = lax.map(block, (qb, cb, jnp.arange(nb)))
    return o.swapaxes(0, 1).reshape(bsz, s_len, GROUP)


def compress(kx, pos, w1, w2):
    s_len = kx.shape[1]
    n_cmp = (s_len - CMP_LEN) // CMP_STRIDE + 1
    idx = jnp.arange(n_cmp)[:, None] * CMP_STRIDE + jnp.arange(CMP_LEN)[None, :]
    blocks = kx[:, idx] + pos
    hid = jax.nn.gelu(jnp.einsum('bnld,ldm->bnm', blocks, w1))
    return hid @ w2


def nsa_mixer(q, kc_in, vc_in, ks_in, vs_in, kw_in, vw_in, g_logit, g_bias, qk_gain, cmp_pos, cmp_w1, cmp_w2, rel_bias):
    bsz, s_len, _ = q.shape
    h, d = NSA_HEADS, HEAD_DIM
    f32 = jnp.float32
    q = rms_norm(q.reshape(bsz, s_len, h, d), qk_gain[0])
    kc = rms_norm(compress(kc_in, cmp_pos[0], cmp_w1[0], cmp_w2[0]), qk_gain[1])
    vc = compress(vc_in, cmp_pos[1], cmp_w1[1], cmp_w2[1])
    ks_ = rms_norm(ks_in, qk_gain[2])
    kw = rms_norm(kw_in, qk_gain[3])
    gates = jax.nn.sigmoid(g_logit + g_bias).reshape(bsz, s_len, 3, h)
    n_cmp = kc.shape[1]
    n_slc = s_len // SLC_BLOCK
    top = min(SLC_TOPK, n_slc)
    cmp_start = jnp.arange(n_cmp) * CMP_STRIDE
    cmp_end = cmp_start + CMP_LEN - 1
    slc_start = jnp.arange(n_slc) * SLC_BLOCK
    overlap = jnp.clip(jnp.minimum(cmp_start[:, None] + CMP_LEN, slc_start[None, :] + SLC_BLOCK)
                       - jnp.maximum(cmp_start[:, None], slc_start[None, :]), 0, CMP_LEN).astype(f32)
    ks_blk = ks_.reshape(bsz, n_slc, SLC_BLOCK, d)
    vs_blk = vs_in.reshape(bsz, n_slc, SLC_BLOCK, d)
    kw_pad = jnp.pad(kw, ((0, 0), (WINDOW, 0), (0, 0)))
    vw_pad = jnp.pad(vw_in, ((0, 0), (WINDOW, 0), (0, 0)))
    nb = s_len // Q_BLOCK
    qb = q.reshape(bsz, nb, Q_BLOCK, h, d).swapaxes(0, 1)
    gb = gates.reshape(bsz, nb, Q_BLOCK, 3, h).swapaxes(0, 1)
    bidx = jnp.arange(bsz)[:, None, None]
    jblk = jnp.arange(n_slc)
    scale = HEAD_DIM ** -0.5

    def block(args):
        qi, gi, bi = args
        t = bi * Q_BLOCK + jnp.arange(Q_BLOCK)
        dist_c = t[:, None] - cmp_end[None, :]
        s_c = (jnp.einsum('bqhd,bnd->bhqn', qi, kc).astype(f32) * scale
               + rel_bias[rel_bucket(dist_c)].transpose(2, 0, 1))
        p_c = masked_softmax(s_c, dist_c >= 0)
        o_c = jnp.einsum('bhqn,bnd->bqhd', p_c.astype(vc.dtype), vc)
        imp = jnp.einsum('bhqn,nm->bqm', p_c, overlap)
        cur = t // SLC_BLOCK
        forced = (jblk[None, :] == 0) | (jblk[None, :] == cur[:, None]) | (jblk[None, :] == cur[:, None] - 1)
        imp = jnp.where(forced, BIG, imp)
        imp = jnp.where(jblk[None, :] <= cur[:, None], imp, -jnp.inf)
        _, sel = lax.top_k(imp, top)
        k_sel = ks_blk[bidx, sel].reshape(bsz, Q_BLOCK, top * SLC_BLOCK, d)
        v_sel = vs_blk[bidx, sel].reshape(bsz, Q_BLOCK, top * SLC_BLOCK, d)
        pos_sel = (sel[..., None] * SLC_BLOCK + jnp.arange(SLC_BLOCK)).reshape(bsz, Q_BLOCK, top * SLC_BLOCK)
        dist_s = t[None, :, None] - pos_sel
        s_s = (jnp.einsum('bqhd,bqnd->bhqn', qi, k_sel).astype(f32) * scale
               + rel_bias[rel_bucket(dist_s)].transpose(0, 3, 1, 2))
        p_s = masked_softmax(s_s, (dist_s >= 0)[:, None])
        o_s = jnp.einsum('bhqn,bqnd->bqhd', p_s.astype(v_sel.dtype), v_sel)
        k_win = lax.dynamic_slice_in_dim(kw_pad, bi * Q_BLOCK, WINDOW + Q_BLOCK, axis=1)
        v_win = lax.dynamic_slice_in_dim(vw_pad, bi * Q_BLOCK, WINDOW + Q_BLOCK, axis=1)
        pos_w = bi * Q_BLOCK - WINDOW + jnp.arange(WINDOW + Q_BLOCK)
        dist_w = t[:, None] - pos_w[None, :]
        mask_w = (dist_w >= 0) & (dist_w < WINDOW) & (pos_w[None, :] >= 0)
        s_w = (jnp.einsum('bqhd,bnd->bhqn', qi, k_win).astype(f32) * scale
               + rel_bias[rel_bucket(dist_w)].transpose(2, 0, 1))
        p_w = masked_softmax(s_w, mask_w)
        o_w = jnp.einsum('bhqn,bnd->bqhd', p_w.astype(v_win.dtype), v_win)
        return gi[:, :, 0, :, None] * o_c + gi[:, :, 1, :, None] * o_s + gi[:, :, 2, :, None] * o_w

    o = lax.map(block, (qb, gb, jnp.arange(nb)))
    return o.swapaxes(0, 1).reshape(bsz, s_len, GROUP)


def setup_inputs(seed: int = 0) -> dict:
    key = jax.random.key(seed)
    ks = jax.random.split(key, 24)
    f32 = jnp.float32
    nl = DEPTH

    def nrm(k, shape, scale):
        return jax.random.normal(k, shape, f32) * scale

    u = jax.random.uniform(ks[7], (nl, GROUP), f32, 0.9, 0.999)
    return {
        'x': jax.random.normal(ks[0], (BATCH, SEQ, D_MODEL), f32),
        'norm_mix': 1.0 + nrm(ks[1], (nl, D_MODEL), 0.02),
        'w_in': nrm(ks[2], (nl, D_MODEL, N_IN), D_MODEL ** -0.5),
        'lru_conv_w': nrm(ks[3], (nl, LRU_CONV, GROUP), LRU_CONV ** -0.5),
        'lru_conv_b': nrm(ks[4], (nl, GROUP), 0.02),
        'lru_w_gates': nrm(ks[5], (nl, 2, LRU_BLOCKS, LRU_BLOCK_DIM, LRU_BLOCK_DIM), LRU_BLOCK_DIM ** -0.5),
        'lru_b_gates': nrm(ks[6], (nl, 2, GROUP), 0.02),
        'lru_lambda': jnp.log(u) - jnp.log1p(-u),
        'sc_conv_w': nrm(ks[8], (nl, SC_CONV, GROUP), SC_CONV ** -0.5),
        'fox_f_bias': 3.0 + nrm(ks[9], (nl, FOX_HEADS), 0.1),
        'fox_qk_norm': 1.0 + nrm(ks[10], (nl, 2, HEAD_DIM), 0.02),
        'nsa_qk_norm': 1.0 + nrm(ks[11], (nl, 4, HEAD_DIM), 0.02),
        'nsa_cmp_pos': nrm(ks[12], (nl, 2, CMP_LEN, HEAD_DIM), 0.1),
        'nsa_cmp_w1': nrm(ks[13], (nl, 2, CMP_LEN, HEAD_DIM, CMP_HIDDEN), (CMP_LEN * HEAD_DIM) ** -0.5),
        'nsa_cmp_w2': nrm(ks[14], (nl, 2, CMP_HIDDEN, HEAD_DIM), CMP_HIDDEN ** -0.5),
        'nsa_gate_bias': nrm(ks[15], (nl, 3 * NSA_HEADS), 0.02),
        'rel_bias': nrm(ks[16], (REL_BUCKETS, NSA_HEADS), 0.2),
        'out_norm': 1.0 + nrm(ks[17], (nl, N_GROUPS, GROUP), 0.02),
        'w_out': nrm(ks[18], (nl, D_MIX, D_MODEL), D_MIX ** -0.5),
        'norm_ffn': 1.0 + nrm(ks[19], (nl, D_MODEL), 0.02),
        'w_gate_up': nrm(ks[20], (nl, D_MODEL, 2 * D_FF), D_MODEL ** -0.5),
        'w_down': nrm(ks[21], (nl, D_FF, D_MODEL), D_FF ** -0.5),
    }


def reference(x, norm_mix, w_in, lru_conv_w, lru_conv_b, lru_w_gates, lru_b_gates, lru_lambda, sc_conv_w,
              fox_f_bias, fox_qk_norm, nsa_qk_norm, nsa_cmp_pos, nsa_cmp_w1, nsa_cmp_w2, nsa_gate_bias,
              rel_bias, out_norm, w_out, norm_ffn, w_gate_up, w_down):
    for l in range(DEPTH):
        z = rms_norm(x, norm_mix[l]) @ w_in[l]
        y_a = rg_lru_mixer(cols(z, C_LRU_X, GROUP), cols(z, C_LRU_G, GROUP), lru_conv_w[l], lru_conv_b[l],
                           lru_w_gates[l], lru_b_gates[l], lru_lambda[l])
        y_b = short_conv_mixer(cols(z, C_SC_B, GROUP), cols(z, C_SC_C, GROUP), cols(z, C_SC_X, GROUP), sc_conv_w[l])
        y_c = fox_mixer(cols(z, C_FOX_Q, GROUP), cols(z, C_FOX_K, GROUP), cols(z, C_FOX_V, GROUP),
                        cols(z, C_FOX_F, FOX_HEADS), fox_f_bias[l], fox_qk_norm[l])
        kv = [cols(z, C_NSA_KV + j * HEAD_DIM, HEAD_DIM) for j in range(6)]
        y_d = nsa_mixer(cols(z, C_NSA_Q, GROUP), kv[0], kv[1], kv[2], kv[3], kv[4], kv[5],
                        cols(z, C_NSA_G, 3 * NSA_HEADS), nsa_gate_bias[l], nsa_qk_norm[l],
                        nsa_cmp_pos[l], nsa_cmp_w1[l], nsa_cmp_w2[l], rel_bias)
        mixed = jnp.concatenate([rms_norm(y_a, out_norm[l, 0]), rms_norm(y_b, out_norm[l, 1]),
                                 rms_norm(y_c, out_norm[l, 2]), rms_norm(y_d, out_norm[l, 3])], axis=-1)
        x = x + mixed @ w_out[l]
        gu = rms_norm(x, norm_ffn[l]) @ w_gate_up[l]
        x = x + (jax.nn.silu(gu[..., :D_FF]) * gu[..., D_FF:]) @ w_down[l]
    return x
```

```python
import functools
import math

import numpy as np
import jax
import jax.numpy as jnp
from jax import lax
from jax.experimental import pallas as pl
from jax.experimental.pallas import tpu as pltpu

D_MODEL = 1024
GROUP = 256
HEAD_DIM = 64
N_HEADS = 4
LRU_CONV = 4
LRU_C = 8.0
SC_CONV = 3
CMP_LEN = 32
CMP_STRIDE = 16
CMP_HIDDEN = 128
SLC_BLOCK = 64
SLC_TOPK = 16
WINDOW = 512
REL_BUCKETS = 32
REL_MAX_DIST = 128
D_FF = 2816
RMS_EPS = 1e-6
NEG = -1e30
BIG = 1e30

C_FOX_F = 8 * GROUP
C_NSA_Q = C_FOX_F + N_HEADS
C_NSA_G = C_NSA_Q + GROUP + 6 * HEAD_DIM

LANES = 128
VMEM_LIMIT_BYTES = 56 * 1024 * 1024

TM_PROJ = 512
TS_MIX = 512
TQ_FOX = 256
TQ_NSA = 128
TM_FFN = 1024
TF_FFN = 256

SM_GATE_LANE = 16

_MXU = jnp.bfloat16
_F32 = jnp.float32


def _params(sem):
    return pltpu.CompilerParams(dimension_semantics=sem, vmem_limit_bytes=VMEM_LIMIT_BYTES)


def _mm(a, b):
    return jnp.dot(a.astype(_MXU), b.astype(_MXU), preferred_element_type=_F32)


def _mm_nt(a, b):
    return lax.dot_general(a.astype(_MXU), b.astype(_MXU), (((1,), (1,)), ((), ())),
                           preferred_element_type=_F32)


def _lane(shape):
    return lax.broadcasted_iota(jnp.int32, shape, len(shape) - 1)


def _row(shape):
    return lax.broadcasted_iota(jnp.int32, shape, 0)


def _rms_scale(x):
    return lax.rsqrt(jnp.mean(x * x, axis=-1, keepdims=True) + RMS_EPS)


def _gelu(x):
    c = math.sqrt(2.0 / math.pi)
    return x * (0.5 * (1.0 + jnp.tanh(c * (x + 0.044715 * (x * x * x)))))


def _round_mxu(x):
    return x.astype(_MXU).astype(_F32)


def _low_half_rms_scale(x, low):
    s = jnp.sum(jnp.where(low, x * x, 0.0), axis=-1, keepdims=True)
    return lax.rsqrt(s * (1.0 / HEAD_DIM) + RMS_EPS)


def _half_rms_scale(x, low):
    x2 = x * x
    s_lo = jnp.sum(jnp.where(low, x2, 0.0), axis=-1, keepdims=True)
    s_hi = jnp.sum(jnp.where(low, 0.0, x2), axis=-1, keepdims=True)
    return jnp.where(low, lax.rsqrt(s_lo * (1.0 / HEAD_DIM) + RMS_EPS),
                     lax.rsqrt(s_hi * (1.0 / HEAD_DIM) + RMS_EPS))


_IN_GROUPS = (1280, 768, 256, 128, 128, 128, 128)
N_IN_PAD = sum(_IN_GROUPS)


def _in_proj_kernel(x_ref, g_ref, w_ref, *out_refs):
    x = x_ref[...]
    xn = (x * _rms_scale(x) * g_ref[...]).astype(_MXU)
    lo = 0
    for ref, width in zip(out_refs, _IN_GROUPS):
        ref[...] = jnp.dot(xn, w_ref[:, lo:lo + width], preferred_element_type=_F32)
        lo += width


def _in_proj(x2d, gain, w):
    t = x2d.shape[0]
    assert t % TM_PROJ == 0
    return pl.pallas_call(
        _in_proj_kernel,
        grid=(t // TM_PROJ,),
        in_specs=[pl.BlockSpec((TM_PROJ, D_MODEL), lambda i: (i, 0)),
                  pl.BlockSpec((1, D_MODEL), lambda i: (0, 0)),
                  pl.BlockSpec((D_MODEL, N_IN_PAD), lambda i: (0, 0))],
        out_specs=[pl.BlockSpec((TM_PROJ, wd), lambda i: (i, 0)) for wd in _IN_GROUPS],
        out_shape=[jax.ShapeDtypeStruct((t, wd), _F32) for wd in _IN_GROUPS],
        compiler_params=_params(("parallel",)),
        name="in_proj",
    )(x2d, gain, w)


def _linear_scan_rows(a, b):
    n = a.shape[0]
    row = _row(a.shape)
    d = 1
    while d < n:
        keep = row >= d
        a_prev = jnp.where(keep, pltpu.roll(a, d, axis=0), 1.0)
        b_prev = jnp.where(keep, pltpu.roll(b, d, axis=0), 0.0)
        b = a * b_prev + b
        a = a * a_prev
        d *= 2
    return a, b


def _cumsum_rows(x):
    n = x.shape[0]
    row = _row(x.shape)
    d = 1
    while d < n:
        x = x + jnp.where(row >= d, pltpu.roll(x, d, axis=0), 0.0)
        d *= 2
    return x


def _mix_ab_kernel(z_ref, cw_ref, cb_ref, wg_ref, bg_ref, lam_ref, scw_ref, on_ref, out_ref,
                   xext_ref, uext_ref, h_ref):
    ts = z_ref.shape[1]
    ti = pl.program_id(1)

    @pl.when(ti == 0)
    def _():
        xext_ref[0:8, :] = jnp.zeros((8, GROUP), _F32)
        uext_ref[0:8, :] = jnp.zeros((8, GROUP), _F32)
        h_ref[...] = jnp.zeros_like(h_ref)

    xr = z_ref[0, :, 0:GROUP]
    gate = z_ref[0, :, GROUP:2 * GROUP]
    bgt = z_ref[0, :, 2 * GROUP:3 * GROUP]
    cgt = z_ref[0, :, 3 * GROUP:4 * GROUP]
    xs = z_ref[0, :, 4 * GROUP:5 * GROUP]

    xext_ref[8:8 + ts, :] = xr
    xc = cb_ref[...] + cw_ref[LRU_CONV - 1:LRU_CONV, :] * xr
    for k in range(LRU_CONV - 1):
        xc = xc + cw_ref[k:k + 1, :] * xext_ref[pl.ds(8 - (LRU_CONV - 1) + k, ts), :]
    xext_ref[0:8, :] = xr[ts - 8:ts, :]

    gi = _mm(xc, wg_ref[...]) + bg_ref[...]
    r = jax.nn.sigmoid(gi[:, 0:GROUP])
    ig = jax.nn.sigmoid(gi[:, GROUP:2 * GROUP])
    lam = lam_ref[...]
    softplus_neg = jnp.maximum(-lam, 0.0) + jnp.log1p(jnp.exp(-jnp.abs(lam)))
    log_a = (-LRU_C * softplus_neg) * r
    a = jnp.exp(log_a)
    b = jnp.sqrt(-jnp.tanh(log_a) * (a * a + 1.0)) * (ig * xc)
    a_cum, h = _linear_scan_rows(a, b)
    h = h + a_cum * h_ref[0:1, :]
    h_ref[...] = jnp.broadcast_to(h[ts - 1:ts, :], h_ref.shape)
    y_a = h * _gelu(gate)

    u = cgt * xs
    uext_ref[8:8 + ts, :] = u
    cv = scw_ref[SC_CONV - 1:SC_CONV, :] * u
    for k in range(SC_CONV - 1):
        cv = cv + scw_ref[k:k + 1, :] * uext_ref[pl.ds(8 - (SC_CONV - 1) + k, ts), :]
    uext_ref[0:8, :] = u[ts - 8:ts, :]
    y_b = bgt * cv

    out_ref[0, :, 0:GROUP] = (y_a * _rms_scale(y_a) * on_ref[:, 0:GROUP]).astype(out_ref.dtype)
    out_ref[0, :, GROUP:2 * GROUP] = (y_b * _rms_scale(y_b) * on_ref[:, GROUP:2 * GROUP]).astype(out_ref.dtype)


def _mix_ab(zab, cw, cb, wg, bg, lam, scw, on):
    bsz, s_len, _ = zab.shape
    ts = min(TS_MIX, s_len)
    assert s_len % ts == 0
    const = lambda shape: pl.BlockSpec(shape, lambda b, t: (0, 0))
    return pl.pallas_call(
        _mix_ab_kernel,
        grid=(bsz, s_len // ts),
        in_specs=[pl.BlockSpec((1, ts, 5 * GROUP), lambda b, t: (b, t, 0)),
                  const((LRU_CONV, GROUP)), const((1, GROUP)), const((GROUP, 2 * GROUP)),
                  const((1, 2 * GROUP)), const((1, GROUP)), const((SC_CONV, GROUP)), const((1, 2 * GROUP))],
        out_specs=pl.BlockSpec((1, ts, 2 * GROUP), lambda b, t: (b, t, 0)),
        out_shape=jax.ShapeDtypeStruct((bsz, s_len, 2 * GROUP), _MXU),
        scratch_shapes=[pltpu.VMEM((ts + 8, GROUP), _F32), pltpu.VMEM((ts + 8, GROUP), _F32),
                        pltpu.VMEM((8, GROUP), _F32)],
        compiler_params=_params(("parallel", "arbitrary")),
        name="mix_ab",
    )(zab, cw, cb, wg, bg, lam, scw, on)


def _fox_prep_kernel(z_ref, sm_ref, fb_ref, gq_ref, gk_ref, q_ref, k_ref, v_ref, c_ref):
    ts = z_ref.shape[1]
    ti = pl.program_id(1)

    @pl.when(ti == 0)
    def _():
        c_ref[...] = jnp.zeros_like(c_ref)

    shape = (ts, LANES)
    lane = _lane(shape)
    low = lane < HEAD_DIM

    f = sm_ref[0] + fb_ref[...]
    logf = jnp.minimum(f, 0.0) - jnp.log1p(jnp.exp(-jnp.abs(f)))
    c = _cumsum_rows(logf) + c_ref[0:1, :]
    c_ref[...] = jnp.broadcast_to(c[ts - 1:ts, :], c_ref.shape)
    neg_c = -c
    p1 = _round_mxu(neg_c)
    r1 = neg_c - p1
    p2 = _round_mxu(r1)
    p3 = _round_mxu(r1 - p2)
    piece = lane - 3 * ((lane * 11) >> 5)
    csel = jnp.where(piece == 0, p1, jnp.where(piece == 1, p2, p3))

    ones_aug = jnp.where(lane < HEAD_DIM + 3, 1.0, 0.0)
    for p in range(N_HEADS // 2):
        qs = z_ref[0, :, LANES * p:LANES * (p + 1)]
        ks = z_ref[0, :, GROUP + LANES * p:GROUP + LANES * (p + 1)]
        qn = qs * _half_rms_scale(qs, low) * gq_ref[...]
        kn = ks * _half_rms_scale(ks, low) * gk_ref[...]
        for half in range(2):
            h = 2 * p + half
            qh = qn if half == 0 else pltpu.roll(qn, HEAD_DIM, axis=1)
            kh = kn if half == 0 else pltpu.roll(kn, HEAD_DIM, axis=1)
            ch = pltpu.roll(csel, HEAD_DIM - 3 * h, axis=1)
            q_ref[0, h] = jnp.where(low, qh, ones_aug).astype(q_ref.dtype)
            k_ref[0, h] = jnp.where(low, kh, jnp.where(lane < HEAD_DIM + 3, ch, 0.0)).astype(k_ref.dtype)
    v_ref[0] = z_ref[0, :, 2 * GROUP:3 * GROUP].astype(v_ref.dtype)


def _fox_prep(zfox, zsm, fb, gq, gk):
    bsz, s_len, _ = zfox.shape
    ts = min(TS_MIX, s_len)
    const = lambda shape: pl.BlockSpec(shape, lambda b, t: (0, 0))
    head_spec = pl.BlockSpec((1, N_HEADS, ts, LANES), lambda b, t: (b, 0, t, 0))
    return pl.pallas_call(
        _fox_prep_kernel,
        grid=(bsz, s_len // ts),
        in_specs=[pl.BlockSpec((1, ts, 3 * GROUP), lambda b, t: (b, t, 0)),
                  pl.BlockSpec((1, ts, LANES), lambda b, t: (b, t, 0)),
                  const((1, LANES)), const((1, LANES)), const((1, LANES))],
        out_specs=[head_spec, head_spec, pl.BlockSpec((1, ts, GROUP), lambda b, t: (b, t, 0))],
        out_shape=[jax.ShapeDtypeStruct((bsz, N_HEADS, s_len, LANES), _MXU),
                   jax.ShapeDtypeStruct((bsz, N_HEADS, s_len, LANES), _MXU),
                   jax.ShapeDtypeStruct((bsz, s_len, GROUP), _MXU)],
        scratch_shapes=[pltpu.VMEM((8, LANES), _F32)],
        compiler_params=_params(("parallel", "arbitrary")),
        name="fox_prep",
    )(zfox, zsm, fb, gq, gk)


def _flash_step(s, v, carry):
    m, l, acc = carry
    m_new = jnp.maximum(m, jnp.max(s, axis=-1, keepdims=True))
    alpha = jnp.exp(m - m_new)
    p = jnp.exp(s - m_new)
    l = alpha * l + jnp.sum(p, axis=-1, keepdims=True)
    acc = alpha * acc + jnp.dot(p.astype(_MXU), v, preferred_element_type=_F32)
    return m_new, l, acc


def _flash_init(rows):
    return (jnp.full((rows, 1), NEG, _F32), jnp.zeros((rows, 1), _F32), jnp.zeros((rows, LANES), _F32))


def _fox_attn_kernel(q_ref, k_ref, v_ref, on_ref, out_ref):
    tq = q_ref.shape[2]
    qi = pl.program_id(1)
    row = _row((tq, tq))
    col = _lane((tq, tq))
    causal = col <= row
    low = _lane((tq, LANES)) < HEAD_DIM
    slabs = []
    for p in range(N_HEADS // 2):
        outs = []
        for half in range(2):
            h = 2 * p + half
            q = q_ref[0, h]

            def body(j, carry, h=h, p=p, q=q):
                start = pl.multiple_of(j * tq, tq)
                s = _mm_nt(q, k_ref[0, h, pl.ds(start, tq), :])
                return _flash_step(s, v_ref[0, pl.ds(start, tq), LANES * p:LANES * (p + 1)], carry)

            carry = lax.fori_loop(0, qi, body, _flash_init(tq))
            start = pl.multiple_of(qi * tq, tq)
            s = _mm_nt(q, k_ref[0, h, pl.ds(start, tq), :])
            s = jnp.where(causal, s, NEG)
            _, l, acc = _flash_step(s, v_ref[0, pl.ds(start, tq), LANES * p:LANES * (p + 1)], carry)
            outs.append(acc / l)
        slabs.append(jnp.where(low, outs[0], outs[1]))
    y = jnp.concatenate(slabs, axis=1)
    out_ref[0] = (y * _rms_scale(y) * on_ref[...]).astype(out_ref.dtype)


def _fox_attn(qf, kf, vf, on):
    bsz, _, s_len, _ = qf.shape
    tq = min(TQ_FOX, s_len)
    return pl.pallas_call(
        _fox_attn_kernel,
        grid=(bsz, s_len // tq),
        in_specs=[pl.BlockSpec((1, N_HEADS, tq, LANES), lambda b, i: (b, 0, i, 0)),
                  pl.BlockSpec((1, N_HEADS, s_len, LANES), lambda b, i: (b, 0, 0, 0)),
                  pl.BlockSpec((1, s_len, GROUP), lambda b, i: (b, 0, 0)),
                  pl.BlockSpec((1, GROUP), lambda b, i: (0, 0))],
        out_specs=pl.BlockSpec((1, tq, GROUP), lambda b, i: (b, i, 0)),
        out_shape=jax.ShapeDtypeStruct((bsz, s_len, GROUP), _MXU),
        compiler_params=_params(("parallel", "arbitrary")),
        name="fox_attn",
    )(qf, kf, vf, on)


def _nsa_prep_kernel(zq_ref, zsel_ref, zwin_ref, sm_ref, gq_ref, gs_ref, gw_ref, gb_ref,
                     q4_ref, ks_ref, vs_ref, kw_ref, vw_ref, gate_ref):
    ts = zq_ref.shape[1]
    ti = pl.program_id(1)
    shape = (ts, LANES)
    lane = _lane(shape)
    low = lane < HEAD_DIM
    for p in range(N_HEADS // 2):
        qs = zq_ref[0, :, LANES * p:LANES * (p + 1)]
        qn = qs * _half_rms_scale(qs, low) * gq_ref[...]
        q4_ref[0, 2 * p] = jnp.where(low, qn, 0.0).astype(q4_ref.dtype)
        q4_ref[0, 2 * p + 1] = jnp.where(low, pltpu.roll(qn, HEAD_DIM, axis=1), 0.0).astype(q4_ref.dtype)

    sel = zsel_ref[0]
    blk = (ti * ts + _row(shape)) >> 6
    onehot = jnp.where(lane - HEAD_DIM == blk, 1.0, 0.0)
    ks_ref[0] = jnp.where(low, sel * _low_half_rms_scale(sel, low) * gs_ref[...], onehot).astype(ks_ref.dtype)
    vs_ref[0] = sel.astype(vs_ref.dtype)
    win = zwin_ref[0]
    kw_ref[0] = jnp.where(low, win * _low_half_rms_scale(win, low) * gw_ref[...], 0.0).astype(kw_ref.dtype)
    vw_ref[0] = win.astype(vw_ref.dtype)
    gate_ref[0] = jax.nn.sigmoid(sm_ref[0] + gb_ref[...])


def _nsa_prep(zq, zsel, zwin, zsm, gq, gs, gw, gb):
    bsz, s_len, _ = zq.shape
    ts = min(TS_MIX, s_len)
    const = lambda shape: pl.BlockSpec(shape, lambda b, t: (0, 0))
    slab = pl.BlockSpec((1, ts, LANES), lambda b, t: (b, t, 0))
    slab_shape = jax.ShapeDtypeStruct((bsz, s_len, LANES), _MXU)
    return pl.pallas_call(
        _nsa_prep_kernel,
        grid=(bsz, s_len // ts),
        in_specs=[pl.BlockSpec((1, ts, GROUP), lambda b, t: (b, t, 0)), slab, slab, slab,
                  const((1, LANES)), const((1, LANES)), const((1, LANES)), const((1, LANES))],
        out_specs=[pl.BlockSpec((1, N_HEADS, ts, LANES), lambda b, t: (b, 0, t, 0)),
                   slab, slab, slab, slab, slab],
        out_shape=[jax.ShapeDtypeStruct((bsz, N_HEADS, s_len, LANES), _MXU),
                   slab_shape, slab_shape, slab_shape, slab_shape,
                   jax.ShapeDtypeStruct((bsz, s_len, LANES), _F32)],
        compiler_params=_params(("parallel", "arbitrary")),
        name="nsa_prep",
    )(zq, zsel, zwin, zsm, gq, gs, gw, gb)


def _compress_kernel(x_ref, pa_ref, pb_ref, w1a_ref, w1b_ref, w2_ref, g_ref, kc_ref, cv_ref):
    nc = x_ref.shape[1]
    x = x_ref[0]
    ua = _mm(x + pa_ref[...], w1a_ref[...])
    ub = _mm(x + pb_ref[...], w1b_ref[...])
    hid = _gelu(ua + pltpu.roll(ub, nc - 1, axis=0))
    out = _mm(hid, w2_ref[...])
    low = _lane(out.shape) < HEAD_DIM
    kc_ref[0] = jnp.where(low, out * _low_half_rms_scale(out, low) * g_ref[...], 0.0).astype(kc_ref.dtype)
    cv_ref[0] = out.astype(cv_ref.dtype)


def _compress(xc, pa, pb, w1a, w1b, w2, g):
    bsz, nc, width = xc.shape
    const = lambda shape: pl.BlockSpec(shape, lambda b: (0, 0))
    slab = pl.BlockSpec((1, nc, LANES), lambda b: (b, 0, 0))
    return pl.pallas_call(
        _compress_kernel,
        grid=(bsz,),
        in_specs=[pl.BlockSpec((1, nc, width), lambda b: (b, 0, 0)),
                  const((1, width)), const((1, width)), const((width, 2 * CMP_HIDDEN)),
                  const((width, 2 * CMP_HIDDEN)), const((2 * CMP_HIDDEN, LANES)), const((1, LANES))],
        out_specs=[slab, slab],
        out_shape=[jax.ShapeDtypeStruct((bsz, nc, LANES), _MXU)] * 2,
        compiler_params=_params(("parallel",)),
        name="compress",
    )(xc, pa, pb, w1a, w1b, w2, g)


def _bucket_thresholds():
    max_exact = REL_BUCKETS // 2
    d = np.arange(0, REL_MAX_DIST + 1)
    large = max_exact + (np.log(np.maximum(d, 1).astype(np.float32) / max_exact)
                         / math.log(REL_MAX_DIST / max_exact) * (REL_BUCKETS - max_exact)).astype(np.int32)
    bucket = np.where(d < max_exact, d, np.minimum(large, REL_BUCKETS - 1))
    assert bucket[-1] == REL_BUCKETS - 1 and np.all(np.diff(bucket) >= 0)
    return [int(np.argmax(bucket >= k)) for k in range(REL_BUCKETS)]


_BUCKET_THR = _bucket_thresholds()
N_NEAR = WINDOW // TQ_NSA + 1


def _rel_bias(dist, rb_ref, h):
    far = rb_ref[REL_BUCKETS - 1, h]
    val = jnp.full(dist.shape, rb_ref[0, h] - far, _F32)
    for k in range(1, REL_BUCKETS):
        val = jnp.where(dist >= _BUCKET_THR[k], rb_ref[k, h] - far, val)
    return val


def _nsa_tables_kernel(rb_ref, near_ref, cmp_ref):
    qi = pl.program_id(0)
    tq = TQ_NSA
    nc = cmp_ref.shape[2]

    @pl.when(qi == 0)
    def _():
        i = _row((tq, tq))
        c = _lane((tq, tq))
        for j in range(N_NEAR):
            dist = i + (N_NEAR - 1 - j) * tq - c
            ok = jnp.where(dist >= 0, jnp.where(dist < WINDOW, 1, 0), 0) > 0
            for h in range(N_HEADS):
                near_ref[j, h * tq:(h + 1) * tq, :] = jnp.where(ok, _rel_bias(dist, rb_ref, h), NEG)

    t = qi * tq + _row((tq, nc))
    dist = t - (CMP_STRIDE * _lane((tq, nc)) + CMP_LEN - 1)
    for h in range(N_HEADS):
        cmp_ref[0, h * tq:(h + 1) * tq, :] = jnp.where(dist >= 0, _rel_bias(dist, rb_ref, h), NEG)


def _nsa_tables(rel_bias, s_len):
    nq = s_len // TQ_NSA
    nc = s_len // CMP_STRIDE
    rows = N_HEADS * TQ_NSA
    return pl.pallas_call(
        _nsa_tables_kernel,
        grid=(nq,),
        in_specs=[pl.BlockSpec(memory_space=pltpu.SMEM)],
        out_specs=[pl.BlockSpec((N_NEAR, rows, TQ_NSA), lambda i: (0, 0, 0)),
                   pl.BlockSpec((1, rows, nc), lambda i: (i, 0, 0))],
        out_shape=[jax.ShapeDtypeStruct((N_NEAR, rows, TQ_NSA), _F32),
                   jax.ShapeDtypeStruct((nq, rows, nc), _F32)],
        compiler_params=_params(("arbitrary",)),
        name="nsa_tables",
    )(rel_bias)


def _nsa_attn_kernel(q4_ref, kc_ref, cv_ref, ks_ref, vs_ref, kw_ref, vw_ref, gate_ref,
                     near_ref, cmp_ref, ovl_ref, on_ref, out_ref):
    tq = TQ_NSA
    rows = N_HEADS * tq
    qi = pl.program_id(1)
    q4 = q4_ref[0].reshape(rows, LANES)
    lane = _lane((tq, LANES))
    low = lane < HEAD_DIM

    s = _mm_nt(q4, kc_ref[0]) + cmp_ref[0]
    valid = s > 0.5 * NEG
    m = jnp.max(s, axis=-1, keepdims=True)
    p = jnp.where(valid, jnp.exp(s - m), 0.0)
    l = jnp.sum(p, axis=-1, keepdims=True)
    p = p / jnp.where(l > 0.0, l, 1.0)
    o_c = jnp.dot(p.astype(_MXU), cv_ref[0], preferred_element_type=_F32)

    psum = p[0:tq] + p[tq:2 * tq] + p[2 * tq:3 * tq] + p[3 * tq:4 * tq]
    p_hi = _round_mxu(psum)
    imp = _mm(p_hi, ovl_ref[...]) + _mm(psum - p_hi, ovl_ref[...])

    n_blk = SLC_BLOCK
    imp_t = imp.T[0:n_blk, :]
    blk = _row((n_blk, tq))
    cur = (qi * tq + _lane((n_blk, tq))) >> 6
    forced = jnp.where(blk == 0, 1, jnp.where(blk == cur, 1, jnp.where(blk == cur - 1, 1, 0))) > 0
    val = jnp.where(blk <= cur, jnp.where(forced, BIG, imp_t), NEG)
    rank = jnp.zeros((n_blk, tq), _F32)
    for j in range(n_blk):
        vj = val[j:j + 1, :]
        rank = rank + jnp.where(vj > val, 1.0, jnp.where(vj == val, jnp.where(blk > j, 1.0, 0.0), 0.0))
    pen_t = jnp.where(rank < float(SLC_TOPK), 0.0, NEG)
    pen = jnp.concatenate([jnp.zeros((LANES - n_blk, tq), _F32), pen_t], axis=0).T
    pen4 = jnp.concatenate([pen] * N_HEADS, axis=0)
    q_aug = jnp.where(_lane((rows, LANES)) < HEAD_DIM, q4.astype(_F32), pen4).astype(_MXU)

    def sel_far(j, carry):
        start = pl.multiple_of(j * tq, tq)
        s = _mm_nt(q_aug, ks_ref[0, pl.ds(start, tq), :])
        return _flash_step(s, vs_ref[0, pl.ds(start, tq), :], carry)

    def sel_near(j, carry):
        start = pl.multiple_of(j * tq, tq)
        s = _mm_nt(q_aug, ks_ref[0, pl.ds(start, tq), :]) + near_ref[j - qi + (N_NEAR - 1)]
        return _flash_step(s, vs_ref[0, pl.ds(start, tq), :], carry)

    n_far = jnp.maximum(qi - 1, 0)
    carry = lax.fori_loop(0, n_far, sel_far, _flash_init(rows))
    _, l_s, acc_s = lax.fori_loop(n_far, qi + 1, sel_near, carry)
    o_s = acc_s / l_s

    def win_step(j, carry):
        start = pl.multiple_of(j * tq, tq)
        s = _mm_nt(q_aug, kw_ref[0, pl.ds(start, tq), :]) + near_ref[j - qi + (N_NEAR - 1)]
        return _flash_step(s, vw_ref[0, pl.ds(start, tq), :], carry)

    _, l_w, acc_w = lax.fori_loop(jnp.maximum(qi - (N_NEAR - 1), 0), qi + 1, win_step, _flash_init(rows))
    o_w = acc_w / l_w

    g = gate_ref[0]
    heads = []
    for h in range(N_HEADS):
        rs = slice(h * tq, (h + 1) * tq)
        gc = g[:, SM_GATE_LANE + h:SM_GATE_LANE + h + 1]
        gs = g[:, SM_GATE_LANE + N_HEADS + h:SM_GATE_LANE + N_HEADS + h + 1]
        gw = g[:, SM_GATE_LANE + 2 * N_HEADS + h:SM_GATE_LANE + 2 * N_HEADS + h + 1]
        heads.append(gc * o_c[rs] + gs * o_s[rs] + gw * o_w[rs])
    slabs = [jnp.where(low, pltpu.roll(heads[2 * p], HEAD_DIM, axis=1), heads[2 * p + 1])
             for p in range(N_HEADS // 2)]
    y = jnp.concatenate(slabs, axis=1)
    out_ref[0] = (y * _rms_scale(y) * on_ref[...]).astype(out_ref.dtype)


def _nsa_attn(q4, kc, cv, ks, vs, kw, vw, gates, near, cmp_tab, ovl, on):
    bsz, _, s_len, _ = q4.shape
    tq = TQ_NSA
    nc = kc.shape[1]
    rows = N_HEADS * tq
    full = lambda n: pl.BlockSpec((1, n, LANES), lambda b, i: (b, 0, 0))
    return pl.pallas_call(
        _nsa_attn_kernel,
        grid=(bsz, s_len // tq),
        in_specs=[pl.BlockSpec((1, N_HEADS, tq, LANES), lambda b, i: (b, 0, i, 0)),
                  full(nc), full(nc), full(s_len), full(s_len), full(s_len), full(s_len),
                  pl.BlockSpec((1, tq, LANES), lambda b, i: (b, i, 0)),
                  pl.BlockSpec((N_NEAR, rows, tq), lambda b, i: (0, 0, 0)),
                  pl.BlockSpec((1, rows, nc), lambda b, i: (i, 0, 0)),
                  pl.BlockSpec((nc, LANES), lambda b, i: (0, 0)),
                  pl.BlockSpec((1, GROUP), lambda b, i: (0, 0))],
        out_specs=pl.BlockSpec((1, tq, GROUP), lambda b, i: (b, i, 0)),
        out_shape=jax.ShapeDtypeStruct((bsz, s_len, GROUP), _MXU),
        compiler_params=_params(("parallel", "arbitrary")),
        name="nsa_attn",
    )(q4, kc, cv, ks, vs, kw, vw, gates, near, cmp_tab, ovl, on)


def _ffn_kernel(x_ref, mab_ref, mc_ref, md_ref, wo_ref, g_ref, wg_ref, wu_ref, wd_ref, out_ref,
                x1_ref, xn_ref, acc_ref):
    j = pl.program_id(1)

    @pl.when(j == 0)
    def _():
        x1 = (x_ref[...]
              + jnp.dot(mab_ref[...], wo_ref[0:2 * GROUP, :], preferred_element_type=_F32)
              + jnp.dot(mc_ref[...], wo_ref[2 * GROUP:3 * GROUP, :], preferred_element_type=_F32)
              + jnp.dot(md_ref[...], wo_ref[3 * GROUP:4 * GROUP, :], preferred_element_type=_F32))
        x1_ref[...] = x1
        xn_ref[...] = (x1 * _rms_scale(x1) * g_ref[...]).astype(xn_ref.dtype)
        acc_ref[...] = jnp.zeros_like(acc_ref)

    xn = xn_ref[...]
    gt = jnp.dot(xn, wg_ref[...], preferred_element_type=_F32)
    up = jnp.dot(xn, wu_ref[...], preferred_element_type=_F32)
    hid = (gt * jax.nn.sigmoid(gt)) * up
    acc_ref[...] += jnp.dot(hid.astype(_MXU), wd_ref[...], preferred_element_type=_F32)

    @pl.when(j == pl.num_programs(1) - 1)
    def _():
        out_ref[...] = x1_ref[...] + acc_ref[...]


def _out_ffn(x2d, mab, mc, md, wo, g, wgu, wd):
    t = x2d.shape[0]
    tm = min(TM_FFN, t)
    nf = D_FF // TF_FFN
    assert t % tm == 0 and D_FF % TF_FFN == 0
    row = lambda width: pl.BlockSpec((tm, width), lambda i, j: (i, 0))
    return pl.pallas_call(
        _ffn_kernel,
        grid=(t // tm, nf),
        in_specs=[row(D_MODEL), row(2 * GROUP), row(GROUP), row(GROUP),
                  pl.BlockSpec((D_MODEL, D_MODEL), lambda i, j: (0, 0)),
                  pl.BlockSpec((1, D_MODEL), lambda i, j: (0, 0)),
                  pl.BlockSpec((D_MODEL, TF_FFN), lambda i, j: (0, j)),
                  pl.BlockSpec((D_MODEL, TF_FFN), lambda i, j: (0, j + nf)),
                  pl.BlockSpec((TF_FFN, D_MODEL), lambda i, j: (j, 0))],
        out_specs=row(D_MODEL),
        out_shape=jax.ShapeDtypeStruct((t, D_MODEL), _F32),
        scratch_shapes=[pltpu.VMEM((tm, D_MODEL), _F32), pltpu.VMEM((tm, D_MODEL), _MXU),
                        pltpu.VMEM((tm, D_MODEL), _F32)],
        compiler_params=_params(("parallel", "arbitrary")),
        name="out_ffn",
    )(x2d, mab, mc, md, wo, g, wgu, wgu, wd)


def _lane_vec(values, start):
    v = jnp.zeros((LANES,), _F32).at[start:start + values.shape[0]].set(values.astype(_F32))
    return v[None, :]


def _block_diag(w):
    h, d, _ = w.shape
    eye = jnp.eye(h, dtype=w.dtype)
    return (eye[:, None, :, None] * w[:, :, None, :]).reshape(h * d, h * d)


def _overlap_ext(s_len):
    nc = s_len // CMP_STRIDE
    n_cmp = nc - 1
    n_slc = s_len // SLC_BLOCK
    cs = np.arange(n_cmp)[:, None] * CMP_STRIDE
    ss = np.arange(n_slc)[None, :] * SLC_BLOCK
    ov = np.clip(np.minimum(cs + CMP_LEN, ss + SLC_BLOCK) - np.maximum(cs, ss), 0, CMP_LEN)
    ext = np.zeros((nc, LANES), np.float32)
    ext[:n_cmp, :n_slc] = ov
    return jnp.asarray(ext, _MXU)


def _layer_params(l, w_in, lru_w_gates, lru_b_gates, fox_f_bias, fox_qk_norm, nsa_qk_norm, nsa_cmp_pos,
                  nsa_cmp_w1, nsa_cmp_w2, nsa_gate_bias, out_norm):
    scale = HEAD_DIM ** -0.5
    w = w_in[l]
    small = jnp.zeros((D_MODEL, LANES), w.dtype)
    small = small.at[:, 0:3 * N_HEADS].set(jnp.repeat(w[:, C_FOX_F:C_FOX_F + N_HEADS], 3, axis=1))
    small = small.at[:, SM_GATE_LANE:SM_GATE_LANE + 3 * N_HEADS].set(w[:, C_NSA_G:C_NSA_G + 3 * N_HEADS])
    w_perm = jnp.concatenate([w[:, :C_FOX_F], w[:, C_NSA_Q:C_NSA_G], small], axis=1).astype(_MXU)

    wg = jnp.concatenate([_block_diag(lru_w_gates[l, 0]), _block_diag(lru_w_gates[l, 1])], axis=1).astype(_MXU)
    bg = lru_b_gates[l].reshape(1, 2 * GROUP)

    fb = _lane_vec(jnp.repeat(fox_f_bias[l], 3), 0)
    fox_gq = (jnp.tile(fox_qk_norm[l, 0], 2) * scale)[None, :]
    fox_gk = jnp.tile(fox_qk_norm[l, 1], 2)[None, :]

    nsa_gq = (jnp.tile(nsa_qk_norm[l, 0], 2) * scale)[None, :]
    nsa_gc = _lane_vec(nsa_qk_norm[l, 1], 0)
    nsa_gs = _lane_vec(nsa_qk_norm[l, 2], 0)
    nsa_gw = _lane_vec(nsa_qk_norm[l, 3], 0)
    gb = _lane_vec(nsa_gate_bias[l], SM_GATE_LANE)

    half = CMP_LEN // 2

    def pos_ext(lo):
        return jnp.concatenate([nsa_cmp_pos[l, 0, lo:lo + half], nsa_cmp_pos[l, 1, lo:lo + half]],
                               axis=1).reshape(1, half * LANES)

    def w1_ext(lo):
        ext = jnp.zeros((half, LANES, 2 * CMP_HIDDEN), _F32)
        ext = ext.at[:, :HEAD_DIM, :CMP_HIDDEN].set(nsa_cmp_w1[l, 0, lo:lo + half])
        ext = ext.at[:, HEAD_DIM:, CMP_HIDDEN:].set(nsa_cmp_w1[l, 1, lo:lo + half])
        return ext.reshape(half * LANES, 2 * CMP_HIDDEN).astype(_MXU)

    w2 = jnp.zeros((2 * CMP_HIDDEN, LANES), _F32)
    w2 = w2.at[:CMP_HIDDEN, :HEAD_DIM].set(nsa_cmp_w2[l, 0]).at[CMP_HIDDEN:, HEAD_DIM:].set(nsa_cmp_w2[l, 1])
    return dict(w_perm=w_perm, wg=wg, bg=bg, fb=fb, fox_gq=fox_gq, fox_gk=fox_gk, nsa_gq=nsa_gq, nsa_gc=nsa_gc,
                nsa_gs=nsa_gs, nsa_gw=nsa_gw, gb=gb, pa=pos_ext(0), pb=pos_ext(half), w1a=w1_ext(0),
                w1b=w1_ext(half), w2=w2.astype(_MXU), on=out_norm[l].reshape(1, 4 * GROUP))


def kernel(x, norm_mix, w_in, lru_conv_w, lru_conv_b, lru_w_gates, lru_b_gates, lru_lambda, sc_conv_w, fox_f_bias, fox_qk_norm, nsa_qk_norm, nsa_cmp_pos, nsa_cmp_w1, nsa_cmp_w2, nsa_gate_bias, rel_bias, out_norm, w_out, norm_ffn, w_gate_up, w_down):
    bsz, s_len, d_model = x.shape
    depth = w_in.shape[0]
    assert d_model == D_MODEL and s_len % TS_MIX == 0 and s_len % TQ_FOX == 0
    assert SLC_TOPK <= s_len // SLC_BLOCK <= SLC_BLOCK
    t = bsz * s_len
    nc = s_len // CMP_STRIDE

    near_tab, cmp_tab = _nsa_tables(rel_bias, s_len)
    ovl = _overlap_ext(s_len)

    x2d = x.reshape(t, D_MODEL)
    for l in range(depth):
        lp = _layer_params(l, w_in, lru_w_gates, lru_b_gates, fox_f_bias, fox_qk_norm, nsa_qk_norm,
                           nsa_cmp_pos, nsa_cmp_w1, nsa_cmp_w2, nsa_gate_bias, out_norm)
        zab, zfox, zq, zcmp, zsel, zwin, zsm = _in_proj(x2d, norm_mix[l][None, :], lp["w_perm"])
        r3 = lambda a: a.reshape(bsz, s_len, a.shape[-1])
        zsm3 = r3(zsm)

        m_ab = _mix_ab(r3(zab), lru_conv_w[l], lru_conv_b[l][None, :], lp["wg"], lp["bg"],
                       lru_lambda[l][None, :], sc_conv_w[l], lp["on"][:, 0:2 * GROUP])

        qf, kf, vf = _fox_prep(r3(zfox), zsm3, lp["fb"], lp["fox_gq"], lp["fox_gk"])
        m_c = _fox_attn(qf, kf, vf, lp["on"][:, 2 * GROUP:3 * GROUP])

        q4, ks, vs, kw, vw, gates = _nsa_prep(r3(zq), r3(zsel), r3(zwin), zsm3, lp["nsa_gq"], lp["nsa_gs"],
                                              lp["nsa_gw"], lp["gb"])
        kc, cv = _compress(zcmp.reshape(bsz, nc, CMP_STRIDE * LANES), lp["pa"], lp["pb"], lp["w1a"],
                           lp["w1b"], lp["w2"], lp["nsa_gc"])
        m_d = _nsa_attn(q4, kc, cv, ks, vs, kw, vw, gates, near_tab, cmp_tab, ovl,
                        lp["on"][:, 3 * GROUP:4 * GROUP])

        x2d = _out_ffn(x2d, m_ab.reshape(t, 2 * GROUP), m_c.reshape(t, GROUP), m_d.reshape(t, GROUP),
                       w_out[l].astype(_MXU), norm_ffn[l][None, :], w_gate_up[l].astype(_MXU),
                       w_down[l].astype(_MXU))
    return x2d.reshape(bsz, s_len, D_MODEL)
```

```python
import functools
import math

import numpy as np
import jax
import jax.numpy as jnp
from jax import lax
from jax.experimental import pallas as pl
from jax.experimental.pallas import tpu as pltpu

D_MODEL = 1024
GROUP = 256
HEAD_DIM = 64
N_HEADS = 4
LRU_CONV = 4
LRU_C = 8.0
SC_CONV = 3
CMP_LEN = 32
CMP_STRIDE = 16
CMP_HIDDEN = 128
SLC_BLOCK = 64
SLC_TOPK = 16
WINDOW = 512
REL_BUCKETS = 32
REL_MAX_DIST = 128
D_FF = 2816
RMS_EPS = 1e-6
NEG = -1e30
BIG = 1e30
LOG2E = math.log2(math.e)

C_FOX_F = 8 * GROUP
C_NSA_Q = C_FOX_F + N_HEADS
C_NSA_G = C_NSA_Q + GROUP + 6 * HEAD_DIM

LANES = 128
VMEM_LIMIT_BYTES = 56 * 1024 * 1024

TM_PROJ = 512
TS_MIX = 512
TQ_FOX = 256
TK_FOX = 512
TQ_NSA = 128
NSA_FAR = 4
NSA_TAIL = 5
NSA_PAD = TS_MIX
TM_FFN = 1024
TF_FFN = 256

SM_GATE_LANE = 16

_MXU = jnp.bfloat16
_F32 = jnp.float32


def _params(sem):
    return pltpu.CompilerParams(dimension_semantics=sem, vmem_limit_bytes=VMEM_LIMIT_BYTES)


def _mm(a, b):
    return jnp.dot(a.astype(_MXU), b.astype(_MXU), preferred_element_type=_F32)


def _mm_nt(a, b):
    return lax.dot_general(a.astype(_MXU), b.astype(_MXU), (((1,), (1,)), ((), ())),
                           preferred_element_type=_F32)


def _lane(shape):
    return lax.broadcasted_iota(jnp.int32, shape, len(shape) - 1)


def _row(shape):
    return lax.broadcasted_iota(jnp.int32, shape, 0)


def _rms_scale(x):
    return lax.rsqrt(jnp.mean(x * x, axis=-1, keepdims=True) + RMS_EPS)


def _gelu(x):
    c = math.sqrt(2.0 / math.pi)
    return x * (0.5 * (1.0 + jnp.tanh(c * (x + 0.044715 * (x * x * x)))))


def _round_mxu(x):
    return x.astype(_MXU).astype(_F32)


def _low_half_rms_scale(x, low):
    s = jnp.sum(jnp.where(low, x * x, 0.0), axis=-1, keepdims=True)
    return lax.rsqrt(s * (1.0 / HEAD_DIM) + RMS_EPS)


def _half_rms_scale(x, low):
    x2 = x * x
    s_lo = jnp.sum(jnp.where(low, x2, 0.0), axis=-1, keepdims=True)
    s_hi = jnp.sum(jnp.where(low, 0.0, x2), axis=-1, keepdims=True)
    return jnp.where(low, lax.rsqrt(s_lo * (1.0 / HEAD_DIM) + RMS_EPS),
                     lax.rsqrt(s_hi * (1.0 / HEAD_DIM) + RMS_EPS))


_IN_GROUPS = (1280, 768, 256, 128, 128, 128, 128)
N_IN_PAD = sum(_IN_GROUPS)


def _in_proj_kernel(x_ref, g_ref, w_ref, *out_refs):
    x = x_ref[...]
    xn = (x * _rms_scale(x) * g_ref[...]).astype(_MXU)
    lo = 0
    for ref, width in zip(out_refs, _IN_GROUPS):
        ref[...] = jnp.dot(xn, w_ref[:, lo:lo + width], preferred_element_type=_F32)
        lo += width


def _in_proj(x2d, gain, w):
    t = x2d.shape[0]
    assert t % TM_PROJ == 0
    return pl.pallas_call(
        _in_proj_kernel,
        grid=(t // TM_PROJ,),
        in_specs=[pl.BlockSpec((TM_PROJ, D_MODEL), lambda i: (i, 0)),
                  pl.BlockSpec((1, D_MODEL), lambda i: (0, 0)),
                  pl.BlockSpec((D_MODEL, N_IN_PAD), lambda i: (0, 0))],
        out_specs=[pl.BlockSpec((TM_PROJ, wd), lambda i: (i, 0)) for wd in _IN_GROUPS],
        out_shape=[jax.ShapeDtypeStruct((t, wd), _F32) for wd in _IN_GROUPS],
        compiler_params=_params(("parallel",)),
        name="in_proj",
    )(x2d, gain, w)


def _linear_scan_rows(a, b):
    n = a.shape[0]
    row = _row(a.shape)
    d = 1
    while d < n:
        keep = row >= d
        a_prev = jnp.where(keep, pltpu.roll(a, d, axis=0), 1.0)
        b_prev = jnp.where(keep, pltpu.roll(b, d, axis=0), 0.0)
        b = a * b_prev + b
        a = a * a_prev
        d *= 2
    return a, b


def _cumsum_rows(x):
    n = x.shape[0]
    row = _row(x.shape)
    d = 1
    while d < n:
        x = x + jnp.where(row >= d, pltpu.roll(x, d, axis=0), 0.0)
        d *= 2
    return x


def _mix_ab_kernel(z_ref, cw_ref, cb_ref, wg_ref, bg_ref, lam_ref, scw_ref, on_ref, out_ref,
                   xext_ref, uext_ref, h_ref):
    ts = z_ref.shape[1]
    ti = pl.program_id(1)

    @pl.when(ti == 0)
    def _():
        xext_ref[0:8, :] = jnp.zeros((8, GROUP), _F32)
        uext_ref[0:8, :] = jnp.zeros((8, GROUP), _F32)
        h_ref[...] = jnp.zeros_like(h_ref)

    xr = z_ref[0, :, 0:GROUP]
    gate = z_ref[0, :, GROUP:2 * GROUP]
    bgt = z_ref[0, :, 2 * GROUP:3 * GROUP]
    cgt = z_ref[0, :, 3 * GROUP:4 * GROUP]
    xs = z_ref[0, :, 4 * GROUP:5 * GROUP]

    xext_ref[8:8 + ts, :] = xr
    xc = cb_ref[...] + cw_ref[LRU_CONV - 1:LRU_CONV, :] * xr
    for k in range(LRU_CONV - 1):
        xc = xc + cw_ref[k:k + 1, :] * xext_ref[pl.ds(8 - (LRU_CONV - 1) + k, ts), :]
    xext_ref[0:8, :] = xr[ts - 8:ts, :]

    gi = _mm(xc, wg_ref[...]) + bg_ref[...]
    r = jax.nn.sigmoid(gi[:, 0:GROUP])
    ig = jax.nn.sigmoid(gi[:, GROUP:2 * GROUP])
    lam = lam_ref[...]
    softplus_neg = jnp.maximum(-lam, 0.0) + jnp.log1p(jnp.exp(-jnp.abs(lam)))
    log_a = (-LRU_C * softplus_neg) * r
    a = jnp.exp(log_a)
    b = jnp.sqrt(-jnp.tanh(log_a) * (a * a + 1.0)) * (ig * xc)
    a_cum, h = _linear_scan_rows(a, b)
    h = h + a_cum * h_ref[0:1, :]
    h_ref[...] = jnp.broadcast_to(h[ts - 1:ts, :], h_ref.shape)
    y_a = h * _gelu(gate)

    u = cgt * xs
    uext_ref[8:8 + ts, :] = u
    cv = scw_ref[SC_CONV - 1:SC_CONV, :] * u
    for k in range(SC_CONV - 1):
        cv = cv + scw_ref[k:k + 1, :] * uext_ref[pl.ds(8 - (SC_CONV - 1) + k, ts), :]
    uext_ref[0:8, :] = u[ts - 8:ts, :]
    y_b = bgt * cv

    out_ref[0, :, 0:GROUP] = (y_a * _rms_scale(y_a) * on_ref[:, 0:GROUP]).astype(out_ref.dtype)
    out_ref[0, :, GROUP:2 * GROUP] = (y_b * _rms_scale(y_b) * on_ref[:, GROUP:2 * GROUP]).astype(out_ref.dtype)


def _mix_ab(zab, cw, cb, wg, bg, lam, scw, on):
    bsz, s_len, _ = zab.shape
    ts = min(TS_MIX, s_len)
    assert s_len % ts == 0
    const = lambda shape: pl.BlockSpec(shape, lambda b, t: (0, 0))
    return pl.pallas_call(
        _mix_ab_kernel,
        grid=(bsz, s_len // ts),
        in_specs=[pl.BlockSpec((1, ts, 5 * GROUP), lambda b, t: (b, t, 0)),
                  const((LRU_CONV, GROUP)), const((1, GROUP)), const((GROUP, 2 * GROUP)),
                  const((1, 2 * GROUP)), const((1, GROUP)), const((SC_CONV, GROUP)), const((1, 2 * GROUP))],
        out_specs=pl.BlockSpec((1, ts, 2 * GROUP), lambda b, t: (b, t, 0)),
        out_shape=jax.ShapeDtypeStruct((bsz, s_len, 2 * GROUP), _MXU),
        scratch_shapes=[pltpu.VMEM((ts + 8, GROUP), _F32), pltpu.VMEM((ts + 8, GROUP), _F32),
                        pltpu.VMEM((8, GROUP), _F32)],
        compiler_params=_params(("parallel", "arbitrary")),
        name="mix_ab",
    )(zab, cw, cb, wg, bg, lam, scw, on)


def _fox_prep_kernel(z_ref, sm_ref, fb_ref, gq_ref, gk_ref, q_ref, k_ref, v_ref, c_ref):
    ts = z_ref.shape[1]
    ti = pl.program_id(1)

    @pl.when(ti == 0)
    def _():
        c_ref[...] = jnp.zeros_like(c_ref)

    shape = (ts, LANES)
    lane = _lane(shape)
    low = lane < HEAD_DIM

    f = sm_ref[0] + fb_ref[...]
    logf = jnp.minimum(f, 0.0) - jnp.log1p(jnp.exp(-jnp.abs(f)))
    c = _cumsum_rows(logf) + c_ref[0:1, :]
    c_ref[...] = jnp.broadcast_to(c[ts - 1:ts, :], c_ref.shape)
    neg_c = -LOG2E * c
    p1 = _round_mxu(neg_c)
    r1 = neg_c - p1
    p2 = _round_mxu(r1)
    p3 = _round_mxu(r1 - p2)
    piece = lane - 3 * ((lane * 11) >> 5)
    csel = jnp.where(piece == 0, p1, jnp.where(piece == 1, p2, p3))

    ones_aug = jnp.where(lane < HEAD_DIM + 3, 1.0, 0.0)
    for p in range(N_HEADS // 2):
        qs = z_ref[0, :, LANES * p:LANES * (p + 1)]
        ks = z_ref[0, :, GROUP + LANES * p:GROUP + LANES * (p + 1)]
        qn = qs * _half_rms_scale(qs, low) * gq_ref[...]
        kn = ks * _half_rms_scale(ks, low) * gk_ref[...]
        for half in range(2):
            h = 2 * p + half
            qh = qn if half == 0 else pltpu.roll(qn, HEAD_DIM, axis=1)
            kh = kn if half == 0 else pltpu.roll(kn, HEAD_DIM, axis=1)
            ch = pltpu.roll(csel, HEAD_DIM - 3 * h, axis=1)
            q_ref[0, h] = jnp.where(low, qh, ones_aug).astype(q_ref.dtype)
            k_ref[0, h] = jnp.where(low, kh, jnp.where(lane < HEAD_DIM + 3, ch, 0.0)).astype(k_ref.dtype)
    v_ref[0] = z_ref[0, :, 2 * GROUP:3 * GROUP].astype(v_ref.dtype)


def _fox_prep(zfox, zsm, fb, gq, gk):
    bsz, s_len, _ = zfox.shape
    ts = min(TS_MIX, s_len)
    const = lambda shape: pl.BlockSpec(shape, lambda b, t: (0, 0))
    head_spec = pl.BlockSpec((1, N_HEADS, ts, LANES), lambda b, t: (b, 0, t, 0))
    return pl.pallas_call(
        _fox_prep_kernel,
        grid=(bsz, s_len // ts),
        in_specs=[pl.BlockSpec((1, ts, 3 * GROUP), lambda b, t: (b, t, 0)),
                  pl.BlockSpec((1, ts, LANES), lambda b, t: (b, t, 0)),
                  const((1, LANES)), const((1, LANES)), const((1, LANES))],
        out_specs=[head_spec, head_spec, pl.BlockSpec((1, ts, GROUP), lambda b, t: (b, t, 0))],
        out_shape=[jax.ShapeDtypeStruct((bsz, N_HEADS, s_len, LANES), _MXU),
                   jax.ShapeDtypeStruct((bsz, N_HEADS, s_len, LANES), _MXU),
                   jax.ShapeDtypeStruct((bsz, s_len, GROUP), _MXU)],
        scratch_shapes=[pltpu.VMEM((8, LANES), _F32)],
        compiler_params=_params(("parallel", "arbitrary")),
        name="fox_prep",
    )(zfox, zsm, fb, gq, gk)


def _flash_step(s, v, carry):
    m, l, acc = carry
    m_new = jnp.maximum(m, jnp.max(s, axis=-1, keepdims=True))
    alpha = jnp.exp2(m - m_new)
    p = jnp.exp2(s - m_new)
    l = alpha * l + jnp.sum(p, axis=-1, keepdims=True)
    acc = alpha * acc + jnp.dot(p.astype(_MXU), v, preferred_element_type=_F32)
    return m_new, l, acc


def _flash_init(rows):
    return (jnp.full((rows, 1), NEG, _F32), jnp.zeros((rows, 1), _F32), jnp.zeros((rows, LANES), _F32))


def _fox_attn_kernel(q_ref, k_ref, v_ref, on_ref, out_ref):
    tq = q_ref.shape[2]
    tk = TK_FOX
    qi = pl.program_id(1)
    qs = [q_ref[0, h] for h in range(N_HEADS)]

    def chunk(c, carries, col_minus_row_max=None):
        start = pl.multiple_of(c * tk, tk)
        new = []
        for h in range(N_HEADS):
            s = _mm_nt(qs[h], k_ref[0, h, pl.ds(start, tk), :])
            if col_minus_row_max is not None:
                s = jnp.where(_lane((tq, tk)) - _row((tq, tk)) <= col_minus_row_max, s, NEG)
            pair = h // 2
            new.append(_flash_step(s, v_ref[0, pl.ds(start, tk), LANES * pair:LANES * (pair + 1)], carries[h]))
        return tuple(new)

    n_full = (qi * tq) // tk
    carries = lax.fori_loop(0, n_full, chunk, tuple(_flash_init(tq) for _ in range(N_HEADS)))
    carries = chunk(n_full, carries, qi * tq - n_full * tk)
    outs = [acc / l for _, l, acc in carries]
    low = _lane((tq, LANES)) < HEAD_DIM
    y = jnp.concatenate([jnp.where(low, outs[2 * p], outs[2 * p + 1]) for p in range(N_HEADS // 2)], axis=1)
    out_ref[0] = (y * _rms_scale(y) * on_ref[...]).astype(out_ref.dtype)


def _fox_attn(qf, kf, vf, on):
    bsz, _, s_len, _ = qf.shape
    tq = min(TQ_FOX, s_len)
    assert s_len % TK_FOX == 0 and TK_FOX % tq == 0
    return pl.pallas_call(
        _fox_attn_kernel,
        grid=(bsz, s_len // tq),
        in_specs=[pl.BlockSpec((1, N_HEADS, tq, LANES), lambda b, i: (b, 0, i, 0)),
                  pl.BlockSpec((1, N_HEADS, s_len, LANES), lambda b, i: (b, 0, 0, 0)),
                  pl.BlockSpec((1, s_len, GROUP), lambda b, i: (b, 0, 0)),
                  pl.BlockSpec((1, GROUP), lambda b, i: (0, 0))],
        out_specs=pl.BlockSpec((1, tq, GROUP), lambda b, i: (b, i, 0)),
        out_shape=jax.ShapeDtypeStruct((bsz, s_len, GROUP), _MXU),
        compiler_params=_params(("parallel", "arbitrary")),
        name="fox_attn",
    )(qf, kf, vf, on)


def _nsa_prep_kernel(zq_ref, zsel_ref, zwin_ref, sm_ref, gq_ref, gs_ref, gw_ref, gb_ref,
                     q4_ref, ks_ref, vs_ref, kw_ref, vw_ref, gate_ref):
    ts = zq_ref.shape[1]
    ti = pl.program_id(1)
    shape = (ts, LANES)
    lane = _lane(shape)
    low = lane < HEAD_DIM
    for p in range(N_HEADS // 2):
        qs = zq_ref[0, :, LANES * p:LANES * (p + 1)]
        qn = qs * _half_rms_scale(qs, low) * gq_ref[...]
        q4_ref[0, 2 * p] = jnp.where(low, qn, 0.0).astype(q4_ref.dtype)
        q4_ref[0, 2 * p + 1] = jnp.where(low, pltpu.roll(qn, HEAD_DIM, axis=1), 0.0).astype(q4_ref.dtype)

    real = ti < pl.num_programs(1) - 1
    sel = jnp.where(real, zsel_ref[0], 0.0)
    blk = (ti * ts + _row(shape)) >> 6
    onehot = jnp.where(real, jnp.where(lane - HEAD_DIM == blk, 1.0, 0.0), 0.0)
    ks_ref[0] = jnp.where(low, sel * _low_half_rms_scale(sel, low) * gs_ref[...], onehot).astype(ks_ref.dtype)
    vs_ref[0] = sel.astype(vs_ref.dtype)
    win = zwin_ref[0]
    kw_ref[0] = jnp.where(low, win * _low_half_rms_scale(win, low) * gw_ref[...], 0.0).astype(kw_ref.dtype)
    vw_ref[0] = win.astype(vw_ref.dtype)
    gate_ref[0] = jax.nn.sigmoid(sm_ref[0] + gb_ref[...])


def _nsa_prep(zq, zsel, zwin, zsm, gq, gs, gw, gb):
    bsz, s_len, _ = zq.shape
    ts = NSA_PAD
    nt = s_len // ts
    const = lambda shape: pl.BlockSpec(shape, lambda b, t: (0, 0))
    last = lambda t: jnp.minimum(t, nt - 1)
    slab = pl.BlockSpec((1, ts, LANES), lambda b, t: (b, last(t), 0))
    padded = pl.BlockSpec((1, ts, LANES), lambda b, t: (b, t, 0))
    slab_shape = jax.ShapeDtypeStruct((bsz, s_len, LANES), _MXU)
    padded_shape = jax.ShapeDtypeStruct((bsz, s_len + NSA_PAD, LANES), _MXU)
    return pl.pallas_call(
        _nsa_prep_kernel,
        grid=(bsz, nt + 1),
        in_specs=[pl.BlockSpec((1, ts, GROUP), lambda b, t: (b, last(t), 0)), slab, slab, slab,
                  const((1, LANES)), const((1, LANES)), const((1, LANES)), const((1, LANES))],
        out_specs=[pl.BlockSpec((1, N_HEADS, ts, LANES), lambda b, t: (b, 0, last(t), 0)),
                   padded, padded, slab, slab, slab],
        out_shape=[jax.ShapeDtypeStruct((bsz, N_HEADS, s_len, LANES), _MXU),
                   padded_shape, padded_shape, slab_shape, slab_shape,
                   jax.ShapeDtypeStruct((bsz, s_len, LANES), _F32)],
        compiler_params=_params(("parallel", "arbitrary")),
        name="nsa_prep",
    )(zq, zsel, zwin, zsm, gq, gs, gw, gb)


def _compress_kernel(x_ref, pa_ref, pb_ref, w1a_ref, w1b_ref, w2_ref, g_ref, kc_ref, cv_ref):
    nc = x_ref.shape[1]
    x = x_ref[0]
    ua = _mm(x + pa_ref[...], w1a_ref[...])
    ub = _mm(x + pb_ref[...], w1b_ref[...])
    hid = _gelu(ua + pltpu.roll(ub, nc - 1, axis=0))
    out = _mm(hid, w2_ref[...])
    low = _lane(out.shape) < HEAD_DIM
    kc_ref[0] = jnp.where(low, out * _low_half_rms_scale(out, low) * g_ref[...], 0.0).astype(kc_ref.dtype)
    cv_ref[0] = out.astype(cv_ref.dtype)


def _compress(xc, pa, pb, w1a, w1b, w2, g):
    bsz, nc, width = xc.shape
    const = lambda shape: pl.BlockSpec(shape, lambda b: (0, 0))
    slab = pl.BlockSpec((1, nc, LANES), lambda b: (b, 0, 0))
    return pl.pallas_call(
        _compress_kernel,
        grid=(bsz,),
        in_specs=[pl.BlockSpec((1, nc, width), lambda b: (b, 0, 0)),
                  const((1, width)), const((1, width)), const((width, 2 * CMP_HIDDEN)),
                  const((width, 2 * CMP_HIDDEN)), const((2 * CMP_HIDDEN, LANES)), const((1, LANES))],
        out_specs=[slab, slab],
        out_shape=[jax.ShapeDtypeStruct((bsz, nc, LANES), _MXU)] * 2,
        compiler_params=_params(("parallel",)),
        name="compress",
    )(xc, pa, pb, w1a, w1b, w2, g)


def _bucket_thresholds():
    max_exact = REL_BUCKETS // 2
    d = np.arange(0, REL_MAX_DIST + 1)
    large = max_exact + (np.log(np.maximum(d, 1).astype(np.float32) / max_exact)
                         / math.log(REL_MAX_DIST / max_exact) * (REL_BUCKETS - max_exact)).astype(np.int32)
    bucket = np.where(d < max_exact, d, np.minimum(large, REL_BUCKETS - 1))
    assert bucket[-1] == REL_BUCKETS - 1 and np.all(np.diff(bucket) >= 0)
    return [int(np.argmax(bucket >= k)) for k in range(REL_BUCKETS)]


_BUCKET_THR = _bucket_thresholds()
assert NSA_TAIL == WINDOW // TQ_NSA + 1
N_TAB = 2 * NSA_TAIL - 1


def _rel_bias(dist, rb_ref, h):
    far = rb_ref[REL_BUCKETS - 1, h]
    val = jnp.full(dist.shape, LOG2E * (rb_ref[0, h] - far), _F32)
    for k in range(1, REL_BUCKETS):
        val = jnp.where(dist >= _BUCKET_THR[k], LOG2E * (rb_ref[k, h] - far), val)
    return val


def _nsa_tables_kernel(rb_ref, win_ref, sel_ref, cmp_ref):
    qi = pl.program_id(0)
    tq = TQ_NSA
    nc = cmp_ref.shape[2]

    @pl.when(qi == 0)
    def _():
        i = _row((tq, tq))
        c = _lane((tq, tq))
        diag = NSA_TAIL - 1
        for e in range(N_TAB):
            for h in range(N_HEADS):
                rs = slice(h * tq, (h + 1) * tq)
                if e > diag:
                    win_ref[e, rs, :] = jnp.full((tq, tq), NEG, _F32)
                    sel_ref[e, rs, :] = jnp.full((tq, tq), NEG, _F32)
                    continue
                dist = i + (diag - e) * tq - c
                ok = jnp.where(dist >= 0, jnp.where(dist < WINDOW, 1, 0), 0) > 0
                tab = jnp.where(ok, _rel_bias(dist, rb_ref, h), NEG)
                win_ref[e, rs, :] = tab
                if e >= diag - 1:
                    sel_ref[e, rs, :] = tab
                else:
                    sel_ref[e, rs, :] = jnp.zeros((tq, tq), _F32)

    t = qi * tq + _row((tq, nc))
    dist = t - (CMP_STRIDE * _lane((tq, nc)) + CMP_LEN - 1)
    for h in range(N_HEADS):
        cmp_ref[0, h * tq:(h + 1) * tq, :] = jnp.where(dist >= 0, _rel_bias(dist, rb_ref, h), NEG)


def _nsa_tables(rel_bias, s_len):
    nq = s_len // TQ_NSA
    nc = s_len // CMP_STRIDE
    rows = N_HEADS * TQ_NSA
    tab_spec = pl.BlockSpec((N_TAB, rows, TQ_NSA), lambda i: (0, 0, 0))
    tab_shape = jax.ShapeDtypeStruct((N_TAB, rows, TQ_NSA), _F32)
    return pl.pallas_call(
        _nsa_tables_kernel,
        grid=(nq,),
        in_specs=[pl.BlockSpec(memory_space=pltpu.SMEM)],
        out_specs=[tab_spec, tab_spec, pl.BlockSpec((1, rows, nc), lambda i: (i, 0, 0))],
        out_shape=[tab_shape, tab_shape, jax.ShapeDtypeStruct((nq, rows, nc), _F32)],
        compiler_params=_params(("arbitrary",)),
        name="nsa_tables",
    )(rel_bias)


def _nsa_attn_kernel(q4_ref, kc_ref, cv_ref, ks_ref, vs_ref, kw_ref, vw_ref, gate_ref,
                     win_ref, sel_ref, cmp_ref, ovl_ref, on_ref, out_ref):
    tq = TQ_NSA
    rows = N_HEADS * tq
    qi = pl.program_id(1)
    q4 = q4_ref[0].reshape(rows, LANES)
    lane = _lane((tq, LANES))
    low = lane < HEAD_DIM

    def tail_table(tab_ref, first_tile):
        e0 = first_tile - qi + (NSA_TAIL - 1)
        return jnp.concatenate([tab_ref[e0 + u] for u in range(NSA_TAIL)], axis=1)

    s = _mm_nt(q4, kc_ref[0]) + cmp_ref[0]
    valid = s > 0.5 * NEG
    m = jnp.max(s, axis=-1, keepdims=True)
    p = jnp.where(valid, jnp.exp2(s - m), 0.0)
    l = jnp.sum(p, axis=-1, keepdims=True)
    p = p / jnp.where(l > 0.0, l, 1.0)
    o_c = jnp.dot(p.astype(_MXU), cv_ref[0], preferred_element_type=_F32)

    psum = p[0:tq] + p[tq:2 * tq] + p[2 * tq:3 * tq] + p[3 * tq:4 * tq]
    p_hi = _round_mxu(psum)
    imp = _mm(p_hi, ovl_ref[...]) + _mm(psum - p_hi, ovl_ref[...])

    n_blk = SLC_BLOCK
    imp_t = imp.T[0:n_blk, :]
    blk = _row((n_blk, tq))
    cur = (qi * tq + _lane((n_blk, tq))) >> 6
    forced = jnp.where(blk == 0, 1, jnp.where(blk == cur, 1, jnp.where(blk == cur - 1, 1, 0))) > 0
    val = jnp.where(blk <= cur, jnp.where(forced, BIG, imp_t), NEG)
    rank = jnp.zeros((n_blk, tq), _F32)
    for j in range(n_blk):
        vj = val[j:j + 1, :]
        rank = rank + jnp.where(vj > val, 1.0, jnp.where(vj == val, jnp.where(blk > j, 1.0, 0.0), 0.0))
    pen_t = jnp.where(rank < float(SLC_TOPK), 0.0, NEG)
    pen = jnp.concatenate([jnp.zeros((LANES - n_blk, tq), _F32), pen_t], axis=0).T
    pen4 = jnp.concatenate([pen] * N_HEADS, axis=0)
    q_aug = jnp.where(_lane((rows, LANES)) < HEAD_DIM, q4.astype(_F32), pen4).astype(_MXU)

    w_tile = jnp.maximum(qi - (NSA_TAIL - 1), 0)
    w_start = pl.multiple_of(w_tile * tq, tq)
    s = _mm_nt(q_aug, kw_ref[0, pl.ds(w_start, NSA_TAIL * tq), :]) + tail_table(win_ref, w_tile)
    _, l_w, acc_w = _flash_step(s, vw_ref[0, pl.ds(w_start, NSA_TAIL * tq), :], _flash_init(rows))
    o_w = acc_w / l_w

    def sel_far(c, carry):
        start = pl.multiple_of(c * (NSA_FAR * tq), NSA_FAR * tq)
        s = _mm_nt(q_aug, ks_ref[0, pl.ds(start, NSA_FAR * tq), :])
        return _flash_step(s, vs_ref[0, pl.ds(start, NSA_FAR * tq), :], carry)

    n_far = jnp.maximum(qi - 1, 0) // NSA_FAR
    carry = lax.fori_loop(0, n_far, sel_far, _flash_init(rows))
    t_tile = n_far * NSA_FAR
    t_start = pl.multiple_of(t_tile * tq, tq)
    s = _mm_nt(q_aug, ks_ref[0, pl.ds(t_start, NSA_TAIL * tq), :]) + tail_table(sel_ref, t_tile)
    _, l_s, acc_s = _flash_step(s, vs_ref[0, pl.ds(t_start, NSA_TAIL * tq), :], carry)
    o_s = acc_s / l_s

    g = gate_ref[0]
    heads = []
    for h in range(N_HEADS):
        rs = slice(h * tq, (h + 1) * tq)
        gc = g[:, SM_GATE_LANE + h:SM_GATE_LANE + h + 1]
        gs = g[:, SM_GATE_LANE + N_HEADS + h:SM_GATE_LANE + N_HEADS + h + 1]
        gw = g[:, SM_GATE_LANE + 2 * N_HEADS + h:SM_GATE_LANE + 2 * N_HEADS + h + 1]
        heads.append(gc * o_c[rs] + gs * o_s[rs] + gw * o_w[rs])
    slabs = [jnp.where(low, pltpu.roll(heads[2 * p], HEAD_DIM, axis=1), heads[2 * p + 1])
             for p in range(N_HEADS // 2)]
    y = jnp.concatenate(slabs, axis=1)
    out_ref[0] = (y * _rms_scale(y) * on_ref[...]).astype(out_ref.dtype)


def _nsa_attn(q4, kc, cv, ks, vs, kw, vw, gates, win_tab, sel_tab, cmp_tab, ovl, on):
    bsz, _, s_len, _ = q4.shape
    tq = TQ_NSA
    nc = kc.shape[1]
    rows = N_HEADS * tq
    assert ks.shape[1] == s_len + NSA_PAD and NSA_PAD >= (NSA_TAIL - 1) * tq and s_len >= NSA_TAIL * tq
    full = lambda n: pl.BlockSpec((1, n, LANES), lambda b, i: (b, 0, 0))
    tab_spec = pl.BlockSpec((N_TAB, rows, tq), lambda b, i: (0, 0, 0))
    return pl.pallas_call(
        _nsa_attn_kernel,
        grid=(bsz, s_len // tq),
        in_specs=[pl.BlockSpec((1, N_HEADS, tq, LANES), lambda b, i: (b, 0, i, 0)),
                  full(nc), full(nc), full(s_len + NSA_PAD), full(s_len + NSA_PAD), full(s_len), full(s_len),
                  pl.BlockSpec((1, tq, LANES), lambda b, i: (b, i, 0)),
                  tab_spec, tab_spec,
                  pl.BlockSpec((1, rows, nc), lambda b, i: (i, 0, 0)),
                  pl.BlockSpec((nc, LANES), lambda b, i: (0, 0)),
                  pl.BlockSpec((1, GROUP), lambda b, i: (0, 0))],
        out_specs=pl.BlockSpec((1, tq, GROUP), lambda b, i: (b, i, 0)),
        out_shape=jax.ShapeDtypeStruct((bsz, s_len, GROUP), _MXU),
        compiler_params=_params(("parallel", "arbitrary")),
        name="nsa_attn",
    )(q4, kc, cv, ks, vs, kw, vw, gates, win_tab, sel_tab, cmp_tab, ovl, on)


def _ffn_kernel(x_ref, mab_ref, mc_ref, md_ref, wo_ref, g_ref, wg_ref, wu_ref, wd_ref, out_ref,
                x1_ref, xn_ref, acc_ref):
    j = pl.program_id(1)

    @pl.when(j == 0)
    def _():
        x1 = (x_ref[...]
              + jnp.dot(mab_ref[...], wo_ref[0:2 * GROUP, :], preferred_element_type=_F32)
              + jnp.dot(mc_ref[...], wo_ref[2 * GROUP:3 * GROUP, :], preferred_element_type=_F32)
              + jnp.dot(md_ref[...], wo_ref[3 * GROUP:4 * GROUP, :], preferred_element_type=_F32))
        x1_ref[...] = x1
        xn_ref[...] = (x1 * _rms_scale(x1) * g_ref[...]).astype(xn_ref.dtype)
        acc_ref[...] = jnp.zeros_like(acc_ref)

    xn = xn_ref[...]
    gt = jnp.dot(xn, wg_ref[...], preferred_element_type=_F32)
    up = jnp.dot(xn, wu_ref[...], preferred_element_type=_F32)
    hid = (gt * jax.nn.sigmoid(gt)) * up
    acc_ref[...] += jnp.dot(hid.astype(_MXU), wd_ref[...], preferred_element_type=_F32)

    @pl.when(j == pl.num_programs(1) - 1)
    def _():
        out_ref[...] = x1_ref[...] + acc_ref[...]


def _out_ffn(x2d, mab, mc, md, wo, g, wgu, wd):
    t = x2d.shape[0]
    tm = min(TM_FFN, t)
    nf = D_FF // TF_FFN
    assert t % tm == 0 and D_FF % TF_FFN == 0
    row = lambda width: pl.BlockSpec((tm, width), lambda i, j: (i, 0))
    return pl.pallas_call(
        _ffn_kernel,
        grid=(t // tm, nf),
        in_specs=[row(D_MODEL), row(2 * GROUP), row(GROUP), row(GROUP),
                  pl.BlockSpec((D_MODEL, D_MODEL), lambda i, j: (0, 0)),
                  pl.BlockSpec((1, D_MODEL), lambda i, j: (0, 0)),
                  pl.BlockSpec((D_MODEL, TF_FFN), lambda i, j: (0, j)),
                  pl.BlockSpec((D_MODEL, TF_FFN), lambda i, j: (0, j + nf)),
                  pl.BlockSpec((TF_FFN, D_MODEL), lambda i, j: (j, 0))],
        out_specs=row(D_MODEL),
        out_shape=jax.ShapeDtypeStruct((t, D_MODEL), _F32),
        scratch_shapes=[pltpu.VMEM((tm, D_MODEL), _F32), pltpu.VMEM((tm, D_MODEL), _MXU),
                        pltpu.VMEM((tm, D_MODEL), _F32)],
        compiler_params=_params(("parallel", "arbitrary")),
        name="out_ffn",
    )(x2d, mab, mc, md, wo, g, wgu, wgu, wd)


def _lane_vec(values, start):
    v = jnp.zeros((LANES,), _F32).at[start:start + values.shape[0]].set(values.astype(_F32))
    return v[None, :]


def _block_diag(w):
    h, d, _ = w.shape
    eye = jnp.eye(h, dtype=w.dtype)
    return (eye[:, None, :, None] * w[:, :, None, :]).reshape(h * d, h * d)


def _overlap_ext(s_len):
    nc = s_len // CMP_STRIDE
    n_cmp = nc - 1
    n_slc = s_len // SLC_BLOCK
    cs = np.arange(n_cmp)[:, None] * CMP_STRIDE
    ss = np.arange(n_slc)[None, :] * SLC_BLOCK
    ov = np.clip(np.minimum(cs + CMP_LEN, ss + SLC_BLOCK) - np.maximum(cs, ss), 0, CMP_LEN)
    ext = np.zeros((nc, LANES), np.float32)
    ext[:n_cmp, :n_slc] = ov
    return jnp.asarray(ext, _MXU)


def _layer_params(l, w_in, lru_w_gates, lru_b_gates, fox_f_bias, fox_qk_norm, nsa_qk_norm, nsa_cmp_pos,
                  nsa_cmp_w1, nsa_cmp_w2, nsa_gate_bias, out_norm):
    scale = HEAD_DIM ** -0.5 * LOG2E
    w = w_in[l]
    small = jnp.zeros((D_MODEL, LANES), w.dtype)
    small = small.at[:, 0:3 * N_HEADS].set(jnp.repeat(w[:, C_FOX_F:C_FOX_F + N_HEADS], 3, axis=1))
    small = small.at[:, SM_GATE_LANE:SM_GATE_LANE + 3 * N_HEADS].set(w[:, C_NSA_G:C_NSA_G + 3 * N_HEADS])
    w_perm = jnp.concatenate([w[:, :C_FOX_F], w[:, C_NSA_Q:C_NSA_G], small], axis=1).astype(_MXU)

    wg = jnp.concatenate([_block_diag(lru_w_gates[l, 0]), _block_diag(lru_w_gates[l, 1])], axis=1).astype(_MXU)
    bg = lru_b_gates[l].reshape(1, 2 * GROUP)

    fb = _lane_vec(jnp.repeat(fox_f_bias[l], 3), 0)
    fox_gq = (jnp.tile(fox_qk_norm[l, 0], 2) * scale)[None, :]
    fox_gk = jnp.tile(fox_qk_norm[l, 1], 2)[None, :]

    nsa_gq = (jnp.tile(nsa_qk_norm[l, 0], 2) * scale)[None, :]
    nsa_gc = _lane_vec(nsa_qk_norm[l, 1], 0)
    nsa_gs = _lane_vec(nsa_qk_norm[l, 2], 0)
    nsa_gw = _lane_vec(nsa_qk_norm[l, 3], 0)
    gb = _lane_vec(nsa_gate_bias[l], SM_GATE_LANE)

    half = CMP_LEN // 2

    def pos_ext(lo):
        return jnp.concatenate([nsa_cmp_pos[l, 0, lo:lo + half], nsa_cmp_pos[l, 1, lo:lo + half]],
                               axis=1).reshape(1, half * LANES)

    def w1_ext(lo):
        ext = jnp.zeros((half, LANES, 2 * CMP_HIDDEN), _F32)
        ext = ext.at[:, :HEAD_DIM, :CMP_HIDDEN].set(nsa_cmp_w1[l, 0, lo:lo + half])
        ext = ext.at[:, HEAD_DIM:, CMP_HIDDEN:].set(nsa_cmp_w1[l, 1, lo:lo + half])
        return ext.reshape(half * LANES, 2 * CMP_HIDDEN).astype(_MXU)

    w2 = jnp.zeros((2 * CMP_HIDDEN, LANES), _F32)
    w2 = w2.at[:CMP_HIDDEN, :HEAD_DIM].set(nsa_cmp_w2[l, 0]).at[CMP_HIDDEN:, HEAD_DIM:].set(nsa_cmp_w2[l, 1])
    return dict(w_perm=w_perm, wg=wg, bg=bg, fb=fb, fox_gq=fox_gq, fox_gk=fox_gk, nsa_gq=nsa_gq, nsa_gc=nsa_gc,
                nsa_gs=nsa_gs, nsa_gw=nsa_gw, gb=gb, pa=pos_ext(0), pb=pos_ext(half), w1a=w1_ext(0),
                w1b=w1_ext(half), w2=w2.astype(_MXU), on=out_norm[l].reshape(1, 4 * GROUP))


def kernel(x, norm_mix, w_in, lru_conv_w, lru_conv_b, lru_w_gates, lru_b_gates, lru_lambda, sc_conv_w, fox_f_bias, fox_qk_norm, nsa_qk_norm, nsa_cmp_pos, nsa_cmp_w1, nsa_cmp_w2, nsa_gate_bias, rel_bias, out_norm, w_out, norm_ffn, w_gate_up, w_down):
    bsz, s_len, d_model = x.shape
    depth = w_in.shape[0]
    assert d_model == D_MODEL and s_len % TS_MIX == 0 and s_len % TQ_FOX == 0
    assert SLC_TOPK <= s_len // SLC_BLOCK <= SLC_BLOCK
    t = bsz * s_len
    nc = s_len // CMP_STRIDE

    win_tab, sel_tab, cmp_tab = _nsa_tables(rel_bias, s_len)
    ovl = _overlap_ext(s_len)

    x2d = x.reshape(t, D_MODEL)
    for l in range(depth):
        lp = _layer_params(l, w_in, lru_w_gates, lru_b_gates, fox_f_bias, fox_qk_norm, nsa_qk_norm,
                           nsa_cmp_pos, nsa_cmp_w1, nsa_cmp_w2, nsa_gate_bias, out_norm)
        zab, zfox, zq, zcmp, zsel, zwin, zsm = _in_proj(x2d, norm_mix[l][None, :], lp["w_perm"])
        r3 = lambda a: a.reshape(bsz, s_len, a.shape[-1])
        zsm3 = r3(zsm)

        m_ab = _mix_ab(r3(zab), lru_conv_w[l], lru_conv_b[l][None, :], lp["wg"], lp["bg"],
                       lru_lambda[l][None, :], sc_conv_w[l], lp["on"][:, 0:2 * GROUP])

        qf, kf, vf = _fox_prep(r3(zfox), zsm3, lp["fb"], lp["fox_gq"], lp["fox_gk"])
        m_c = _fox_attn(qf, kf, vf, lp["on"][:, 2 * GROUP:3 * GROUP])

        q4, ks, vs, kw, vw, gates = _nsa_prep(r3(zq), r3(zsel), r3(zwin), zsm3, lp["nsa_gq"], lp["nsa_gs"],
                                              lp["nsa_gw"], lp["gb"])
        kc, cv = _compress(zcmp.reshape(bsz, nc, CMP_STRIDE * LANES), lp["pa"], lp["pb"], lp["w1a"],
                           lp["w1b"], lp["w2"], lp["nsa_gc"])
        m_d = _nsa_attn(q4, kc, cv, ks, vs, kw, vw, gates, win_tab, sel_tab, cmp_tab, ovl,
                        lp["on"][:, 3 * GROUP:4 * GROUP])

        x2d = _out_ffn(x2d, m_ab.reshape(t, 2 * GROUP), m_c.reshape(t, GROUP), m_d.reshape(t, GROUP),
                       w_out[l].astype(_MXU), norm_ffn[l][None, :], w_gate_up[l].astype(_MXU),
                       w_down[l].astype(_MXU))
    return x2d.reshape(bsz, s_len, D_MODEL)
```

```python
import functools
import math

import numpy as np
import jax
import jax.numpy as jnp
from jax import lax
from jax.experimental import pallas as pl
from jax.experimental.pallas import tpu as pltpu

D_MODEL = 1024
GROUP = 256
HEAD_DIM = 64
N_HEADS = 4
LRU_CONV = 4
LRU_C = 8.0
SC_CONV = 3
CMP_LEN = 32
CMP_STRIDE = 16
CMP_HIDDEN = 128
SLC_BLOCK = 64
SLC_TOPK = 16
WINDOW = 512
REL_BUCKETS = 32
REL_MAX_DIST = 128
D_FF = 2816
RMS_EPS = 1e-6
NEG = -1e30
BIG = 1e30
LOG2E = math.log2(math.e)

C_FOX_F = 8 * GROUP
C_NSA_Q = C_FOX_F + N_HEADS
C_NSA_G = C_NSA_Q + GROUP + 6 * HEAD_DIM

LANES = 128
VMEM_LIMIT_BYTES = 56 * 1024 * 1024

TM_PROJ = 512
TS_MIX = 512
TQ_FOX = 512
TK_FOX = 1024
KT_NSA = 128
TQ_NSA = 256
NSA_FAR = 4
NSA_PAD = TS_MIX
TM_FFN = 1024
TF_FFN = 256

SM_GATE_LANE = 16

_MXU = jnp.bfloat16
_F32 = jnp.float32


def _params(sem):
    return pltpu.CompilerParams(dimension_semantics=sem, vmem_limit_bytes=VMEM_LIMIT_BYTES)


def _mm(a, b):
    return jnp.dot(a.astype(_MXU), b.astype(_MXU), preferred_element_type=_F32)


def _mm_nt(a, b):
    return lax.dot_general(a.astype(_MXU), b.astype(_MXU), (((1,), (1,)), ((), ())),
                           preferred_element_type=_F32)


def _lane(shape):
    return lax.broadcasted_iota(jnp.int32, shape, len(shape) - 1)


def _row(shape):
    return lax.broadcasted_iota(jnp.int32, shape, 0)


def _rms_scale(x):
    return lax.rsqrt(jnp.mean(x * x, axis=-1, keepdims=True) + RMS_EPS)


def _gelu(x):
    c = math.sqrt(2.0 / math.pi)
    return x * (0.5 * (1.0 + jnp.tanh(c * (x + 0.044715 * (x * x * x)))))


def _round_mxu(x):
    return x.astype(_MXU).astype(_F32)


def _low_half_rms_scale(x, low):
    s = jnp.sum(jnp.where(low, x * x, 0.0), axis=-1, keepdims=True)
    return lax.rsqrt(s * (1.0 / HEAD_DIM) + RMS_EPS)


def _half_rms_scale(x, low):
    x2 = x * x
    s_lo = jnp.sum(jnp.where(low, x2, 0.0), axis=-1, keepdims=True)
    s_hi = jnp.sum(jnp.where(low, 0.0, x2), axis=-1, keepdims=True)
    return jnp.where(low, lax.rsqrt(s_lo * (1.0 / HEAD_DIM) + RMS_EPS),
                     lax.rsqrt(s_hi * (1.0 / HEAD_DIM) + RMS_EPS))


_IN_GROUPS = (1280, 768, 256, 128, 128, 128, 128)
N_IN_PAD = sum(_IN_GROUPS)


def _in_proj_kernel(x_ref, g_ref, w_ref, *out_refs):
    x = x_ref[...]
    xn = (x * _rms_scale(x) * g_ref[...]).astype(_MXU)
    lo = 0
    for ref, width in zip(out_refs, _IN_GROUPS):
        ref[...] = jnp.dot(xn, w_ref[:, lo:lo + width], preferred_element_type=_F32)
        lo += width


def _in_proj(x2d, gain, w):
    t = x2d.shape[0]
    assert t % TM_PROJ == 0
    return pl.pallas_call(
        _in_proj_kernel,
        grid=(t // TM_PROJ,),
        in_specs=[pl.BlockSpec((TM_PROJ, D_MODEL), lambda i: (i, 0)),
                  pl.BlockSpec((1, D_MODEL), lambda i: (0, 0)),
                  pl.BlockSpec((D_MODEL, N_IN_PAD), lambda i: (0, 0))],
        out_specs=[pl.BlockSpec((TM_PROJ, wd), lambda i: (i, 0)) for wd in _IN_GROUPS],
        out_shape=[jax.ShapeDtypeStruct((t, wd), _F32) for wd in _IN_GROUPS],
        compiler_params=_params(("parallel",)),
        name="in_proj",
    )(x2d, gain, w)


def _linear_scan_rows(a, b):
    n = a.shape[0]
    row = _row(a.shape)
    d = 1
    while d < n:
        keep = row >= d
        a_prev = jnp.where(keep, pltpu.roll(a, d, axis=0), 1.0)
        b_prev = jnp.where(keep, pltpu.roll(b, d, axis=0), 0.0)
        b = a * b_prev + b
        a = a * a_prev
        d *= 2
    return a, b


def _cumsum_rows(x):
    n = x.shape[0]
    row = _row(x.shape)
    d = 1
    while d < n:
        x = x + jnp.where(row >= d, pltpu.roll(x, d, axis=0), 0.0)
        d *= 2
    return x


def _mix_ab_kernel(z_ref, cw_ref, cb_ref, wg_ref, bg_ref, lam_ref, scw_ref, on_ref, out_ref,
                   xext_ref, uext_ref, h_ref):
    ts = z_ref.shape[1]
    ti = pl.program_id(1)

    @pl.when(ti == 0)
    def _():
        xext_ref[0:8, :] = jnp.zeros((8, GROUP), _F32)
        uext_ref[0:8, :] = jnp.zeros((8, GROUP), _F32)
        h_ref[...] = jnp.zeros_like(h_ref)

    xr = z_ref[0, :, 0:GROUP]
    gate = z_ref[0, :, GROUP:2 * GROUP]
    bgt = z_ref[0, :, 2 * GROUP:3 * GROUP]
    cgt = z_ref[0, :, 3 * GROUP:4 * GROUP]
    xs = z_ref[0, :, 4 * GROUP:5 * GROUP]

    xext_ref[8:8 + ts, :] = xr
    xc = cb_ref[...] + cw_ref[LRU_CONV - 1:LRU_CONV, :] * xr
    for k in range(LRU_CONV - 1):
        xc = xc + cw_ref[k:k + 1, :] * xext_ref[pl.ds(8 - (LRU_CONV - 1) + k, ts), :]
    xext_ref[0:8, :] = xr[ts - 8:ts, :]

    gi = _mm(xc, wg_ref[...]) + bg_ref[...]
    r = jax.nn.sigmoid(gi[:, 0:GROUP])
    ig = jax.nn.sigmoid(gi[:, GROUP:2 * GROUP])
    lam = lam_ref[...]
    softplus_neg = jnp.maximum(-lam, 0.0) + jnp.log1p(jnp.exp(-jnp.abs(lam)))
    log_a = (-LRU_C * softplus_neg) * r
    a = jnp.exp(log_a)
    b = jnp.sqrt(-jnp.tanh(log_a) * (a * a + 1.0)) * (ig * xc)
    a_cum, h = _linear_scan_rows(a, b)
    h = h + a_cum * h_ref[0:1, :]
    h_ref[...] = jnp.broadcast_to(h[ts - 1:ts, :], h_ref.shape)
    y_a = h * _gelu(gate)

    u = cgt * xs
    uext_ref[8:8 + ts, :] = u
    cv = scw_ref[SC_CONV - 1:SC_CONV, :] * u
    for k in range(SC_CONV - 1):
        cv = cv + scw_ref[k:k + 1, :] * uext_ref[pl.ds(8 - (SC_CONV - 1) + k, ts), :]
    uext_ref[0:8, :] = u[ts - 8:ts, :]
    y_b = bgt * cv

    out_ref[0, :, 0:GROUP] = (y_a * _rms_scale(y_a) * on_ref[:, 0:GROUP]).astype(out_ref.dtype)
    out_ref[0, :, GROUP:2 * GROUP] = (y_b * _rms_scale(y_b) * on_ref[:, GROUP:2 * GROUP]).astype(out_ref.dtype)


def _mix_ab(zab, cw, cb, wg, bg, lam, scw, on):
    bsz, s_len, _ = zab.shape
    ts = min(TS_MIX, s_len)
    assert s_len % ts == 0
    const = lambda shape: pl.BlockSpec(shape, lambda b, t: (0, 0))
    return pl.pallas_call(
        _mix_ab_kernel,
        grid=(bsz, s_len // ts),
        in_specs=[pl.BlockSpec((1, ts, 5 * GROUP), lambda b, t: (b, t, 0)),
                  const((LRU_CONV, GROUP)), const((1, GROUP)), const((GROUP, 2 * GROUP)),
                  const((1, 2 * GROUP)), const((1, GROUP)), const((SC_CONV, GROUP)), const((1, 2 * GROUP))],
        out_specs=pl.BlockSpec((1, ts, 2 * GROUP), lambda b, t: (b, t, 0)),
        out_shape=jax.ShapeDtypeStruct((bsz, s_len, 2 * GROUP), _MXU),
        scratch_shapes=[pltpu.VMEM((ts + 8, GROUP), _F32), pltpu.VMEM((ts + 8, GROUP), _F32),
                        pltpu.VMEM((8, GROUP), _F32)],
        compiler_params=_params(("parallel", "arbitrary")),
        name="mix_ab",
    )(zab, cw, cb, wg, bg, lam, scw, on)


def _fox_prep_kernel(z_ref, sm_ref, fb_ref, gq_ref, gk_ref, q_ref, k_ref, v_ref, c_ref):
    ts = z_ref.shape[1]
    ti = pl.program_id(1)

    @pl.when(ti == 0)
    def _():
        c_ref[...] = jnp.zeros_like(c_ref)

    shape = (ts, LANES)
    lane = _lane(shape)
    low = lane < HEAD_DIM

    f = sm_ref[0] + fb_ref[...]
    logf = jnp.minimum(f, 0.0) - jnp.log1p(jnp.exp(-jnp.abs(f)))
    c = _cumsum_rows(logf) + c_ref[0:1, :]
    c_ref[...] = jnp.broadcast_to(c[ts - 1:ts, :], c_ref.shape)
    neg_c = -LOG2E * c
    p1 = _round_mxu(neg_c)
    r1 = neg_c - p1
    p2 = _round_mxu(r1)
    p3 = _round_mxu(r1 - p2)
    piece = lane - 3 * ((lane * 11) >> 5)
    csel = jnp.where(piece == 0, p1, jnp.where(piece == 1, p2, p3))

    ones_aug = jnp.where(lane < HEAD_DIM + 3, 1.0, 0.0)
    for p in range(N_HEADS // 2):
        qs = z_ref[0, :, LANES * p:LANES * (p + 1)]
        ks = z_ref[0, :, GROUP + LANES * p:GROUP + LANES * (p + 1)]
        qn = qs * _half_rms_scale(qs, low) * gq_ref[...]
        kn = ks * _half_rms_scale(ks, low) * gk_ref[...]
        for half in range(2):
            h = 2 * p + half
            qh = qn if half == 0 else pltpu.roll(qn, HEAD_DIM, axis=1)
            kh = kn if half == 0 else pltpu.roll(kn, HEAD_DIM, axis=1)
            ch = pltpu.roll(csel, HEAD_DIM - 3 * h, axis=1)
            q_ref[0, h] = jnp.where(low, qh, ones_aug).astype(q_ref.dtype)
            k_ref[0, h] = jnp.where(low, kh, jnp.where(lane < HEAD_DIM + 3, ch, 0.0)).astype(k_ref.dtype)
    v_ref[0] = z_ref[0, :, 2 * GROUP:3 * GROUP].astype(v_ref.dtype)


def _fox_prep(zfox, zsm, fb, gq, gk):
    bsz, s_len, _ = zfox.shape
    ts = min(TS_MIX, s_len)
    const = lambda shape: pl.BlockSpec(shape, lambda b, t: (0, 0))
    head_spec = pl.BlockSpec((1, N_HEADS, ts, LANES), lambda b, t: (b, 0, t, 0))
    return pl.pallas_call(
        _fox_prep_kernel,
        grid=(bsz, s_len // ts),
        in_specs=[pl.BlockSpec((1, ts, 3 * GROUP), lambda b, t: (b, t, 0)),
                  pl.BlockSpec((1, ts, LANES), lambda b, t: (b, t, 0)),
                  const((1, LANES)), const((1, LANES)), const((1, LANES))],
        out_specs=[head_spec, head_spec, pl.BlockSpec((1, ts, GROUP), lambda b, t: (b, t, 0))],
        out_shape=[jax.ShapeDtypeStruct((bsz, N_HEADS, s_len, LANES), _MXU),
                   jax.ShapeDtypeStruct((bsz, N_HEADS, s_len, LANES), _MXU),
                   jax.ShapeDtypeStruct((bsz, s_len, GROUP), _MXU)],
        scratch_shapes=[pltpu.VMEM((8, LANES), _F32)],
        compiler_params=_params(("parallel", "arbitrary")),
        name="fox_prep",
    )(zfox, zsm, fb, gq, gk)


def _flash_step(s, v, carry):
    m, l, acc = carry
    m_new = jnp.maximum(m, jnp.max(s, axis=-1, keepdims=True))
    alpha = jnp.exp2(m - m_new)
    p = jnp.exp2(s - m_new)
    l = alpha * l + jnp.sum(p, axis=-1, keepdims=True)
    acc = alpha * acc + jnp.dot(p.astype(_MXU), v, preferred_element_type=_F32)
    return m_new, l, acc


def _flash_init(rows):
    return (jnp.full((rows, 1), NEG, _F32), jnp.zeros((rows, 1), _F32), jnp.zeros((rows, LANES), _F32))


def _fox_attn_kernel(q_ref, k_ref, v_ref, on_ref, out_ref):
    tq = q_ref.shape[2]
    qi = pl.program_id(1)
    qs = [q_ref[0, h] for h in range(N_HEADS)]

    def chunk(start, width, carries, causal=False):
        new = []
        for h in range(N_HEADS):
            s = _mm_nt(qs[h], k_ref[0, h, pl.ds(start, width), :])
            if causal:
                s = jnp.where(_lane((tq, width)) <= _row((tq, width)), s, NEG)
            pair = h // 2
            new.append(_flash_step(s, v_ref[0, pl.ds(start, width), LANES * pair:LANES * (pair + 1)], carries[h]))
        return tuple(new)

    per_wide = TK_FOX // tq
    n_wide = qi // per_wide
    carries = lax.fori_loop(0, n_wide, lambda c, cr: chunk(pl.multiple_of(c * TK_FOX, TK_FOX), TK_FOX, cr),
                            tuple(_flash_init(tq) for _ in range(N_HEADS)))
    carries = lax.fori_loop(n_wide * per_wide, qi, lambda c, cr: chunk(pl.multiple_of(c * tq, tq), tq, cr), carries)
    carries = chunk(pl.multiple_of(qi * tq, tq), tq, carries, causal=True)
    outs = [acc / l for _, l, acc in carries]
    low = _lane((tq, LANES)) < HEAD_DIM
    y = jnp.concatenate([jnp.where(low, outs[2 * p], outs[2 * p + 1]) for p in range(N_HEADS // 2)], axis=1)
    out_ref[0] = (y * _rms_scale(y) * on_ref[...]).astype(out_ref.dtype)


def _fox_attn(qf, kf, vf, on):
    bsz, _, s_len, _ = qf.shape
    tq = min(TQ_FOX, s_len)
    assert s_len % TK_FOX == 0 and TK_FOX % tq == 0
    return pl.pallas_call(
        _fox_attn_kernel,
        grid=(bsz, s_len // tq),
        in_specs=[pl.BlockSpec((1, N_HEADS, tq, LANES), lambda b, i: (b, 0, i, 0)),
                  pl.BlockSpec((1, N_HEADS, s_len, LANES), lambda b, i: (b, 0, 0, 0)),
                  pl.BlockSpec((1, s_len, GROUP), lambda b, i: (b, 0, 0)),
                  pl.BlockSpec((1, GROUP), lambda b, i: (0, 0))],
        out_specs=pl.BlockSpec((1, tq, GROUP), lambda b, i: (b, i, 0)),
        out_shape=jax.ShapeDtypeStruct((bsz, s_len, GROUP), _MXU),
        compiler_params=_params(("parallel", "arbitrary")),
        name="fox_attn",
    )(qf, kf, vf, on)


def _nsa_prep_kernel(zq_ref, zsel_ref, zwin_ref, sm_ref, gq_ref, gs_ref, gw_ref, gb_ref,
                     q4_ref, ks_ref, vs_ref, kw_ref, vw_ref, gate_ref):
    ts = zq_ref.shape[1]
    ti = pl.program_id(1)
    shape = (ts, LANES)
    lane = _lane(shape)
    low = lane < HEAD_DIM
    for p in range(N_HEADS // 2):
        qs = zq_ref[0, :, LANES * p:LANES * (p + 1)]
        qn = qs * _half_rms_scale(qs, low) * gq_ref[...]
        q4_ref[0, 2 * p] = jnp.where(low, qn, 0.0).astype(q4_ref.dtype)
        q4_ref[0, 2 * p + 1] = jnp.where(low, pltpu.roll(qn, HEAD_DIM, axis=1), 0.0).astype(q4_ref.dtype)

    real = ti < pl.num_programs(1) - 1
    sel = jnp.where(real, zsel_ref[0], 0.0)
    blk = (ti * ts + _row(shape)) >> 6
    onehot = jnp.where(real, jnp.where(lane - HEAD_DIM == blk, 1.0, 0.0), 0.0)
    ks_ref[0] = jnp.where(low, sel * _low_half_rms_scale(sel, low) * gs_ref[...], onehot).astype(ks_ref.dtype)
    vs_ref[0] = sel.astype(vs_ref.dtype)
    win = zwin_ref[0]
    kw_ref[0] = jnp.where(low, win * _low_half_rms_scale(win, low) * gw_ref[...], 0.0).astype(kw_ref.dtype)
    vw_ref[0] = win.astype(vw_ref.dtype)
    gate_ref[0] = jax.nn.sigmoid(sm_ref[0] + gb_ref[...])


def _nsa_prep(zq, zsel, zwin, zsm, gq, gs, gw, gb):
    bsz, s_len, _ = zq.shape
    ts = NSA_PAD
    nt = s_len // ts
    const = lambda shape: pl.BlockSpec(shape, lambda b, t: (0, 0))
    last = lambda t: jnp.minimum(t, nt - 1)
    slab = pl.BlockSpec((1, ts, LANES), lambda b, t: (b, last(t), 0))
    padded = pl.BlockSpec((1, ts, LANES), lambda b, t: (b, t, 0))
    slab_shape = jax.ShapeDtypeStruct((bsz, s_len, LANES), _MXU)
    padded_shape = jax.ShapeDtypeStruct((bsz, s_len + NSA_PAD, LANES), _MXU)
    return pl.pallas_call(
        _nsa_prep_kernel,
        grid=(bsz, nt + 1),
        in_specs=[pl.BlockSpec((1, ts, GROUP), lambda b, t: (b, last(t), 0)), slab, slab, slab,
                  const((1, LANES)), const((1, LANES)), const((1, LANES)), const((1, LANES))],
        out_specs=[pl.BlockSpec((1, N_HEADS, ts, LANES), lambda b, t: (b, 0, last(t), 0)),
                   padded, padded, slab, slab, slab],
        out_shape=[jax.ShapeDtypeStruct((bsz, N_HEADS, s_len, LANES), _MXU),
                   padded_shape, padded_shape, slab_shape, slab_shape,
                   jax.ShapeDtypeStruct((bsz, s_len, LANES), _F32)],
        compiler_params=_params(("parallel", "arbitrary")),
        name="nsa_prep",
    )(zq, zsel, zwin, zsm, gq, gs, gw, gb)


def _compress_kernel(x_ref, pa_ref, pb_ref, w1a_ref, w1b_ref, w2_ref, g_ref, kc_ref, cv_ref):
    nc = x_ref.shape[1]
    x = x_ref[0]
    ua = _mm(x + pa_ref[...], w1a_ref[...])
    ub = _mm(x + pb_ref[...], w1b_ref[...])
    hid = _gelu(ua + pltpu.roll(ub, nc - 1, axis=0))
    out = _mm(hid, w2_ref[...])
    low = _lane(out.shape) < HEAD_DIM
    kc_ref[0] = jnp.where(low, out * _low_half_rms_scale(out, low) * g_ref[...], 0.0).astype(kc_ref.dtype)
    cv_ref[0] = out.astype(cv_ref.dtype)


def _compress(xc, pa, pb, w1a, w1b, w2, g):
    bsz, nc, width = xc.shape
    const = lambda shape: pl.BlockSpec(shape, lambda b: (0, 0))
    slab = pl.BlockSpec((1, nc, LANES), lambda b: (b, 0, 0))
    return pl.pallas_call(
        _compress_kernel,
        grid=(bsz,),
        in_specs=[pl.BlockSpec((1, nc, width), lambda b: (b, 0, 0)),
                  const((1, width)), const((1, width)), const((width, 2 * CMP_HIDDEN)),
                  const((width, 2 * CMP_HIDDEN)), const((2 * CMP_HIDDEN, LANES)), const((1, LANES))],
        out_specs=[slab, slab],
        out_shape=[jax.ShapeDtypeStruct((bsz, nc, LANES), _MXU)] * 2,
        compiler_params=_params(("parallel",)),
        name="compress",
    )(xc, pa, pb, w1a, w1b, w2, g)


def _bucket_thresholds():
    max_exact = REL_BUCKETS // 2
    d = np.arange(0, REL_MAX_DIST + 1)
    large = max_exact + (np.log(np.maximum(d, 1).astype(np.float32) / max_exact)
                         / math.log(REL_MAX_DIST / max_exact) * (REL_BUCKETS - max_exact)).astype(np.int32)
    bucket = np.where(d < max_exact, d, np.minimum(large, REL_BUCKETS - 1))
    assert bucket[-1] == REL_BUCKETS - 1 and np.all(np.diff(bucket) >= 0)
    return [int(np.argmax(bucket >= k)) for k in range(REL_BUCKETS)]


_BUCKET_THR = _bucket_thresholds()

NSA_R = TQ_NSA // KT_NSA
NSA_D = WINDOW // KT_NSA
NSA_TAIL = NSA_D + NSA_R
WIN_ENTRIES = NSA_TAIL + 1
SEL_ENTRIES = NSA_R + 3


def _win_entry(e):
    return jnp.minimum(e, NSA_TAIL)


def _sel_entry(e):
    return jnp.clip(e - (NSA_D - 2), 0, NSA_R + 2)


def _rel_bias(dist, rb_ref, h):
    far = rb_ref[REL_BUCKETS - 1, h]
    val = jnp.full(dist.shape, LOG2E * (rb_ref[0, h] - far), _F32)
    for k in range(1, REL_BUCKETS):
        val = jnp.where(dist >= _BUCKET_THR[k], LOG2E * (rb_ref[k, h] - far), val)
    return val


def _nsa_tables_kernel(rb_ref, win_ref, sel_ref, cmp_ref):
    qi = pl.program_id(0)
    tq = TQ_NSA
    nc = cmp_ref.shape[2]

    @pl.when(qi == 0)
    def _():
        i = _row((tq, KT_NSA))
        c = _lane((tq, KT_NSA))
        for h in range(N_HEADS):
            rs = slice(h * tq, (h + 1) * tq)
            for e in range(NSA_TAIL):
                dist = i + (NSA_D - e) * KT_NSA - c
                ok = jnp.where(dist >= 0, jnp.where(dist < WINDOW, 1, 0), 0) > 0
                tab = jnp.where(ok, _rel_bias(dist, rb_ref, h), NEG)
                win_ref[e, rs, :] = tab
                if e >= NSA_D - 1:
                    sel_ref[e - (NSA_D - 2), rs, :] = tab
            win_ref[NSA_TAIL, rs, :] = jnp.full((tq, KT_NSA), NEG, _F32)
            sel_ref[0, rs, :] = jnp.zeros((tq, KT_NSA), _F32)
            sel_ref[NSA_R + 2, rs, :] = jnp.full((tq, KT_NSA), NEG, _F32)

    t = qi * tq + _row((tq, nc))
    dist = t - (CMP_STRIDE * _lane((tq, nc)) + CMP_LEN - 1)
    for h in range(N_HEADS):
        cmp_ref[0, h * tq:(h + 1) * tq, :] = jnp.where(dist >= 0, _rel_bias(dist, rb_ref, h), NEG)


def _nsa_tables(rel_bias, s_len):
    nq = s_len // TQ_NSA
    nc = s_len // CMP_STRIDE
    rows = N_HEADS * TQ_NSA
    return pl.pallas_call(
        _nsa_tables_kernel,
        grid=(nq,),
        in_specs=[pl.BlockSpec(memory_space=pltpu.SMEM)],
        out_specs=[pl.BlockSpec((WIN_ENTRIES, rows, KT_NSA), lambda i: (0, 0, 0)),
                   pl.BlockSpec((SEL_ENTRIES, rows, KT_NSA), lambda i: (0, 0, 0)),
                   pl.BlockSpec((1, rows, nc), lambda i: (i, 0, 0))],
        out_shape=[jax.ShapeDtypeStruct((WIN_ENTRIES, rows, KT_NSA), _F32),
                   jax.ShapeDtypeStruct((SEL_ENTRIES, rows, KT_NSA), _F32),
                   jax.ShapeDtypeStruct((nq, rows, nc), _F32)],
        compiler_params=_params(("arbitrary",)),
        name="nsa_tables",
    )(rel_bias)


def _nsa_attn_kernel(q4_ref, kc_ref, cv_ref, ks_ref, vs_ref, kw_ref, vw_ref, gate_ref,
                     win_ref, sel_ref, cmp_ref, ovl_ref, on_ref, out_ref):
    tq = TQ_NSA
    kt = KT_NSA
    rows = N_HEADS * tq
    qi = pl.program_id(1)
    first = qi * NSA_R
    q4 = q4_ref[0].reshape(rows, LANES)
    lane = _lane((tq, LANES))
    low = lane < HEAD_DIM
    tail_w = NSA_TAIL * kt

    def tail_table(tab_ref, entry_of, first_tile):
        e0 = first_tile - first + NSA_D
        return jnp.concatenate([tab_ref[entry_of(e0 + u)] for u in range(NSA_TAIL)], axis=1)

    w_tile = jnp.maximum(first - NSA_D, 0)
    w_start = pl.multiple_of(w_tile * kt, kt)
    s = _mm_nt(q4, kw_ref[0, pl.ds(w_start, tail_w), :]) + tail_table(win_ref, _win_entry, w_tile)
    _, l_w, acc_w = _flash_step(s, vw_ref[0, pl.ds(w_start, tail_w), :], _flash_init(rows))
    o_w = acc_w / l_w

    s = _mm_nt(q4, kc_ref[0]) + cmp_ref[0]
    valid = s > 0.5 * NEG
    m = jnp.max(s, axis=-1, keepdims=True)
    p = jnp.where(valid, jnp.exp2(s - m), 0.0)
    l = jnp.sum(p, axis=-1, keepdims=True)
    p = p / jnp.where(l > 0.0, l, 1.0)
    o_c = jnp.dot(p.astype(_MXU), cv_ref[0], preferred_element_type=_F32)

    psum = p[0:tq] + p[tq:2 * tq] + p[2 * tq:3 * tq] + p[3 * tq:4 * tq]
    p_hi = _round_mxu(psum)
    imp = _mm(p_hi, ovl_ref[...]) + _mm(psum - p_hi, ovl_ref[...])

    n_blk = SLC_BLOCK
    imp_t = imp.T[0:n_blk, :]
    blk = _row((n_blk, tq))
    cur = (qi * tq + _lane((n_blk, tq))) >> 6
    forced = jnp.where(blk == 0, 1, jnp.where(blk == cur, 1, jnp.where(blk == cur - 1, 1, 0))) > 0
    val = jnp.where(blk <= cur, jnp.where(forced, BIG, imp_t), NEG)
    rank = jnp.zeros((n_blk, tq), _F32)
    for j in range(n_blk):
        vj = val[j:j + 1, :]
        rank = rank + jnp.where(vj > val, 1.0, jnp.where(vj == val, jnp.where(blk > j, 1.0, 0.0), 0.0))
    pen_t = jnp.where(rank < float(SLC_TOPK), 0.0, NEG)
    pen = jnp.concatenate([jnp.zeros((LANES - n_blk, tq), _F32), pen_t], axis=0).T
    pen4 = jnp.concatenate([pen] * N_HEADS, axis=0)
    q_aug = jnp.where(_lane((rows, LANES)) < HEAD_DIM, q4.astype(_F32), pen4).astype(_MXU)

    def sel_far(width):
        def step(c, carry):
            start = pl.multiple_of(c * width, width)
            s = _mm_nt(q_aug, ks_ref[0, pl.ds(start, width), :])
            return _flash_step(s, vs_ref[0, pl.ds(start, width), :], carry)
        return step

    far_w = NSA_FAR * kt
    n_far = jnp.maximum(first - 1, 0) // NSA_FAR
    n_dbl = n_far // 2
    carry = lax.fori_loop(0, n_dbl, sel_far(2 * far_w), _flash_init(rows))
    carry = lax.fori_loop(2 * n_dbl, n_far, sel_far(far_w), carry)
    t_tile = n_far * NSA_FAR
    t_start = pl.multiple_of(t_tile * kt, kt)
    s = _mm_nt(q_aug, ks_ref[0, pl.ds(t_start, tail_w), :]) + tail_table(sel_ref, _sel_entry, t_tile)
    _, l_s, acc_s = _flash_step(s, vs_ref[0, pl.ds(t_start, tail_w), :], carry)
    o_s = acc_s / l_s

    g = gate_ref[0]
    heads = []
    for h in range(N_HEADS):
        rs = slice(h * tq, (h + 1) * tq)
        gc = g[:, SM_GATE_LANE + h:SM_GATE_LANE + h + 1]
        gs = g[:, SM_GATE_LANE + N_HEADS + h:SM_GATE_LANE + N_HEADS + h + 1]
        gw = g[:, SM_GATE_LANE + 2 * N_HEADS + h:SM_GATE_LANE + 2 * N_HEADS + h + 1]
        heads.append(gc * o_c[rs] + gs * o_s[rs] + gw * o_w[rs])
    slabs = [jnp.where(low, pltpu.roll(heads[2 * p], HEAD_DIM, axis=1), heads[2 * p + 1])
             for p in range(N_HEADS // 2)]
    y = jnp.concatenate(slabs, axis=1)
    out_ref[0] = (y * _rms_scale(y) * on_ref[...]).astype(out_ref.dtype)


def _nsa_attn(q4, kc, cv, ks, vs, kw, vw, gates, win_tab, sel_tab, cmp_tab, ovl, on):
    bsz, _, s_len, _ = q4.shape
    tq = TQ_NSA
    nc = kc.shape[1]
    rows = N_HEADS * tq
    assert ks.shape[1] == s_len + NSA_PAD and NSA_PAD >= NSA_D * KT_NSA and s_len >= NSA_TAIL * KT_NSA
    full = lambda n: pl.BlockSpec((1, n, LANES), lambda b, i: (b, 0, 0))
    return pl.pallas_call(
        _nsa_attn_kernel,
        grid=(bsz, s_len // tq),
        in_specs=[pl.BlockSpec((1, N_HEADS, tq, LANES), lambda b, i: (b, 0, i, 0)),
                  full(nc), full(nc), full(s_len + NSA_PAD), full(s_len + NSA_PAD), full(s_len), full(s_len),
                  pl.BlockSpec((1, tq, LANES), lambda b, i: (b, i, 0)),
                  pl.BlockSpec((WIN_ENTRIES, rows, KT_NSA), lambda b, i: (0, 0, 0)),
                  pl.BlockSpec((SEL_ENTRIES, rows, KT_NSA), lambda b, i: (0, 0, 0)),
                  pl.BlockSpec((1, rows, nc), lambda b, i: (i, 0, 0)),
                  pl.BlockSpec((nc, LANES), lambda b, i: (0, 0)),
                  pl.BlockSpec((1, GROUP), lambda b, i: (0, 0))],
        out_specs=pl.BlockSpec((1, tq, GROUP), lambda b, i: (b, i, 0)),
        out_shape=jax.ShapeDtypeStruct((bsz, s_len, GROUP), _MXU),
        compiler_params=_params(("parallel", "arbitrary")),
        name="nsa_attn",
    )(q4, kc, cv, ks, vs, kw, vw, gates, win_tab, sel_tab, cmp_tab, ovl, on)


def _ffn_kernel(x_ref, mab_ref, mc_ref, md_ref, wo_ref, g_ref, wg_ref, wu_ref, wd_ref, out_ref,
                x1_ref, xn_ref, acc_ref):
    j = pl.program_id(1)

    @pl.when(j == 0)
    def _():
        x1 = (x_ref[...]
              + jnp.dot(mab_ref[...], wo_ref[0:2 * GROUP, :], preferred_element_type=_F32)
              + jnp.dot(mc_ref[...], wo_ref[2 * GROUP:3 * GROUP, :], preferred_element_type=_F32)
              + jnp.dot(md_ref[...], wo_ref[3 * GROUP:4 * GROUP, :], preferred_element_type=_F32))
        x1_ref[...] = x1
        xn_ref[...] = (x1 * _rms_scale(x1) * g_ref[...]).astype(xn_ref.dtype)
        acc_ref[...] = jnp.zeros_like(acc_ref)

    xn = xn_ref[...]
    gt = jnp.dot(xn, wg_ref[...], preferred_element_type=_F32)
    up = jnp.dot(xn, wu_ref[...], preferred_element_type=_F32)
    hid = (gt * jax.nn.sigmoid(gt)) * up
    acc_ref[...] += jnp.dot(hid.astype(_MXU), wd_ref[...], preferred_element_type=_F32)

    @pl.when(j == pl.num_programs(1) - 1)
    def _():
        out_ref[...] = x1_ref[...] + acc_ref[...]


def _out_ffn(x2d, mab, mc, md, wo, g, wgu, wd):
    t = x2d.shape[0]
    tm = min(TM_FFN, t)
    nf = D_FF // TF_FFN
    assert t % tm == 0 and D_FF % TF_FFN == 0
    row = lambda width: pl.BlockSpec((tm, width), lambda i, j: (i, 0))
    return pl.pallas_call(
        _ffn_kernel,
        grid=(t // tm, nf),
        in_specs=[row(D_MODEL), row(2 * GROUP), row(GROUP), row(GROUP),
                  pl.BlockSpec((D_MODEL, D_MODEL), lambda i, j: (0, 0)),
                  pl.BlockSpec((1, D_MODEL), lambda i, j: (0, 0)),
                  pl.BlockSpec((D_MODEL, TF_FFN), lambda i, j: (0, j)),
                  pl.BlockSpec((D_MODEL, TF_FFN), lambda i, j: (0, j + nf)),
                  pl.BlockSpec((TF_FFN, D_MODEL), lambda i, j: (j, 0))],
        out_specs=row(D_MODEL),
        out_shape=jax.ShapeDtypeStruct((t, D_MODEL), _F32),
        scratch_shapes=[pltpu.VMEM((tm, D_MODEL), _F32), pltpu.VMEM((tm, D_MODEL), _MXU),
                        pltpu.VMEM((tm, D_MODEL), _F32)],
        compiler_params=_params(("parallel", "arbitrary")),
        name="out_ffn",
    )(x2d, mab, mc, md, wo, g, wgu, wgu, wd)


def _lane_vec(values, start):
    v = jnp.zeros((LANES,), _F32).at[start:start + values.shape[0]].set(values.astype(_F32))
    return v[None, :]


def _block_diag(w):
    h, d, _ = w.shape
    eye = jnp.eye(h, dtype=w.dtype)
    return (eye[:, None, :, None] * w[:, :, None, :]).reshape(h * d, h * d)


def _overlap_ext(s_len):
    nc = s_len // CMP_STRIDE
    n_cmp = nc - 1
    n_slc = s_len // SLC_BLOCK
    cs = np.arange(n_cmp)[:, None] * CMP_STRIDE
    ss = np.arange(n_slc)[None, :] * SLC_BLOCK
    ov = np.clip(np.minimum(cs + CMP_LEN, ss + SLC_BLOCK) - np.maximum(cs, ss), 0, CMP_LEN)
    ext = np.zeros((nc, LANES), np.float32)
    ext[:n_cmp, :n_slc] = ov
    return jnp.asarray(ext, _MXU)


def _layer_params(l, w_in, lru_w_gates, lru_b_gates, fox_f_bias, fox_qk_norm, nsa_qk_norm, nsa_cmp_pos,
                  nsa_cmp_w1, nsa_cmp_w2, nsa_gate_bias, out_norm):
    scale = HEAD_DIM ** -0.5 * LOG2E
    w = w_in[l]
    small = jnp.zeros((D_MODEL, LANES), w.dtype)
    small = small.at[:, 0:3 * N_HEADS].set(jnp.repeat(w[:, C_FOX_F:C_FOX_F + N_HEADS], 3, axis=1))
    small = small.at[:, SM_GATE_LANE:SM_GATE_LANE + 3 * N_HEADS].set(w[:, C_NSA_G:C_NSA_G + 3 * N_HEADS])
    w_perm = jnp.concatenate([w[:, :C_FOX_F], w[:, C_NSA_Q:C_NSA_G], small], axis=1).astype(_MXU)

    wg = jnp.concatenate([_block_diag(lru_w_gates[l, 0]), _block_diag(lru_w_gates[l, 1])], axis=1).astype(_MXU)
    bg = lru_b_gates[l].reshape(1, 2 * GROUP)

    fb = _lane_vec(jnp.repeat(fox_f_bias[l], 3), 0)
    fox_gq = (jnp.tile(fox_qk_norm[l, 0], 2) * scale)[None, :]
    fox_gk = jnp.tile(fox_qk_norm[l, 1], 2)[None, :]

    nsa_gq = (jnp.tile(nsa_qk_norm[l, 0], 2) * scale)[None, :]
    nsa_gc = _lane_vec(nsa_qk_norm[l, 1], 0)
    nsa_gs = _lane_vec(nsa_qk_norm[l, 2], 0)
    nsa_gw = _lane_vec(nsa_qk_norm[l, 3], 0)
    gb = _lane_vec(nsa_gate_bias[l], SM_GATE_LANE)

    half = CMP_LEN // 2

    def pos_ext(lo):
        return jnp.concatenate([nsa_cmp_pos[l, 0, lo:lo + half], nsa_cmp_pos[l, 1, lo:lo + half]],
                               axis=1).reshape(1, half * LANES)

    def w1_ext(lo):
        ext = jnp.zeros((half, LANES, 2 * CMP_HIDDEN), _F32)
        ext = ext.at[:, :HEAD_DIM, :CMP_HIDDEN].set(nsa_cmp_w1[l, 0, lo:lo + half])
        ext = ext.at[:, HEAD_DIM:, CMP_HIDDEN:].set(nsa_cmp_w1[l, 1, lo:lo + half])
        return ext.reshape(half * LANES, 2 * CMP_HIDDEN).astype(_MXU)

    w2 = jnp.zeros((2 * CMP_HIDDEN, LANES), _F32)
    w2 = w2.at[:CMP_HIDDEN, :HEAD_DIM].set(nsa_cmp_w2[l, 0]).at[CMP_HIDDEN:, HEAD_DIM:].set(nsa_cmp_w2[l, 1])
    return dict(w_perm=w_perm, wg=wg, bg=bg, fb=fb, fox_gq=fox_gq, fox_gk=fox_gk, nsa_gq=nsa_gq, nsa_gc=nsa_gc,
                nsa_gs=nsa_gs, nsa_gw=nsa_gw, gb=gb, pa=pos_ext(0), pb=pos_ext(half), w1a=w1_ext(0),
                w1b=w1_ext(half), w2=w2.astype(_MXU), on=out_norm[l].reshape(1, 4 * GROUP))


def kernel(x, norm_mix, w_in, lru_conv_w, lru_conv_b, lru_w_gates, lru_b_gates, lru_lambda, sc_conv_w, fox_f_bias, fox_qk_norm, nsa_qk_norm, nsa_cmp_pos, nsa_cmp_w1, nsa_cmp_w2, nsa_gate_bias, rel_bias, out_norm, w_out, norm_ffn, w_gate_up, w_down):
    bsz, s_len, d_model = x.shape
    depth = w_in.shape[0]
    assert d_model == D_MODEL and s_len % TS_MIX == 0 and s_len % TQ_FOX == 0
    assert SLC_TOPK <= s_len // SLC_BLOCK <= SLC_BLOCK
    t = bsz * s_len
    nc = s_len // CMP_STRIDE

    win_tab, sel_tab, cmp_tab = _nsa_tables(rel_bias, s_len)
    ovl = _overlap_ext(s_len)

    x2d = x.reshape(t, D_MODEL)
    for l in range(depth):
        lp = _layer_params(l, w_in, lru_w_gates, lru_b_gates, fox_f_bias, fox_qk_norm, nsa_qk_norm,
                           nsa_cmp_pos, nsa_cmp_w1, nsa_cmp_w2, nsa_gate_bias, out_norm)
        zab, zfox, zq, zcmp, zsel, zwin, zsm = _in_proj(x2d, norm_mix[l][None, :], lp["w_perm"])
        r3 = lambda a: a.reshape(bsz, s_len, a.shape[-1])
        zsm3 = r3(zsm)

        m_ab = _mix_ab(r3(zab), lru_conv_w[l], lru_conv_b[l][None, :], lp["wg"], lp["bg"],
                       lru_lambda[l][None, :], sc_conv_w[l], lp["on"][:, 0:2 * GROUP])

        qf, kf, vf = _fox_prep(r3(zfox), zsm3, lp["fb"], lp["fox_gq"], lp["fox_gk"])
        m_c = _fox_attn(qf, kf, vf, lp["on"][:, 2 * GROUP:3 * GROUP])

        q4, ks, vs, kw, vw, gates = _nsa_prep(r3(zq), r3(zsel), r3(zwin), zsm3, lp["nsa_gq"], lp["nsa_gs"],
                                              lp["nsa_gw"], lp["gb"])
        kc, cv = _compress(zcmp.reshape(bsz, nc, CMP_STRIDE * LANES), lp["pa"], lp["pb"], lp["w1a"],
                           lp["w1b"], lp["w2"], lp["nsa_gc"])
        m_d = _nsa_attn(q4, kc, cv, ks, vs, kw, vw, gates, win_tab, sel_tab, cmp_tab, ovl,
                        lp["on"][:, 3 * GROUP:4 * GROUP])

        x2d = _out_ffn(x2d, m_ab.reshape(t, 2 * GROUP), m_c.reshape(t, GROUP), m_d.reshape(t, GROUP),
                       w_out[l].astype(_MXU), norm_ffn[l][None, :], w_gate_up[l].astype(_MXU),
                       w_down[l].astype(_MXU))
    return x2d.reshape(bsz, s_len, D_MODEL)
```

```python
import math

import numpy as np
import jax
import jax.numpy as jnp
from jax import lax
from jax.experimental import pallas as pl
from jax.experimental.pallas import tpu as pltpu

D_MODEL = 1024
GROUP = 256
HEAD_DIM = 64
N_HEADS = 4
LRU_CONV = 4
LRU_C = 8.0
SC_CONV = 3
CMP_LEN = 32
CMP_STRIDE = 16
CMP_HIDDEN = 128
SLC_BLOCK = 64
SLC_TOPK = 16
WINDOW = 512
REL_BUCKETS = 32
REL_MAX_DIST = 128
D_FF = 2816
RMS_EPS = 1e-6
NEG = -1e30
BIG = 1e30
LOG2E = math.log2(math.e)

C_FOX_F = 8 * GROUP
C_NSA_Q = C_FOX_F + N_HEADS
C_NSA_G = C_NSA_Q + GROUP + 6 * HEAD_DIM

LANES = 128
SUBLANES = 8
VMEM_LIMIT_BYTES = 56 * 1024 * 1024

TS_MIX = 512
TQ_FOX = 512
TK_FOX = 1024
KT_NSA = 128
TQ_NSA = 256
NSA_FAR = 4
TM_FFN = 512
TF_FFN = 256

SM_GATE_LANE = 16

COL_FOX = 5 * GROUP
COL_NSA_Q = COL_FOX + 3 * GROUP
COL_SLABS = COL_NSA_Q + GROUP
N_IN_PAD = COL_SLABS + 4 * LANES

_MXU = jnp.bfloat16
_F32 = jnp.float32


def _params(sem):
    return pltpu.CompilerParams(dimension_semantics=sem, vmem_limit_bytes=VMEM_LIMIT_BYTES)


def _resident(shape):
    zeros = (0,) * len(shape)
    return pl.BlockSpec(shape, lambda *_: zeros, pipeline_mode=pl.Buffered(1))


def _mm(a, b):
    return jnp.dot(a.astype(_MXU), b.astype(_MXU), preferred_element_type=_F32)


def _mm_nt(a, b):
    return lax.dot_general(a.astype(_MXU), b.astype(_MXU), (((1,), (1,)), ((), ())),
                           preferred_element_type=_F32)


def _lane(shape):
    return lax.broadcasted_iota(jnp.int32, shape, len(shape) - 1)


def _row(shape):
    return lax.broadcasted_iota(jnp.int32, shape, 0)


def _rms_scale(x):
    return lax.rsqrt(jnp.mean(x * x, axis=-1, keepdims=True) + RMS_EPS)


def _gelu(x):
    c = math.sqrt(2.0 / math.pi)
    return x * (0.5 * (1.0 + jnp.tanh(c * (x + 0.044715 * (x * x * x)))))


def _round_mxu(x):
    return x.astype(_MXU).astype(_F32)


def _low_half_rms_scale(x, low):
    s = jnp.sum(jnp.where(low, x * x, 0.0), axis=-1, keepdims=True)
    return lax.rsqrt(s * (1.0 / HEAD_DIM) + RMS_EPS)


def _half_rms_scale(x, low):
    x2 = x * x
    s_lo = jnp.sum(jnp.where(low, x2, 0.0), axis=-1, keepdims=True)
    s_hi = jnp.sum(jnp.where(low, 0.0, x2), axis=-1, keepdims=True)
    return jnp.where(low, lax.rsqrt(s_lo * (1.0 / HEAD_DIM) + RMS_EPS),
                     lax.rsqrt(s_hi * (1.0 / HEAD_DIM) + RMS_EPS))


def _linear_scan_rows(a, b):
    n = a.shape[0]
    row = _row(a.shape)
    d = 1
    while d < n:
        keep = row >= d
        a_prev = jnp.where(keep, pltpu.roll(a, d, axis=0), 1.0)
        b_prev = jnp.where(keep, pltpu.roll(b, d, axis=0), 0.0)
        b = a * b_prev + b
        a = a * a_prev
        d *= 2
    return a, b


def _cumsum_rows(x):
    n = x.shape[0]
    row = _row(x.shape)
    d = 1
    while d < n:
        x = x + jnp.where(row >= d, pltpu.roll(x, d, axis=0), 0.0)
        d *= 2
    return x


def _mix_ab(zab, cw_ref, cb_ref, wg_ref, bg_ref, lam_ref, scw_ref, on_ref, out_ref, xext_ref, uext_ref, h_ref):
    ts = zab.shape[0]
    xr = zab[:, 0:GROUP]
    gate = zab[:, GROUP:2 * GROUP]
    bgt = zab[:, 2 * GROUP:3 * GROUP]
    cgt = zab[:, 3 * GROUP:4 * GROUP]
    xs = zab[:, 4 * GROUP:5 * GROUP]

    xext_ref[SUBLANES:SUBLANES + ts, :] = xr
    xc = cb_ref[...] + cw_ref[LRU_CONV - 1:LRU_CONV, :] * xr
    for k in range(LRU_CONV - 1):
        xc = xc + cw_ref[k:k + 1, :] * xext_ref[pl.ds(SUBLANES - (LRU_CONV - 1) + k, ts), :]
    xext_ref[0:SUBLANES, :] = xr[ts - SUBLANES:ts, :]

    gi = _mm(xc, wg_ref[...]) + bg_ref[...]
    r = jax.nn.sigmoid(gi[:, 0:GROUP])
    ig = jax.nn.sigmoid(gi[:, GROUP:2 * GROUP])
    lam = lam_ref[...]
    softplus_neg = jnp.maximum(-lam, 0.0) + jnp.log1p(jnp.exp(-jnp.abs(lam)))
    log_a = (-LRU_C * softplus_neg) * r
    a = jnp.exp(log_a)
    b = jnp.sqrt(-jnp.tanh(log_a) * (a * a + 1.0)) * (ig * xc)
    a_cum, h = _linear_scan_rows(a, b)
    h = h + a_cum * h_ref[0:1, :]
    h_ref[...] = jnp.broadcast_to(h[ts - 1:ts, :], h_ref.shape)
    y_a = h * _gelu(gate)

    u = cgt * xs
    uext_ref[SUBLANES:SUBLANES + ts, :] = u
    cv = scw_ref[SC_CONV - 1:SC_CONV, :] * u
    for k in range(SC_CONV - 1):
        cv = cv + scw_ref[k:k + 1, :] * uext_ref[pl.ds(SUBLANES - (SC_CONV - 1) + k, ts), :]
    uext_ref[0:SUBLANES, :] = u[ts - SUBLANES:ts, :]
    y_b = bgt * cv

    out_ref[0, :, 0:GROUP] = (y_a * _rms_scale(y_a) * on_ref[:, 0:GROUP]).astype(out_ref.dtype)
    out_ref[0, :, GROUP:2 * GROUP] = (y_b * _rms_scale(y_b) * on_ref[:, GROUP:2 * GROUP]).astype(out_ref.dtype)


def _fox_prep(zfox, zsm, fb_ref, gq_ref, gk_ref, q_ref, k_ref, v_ref, c_ref):
    ts = zfox.shape[0]
    shape = (ts, LANES)
    lane = _lane(shape)
    low = lane < HEAD_DIM

    f = zsm + fb_ref[...]
    logf = jnp.minimum(f, 0.0) - jnp.log1p(jnp.exp(-jnp.abs(f)))
    c = _cumsum_rows(logf) + c_ref[0:1, :]
    c_ref[...] = jnp.broadcast_to(c[ts - 1:ts, :], c_ref.shape)
    neg_c = -LOG2E * c
    p1 = _round_mxu(neg_c)
    r1 = neg_c - p1
    p2 = _round_mxu(r1)
    p3 = _round_mxu(r1 - p2)
    piece = lane - 3 * ((lane * 11) >> 5)
    csel = jnp.where(piece == 0, p1, jnp.where(piece == 1, p2, p3))

    ones_aug = jnp.where(lane < HEAD_DIM + 3, 1.0, 0.0)
    for p in range(N_HEADS // 2):
        qs = zfox[:, LANES * p:LANES * (p + 1)]
        ks = zfox[:, GROUP + LANES * p:GROUP + LANES * (p + 1)]
        qn = qs * _half_rms_scale(qs, low) * gq_ref[...]
        kn = ks * _half_rms_scale(ks, low) * gk_ref[...]
        for half in range(2):
            h = 2 * p + half
            qh = qn if half == 0 else pltpu.roll(qn, HEAD_DIM, axis=1)
            kh = kn if half == 0 else pltpu.roll(kn, HEAD_DIM, axis=1)
            ch = pltpu.roll(csel, HEAD_DIM - 3 * h, axis=1)
            q_ref[0, h] = jnp.where(low, qh, ones_aug).astype(q_ref.dtype)
            k_ref[0, h] = jnp.where(low, kh, jnp.where(lane < HEAD_DIM + 3, ch, 0.0)).astype(k_ref.dtype)
    v_ref[0] = zfox[:, 2 * GROUP:3 * GROUP].astype(v_ref.dtype)


def _nsa_prep(zq, zsel, zwin, zsm, row0, gq_ref, gs_ref, gw_ref, gb_ref,
              q4_ref, ks_ref, vs_ref, kw_ref, vw_ref, gate_ref):
    ts = zq.shape[0]
    shape = (ts, LANES)
    lane = _lane(shape)
    low = lane < HEAD_DIM
    for p in range(N_HEADS // 2):
        qs = zq[:, LANES * p:LANES * (p + 1)]
        qn = qs * _half_rms_scale(qs, low) * gq_ref[...]
        q4_ref[0, 2 * p] = jnp.where(low, qn, 0.0).astype(q4_ref.dtype)
        q4_ref[0, 2 * p + 1] = jnp.where(low, pltpu.roll(qn, HEAD_DIM, axis=1), 0.0).astype(q4_ref.dtype)

    blk = (row0 + _row(shape)) >> 6
    onehot = jnp.where(lane - HEAD_DIM == blk, 1.0, 0.0)
    ks_ref[0] = jnp.where(low, zsel * _low_half_rms_scale(zsel, low) * gs_ref[...], onehot).astype(ks_ref.dtype)
    vs_ref[0] = zsel.astype(vs_ref.dtype)
    kw_ref[0] = jnp.where(low, zwin * _low_half_rms_scale(zwin, low) * gw_ref[...], 0.0).astype(kw_ref.dtype)
    vw_ref[0] = zwin.astype(vw_ref.dtype)
    gate_ref[0] = jax.nn.sigmoid(zsm + gb_ref[...])


def _proj_mix_kernel(x_ref, g_ref, w_ref,
                     cw_ref, cb_ref, wg_ref, bg_ref, lam_ref, scw_ref, on_ref,
                     fb_ref, fgq_ref, fgk_ref, ngq_ref, ngs_ref, ngw_ref, gb_ref,
                     mab_ref, qf_ref, kf_ref, vf_ref, q4_ref, ks_ref, vs_ref, kw_ref, vw_ref, gate_ref, zcmp_ref,
                     xext_ref, uext_ref, h_ref, c_ref):
    ts = x_ref.shape[1]
    ti = pl.program_id(1)

    @pl.when(ti == 0)
    def _():
        xext_ref[0:SUBLANES, :] = jnp.zeros((SUBLANES, GROUP), _F32)
        uext_ref[0:SUBLANES, :] = jnp.zeros((SUBLANES, GROUP), _F32)
        h_ref[...] = jnp.zeros_like(h_ref)
        c_ref[...] = jnp.zeros_like(c_ref)

    x = x_ref[0]
    xn = (x * _rms_scale(x) * g_ref[...]).astype(_MXU)
    zab = jnp.dot(xn, w_ref[:, 0:COL_FOX], preferred_element_type=_F32)
    _mix_ab(zab, cw_ref, cb_ref, wg_ref, bg_ref, lam_ref, scw_ref, on_ref, mab_ref, xext_ref, uext_ref, h_ref)
    slabs = jnp.dot(xn, w_ref[:, COL_SLABS:N_IN_PAD], preferred_element_type=_F32)
    zcmp_ref[0] = slabs[:, 0:LANES]
    zsel = slabs[:, LANES:2 * LANES]
    zwin = slabs[:, 2 * LANES:3 * LANES]
    zsm = slabs[:, 3 * LANES:4 * LANES]
    zfox = jnp.dot(xn, w_ref[:, COL_FOX:COL_NSA_Q], preferred_element_type=_F32)
    _fox_prep(zfox, zsm, fb_ref, fgq_ref, fgk_ref, qf_ref, kf_ref, vf_ref, c_ref)
    zq = jnp.dot(xn, w_ref[:, COL_NSA_Q:COL_SLABS], preferred_element_type=_F32)
    _nsa_prep(zq, zsel, zwin, zsm, ti * ts, ngq_ref, ngs_ref, ngw_ref, gb_ref,
              q4_ref, ks_ref, vs_ref, kw_ref, vw_ref, gate_ref)


def _proj_mix(x, gain, w, cw, cb, wg, bg, lam, scw, on_ab, fb, fgq, fgk, ngq, ngs, ngw, gb):
    bsz, s_len, _ = x.shape
    ts = TS_MIX
    vec = lambda width: pl.BlockSpec((1, width), lambda b, t: (0, 0))
    rows = lambda width: pl.BlockSpec((1, ts, width), lambda b, t: (b, t, 0))
    heads = pl.BlockSpec((1, N_HEADS, ts, LANES), lambda b, t: (b, 0, t, 0))
    sds = jax.ShapeDtypeStruct
    return pl.pallas_call(
        _proj_mix_kernel,
        grid=(bsz, s_len // ts),
        in_specs=[rows(D_MODEL), vec(D_MODEL), _resident((D_MODEL, N_IN_PAD)),
                  pl.BlockSpec((LRU_CONV, GROUP), lambda b, t: (0, 0)), vec(GROUP), _resident((GROUP, 2 * GROUP)),
                  vec(2 * GROUP), vec(GROUP), pl.BlockSpec((SC_CONV, GROUP), lambda b, t: (0, 0)), vec(2 * GROUP),
                  vec(LANES), vec(LANES), vec(LANES), vec(LANES), vec(LANES), vec(LANES), vec(LANES)],
        out_specs=[rows(2 * GROUP), heads, heads, rows(GROUP), heads, rows(LANES), rows(LANES),
                   rows(LANES), rows(LANES), rows(LANES), rows(LANES)],
        out_shape=[sds((bsz, s_len, 2 * GROUP), _MXU),
                   sds((bsz, N_HEADS, s_len, LANES), _MXU), sds((bsz, N_HEADS, s_len, LANES), _MXU),
                   sds((bsz, s_len, GROUP), _MXU),
                   sds((bsz, N_HEADS, s_len, LANES), _MXU),
                   sds((bsz, s_len, LANES), _MXU), sds((bsz, s_len, LANES), _MXU),
                   sds((bsz, s_len, LANES), _MXU), sds((bsz, s_len, LANES), _MXU),
                   sds((bsz, s_len, LANES), _F32), sds((bsz, s_len, LANES), _F32)],
        scratch_shapes=[pltpu.VMEM((ts + SUBLANES, GROUP), _F32), pltpu.VMEM((ts + SUBLANES, GROUP), _F32),
                        pltpu.VMEM((SUBLANES, GROUP), _F32), pltpu.VMEM((SUBLANES, LANES), _F32)],
        compiler_params=_params(("parallel", "arbitrary")),
        name="proj_mix",
    )(x, gain, w, cw, cb, wg, bg, lam, scw, on_ab, fb, fgq, fgk, ngq, ngs, ngw, gb)


def _flash_step(s, v, carry):
    m, l, acc = carry
    m_new = jnp.maximum(m, jnp.max(s, axis=-1, keepdims=True))
    alpha = jnp.exp2(m - m_new)
    p = jnp.exp2(s - m_new)
    l = alpha * l + jnp.sum(p, axis=-1, keepdims=True)
    acc = alpha * acc + jnp.dot(p.astype(_MXU), v, preferred_element_type=_F32)
    return m_new, l, acc


def _flash_init(rows):
    return (jnp.full((rows, 1), NEG, _F32), jnp.zeros((rows, 1), _F32), jnp.zeros((rows, LANES), _F32))


def _fox_attn_kernel(q_ref, k_ref, v_ref, on_ref, out_ref):
    tq = q_ref.shape[2]
    qi = pl.program_id(1)
    qs = [q_ref[0, h] for h in range(N_HEADS)]

    def chunk(start, width, carries, causal=False):
        new = []
        for h in range(N_HEADS):
            s = _mm_nt(qs[h], k_ref[0, h, pl.ds(start, width), :])
            if causal:
                s = jnp.where(_lane((tq, width)) <= _row((tq, width)), s, NEG)
            pair = h // 2
            new.append(_flash_step(s, v_ref[0, pl.ds(start, width), LANES * pair:LANES * (pair + 1)], carries[h]))
        return tuple(new)

    per_wide = TK_FOX // tq
    n_wide = qi // per_wide
    carries = lax.fori_loop(0, n_wide, lambda c, cr: chunk(pl.multiple_of(c * TK_FOX, TK_FOX), TK_FOX, cr),
                            tuple(_flash_init(tq) for _ in range(N_HEADS)))
    carries = lax.fori_loop(n_wide * per_wide, qi, lambda c, cr: chunk(pl.multiple_of(c * tq, tq), tq, cr), carries)
    carries = chunk(pl.multiple_of(qi * tq, tq), tq, carries, causal=True)
    outs = [acc / l for _, l, acc in carries]
    low = _lane((tq, LANES)) < HEAD_DIM
    y = jnp.concatenate([jnp.where(low, outs[2 * p], outs[2 * p + 1]) for p in range(N_HEADS // 2)], axis=1)
    out_ref[0] = (y * _rms_scale(y) * on_ref[...]).astype(out_ref.dtype)


def _fox_attn(qf, kf, vf, on):
    bsz, _, s_len, _ = qf.shape
    tq = min(TQ_FOX, s_len)
    assert s_len % TK_FOX == 0 and TK_FOX % tq == 0
    return pl.pallas_call(
        _fox_attn_kernel,
        grid=(bsz, s_len // tq),
        in_specs=[pl.BlockSpec((1, N_HEADS, tq, LANES), lambda b, i: (b, 0, i, 0)),
                  pl.BlockSpec((1, N_HEADS, s_len, LANES), lambda b, i: (b, 0, 0, 0)),
                  pl.BlockSpec((1, s_len, GROUP), lambda b, i: (b, 0, 0)),
                  pl.BlockSpec((1, GROUP), lambda b, i: (0, 0))],
        out_specs=pl.BlockSpec((1, tq, GROUP), lambda b, i: (b, i, 0)),
        out_shape=jax.ShapeDtypeStruct((bsz, s_len, GROUP), _MXU),
        compiler_params=_params(("parallel", "arbitrary")),
        name="fox_attn",
    )(qf, kf, vf, on)


def _compress_kernel(x_ref, pa_ref, pb_ref, w1a_ref, w1b_ref, w2_ref, g_ref, kc_ref, cv_ref):
    nc = x_ref.shape[1]
    x = x_ref[0]
    ua = _mm(x + pa_ref[...], w1a_ref[...])
    ub = _mm(x + pb_ref[...], w1b_ref[...])
    hid = _gelu(ua + pltpu.roll(ub, nc - 1, axis=0))
    out = _mm(hid, w2_ref[...])
    low = _lane(out.shape) < HEAD_DIM
    kc_ref[0] = jnp.where(low, out * _low_half_rms_scale(out, low) * g_ref[...], 0.0).astype(kc_ref.dtype)
    cv_ref[0] = out.astype(cv_ref.dtype)


def _compress(xc, pa, pb, w1a, w1b, w2, g):
    bsz, nc, width = xc.shape
    const = lambda shape: pl.BlockSpec(shape, lambda b: (0, 0))
    slab = pl.BlockSpec((1, nc, LANES), lambda b: (b, 0, 0))
    return pl.pallas_call(
        _compress_kernel,
        grid=(bsz,),
        in_specs=[pl.BlockSpec((1, nc, width), lambda b: (b, 0, 0)),
                  const((1, width)), const((1, width)), const((width, 2 * CMP_HIDDEN)),
                  const((width, 2 * CMP_HIDDEN)), const((2 * CMP_HIDDEN, LANES)), const((1, LANES))],
        out_specs=[slab, slab],
        out_shape=[jax.ShapeDtypeStruct((bsz, nc, LANES), _MXU)] * 2,
        compiler_params=_params(("parallel",)),
        name="compress",
    )(xc, pa, pb, w1a, w1b, w2, g)


def _bucket_thresholds():
    max_exact = REL_BUCKETS // 2
    d = np.arange(0, REL_MAX_DIST + 1)
    large = max_exact + (np.log(np.maximum(d, 1).astype(np.float32) / max_exact)
                         / math.log(REL_MAX_DIST / max_exact) * (REL_BUCKETS - max_exact)).astype(np.int32)
    bucket = np.where(d < max_exact, d, np.minimum(large, REL_BUCKETS - 1))
    assert bucket[-1] == REL_BUCKETS - 1 and np.all(np.diff(bucket) >= 0)
    return [int(np.argmax(bucket >= k)) for k in range(REL_BUCKETS)]


_BUCKET_THR = _bucket_thresholds()

NSA_R = TQ_NSA // KT_NSA
NSA_D = WINDOW // KT_NSA
NSA_TAIL = NSA_D + NSA_R
WIN_ENTRIES = NSA_TAIL + 1
SEL_ENTRIES = NSA_R + 3


def _win_entry(e):
    return jnp.minimum(e, NSA_TAIL)


def _sel_entry(e):
    return jnp.clip(e - (NSA_D - 2), 0, NSA_R + 2)


def _rel_bias(dist, rb_ref, h):
    far = rb_ref[REL_BUCKETS - 1, h]
    val = jnp.full(dist.shape, LOG2E * (rb_ref[0, h] - far), _F32)
    for k in range(1, REL_BUCKETS):
        val = jnp.where(dist >= _BUCKET_THR[k], LOG2E * (rb_ref[k, h] - far), val)
    return val


def _nsa_tables_kernel(rb_ref, win_ref, sel_ref, cmp_ref):
    qi = pl.program_id(0)
    tq = TQ_NSA
    nc = cmp_ref.shape[2]

    @pl.when(qi == 0)
    def _():
        i = _row((tq, KT_NSA))
        c = _lane((tq, KT_NSA))
        for h in range(N_HEADS):
            rs = slice(h * tq, (h + 1) * tq)
            for e in range(NSA_TAIL):
                dist = i + (NSA_D - e) * KT_NSA - c
                ok = jnp.where(dist >= 0, jnp.where(dist < WINDOW, 1, 0), 0) > 0
                tab = jnp.where(ok, _rel_bias(dist, rb_ref, h), NEG)
                win_ref[e, rs, :] = tab
                if e >= NSA_D - 1:
                    sel_ref[e - (NSA_D - 2), rs, :] = tab
            win_ref[NSA_TAIL, rs, :] = jnp.full((tq, KT_NSA), NEG, _F32)
            sel_ref[0, rs, :] = jnp.zeros((tq, KT_NSA), _F32)
            sel_ref[NSA_R + 2, rs, :] = jnp.full((tq, KT_NSA), NEG, _F32)

    t = qi * tq + _row((tq, nc))
    dist = t - (CMP_STRIDE * _lane((tq, nc)) + CMP_LEN - 1)
    for h in range(N_HEADS):
        cmp_ref[0, h * tq:(h + 1) * tq, :] = jnp.where(dist >= 0, _rel_bias(dist, rb_ref, h), NEG)


def _nsa_tables(rel_bias, s_len):
    nq = s_len // TQ_NSA
    nc = s_len // CMP_STRIDE
    rows = N_HEADS * TQ_NSA
    return pl.pallas_call(
        _nsa_tables_kernel,
        grid=(nq,),
        in_specs=[pl.BlockSpec(memory_space=pltpu.SMEM)],
        out_specs=[pl.BlockSpec((WIN_ENTRIES, rows, KT_NSA), lambda i: (0, 0, 0)),
                   pl.BlockSpec((SEL_ENTRIES, rows, KT_NSA), lambda i: (0, 0, 0)),
                   pl.BlockSpec((1, rows, nc), lambda i: (i, 0, 0))],
        out_shape=[jax.ShapeDtypeStruct((WIN_ENTRIES, rows, KT_NSA), _F32),
                   jax.ShapeDtypeStruct((SEL_ENTRIES, rows, KT_NSA), _F32),
                   jax.ShapeDtypeStruct((nq, rows, nc), _F32)],
        compiler_params=_params(("arbitrary",)),
        name="nsa_tables",
    )(rel_bias)


def _topk_penalty(imp_t, qi, tq):
    n_blk = imp_t.shape[0]
    blk = _row((n_blk, tq))
    cur = (qi * tq + _lane((n_blk, tq))) >> 6
    forced = jnp.where(blk == 0, 1, jnp.where(blk == cur, 1, jnp.where(blk == cur - 1, 1, 0))) > 0
    val = jnp.where(blk <= cur, jnp.where(forced, BIG, imp_t), NEG)
    groups = [val[g * SUBLANES:(g + 1) * SUBLANES, :] for g in range(n_blk // SUBLANES)]
    sub = _row((SUBLANES, tq))
    ranks = [jnp.zeros((SUBLANES, tq), _F32) for _ in groups]
    for j in range(n_blk):
        vj = val[j:j + 1, :]
        jg, jr = divmod(j, SUBLANES)
        for g, vg in enumerate(groups):
            ge = jnp.where(vj >= vg, 1.0, 0.0)
            gt = jnp.where(vj > vg, 1.0, 0.0)
            if g > jg:
                ranks[g] = ranks[g] + ge
            elif g < jg:
                ranks[g] = ranks[g] + gt
            else:
                ranks[g] = ranks[g] + jnp.where(sub > jr, ge, gt)
    rank = jnp.concatenate(ranks, axis=0)
    return jnp.where(rank < float(SLC_TOPK), 0.0, NEG)


def _nsa_attn_kernel(q4_ref, kc_ref, cv_ref, ks_ref, vs_ref, kw_ref, vw_ref, gate_ref,
                     win_ref, sel_ref, cmp_ref, ovl_ref, on_ref, out_ref):
    tq = TQ_NSA
    kt = KT_NSA
    rows = N_HEADS * tq
    qi = pl.program_id(1)
    first = qi * NSA_R
    q4 = q4_ref[0].reshape(rows, LANES)
    lane = _lane((tq, LANES))
    low = lane < HEAD_DIM
    tail_w = NSA_TAIL * kt

    def tail_table(tab_ref, entry_of, first_tile, skip_before=0):
        pieces = []
        for u in range(NSA_TAIL):
            tile = first_tile + u
            entry = entry_of(tile - first + NSA_D)
            pieces.append(tab_ref[jnp.where(tile < skip_before, tab_ref.shape[0] - 1, entry)])
        return jnp.concatenate(pieces, axis=1)

    w_tile = jnp.maximum(first - NSA_D, 0)
    w_start = pl.multiple_of(w_tile * kt, kt)
    s = _mm_nt(q4, kw_ref[0, pl.ds(w_start, tail_w), :]) + tail_table(win_ref, _win_entry, w_tile)
    _, l_w, acc_w = _flash_step(s, vw_ref[0, pl.ds(w_start, tail_w), :], _flash_init(rows))
    o_w = acc_w / l_w

    s = _mm_nt(q4, kc_ref[0]) + cmp_ref[0]
    valid = s > 0.5 * NEG
    m = jnp.max(s, axis=-1, keepdims=True)
    p = jnp.where(valid, jnp.exp2(s - m), 0.0)
    l = jnp.sum(p, axis=-1, keepdims=True)
    p = p / jnp.where(l > 0.0, l, 1.0)
    o_c = jnp.dot(p.astype(_MXU), cv_ref[0], preferred_element_type=_F32)

    psum = p[0:tq] + p[tq:2 * tq] + p[2 * tq:3 * tq] + p[3 * tq:4 * tq]
    p_hi = _round_mxu(psum)
    imp = _mm(p_hi, ovl_ref[...]) + _mm(psum - p_hi, ovl_ref[...])

    pen_t = _topk_penalty(imp.T[0:SLC_BLOCK, :], qi, tq)
    pen = jnp.concatenate([jnp.zeros((LANES - SLC_BLOCK, tq), _F32), pen_t], axis=0).T
    pen4 = jnp.concatenate([pen] * N_HEADS, axis=0)
    q_aug = jnp.where(_lane((rows, LANES)) < HEAD_DIM, q4.astype(_F32), pen4).astype(_MXU)

    def sel_far(width):
        def step(c, carry):
            start = pl.multiple_of(c * width, width)
            s = _mm_nt(q_aug, ks_ref[0, pl.ds(start, width), :])
            return _flash_step(s, vs_ref[0, pl.ds(start, width), :], carry)
        return step

    far_w = NSA_FAR * kt
    n_far = jnp.maximum(first - 1, 0) // NSA_FAR
    n_dbl = n_far // 2
    carry = lax.fori_loop(0, n_dbl, sel_far(2 * far_w), _flash_init(rows))
    carry = lax.fori_loop(2 * n_dbl, n_far, sel_far(far_w), carry)
    done = n_far * NSA_FAR
    t_tile = jnp.minimum(done, ks_ref.shape[1] // kt - NSA_TAIL)
    t_start = pl.multiple_of(t_tile * kt, kt)
    s = _mm_nt(q_aug, ks_ref[0, pl.ds(t_start, tail_w), :]) + tail_table(sel_ref, _sel_entry, t_tile, done)
    _, l_s, acc_s = _flash_step(s, vs_ref[0, pl.ds(t_start, tail_w), :], carry)
    o_s = acc_s / l_s

    g = gate_ref[0]
    heads = []
    for h in range(N_HEADS):
        rs = slice(h * tq, (h + 1) * tq)
        gc = g[:, SM_GATE_LANE + h:SM_GATE_LANE + h + 1]
        gs = g[:, SM_GATE_LANE + N_HEADS + h:SM_GATE_LANE + N_HEADS + h + 1]
        gw = g[:, SM_GATE_LANE + 2 * N_HEADS + h:SM_GATE_LANE + 2 * N_HEADS + h + 1]
        heads.append(gc * o_c[rs] + gs * o_s[rs] + gw * o_w[rs])
    slabs = [jnp.where(low, pltpu.roll(heads[2 * p], HEAD_DIM, axis=1), heads[2 * p + 1])
             for p in range(N_HEADS // 2)]
    y = jnp.concatenate(slabs, axis=1)
    out_ref[0] = (y * _rms_scale(y) * on_ref[...]).astype(out_ref.dtype)


def _nsa_attn(q4, kc, cv, ks, vs, kw, vw, gates, win_tab, sel_tab, cmp_tab, ovl, on):
    bsz, _, s_len, _ = q4.shape
    tq = TQ_NSA
    nc = kc.shape[1]
    rows = N_HEADS * tq
    assert s_len >= NSA_TAIL * KT_NSA
    full = lambda n: pl.BlockSpec((1, n, LANES), lambda b, i: (b, 0, 0))
    return pl.pallas_call(
        _nsa_attn_kernel,
        grid=(bsz, s_len // tq),
        in_specs=[pl.BlockSpec((1, N_HEADS, tq, LANES), lambda b, i: (b, 0, i, 0)),
                  full(nc), full(nc), full(s_len), full(s_len), full(s_len), full(s_len),
                  pl.BlockSpec((1, tq, LANES), lambda b, i: (b, i, 0)),
                  _resident((WIN_ENTRIES, rows, KT_NSA)), _resident((SEL_ENTRIES, rows, KT_NSA)),
                  pl.BlockSpec((1, rows, nc), lambda b, i: (i, 0, 0)),
                  _resident((nc, LANES)),
                  pl.BlockSpec((1, GROUP), lambda b, i: (0, 0))],
        out_specs=pl.BlockSpec((1, tq, GROUP), lambda b, i: (b, i, 0)),
        out_shape=jax.ShapeDtypeStruct((bsz, s_len, GROUP), _MXU),
        compiler_params=_params(("parallel", "arbitrary")),
        name="nsa_attn",
    )(q4, kc, cv, ks, vs, kw, vw, gates, win_tab, sel_tab, cmp_tab, ovl, on)


def _ffn_kernel(x_ref, mab_ref, mc_ref, md_ref, wo_ref, g_ref, wgu_ref, wd_ref, out_ref):
    x1 = (x_ref[...]
          + jnp.dot(mab_ref[...], wo_ref[0:2 * GROUP, :], preferred_element_type=_F32)
          + jnp.dot(mc_ref[...], wo_ref[2 * GROUP:3 * GROUP, :], preferred_element_type=_F32)
          + jnp.dot(md_ref[...], wo_ref[3 * GROUP:4 * GROUP, :], preferred_element_type=_F32))
    xn = (x1 * _rms_scale(x1) * g_ref[...]).astype(_MXU)
    out_ref[...] = x1
    for j in range(D_FF // TF_FFN):
        cols = slice(j * TF_FFN, (j + 1) * TF_FFN)
        gt = jnp.dot(xn, wgu_ref[:, cols], preferred_element_type=_F32)
        up = jnp.dot(xn, wgu_ref[:, D_FF + j * TF_FFN:D_FF + (j + 1) * TF_FFN], preferred_element_type=_F32)
        hid = (gt * jax.nn.sigmoid(gt)) * up
        out_ref[...] += jnp.dot(hid.astype(_MXU), wd_ref[cols, :], preferred_element_type=_F32)


def _out_ffn(x2d, mab, mc, md, wo, g, wgu, wd):
    t = x2d.shape[0]
    tm = min(TM_FFN, t)
    assert t % tm == 0 and D_FF % TF_FFN == 0
    row = lambda width: pl.BlockSpec((tm, width), lambda i: (i, 0))
    return pl.pallas_call(
        _ffn_kernel,
        grid=(t // tm,),
        in_specs=[row(D_MODEL), row(2 * GROUP), row(GROUP), row(GROUP),
                  pl.BlockSpec((D_MODEL, D_MODEL), lambda i: (0, 0)), pl.BlockSpec((1, D_MODEL), lambda i: (0, 0)),
                  pl.BlockSpec((D_MODEL, 2 * D_FF), lambda i: (0, 0)), pl.BlockSpec((D_FF, D_MODEL), lambda i: (0, 0))],
        out_specs=row(D_MODEL),
        out_shape=jax.ShapeDtypeStruct((t, D_MODEL), _F32),
        compiler_params=_params(("parallel",)),
        name="out_ffn",
    )(x2d, mab, mc, md, wo, g, wgu, wd)


def _lane_vec(values, start):
    v = jnp.zeros((LANES,), _F32).at[start:start + values.shape[0]].set(values.astype(_F32))
    return v[None, :]


def _block_diag(w):
    h, d, _ = w.shape
    eye = jnp.eye(h, dtype=w.dtype)
    return (eye[:, None, :, None] * w[:, :, None, :]).reshape(h * d, h * d)


def _overlap_ext(s_len):
    nc = s_len // CMP_STRIDE
    n_cmp = nc - 1
    n_slc = s_len // SLC_BLOCK
    cs = np.arange(n_cmp)[:, None] * CMP_STRIDE
    ss = np.arange(n_slc)[None, :] * SLC_BLOCK
    ov = np.clip(np.minimum(cs + CMP_LEN, ss + SLC_BLOCK) - np.maximum(cs, ss), 0, CMP_LEN)
    ext = np.zeros((nc, LANES), np.float32)
    ext[:n_cmp, :n_slc] = ov
    return jnp.asarray(ext, _MXU)


def _layer_params(l, w_in, lru_w_gates, lru_b_gates, fox_f_bias, fox_qk_norm, nsa_qk_norm, nsa_cmp_pos,
                  nsa_cmp_w1, nsa_cmp_w2, nsa_gate_bias, out_norm):
    scale = HEAD_DIM ** -0.5 * LOG2E
    w = w_in[l]
    small = jnp.zeros((D_MODEL, LANES), w.dtype)
    small = small.at[:, 0:3 * N_HEADS].set(jnp.repeat(w[:, C_FOX_F:C_FOX_F + N_HEADS], 3, axis=1))
    small = small.at[:, SM_GATE_LANE:SM_GATE_LANE + 3 * N_HEADS].set(w[:, C_NSA_G:C_NSA_G + 3 * N_HEADS])
    w_perm = jnp.concatenate([w[:, :C_FOX_F], w[:, C_NSA_Q:C_NSA_G], small], axis=1).astype(_MXU)
    assert w_perm.shape[1] == N_IN_PAD

    wg = jnp.concatenate([_block_diag(lru_w_gates[l, 0]), _block_diag(lru_w_gates[l, 1])], axis=1).astype(_MXU)
    bg = lru_b_gates[l].reshape(1, 2 * GROUP)

    fb = _lane_vec(jnp.repeat(fox_f_bias[l], 3), 0)
    fox_gq = (jnp.tile(fox_qk_norm[l, 0], 2) * scale)[None, :]
    fox_gk = jnp.tile(fox_qk_norm[l, 1], 2)[None, :]

    nsa_gq = (jnp.tile(nsa_qk_norm[l, 0], 2) * scale)[None, :]
    nsa_gc = _lane_vec(nsa_qk_norm[l, 1], 0)
    nsa_gs = _lane_vec(nsa_qk_norm[l, 2], 0)
    nsa_gw = _lane_vec(nsa_qk_norm[l, 3], 0)
    gb = _lane_vec(nsa_gate_bias[l], SM_GATE_LANE)

    half = CMP_LEN // 2

    def pos_ext(lo):
        return jnp.concatenate([nsa_cmp_pos[l, 0, lo:lo + half], nsa_cmp_pos[l, 1, lo:lo + half]],
                               axis=1).reshape(1, half * LANES)

    def w1_ext(lo):
        ext = jnp.zeros((half, LANES, 2 * CMP_HIDDEN), _F32)
        ext = ext.at[:, :HEAD_DIM, :CMP_HIDDEN].set(nsa_cmp_w1[l, 0, lo:lo + half])
        ext = ext.at[:, HEAD_DIM:, CMP_HIDDEN:].set(nsa_cmp_w1[l, 1, lo:lo + half])
        return ext.reshape(half * LANES, 2 * CMP_HIDDEN).astype(_MXU)

    w2 = jnp.zeros((2 * CMP_HIDDEN, LANES), _F32)
    w2 = w2.at[:CMP_HIDDEN, :HEAD_DIM].set(nsa_cmp_w2[l, 0]).at[CMP_HIDDEN:, HEAD_DIM:].set(nsa_cmp_w2[l, 1])
    return dict(w_perm=w_perm, wg=wg, bg=bg, fb=fb, fox_gq=fox_gq, fox_gk=fox_gk, nsa_gq=nsa_gq, nsa_gc=nsa_gc,
                nsa_gs=nsa_gs, nsa_gw=nsa_gw, gb=gb, pa=pos_ext(0), pb=pos_ext(half), w1a=w1_ext(0),
                w1b=w1_ext(half), w2=w2.astype(_MXU), on=out_norm[l].reshape(1, 4 * GROUP))


def kernel(x, norm_mix, w_in, lru_conv_w, lru_conv_b, lru_w_gates, lru_b_gates, lru_lambda, sc_conv_w, fox_f_bias, fox_qk_norm, nsa_qk_norm, nsa_cmp_pos, nsa_cmp_w1, nsa_cmp_w2, nsa_gate_bias, rel_bias, out_norm, w_out, norm_ffn, w_gate_up, w_down):
    bsz, s_len, d_model = x.shape
    depth = w_in.shape[0]
    assert d_model == D_MODEL and s_len % TS_MIX == 0 and s_len % TQ_FOX == 0
    assert SLC_TOPK <= s_len // SLC_BLOCK <= SLC_BLOCK
    t = bsz * s_len
    nc = s_len // CMP_STRIDE

    win_tab, sel_tab, cmp_tab = _nsa_tables(rel_bias, s_len)
    ovl = _overlap_ext(s_len)

    for l in range(depth):
        lp = _layer_params(l, w_in, lru_w_gates, lru_b_gates, fox_f_bias, fox_qk_norm, nsa_qk_norm,
                           nsa_cmp_pos, nsa_cmp_w1, nsa_cmp_w2, nsa_gate_bias, out_norm)
        m_ab, qf, kf, vf, q4, ks, vs, kw, vw, gates, zcmp = _proj_mix(
            x, norm_mix[l][None, :], lp["w_perm"], lru_conv_w[l], lru_conv_b[l][None, :], lp["wg"], lp["bg"],
            lru_lambda[l][None, :], sc_conv_w[l], lp["on"][:, 0:2 * GROUP],
            lp["fb"], lp["fox_gq"], lp["fox_gk"], lp["nsa_gq"], lp["nsa_gs"], lp["nsa_gw"], lp["gb"])
        m_c = _fox_attn(qf, kf, vf, lp["on"][:, 2 * GROUP:3 * GROUP])
        kc, cv = _compress(zcmp.reshape(bsz, nc, CMP_STRIDE * LANES), lp["pa"], lp["pb"], lp["w1a"],
                           lp["w1b"], lp["w2"], lp["nsa_gc"])
        m_d = _nsa_attn(q4, kc, cv, ks, vs, kw, vw, gates, win_tab, sel_tab, cmp_tab, ovl,
                        lp["on"][:, 3 * GROUP:4 * GROUP])
        x = _out_ffn(x.reshape(t, D_MODEL), m_ab.reshape(t, 2 * GROUP), m_c.reshape(t, GROUP),
                     m_d.reshape(t, GROUP), w_out[l].astype(_MXU), norm_ffn[l][None, :],
                     w_gate_up[l].astype(_MXU), w_down[l].astype(_MXU)).reshape(bsz, s_len, D_MODEL)
    return x
```

```python
import math

import numpy as np
import jax
import jax.numpy as jnp
from jax import lax
from jax.experimental import pallas as pl
from jax.experimental.pallas import tpu as pltpu

D_MODEL = 1024
GROUP = 256
HEAD_DIM = 64
N_HEADS = 4
LRU_CONV = 4
LRU_C = 8.0
SC_CONV = 3
CMP_LEN = 32
CMP_STRIDE = 16
CMP_HIDDEN = 128
SLC_BLOCK = 64
SLC_TOPK = 16
WINDOW = 512
REL_BUCKETS = 32
REL_MAX_DIST = 128
D_FF = 2816
RMS_EPS = 1e-6
NEG = -1e30
BIG = 1e30
LOG2E = math.log2(math.e)

C_FOX_F = 8 * GROUP
C_NSA_Q = C_FOX_F + N_HEADS
C_NSA_G = C_NSA_Q + GROUP + 6 * HEAD_DIM

LANES = 128
SUBLANES = 8
VMEM_LIMIT_BYTES = 56 * 1024 * 1024

TS_MIX = 512
TQ_FOX = 512
TK_FOX = 1024
KT_NSA = 128
TQ_NSA = 256
NSA_FAR = 4
TM_FFN = 512
TF_FFN = 256

SM_GATE_LANE = 16
SUM_LANE = 0

COL_FOX = 5 * GROUP
COL_NSA_Q = COL_FOX + 3 * GROUP
COL_SLABS = COL_NSA_Q + GROUP
N_IN_PAD = COL_SLABS + 4 * LANES

_MXU = jnp.bfloat16
_F32 = jnp.float32


def _params(sem):
    return pltpu.CompilerParams(dimension_semantics=sem, vmem_limit_bytes=VMEM_LIMIT_BYTES)


def _resident(shape):
    zeros = (0,) * len(shape)
    return pl.BlockSpec(shape, lambda *_: zeros, pipeline_mode=pl.Buffered(1))


def _mm(a, b):
    return jnp.dot(a.astype(_MXU), b.astype(_MXU), preferred_element_type=_F32)


def _mm_nt(a, b):
    return lax.dot_general(a.astype(_MXU), b.astype(_MXU), (((1,), (1,)), ((), ())),
                           preferred_element_type=_F32)


def _lane(shape):
    return lax.broadcasted_iota(jnp.int32, shape, len(shape) - 1)


def _row(shape):
    return lax.broadcasted_iota(jnp.int32, shape, 0)


def _rms_scale(x):
    return lax.rsqrt(jnp.mean(x * x, axis=-1, keepdims=True) + RMS_EPS)


def _gelu(x):
    c = math.sqrt(2.0 / math.pi)
    return x * (0.5 * (1.0 + jnp.tanh(c * (x + 0.044715 * (x * x * x)))))


def _round_mxu(x):
    return x.astype(_MXU).astype(_F32)


def _low_half_rms_scale(x, low):
    s = jnp.sum(jnp.where(low, x * x, 0.0), axis=-1, keepdims=True)
    return lax.rsqrt(s * (1.0 / HEAD_DIM) + RMS_EPS)


def _half_rms_scale(x, low):
    x2 = x * x
    s_lo = jnp.sum(jnp.where(low, x2, 0.0), axis=-1, keepdims=True)
    s_hi = jnp.sum(jnp.where(low, 0.0, x2), axis=-1, keepdims=True)
    return jnp.where(low, lax.rsqrt(s_lo * (1.0 / HEAD_DIM) + RMS_EPS),
                     lax.rsqrt(s_hi * (1.0 / HEAD_DIM) + RMS_EPS))


def _linear_scan_rows(a, b):
    n = a.shape[0]
    row = _row(a.shape)
    d = 1
    while d < n:
        keep = row >= d
        a_prev = jnp.where(keep, pltpu.roll(a, d, axis=0), 1.0)
        b_prev = jnp.where(keep, pltpu.roll(b, d, axis=0), 0.0)
        b = a * b_prev + b
        a = a * a_prev
        d *= 2
    return a, b


def _cumsum_rows(x):
    n = x.shape[0]
    row = _row(x.shape)
    d = 1
    while d < n:
        x = x + jnp.where(row >= d, pltpu.roll(x, d, axis=0), 0.0)
        d *= 2
    return x


def _mix_ab(zab, cw_ref, cb_ref, wg_ref, bg_ref, lam_ref, scw_ref, on_ref, out_ref, xext_ref, uext_ref, h_ref):
    ts = zab.shape[0]
    xr = zab[:, 0:GROUP]
    gate = zab[:, GROUP:2 * GROUP]
    bgt = zab[:, 2 * GROUP:3 * GROUP]
    cgt = zab[:, 3 * GROUP:4 * GROUP]
    xs = zab[:, 4 * GROUP:5 * GROUP]

    xext_ref[SUBLANES:SUBLANES + ts, :] = xr
    xc = cb_ref[...] + cw_ref[LRU_CONV - 1:LRU_CONV, :] * xr
    for k in range(LRU_CONV - 1):
        xc = xc + cw_ref[k:k + 1, :] * xext_ref[pl.ds(SUBLANES - (LRU_CONV - 1) + k, ts), :]
    xext_ref[0:SUBLANES, :] = xr[ts - SUBLANES:ts, :]

    gi = _mm(xc, wg_ref[...]) + bg_ref[...]
    r = jax.nn.sigmoid(gi[:, 0:GROUP])
    ig = jax.nn.sigmoid(gi[:, GROUP:2 * GROUP])
    lam = lam_ref[...]
    softplus_neg = jnp.maximum(-lam, 0.0) + jnp.log1p(jnp.exp(-jnp.abs(lam)))
    log_a = (-LRU_C * softplus_neg) * r
    a = jnp.exp(log_a)
    b = jnp.sqrt(-jnp.tanh(log_a) * (a * a + 1.0)) * (ig * xc)
    a_cum, h = _linear_scan_rows(a, b)
    h = h + a_cum * h_ref[0:1, :]
    h_ref[...] = jnp.broadcast_to(h[ts - 1:ts, :], h_ref.shape)
    y_a = h * _gelu(gate)

    u = cgt * xs
    uext_ref[SUBLANES:SUBLANES + ts, :] = u
    cv = scw_ref[SC_CONV - 1:SC_CONV, :] * u
    for k in range(SC_CONV - 1):
        cv = cv + scw_ref[k:k + 1, :] * uext_ref[pl.ds(SUBLANES - (SC_CONV - 1) + k, ts), :]
    uext_ref[0:SUBLANES, :] = u[ts - SUBLANES:ts, :]
    y_b = bgt * cv

    out_ref[0, :, 0:GROUP] = (y_a * _rms_scale(y_a) * on_ref[:, 0:GROUP]).astype(out_ref.dtype)
    out_ref[0, :, GROUP:2 * GROUP] = (y_b * _rms_scale(y_b) * on_ref[:, GROUP:2 * GROUP]).astype(out_ref.dtype)


def _fox_prep(zfox, zsm, fb_ref, gq_ref, gk_ref, q_ref, k_ref, v_ref, c_ref):
    ts = zfox.shape[0]
    shape = (ts, LANES)
    lane = _lane(shape)
    low = lane < HEAD_DIM

    f = zsm + fb_ref[...]
    logf = jnp.minimum(f, 0.0) - jnp.log1p(jnp.exp(-jnp.abs(f)))
    c = _cumsum_rows(logf) + c_ref[0:1, :]
    c_ref[...] = jnp.broadcast_to(c[ts - 1:ts, :], c_ref.shape)
    neg_c = -LOG2E * c
    p1 = _round_mxu(neg_c)
    r1 = neg_c - p1
    p2 = _round_mxu(r1)
    p3 = _round_mxu(r1 - p2)
    piece = lane - 3 * ((lane * 11) >> 5)
    csel = jnp.where(piece == 0, p1, jnp.where(piece == 1, p2, p3))

    ones_aug = jnp.where(lane < HEAD_DIM + 3, 1.0, 0.0)
    for p in range(N_HEADS // 2):
        qs = zfox[:, LANES * p:LANES * (p + 1)]
        ks = zfox[:, GROUP + LANES * p:GROUP + LANES * (p + 1)]
        qn = qs * _half_rms_scale(qs, low) * gq_ref[...]
        kn = ks * _half_rms_scale(ks, low) * gk_ref[...]
        for half in range(2):
            h = 2 * p + half
            qh = qn if half == 0 else pltpu.roll(qn, HEAD_DIM, axis=1)
            kh = kn if half == 0 else pltpu.roll(kn, HEAD_DIM, axis=1)
            ch = pltpu.roll(csel, HEAD_DIM - 3 * h, axis=1)
            q_ref[0, h] = jnp.where(low, qh, ones_aug).astype(q_ref.dtype)
            k_ref[0, h] = jnp.where(low, kh, jnp.where(lane < HEAD_DIM + 3, ch, 0.0)).astype(k_ref.dtype)
    v_ref[0] = zfox[:, 2 * GROUP:3 * GROUP].astype(v_ref.dtype)


def _nsa_prep(zq, zsel, zwin, zsm, row0, gq_ref, gs_ref, gw_ref, gb_ref,
              q4_ref, ks_ref, vs_ref, kw_ref, vw_ref, gate_ref):
    ts = zq.shape[0]
    shape = (ts, LANES)
    lane = _lane(shape)
    low = lane < HEAD_DIM
    for p in range(N_HEADS // 2):
        qs = zq[:, LANES * p:LANES * (p + 1)]
        qn = qs * _half_rms_scale(qs, low) * gq_ref[...]
        q4_ref[0, 2 * p] = jnp.where(low, qn, 0.0).astype(q4_ref.dtype)
        q4_ref[0, 2 * p + 1] = jnp.where(low, pltpu.roll(qn, HEAD_DIM, axis=1), 0.0).astype(q4_ref.dtype)

    blk = (row0 + _row(shape)) >> 6
    onehot = jnp.where(lane - HEAD_DIM == blk, 1.0, 0.0)
    ks_ref[0] = jnp.where(low, zsel * _low_half_rms_scale(zsel, low) * gs_ref[...], onehot).astype(ks_ref.dtype)
    vs_ref[0] = jnp.where(lane == SUM_LANE, 1.0, zsel).astype(vs_ref.dtype)
    kw_ref[0] = jnp.where(low, zwin * _low_half_rms_scale(zwin, low) * gw_ref[...], 0.0).astype(kw_ref.dtype)
    vw_ref[0] = jnp.where(lane == SUM_LANE, 1.0, zwin).astype(vw_ref.dtype)
    gate_ref[0] = jax.nn.sigmoid(zsm + gb_ref[...])


def _proj_mix_kernel(x_ref, g_ref, w_ref,
                     cw_ref, cb_ref, wg_ref, bg_ref, lam_ref, scw_ref, on_ref,
                     fb_ref, fgq_ref, fgk_ref, ngq_ref, ngs_ref, ngw_ref, gb_ref,
                     mab_ref, qf_ref, kf_ref, vf_ref, q4_ref, ks_ref, vs_ref, kw_ref, vw_ref, gate_ref, zcmp_ref,
                     xext_ref, uext_ref, h_ref, c_ref):
    ts = x_ref.shape[1]
    ti = pl.program_id(1)

    @pl.when(ti == 0)
    def _():
        xext_ref[0:SUBLANES, :] = jnp.zeros((SUBLANES, GROUP), _F32)
        uext_ref[0:SUBLANES, :] = jnp.zeros((SUBLANES, GROUP), _F32)
        h_ref[...] = jnp.zeros_like(h_ref)
        c_ref[...] = jnp.zeros_like(c_ref)

    x = x_ref[0]
    xn = (x * _rms_scale(x) * g_ref[...]).astype(_MXU)
    zab = jnp.dot(xn, w_ref[:, 0:COL_FOX], preferred_element_type=_F32)
    _mix_ab(zab, cw_ref, cb_ref, wg_ref, bg_ref, lam_ref, scw_ref, on_ref, mab_ref, xext_ref, uext_ref, h_ref)
    slabs = jnp.dot(xn, w_ref[:, COL_SLABS:N_IN_PAD], preferred_element_type=_F32)
    zcmp_ref[0] = slabs[:, 0:LANES]
    zsel = slabs[:, LANES:2 * LANES]
    zwin = slabs[:, 2 * LANES:3 * LANES]
    zsm = slabs[:, 3 * LANES:4 * LANES]
    zfox = jnp.dot(xn, w_ref[:, COL_FOX:COL_NSA_Q], preferred_element_type=_F32)
    _fox_prep(zfox, zsm, fb_ref, fgq_ref, fgk_ref, qf_ref, kf_ref, vf_ref, c_ref)
    zq = jnp.dot(xn, w_ref[:, COL_NSA_Q:COL_SLABS], preferred_element_type=_F32)
    _nsa_prep(zq, zsel, zwin, zsm, ti * ts, ngq_ref, ngs_ref, ngw_ref, gb_ref,
              q4_ref, ks_ref, vs_ref, kw_ref, vw_ref, gate_ref)


def _proj_mix(x, gain, w, cw, cb, wg, bg, lam, scw, on_ab, fb, fgq, fgk, ngq, ngs, ngw, gb):
    bsz, s_len, _ = x.shape
    ts = TS_MIX
    vec = lambda width: pl.BlockSpec((1, width), lambda b, t: (0, 0))
    rows = lambda width: pl.BlockSpec((1, ts, width), lambda b, t: (b, t, 0))
    heads = pl.BlockSpec((1, N_HEADS, ts, LANES), lambda b, t: (b, 0, t, 0))
    sds = jax.ShapeDtypeStruct
    return pl.pallas_call(
        _proj_mix_kernel,
        grid=(bsz, s_len // ts),
        in_specs=[rows(D_MODEL), vec(D_MODEL), _resident((D_MODEL, N_IN_PAD)),
                  pl.BlockSpec((LRU_CONV, GROUP), lambda b, t: (0, 0)), vec(GROUP), _resident((GROUP, 2 * GROUP)),
                  vec(2 * GROUP), vec(GROUP), pl.BlockSpec((SC_CONV, GROUP), lambda b, t: (0, 0)), vec(2 * GROUP),
                  vec(LANES), vec(LANES), vec(LANES), vec(LANES), vec(LANES), vec(LANES), vec(LANES)],
        out_specs=[rows(2 * GROUP), heads, heads, rows(GROUP), heads, rows(LANES), rows(LANES),
                   rows(LANES), rows(LANES), rows(LANES), rows(LANES)],
        out_shape=[sds((bsz, s_len, 2 * GROUP), _MXU),
                   sds((bsz, N_HEADS, s_len, LANES), _MXU), sds((bsz, N_HEADS, s_len, LANES), _MXU),
                   sds((bsz, s_len, GROUP), _MXU),
                   sds((bsz, N_HEADS, s_len, LANES), _MXU),
                   sds((bsz, s_len, LANES), _MXU), sds((bsz, s_len, LANES), _MXU),
                   sds((bsz, s_len, LANES), _MXU), sds((bsz, s_len, LANES), _MXU),
                   sds((bsz, s_len, LANES), _F32), sds((bsz, s_len, LANES), _F32)],
        scratch_shapes=[pltpu.VMEM((ts + SUBLANES, GROUP), _F32), pltpu.VMEM((ts + SUBLANES, GROUP), _F32),
                        pltpu.VMEM((SUBLANES, GROUP), _F32), pltpu.VMEM((SUBLANES, LANES), _F32)],
        compiler_params=_params(("parallel", "arbitrary")),
        name="proj_mix",
    )(x, gain, w, cw, cb, wg, bg, lam, scw, on_ab, fb, fgq, fgk, ngq, ngs, ngw, gb)


def _flash_step(s, v, carry):
    m, l, acc = carry
    m_new = jnp.maximum(m, jnp.max(s, axis=-1, keepdims=True))
    alpha = jnp.exp2(m - m_new)
    p = jnp.exp2(s - m_new)
    l = alpha * l + jnp.sum(p, axis=-1, keepdims=True)
    acc = alpha * acc + jnp.dot(p.astype(_MXU), v, preferred_element_type=_F32)
    return m_new, l, acc


def _flash_init(rows):
    return (jnp.full((rows, 1), NEG, _F32), jnp.zeros((rows, 1), _F32), jnp.zeros((rows, LANES), _F32))


def _flash_step_sumlane(s, v, carry):
    m, acc = carry
    m_new = jnp.maximum(m, jnp.max(s, axis=-1, keepdims=True))
    p = jnp.exp2(s - m_new).astype(_MXU)
    acc = jnp.exp2(m - m_new) * acc + jnp.dot(p, v, preferred_element_type=_F32)
    return m_new, acc


def _flash_init_sumlane(rows):
    return (jnp.full((rows, 1), NEG, _F32), jnp.zeros((rows, LANES), _F32))


def _sumlane_normalize(acc, sum_lane):
    return acc / acc[:, sum_lane:sum_lane + 1]


def _fox_attn_kernel(q_ref, k_ref, v_ref, on_ref, out_ref):
    tq = q_ref.shape[2]
    qi = pl.program_id(1)
    qs = [q_ref[0, h] for h in range(N_HEADS)]

    def chunk(start, width, carries, causal=False):
        new = []
        for h in range(N_HEADS):
            s = _mm_nt(qs[h], k_ref[0, h, pl.ds(start, width), :])
            if causal:
                s = jnp.where(_lane((tq, width)) <= _row((tq, width)), s, NEG)
            pair = h // 2
            new.append(_flash_step(s, v_ref[0, pl.ds(start, width), LANES * pair:LANES * (pair + 1)], carries[h]))
        return tuple(new)

    per_wide = TK_FOX // tq
    n_wide = qi // per_wide
    carries = lax.fori_loop(0, n_wide, lambda c, cr: chunk(pl.multiple_of(c * TK_FOX, TK_FOX), TK_FOX, cr),
                            tuple(_flash_init(tq) for _ in range(N_HEADS)))
    carries = lax.fori_loop(n_wide * per_wide, qi, lambda c, cr: chunk(pl.multiple_of(c * tq, tq), tq, cr), carries)
    carries = chunk(pl.multiple_of(qi * tq, tq), tq, carries, causal=True)
    outs = [acc / l for _, l, acc in carries]
    low = _lane((tq, LANES)) < HEAD_DIM
    y = jnp.concatenate([jnp.where(low, outs[2 * p], outs[2 * p + 1]) for p in range(N_HEADS // 2)], axis=1)
    out_ref[0] = (y * _rms_scale(y) * on_ref[...]).astype(out_ref.dtype)


def _fox_attn(qf, kf, vf, on):
    bsz, _, s_len, _ = qf.shape
    tq = min(TQ_FOX, s_len)
    assert s_len % TK_FOX == 0 and TK_FOX % tq == 0
    return pl.pallas_call(
        _fox_attn_kernel,
        grid=(bsz, s_len // tq),
        in_specs=[pl.BlockSpec((1, N_HEADS, tq, LANES), lambda b, i: (b, 0, i, 0)),
                  pl.BlockSpec((1, N_HEADS, s_len, LANES), lambda b, i: (b, 0, 0, 0)),
                  pl.BlockSpec((1, s_len, GROUP), lambda b, i: (b, 0, 0)),
                  pl.BlockSpec((1, GROUP), lambda b, i: (0, 0))],
        out_specs=pl.BlockSpec((1, tq, GROUP), lambda b, i: (b, i, 0)),
        out_shape=jax.ShapeDtypeStruct((bsz, s_len, GROUP), _MXU),
        compiler_params=_params(("parallel", "arbitrary")),
        name="fox_attn",
    )(qf, kf, vf, on)


def _compress_kernel(x_ref, pa_ref, pb_ref, w1a_ref, w1b_ref, w2_ref, g_ref, kc_ref, cv_ref):
    nc = x_ref.shape[1]
    x = x_ref[0]
    ua = _mm(x + pa_ref[...], w1a_ref[...])
    ub = _mm(x + pb_ref[...], w1b_ref[...])
    hid = _gelu(ua + pltpu.roll(ub, nc - 1, axis=0))
    out = _mm(hid, w2_ref[...])
    low = _lane(out.shape) < HEAD_DIM
    kc_ref[0] = jnp.where(low, out * _low_half_rms_scale(out, low) * g_ref[...], 0.0).astype(kc_ref.dtype)
    cv_ref[0] = out.astype(cv_ref.dtype)


def _compress(xc, pa, pb, w1a, w1b, w2, g):
    bsz, nc, width = xc.shape
    const = lambda shape: pl.BlockSpec(shape, lambda b: (0, 0))
    slab = pl.BlockSpec((1, nc, LANES), lambda b: (b, 0, 0))
    return pl.pallas_call(
        _compress_kernel,
        grid=(bsz,),
        in_specs=[pl.BlockSpec((1, nc, width), lambda b: (b, 0, 0)),
                  const((1, width)), const((1, width)), const((width, 2 * CMP_HIDDEN)),
                  const((width, 2 * CMP_HIDDEN)), const((2 * CMP_HIDDEN, LANES)), const((1, LANES))],
        out_specs=[slab, slab],
        out_shape=[jax.ShapeDtypeStruct((bsz, nc, LANES), _MXU)] * 2,
        compiler_params=_params(("parallel",)),
        name="compress",
    )(xc, pa, pb, w1a, w1b, w2, g)


def _bucket_thresholds():
    max_exact = REL_BUCKETS // 2
    d = np.arange(0, REL_MAX_DIST + 1)
    large = max_exact + (np.log(np.maximum(d, 1).astype(np.float32) / max_exact)
                         / math.log(REL_MAX_DIST / max_exact) * (REL_BUCKETS - max_exact)).astype(np.int32)
    bucket = np.where(d < max_exact, d, np.minimum(large, REL_BUCKETS - 1))
    assert bucket[-1] == REL_BUCKETS - 1 and np.all(np.diff(bucket) >= 0)
    return [int(np.argmax(bucket >= k)) for k in range(REL_BUCKETS)]


_BUCKET_THR = _bucket_thresholds()

NSA_R = TQ_NSA // KT_NSA
NSA_D = WINDOW // KT_NSA
NSA_TAIL = NSA_D + NSA_R
WIN_ENTRIES = NSA_TAIL + 1
SEL_ENTRIES = NSA_R + 3


def _win_entry(e):
    return jnp.minimum(e, NSA_TAIL)


def _sel_entry(e):
    return jnp.clip(e - (NSA_D - 2), 0, NSA_R + 2)


def _rel_bias(dist, rb_ref, h):
    far = rb_ref[REL_BUCKETS - 1, h]
    val = jnp.full(dist.shape, LOG2E * (rb_ref[0, h] - far), _F32)
    for k in range(1, REL_BUCKETS):
        val = jnp.where(dist >= _BUCKET_THR[k], LOG2E * (rb_ref[k, h] - far), val)
    return val


def _nsa_tables_kernel(rb_ref, win_ref, sel_ref, cmp_ref):
    qi = pl.program_id(0)
    tq = TQ_NSA
    nc = cmp_ref.shape[2]

    @pl.when(qi == 0)
    def _():
        i = _row((tq, KT_NSA))
        c = _lane((tq, KT_NSA))
        for h in range(N_HEADS):
            rs = slice(h * tq, (h + 1) * tq)
            for e in range(NSA_TAIL):
                dist = i + (NSA_D - e) * KT_NSA - c
                ok = jnp.where(dist >= 0, jnp.where(dist < WINDOW, 1, 0), 0) > 0
                tab = jnp.where(ok, _rel_bias(dist, rb_ref, h), NEG)
                win_ref[e, rs, :] = tab
                if e >= NSA_D - 1:
                    sel_ref[e - (NSA_D - 2), rs, :] = tab
            win_ref[NSA_TAIL, rs, :] = jnp.full((tq, KT_NSA), NEG, _F32)
            sel_ref[0, rs, :] = jnp.zeros((tq, KT_NSA), _F32)
            sel_ref[NSA_R + 2, rs, :] = jnp.full((tq, KT_NSA), NEG, _F32)

    t = qi * tq + _row((tq, nc))
    dist = t - (CMP_STRIDE * _lane((tq, nc)) + CMP_LEN - 1)
    for h in range(N_HEADS):
        cmp_ref[0, h * tq:(h + 1) * tq, :] = jnp.where(dist >= 0, _rel_bias(dist, rb_ref, h), NEG)


def _nsa_tables(rel_bias, s_len):
    nq = s_len // TQ_NSA
    nc = s_len // CMP_STRIDE
    rows = N_HEADS * TQ_NSA
    return pl.pallas_call(
        _nsa_tables_kernel,
        grid=(nq,),
        in_specs=[pl.BlockSpec(memory_space=pltpu.SMEM)],
        out_specs=[pl.BlockSpec((WIN_ENTRIES, rows, KT_NSA), lambda i: (0, 0, 0)),
                   pl.BlockSpec((SEL_ENTRIES, rows, KT_NSA), lambda i: (0, 0, 0)),
                   pl.BlockSpec((1, rows, nc), lambda i: (i, 0, 0))],
        out_shape=[jax.ShapeDtypeStruct((WIN_ENTRIES, rows, KT_NSA), _F32),
                   jax.ShapeDtypeStruct((SEL_ENTRIES, rows, KT_NSA), _F32),
                   jax.ShapeDtypeStruct((nq, rows, nc), _F32)],
        compiler_params=_params(("arbitrary",)),
        name="nsa_tables",
    )(rel_bias)


def _topk_penalty(imp_t, qi, tq):
    n_blk = imp_t.shape[0]
    blk = _row((n_blk, tq))
    cur = (qi * tq + _lane((n_blk, tq))) >> 6
    forced = jnp.where(blk == 0, 1, jnp.where(blk == cur, 1, jnp.where(blk == cur - 1, 1, 0))) > 0
    val = jnp.where(blk <= cur, jnp.where(forced, BIG, imp_t), NEG)
    groups = [val[g * SUBLANES:(g + 1) * SUBLANES, :] for g in range(n_blk // SUBLANES)]
    sub = _row((SUBLANES, tq))
    ranks = [jnp.zeros((SUBLANES, tq), _F32) for _ in groups]
    for j in range(n_blk):
        vj = val[j:j + 1, :]
        jg, jr = divmod(j, SUBLANES)
        for g, vg in enumerate(groups):
            ge = jnp.where(vj >= vg, 1.0, 0.0)
            gt = jnp.where(vj > vg, 1.0, 0.0)
            if g > jg:
                ranks[g] = ranks[g] + ge
            elif g < jg:
                ranks[g] = ranks[g] + gt
            else:
                ranks[g] = ranks[g] + jnp.where(sub > jr, ge, gt)
    rank = jnp.concatenate(ranks, axis=0)
    return jnp.where(rank < float(SLC_TOPK), 0.0, NEG)


def _nsa_attn_kernel(q4_ref, kc_ref, cv_ref, ks_ref, vs_ref, kw_ref, vw_ref, gate_ref,
                     win_ref, sel_ref, cmp_ref, ovl_ref, on_ref, out_ref):
    tq = TQ_NSA
    kt = KT_NSA
    rows = N_HEADS * tq
    qi = pl.program_id(1)
    first = qi * NSA_R
    q4 = q4_ref[0].reshape(rows, LANES)
    lane = _lane((tq, LANES))
    low = lane < HEAD_DIM
    tail_w = NSA_TAIL * kt

    def tail_table(tab_ref, entry_of, first_tile, skip_before=0):
        pieces = []
        for u in range(NSA_TAIL):
            tile = first_tile + u
            entry = entry_of(tile - first + NSA_D)
            pieces.append(tab_ref[jnp.where(tile < skip_before, tab_ref.shape[0] - 1, entry)])
        return jnp.concatenate(pieces, axis=1)

    w_tile = jnp.maximum(first - NSA_D, 0)
    w_start = pl.multiple_of(w_tile * kt, kt)
    s = _mm_nt(q4, kw_ref[0, pl.ds(w_start, tail_w), :]) + tail_table(win_ref, _win_entry, w_tile)
    _, acc_w = _flash_step_sumlane(s, vw_ref[0, pl.ds(w_start, tail_w), :], _flash_init_sumlane(rows))
    o_w = _sumlane_normalize(acc_w, SUM_LANE)

    s = _mm_nt(q4, kc_ref[0]) + cmp_ref[0]
    valid = s > 0.5 * NEG
    m = jnp.max(s, axis=-1, keepdims=True)
    p = jnp.where(valid, jnp.exp2(s - m), 0.0)
    l = jnp.sum(p, axis=-1, keepdims=True)
    p = p / jnp.where(l > 0.0, l, 1.0)
    o_c = jnp.dot(p.astype(_MXU), cv_ref[0], preferred_element_type=_F32)

    psum = p[0:tq] + p[tq:2 * tq] + p[2 * tq:3 * tq] + p[3 * tq:4 * tq]
    p_hi = _round_mxu(psum)
    imp = _mm(p_hi, ovl_ref[...]) + _mm(psum - p_hi, ovl_ref[...])

    pen_t = _topk_penalty(imp.T[0:SLC_BLOCK, :], qi, tq)
    pen = jnp.concatenate([jnp.zeros((LANES - SLC_BLOCK, tq), _F32), pen_t], axis=0).T
    pen4 = jnp.concatenate([pen] * N_HEADS, axis=0)
    q_aug = jnp.where(_lane((rows, LANES)) < HEAD_DIM, q4.astype(_F32), pen4).astype(_MXU)

    def sel_far(width):
        def step(c, carry):
            start = pl.multiple_of(c * width, width)
            s = _mm_nt(q_aug, ks_ref[0, pl.ds(start, width), :])
            return _flash_step_sumlane(s, vs_ref[0, pl.ds(start, width), :], carry)
        return step

    far_w = NSA_FAR * kt
    n_far = jnp.maximum(first - 1, 0) // NSA_FAR
    n_dbl = n_far // 2
    carry = lax.fori_loop(0, n_dbl, sel_far(2 * far_w), _flash_init_sumlane(rows))
    carry = lax.fori_loop(2 * n_dbl, n_far, sel_far(far_w), carry)
    done = n_far * NSA_FAR
    t_tile = jnp.minimum(done, ks_ref.shape[1] // kt - NSA_TAIL)
    t_start = pl.multiple_of(t_tile * kt, kt)
    s = _mm_nt(q_aug, ks_ref[0, pl.ds(t_start, tail_w), :]) + tail_table(sel_ref, _sel_entry, t_tile, done)
    _, acc_s = _flash_step_sumlane(s, vs_ref[0, pl.ds(t_start, tail_w), :], carry)
    o_s = _sumlane_normalize(acc_s, SUM_LANE)

    g = gate_ref[0]
    heads = []
    for h in range(N_HEADS):
        rs = slice(h * tq, (h + 1) * tq)
        gc = g[:, SM_GATE_LANE + h:SM_GATE_LANE + h + 1]
        gs = g[:, SM_GATE_LANE + N_HEADS + h:SM_GATE_LANE + N_HEADS + h + 1]
        gw = g[:, SM_GATE_LANE + 2 * N_HEADS + h:SM_GATE_LANE + 2 * N_HEADS + h + 1]
        heads.append(gc * o_c[rs] + gs * o_s[rs] + gw * o_w[rs])
    slabs = [jnp.where(low, pltpu.roll(heads[2 * p], HEAD_DIM, axis=1), heads[2 * p + 1])
             for p in range(N_HEADS // 2)]
    y = jnp.concatenate(slabs, axis=1)
    out_ref[0] = (y * _rms_scale(y) * on_ref[...]).astype(out_ref.dtype)


def _nsa_attn(q4, kc, cv, ks, vs, kw, vw, gates, win_tab, sel_tab, cmp_tab, ovl, on):
    bsz, _, s_len, _ = q4.shape
    tq = TQ_NSA
    nc = kc.shape[1]
    rows = N_HEADS * tq
    assert s_len >= NSA_TAIL * KT_NSA
    full = lambda n: pl.BlockSpec((1, n, LANES), lambda b, i: (b, 0, 0))
    return pl.pallas_call(
        _nsa_attn_kernel,
        grid=(bsz, s_len // tq),
        in_specs=[pl.BlockSpec((1, N_HEADS, tq, LANES), lambda b, i: (b, 0, i, 0)),
                  full(nc), full(nc), full(s_len), full(s_len), full(s_len), full(s_len),
                  pl.BlockSpec((1, tq, LANES), lambda b, i: (b, i, 0)),
                  _resident((WIN_ENTRIES, rows, KT_NSA)), _resident((SEL_ENTRIES, rows, KT_NSA)),
                  pl.BlockSpec((1, rows, nc), lambda b, i: (i, 0, 0)),
                  _resident((nc, LANES)),
                  pl.BlockSpec((1, GROUP), lambda b, i: (0, 0))],
        out_specs=pl.BlockSpec((1, tq, GROUP), lambda b, i: (b, i, 0)),
        out_shape=jax.ShapeDtypeStruct((bsz, s_len, GROUP), _MXU),
        compiler_params=_params(("parallel", "arbitrary")),
        name="nsa_attn",
    )(q4, kc, cv, ks, vs, kw, vw, gates, win_tab, sel_tab, cmp_tab, ovl, on)


def _ffn_kernel(x_ref, mab_ref, mc_ref, md_ref, wo_ref, g_ref, wgu_ref, wd_ref, out_ref):
    x1 = (x_ref[...]
          + jnp.dot(mab_ref[...], wo_ref[0:2 * GROUP, :], preferred_element_type=_F32)
          + jnp.dot(mc_ref[...], wo_ref[2 * GROUP:3 * GROUP, :], preferred_element_type=_F32)
          + jnp.dot(md_ref[...], wo_ref[3 * GROUP:4 * GROUP, :], preferred_element_type=_F32))
    xn = (x1 * _rms_scale(x1) * g_ref[...]).astype(_MXU)
    out_ref[...] = x1
    for j in range(D_FF // TF_FFN):
        cols = slice(j * TF_FFN, (j + 1) * TF_FFN)
        gt = jnp.dot(xn, wgu_ref[:, cols], preferred_element_type=_F32)
        up = jnp.dot(xn, wgu_ref[:, D_FF + j * TF_FFN:D_FF + (j + 1) * TF_FFN], preferred_element_type=_F32)
        hid = (gt * jax.nn.sigmoid(gt)) * up
        out_ref[...] += jnp.dot(hid.astype(_MXU), wd_ref[cols, :], preferred_element_type=_F32)


def _out_ffn(x2d, mab, mc, md, wo, g, wgu, wd):
    t = x2d.shape[0]
    tm = min(TM_FFN, t)
    assert t % tm == 0 and D_FF % TF_FFN == 0
    row = lambda width: pl.BlockSpec((tm, width), lambda i: (i, 0))
    return pl.pallas_call(
        _ffn_kernel,
        grid=(t // tm,),
        in_specs=[row(D_MODEL), row(2 * GROUP), row(GROUP), row(GROUP),
                  pl.BlockSpec((D_MODEL, D_MODEL), lambda i: (0, 0)), pl.BlockSpec((1, D_MODEL), lambda i: (0, 0)),
                  pl.BlockSpec((D_MODEL, 2 * D_FF), lambda i: (0, 0)), pl.BlockSpec((D_FF, D_MODEL), lambda i: (0, 0))],
        out_specs=row(D_MODEL),
        out_shape=jax.ShapeDtypeStruct((t, D_MODEL), _F32),
        compiler_params=_params(("parallel",)),
        name="out_ffn",
    )(x2d, mab, mc, md, wo, g, wgu, wd)


def _lane_vec(values, start):
    v = jnp.zeros((LANES,), _F32).at[start:start + values.shape[0]].set(values.astype(_F32))
    return v[None, :]


def _block_diag(w):
    h, d, _ = w.shape
    eye = jnp.eye(h, dtype=w.dtype)
    return (eye[:, None, :, None] * w[:, :, None, :]).reshape(h * d, h * d)


def _overlap_ext(s_len):
    nc = s_len // CMP_STRIDE
    n_cmp = nc - 1
    n_slc = s_len // SLC_BLOCK
    cs = np.arange(n_cmp)[:, None] * CMP_STRIDE
    ss = np.arange(n_slc)[None, :] * SLC_BLOCK
    ov = np.clip(np.minimum(cs + CMP_LEN, ss + SLC_BLOCK) - np.maximum(cs, ss), 0, CMP_LEN)
    ext = np.zeros((nc, LANES), np.float32)
    ext[:n_cmp, :n_slc] = ov
    return jnp.asarray(ext, _MXU)


def _layer_params(l, w_in, lru_w_gates, lru_b_gates, fox_f_bias, fox_qk_norm, nsa_qk_norm, nsa_cmp_pos,
                  nsa_cmp_w1, nsa_cmp_w2, nsa_gate_bias, out_norm):
    scale = HEAD_DIM ** -0.5 * LOG2E
    w = w_in[l]
    small = jnp.zeros((D_MODEL, LANES), w.dtype)
    small = small.at[:, 0:3 * N_HEADS].set(jnp.repeat(w[:, C_FOX_F:C_FOX_F + N_HEADS], 3, axis=1))
    small = small.at[:, SM_GATE_LANE:SM_GATE_LANE + 3 * N_HEADS].set(w[:, C_NSA_G:C_NSA_G + 3 * N_HEADS])
    w_perm = jnp.concatenate([w[:, :C_FOX_F], w[:, C_NSA_Q:C_NSA_G], small], axis=1).astype(_MXU)
    assert w_perm.shape[1] == N_IN_PAD

    wg = jnp.concatenate([_block_diag(lru_w_gates[l, 0]), _block_diag(lru_w_gates[l, 1])], axis=1).astype(_MXU)
    bg = lru_b_gates[l].reshape(1, 2 * GROUP)

    fb = _lane_vec(jnp.repeat(fox_f_bias[l], 3), 0)
    fox_gq = (jnp.tile(fox_qk_norm[l, 0], 2) * scale)[None, :]
    fox_gk = jnp.tile(fox_qk_norm[l, 1], 2)[None, :]

    nsa_gq = (jnp.tile(nsa_qk_norm[l, 0], 2) * scale)[None, :]
    nsa_gc = _lane_vec(nsa_qk_norm[l, 1], 0)
    nsa_gs = _lane_vec(nsa_qk_norm[l, 2], 0)
    nsa_gw = _lane_vec(nsa_qk_norm[l, 3], 0)
    gb = _lane_vec(nsa_gate_bias[l], SM_GATE_LANE)

    half = CMP_LEN // 2

    def pos_ext(lo):
        return jnp.concatenate([nsa_cmp_pos[l, 0, lo:lo + half], nsa_cmp_pos[l, 1, lo:lo + half]],
                               axis=1).reshape(1, half * LANES)

    def w1_ext(lo):
        ext = jnp.zeros((half, LANES, 2 * CMP_HIDDEN), _F32)
        ext = ext.at[:, :HEAD_DIM, :CMP_HIDDEN].set(nsa_cmp_w1[l, 0, lo:lo + half])
        ext = ext.at[:, HEAD_DIM:, CMP_HIDDEN:].set(nsa_cmp_w1[l, 1, lo:lo + half])
        return ext.reshape(half * LANES, 2 * CMP_HIDDEN).astype(_MXU)

    w2 = jnp.zeros((2 * CMP_HIDDEN, LANES), _F32)
    w2 = w2.at[:CMP_HIDDEN, :HEAD_DIM].set(nsa_cmp_w2[l, 0]).at[CMP_HIDDEN:, HEAD_DIM:].set(nsa_cmp_w2[l, 1])
    return dict(w_perm=w_perm, wg=wg, bg=bg, fb=fb, fox_gq=fox_gq, fox_gk=fox_gk, nsa_gq=nsa_gq, nsa_gc=nsa_gc,
                nsa_gs=nsa_gs, nsa_gw=nsa_gw, gb=gb, pa=pos_ext(0), pb=pos_ext(half), w1a=w1_ext(0),
                w1b=w1_ext(half), w2=w2.astype(_MXU), on=out_norm[l].reshape(1, 4 * GROUP))


def kernel(x, norm_mix, w_in, lru_conv_w, lru_conv_b, lru_w_gates, lru_b_gates, lru_lambda, sc_conv_w, fox_f_bias, fox_qk_norm, nsa_qk_norm, nsa_cmp_pos, nsa_cmp_w1, nsa_cmp_w2, nsa_gate_bias, rel_bias, out_norm, w_out, norm_ffn, w_gate_up, w_down):
    bsz, s_len, d_model = x.shape
    depth = w_in.shape[0]
    assert d_model == D_MODEL and s_len % TS_MIX == 0 and s_len % TQ_FOX == 0
    assert SLC_TOPK <= s_len // SLC_BLOCK <= SLC_BLOCK
    t = bsz * s_len
    nc = s_len // CMP_STRIDE

    win_tab, sel_tab, cmp_tab = _nsa_tables(rel_bias, s_len)
    ovl = _overlap_ext(s_len)

    for l in range(depth):
        lp = _layer_params(l, w_in, lru_w_gates, lru_b_gates, fox_f_bias, fox_qk_norm, nsa_qk_norm,
                           nsa_cmp_pos, nsa_cmp_w1, nsa_cmp_w2, nsa_gate_bias, out_norm)
        m_ab, qf, kf, vf, q4, ks, vs, kw, vw, gates, zcmp = _proj_mix(
            x, norm_mix[l][None, :], lp["w_perm"], lru_conv_w[l], lru_conv_b[l][None, :], lp["wg"], lp["bg"],
            lru_lambda[l][None, :], sc_conv_w[l], lp["on"][:, 0:2 * GROUP],
            lp["fb"], lp["fox_gq"], lp["fox_gk"], lp["nsa_gq"], lp["nsa_gs"], lp["nsa_gw"], lp["gb"])
        m_c = _fox_attn(qf, kf, vf, lp["on"][:, 2 * GROUP:3 * GROUP])
        kc, cv = _compress(zcmp.reshape(bsz, nc, CMP_STRIDE * LANES), lp["pa"], lp["pb"], lp["w1a"],
                           lp["w1b"], lp["w2"], lp["nsa_gc"])
        m_d = _nsa_attn(q4, kc, cv, ks, vs, kw, vw, gates, win_tab, sel_tab, cmp_tab, ovl,
                        lp["on"][:, 3 * GROUP:4 * GROUP])
        x = _out_ffn(x.reshape(t, D_MODEL), m_ab.reshape(t, 2 * GROUP), m_c.reshape(t, GROUP),
                     m_d.reshape(t, GROUP), w_out[l].astype(_MXU), norm_ffn[l][None, :],
                     w_gate_up[l].astype(_MXU), w_down[l].astype(_MXU)).reshape(bsz, s_len, D_MODEL)
    return x
```

```python
import functools
import math

import numpy as np
import jax
import jax.numpy as jnp
from jax import lax
from jax.experimental import pallas as pl
from jax.experimental.pallas import tpu as pltpu

D_MODEL = 1024
GROUP = 256
HEAD_DIM = 64
N_HEADS = 4
LRU_CONV = 4
LRU_C = 8.0
SC_CONV = 3
CMP_LEN = 32
CMP_STRIDE = 16
CMP_HIDDEN = 128
SLC_BLOCK = 64
SLC_TOPK = 16
WINDOW = 512
REL_BUCKETS = 32
REL_MAX_DIST = 128
D_FF = 2816
RMS_EPS = 1e-6
NEG = -1e30
BIG = 1e30
LOG2E = math.log2(math.e)

C_FOX_F = 8 * GROUP
C_NSA_Q = C_FOX_F + N_HEADS
C_NSA_G = C_NSA_Q + GROUP + 6 * HEAD_DIM

LANES = 128
SUBLANES = 8
VMEM_LIMIT_BYTES = 56 * 1024 * 1024

TS_MIX = 512
TQ_FOX = 512
TK_FOX = 1024
KT_NSA = 128
TQ_NSA = 256
NSA_FAR = 4
TM_FFN = 512
TF_FFN = 256

SM_GATE_LANE = 16
SUM_LANE = 0

COL_FOX = 5 * GROUP
COL_NSA_Q = COL_FOX + 3 * GROUP
COL_SLABS = COL_NSA_Q + GROUP
N_IN_PAD = COL_SLABS + 4 * LANES

_MXU = jnp.bfloat16
_F32 = jnp.float32


def _params(sem):
    return pltpu.CompilerParams(dimension_semantics=sem, vmem_limit_bytes=VMEM_LIMIT_BYTES)


def _resident(shape):
    zeros = (0,) * len(shape)
    return pl.BlockSpec(shape, lambda *_: zeros, pipeline_mode=pl.Buffered(1))


def _mm(a, b):
    return jnp.dot(a.astype(_MXU), b.astype(_MXU), preferred_element_type=_F32)


def _mm_nt(a, b):
    return lax.dot_general(a.astype(_MXU), b.astype(_MXU), (((1,), (1,)), ((), ())),
                           preferred_element_type=_F32)


def _lane(shape):
    return lax.broadcasted_iota(jnp.int32, shape, len(shape) - 1)


def _row(shape):
    return lax.broadcasted_iota(jnp.int32, shape, 0)


def _rms_scale(x):
    return lax.rsqrt(jnp.mean(x * x, axis=-1, keepdims=True) + RMS_EPS)


def _gelu(x):
    c = math.sqrt(2.0 / math.pi)
    return x * (0.5 * (1.0 + jnp.tanh(c * (x + 0.044715 * (x * x * x)))))


def _round_mxu(x):
    return x.astype(_MXU).astype(_F32)


def _low_half_rms_scale(x, low):
    s = jnp.sum(jnp.where(low, x * x, 0.0), axis=-1, keepdims=True)
    return lax.rsqrt(s * (1.0 / HEAD_DIM) + RMS_EPS)


def _half_rms_scale(x, low):
    x2 = x * x
    s_lo = jnp.sum(jnp.where(low, x2, 0.0), axis=-1, keepdims=True)
    s_hi = jnp.sum(jnp.where(low, 0.0, x2), axis=-1, keepdims=True)
    return jnp.where(low, lax.rsqrt(s_lo * (1.0 / HEAD_DIM) + RMS_EPS),
                     lax.rsqrt(s_hi * (1.0 / HEAD_DIM) + RMS_EPS))


def _linear_scan_rows(a, b):
    n = a.shape[0]
    row = _row(a.shape)
    d = 1
    while d < n:
        keep = row >= d
        a_prev = jnp.where(keep, pltpu.roll(a, d, axis=0), 1.0)
        b_prev = jnp.where(keep, pltpu.roll(b, d, axis=0), 0.0)
        b = a * b_prev + b
        a = a * a_prev
        d *= 2
    return a, b


def _cumsum_rows(x):
    n = x.shape[0]
    row = _row(x.shape)
    d = 1
    while d < n:
        x = x + jnp.where(row >= d, pltpu.roll(x, d, axis=0), 0.0)
        d *= 2
    return x


def _mix_ab(zab, cw_ref, cb_ref, wg_ref, bg_ref, lam_ref, scw_ref, on_ref, xext_ref, uext_ref, h_ref):
    ts = zab.shape[0]
    xr = zab[:, 0:GROUP]
    gate = zab[:, GROUP:2 * GROUP]
    bgt = zab[:, 2 * GROUP:3 * GROUP]
    cgt = zab[:, 3 * GROUP:4 * GROUP]
    xs = zab[:, 4 * GROUP:5 * GROUP]

    xext_ref[SUBLANES:SUBLANES + ts, :] = xr
    xc = cb_ref[...] + cw_ref[LRU_CONV - 1:LRU_CONV, :] * xr
    for k in range(LRU_CONV - 1):
        xc = xc + cw_ref[k:k + 1, :] * xext_ref[pl.ds(SUBLANES - (LRU_CONV - 1) + k, ts), :]
    xext_ref[0:SUBLANES, :] = xr[ts - SUBLANES:ts, :]

    gi = _mm(xc, wg_ref[...]) + bg_ref[...]
    r = jax.nn.sigmoid(gi[:, 0:GROUP])
    ig = jax.nn.sigmoid(gi[:, GROUP:2 * GROUP])
    lam = lam_ref[...]
    softplus_neg = jnp.maximum(-lam, 0.0) + jnp.log1p(jnp.exp(-jnp.abs(lam)))
    log_a = (-LRU_C * softplus_neg) * r
    a = jnp.exp(log_a)
    b = jnp.sqrt(-jnp.tanh(log_a) * (a * a + 1.0)) * (ig * xc)
    a_cum, h = _linear_scan_rows(a, b)
    h = h + a_cum * h_ref[0:1, :]
    h_ref[...] = jnp.broadcast_to(h[ts - 1:ts, :], h_ref.shape)
    y_a = h * _gelu(gate)

    u = cgt * xs
    uext_ref[SUBLANES:SUBLANES + ts, :] = u
    cv = scw_ref[SC_CONV - 1:SC_CONV, :] * u
    for k in range(SC_CONV - 1):
        cv = cv + scw_ref[k:k + 1, :] * uext_ref[pl.ds(SUBLANES - (SC_CONV - 1) + k, ts), :]
    uext_ref[0:SUBLANES, :] = u[ts - SUBLANES:ts, :]
    y_b = bgt * cv

    return jnp.concatenate([(y_a * _rms_scale(y_a) * on_ref[:, 0:GROUP]).astype(_MXU),
                            (y_b * _rms_scale(y_b) * on_ref[:, GROUP:2 * GROUP]).astype(_MXU)], axis=1)


def _fox_prep(zfox, zsm, fb_ref, gq_ref, gk_ref, q_ref, k_ref, v_ref, c_ref):
    ts = zfox.shape[0]
    shape = (ts, LANES)
    lane = _lane(shape)
    low = lane < HEAD_DIM

    f = zsm + fb_ref[...]
    logf = jnp.minimum(f, 0.0) - jnp.log1p(jnp.exp(-jnp.abs(f)))
    c = _cumsum_rows(logf) + c_ref[0:1, :]
    c_ref[...] = jnp.broadcast_to(c[ts - 1:ts, :], c_ref.shape)
    neg_c = -LOG2E * c
    p1 = _round_mxu(neg_c)
    r1 = neg_c - p1
    p2 = _round_mxu(r1)
    p3 = _round_mxu(r1 - p2)
    piece = lane - 3 * ((lane * 11) >> 5)
    csel = jnp.where(piece == 0, p1, jnp.where(piece == 1, p2, p3))

    ones_aug = jnp.where(lane < HEAD_DIM + 3, 1.0, 0.0)
    for p in range(N_HEADS // 2):
        qs = zfox[:, LANES * p:LANES * (p + 1)]
        ks = zfox[:, GROUP + LANES * p:GROUP + LANES * (p + 1)]
        qn = qs * _half_rms_scale(qs, low) * gq_ref[...]
        kn = ks * _half_rms_scale(ks, low) * gk_ref[...]
        for half in range(2):
            h = 2 * p + half
            qh = qn if half == 0 else pltpu.roll(qn, HEAD_DIM, axis=1)
            kh = kn if half == 0 else pltpu.roll(kn, HEAD_DIM, axis=1)
            ch = pltpu.roll(csel, HEAD_DIM - 3 * h, axis=1)
            q_ref[0, h] = jnp.where(low, qh, ones_aug).astype(q_ref.dtype)
            k_ref[0, h] = jnp.where(low, kh, jnp.where(lane < HEAD_DIM + 3, ch, 0.0)).astype(k_ref.dtype)
    v_ref[0] = zfox[:, 2 * GROUP:3 * GROUP].astype(v_ref.dtype)


def _nsa_prep(zq, zsel, zwin, zsm, row0, gq_ref, gs_ref, gw_ref, gb_ref,
              q4_ref, ks_ref, vs_ref, kw_ref, vw_ref, gate_ref):
    ts = zq.shape[0]
    shape = (ts, LANES)
    lane = _lane(shape)
    low = lane < HEAD_DIM
    for p in range(N_HEADS // 2):
        qs = zq[:, LANES * p:LANES * (p + 1)]
        qn = qs * _half_rms_scale(qs, low) * gq_ref[...]
        q4_ref[0, 2 * p] = jnp.where(low, qn, 0.0).astype(q4_ref.dtype)
        q4_ref[0, 2 * p + 1] = jnp.where(low, pltpu.roll(qn, HEAD_DIM, axis=1), 0.0).astype(q4_ref.dtype)

    blk = (row0 + _row(shape)) >> 6
    onehot = jnp.where(lane - HEAD_DIM == blk, 1.0, 0.0)
    ks_ref[0] = jnp.where(low, zsel * _low_half_rms_scale(zsel, low) * gs_ref[...], onehot).astype(ks_ref.dtype)
    vs_ref[0] = jnp.where(lane == SUM_LANE, 1.0, zsel).astype(vs_ref.dtype)
    kw_ref[0] = jnp.where(low, zwin * _low_half_rms_scale(zwin, low) * gw_ref[...], 0.0).astype(kw_ref.dtype)
    vw_ref[0] = jnp.where(lane == SUM_LANE, 1.0, zwin).astype(vw_ref.dtype)
    gate_ref[0] = jax.nn.sigmoid(zsm + gb_ref[...])


def _proj_mix_kernel(x_ref, g_ref, w_ref, fb_ref, fgq_ref, fgk_ref, ngq_ref, ngs_ref, ngw_ref, gb_ref,
                     zab_ref, qf_ref, kf_ref, vf_ref, q4_ref, ks_ref, vs_ref, kw_ref, vw_ref, gate_ref, zcmp_ref,
                     c_ref):
    ts = x_ref.shape[1]
    ti = pl.program_id(1)

    @pl.when(ti == 0)
    def _():
        c_ref[...] = jnp.zeros_like(c_ref)

    x = x_ref[0]
    xn = (x * _rms_scale(x) * g_ref[...]).astype(_MXU)
    zab_ref[0] = jnp.dot(xn, w_ref[:, 0:COL_FOX], preferred_element_type=_F32)
    slabs = jnp.dot(xn, w_ref[:, COL_SLABS:N_IN_PAD], preferred_element_type=_F32)
    zcmp_ref[0] = slabs[:, 0:LANES]
    zsel = slabs[:, LANES:2 * LANES]
    zwin = slabs[:, 2 * LANES:3 * LANES]
    zsm = slabs[:, 3 * LANES:4 * LANES]
    zfox = jnp.dot(xn, w_ref[:, COL_FOX:COL_NSA_Q], preferred_element_type=_F32)
    _fox_prep(zfox, zsm, fb_ref, fgq_ref, fgk_ref, qf_ref, kf_ref, vf_ref, c_ref)
    zq = jnp.dot(xn, w_ref[:, COL_NSA_Q:COL_SLABS], preferred_element_type=_F32)
    _nsa_prep(zq, zsel, zwin, zsm, ti * ts, ngq_ref, ngs_ref, ngw_ref, gb_ref,
              q4_ref, ks_ref, vs_ref, kw_ref, vw_ref, gate_ref)


def _proj_mix(x, gain, w, fb, fgq, fgk, ngq, ngs, ngw, gb):
    bsz, s_len, _ = x.shape
    ts = TS_MIX
    vec = lambda width: pl.BlockSpec((1, width), lambda b, t: (0, 0))
    rows = lambda width: pl.BlockSpec((1, ts, width), lambda b, t: (b, t, 0))
    heads = pl.BlockSpec((1, N_HEADS, ts, LANES), lambda b, t: (b, 0, t, 0))
    sds = jax.ShapeDtypeStruct
    return pl.pallas_call(
        _proj_mix_kernel,
        grid=(bsz, s_len // ts),
        in_specs=[rows(D_MODEL), vec(D_MODEL), _resident((D_MODEL, N_IN_PAD)),
                  vec(LANES), vec(LANES), vec(LANES), vec(LANES), vec(LANES), vec(LANES), vec(LANES)],
        out_specs=[rows(COL_FOX), heads, heads, rows(GROUP), heads, rows(LANES), rows(LANES),
                   rows(LANES), rows(LANES), rows(LANES), rows(LANES)],
        out_shape=[sds((bsz, s_len, COL_FOX), _F32),
                   sds((bsz, N_HEADS, s_len, LANES), _MXU), sds((bsz, N_HEADS, s_len, LANES), _MXU),
                   sds((bsz, s_len, GROUP), _MXU),
                   sds((bsz, N_HEADS, s_len, LANES), _MXU),
                   sds((bsz, s_len, LANES), _MXU), sds((bsz, s_len, LANES), _MXU),
                   sds((bsz, s_len, LANES), _MXU), sds((bsz, s_len, LANES), _MXU),
                   sds((bsz, s_len, LANES), _F32), sds((bsz, s_len, LANES), _F32)],
        scratch_shapes=[pltpu.VMEM((SUBLANES, LANES), _F32)],
        compiler_params=_params(("parallel", "arbitrary")),
        name="proj_mix",
    )(x, gain, w, fb, fgq, fgk, ngq, ngs, ngw, gb)


def _flash_step(s, v, carry):
    m, l, acc = carry
    m_new = jnp.maximum(m, jnp.max(s, axis=-1, keepdims=True))
    alpha = jnp.exp2(m - m_new)
    p = jnp.exp2(s - m_new)
    l = alpha * l + jnp.sum(p, axis=-1, keepdims=True)
    acc = alpha * acc + jnp.dot(p.astype(_MXU), v, preferred_element_type=_F32)
    return m_new, l, acc


def _flash_init(rows):
    return (jnp.full((rows, 1), NEG, _F32), jnp.zeros((rows, 1), _F32), jnp.zeros((rows, LANES), _F32))


def _flash_step_sumlane(s, v, carry):
    m, acc = carry
    m_new = jnp.maximum(m, jnp.max(s, axis=-1, keepdims=True))
    p = jnp.exp2(s - m_new).astype(_MXU)
    acc = jnp.exp2(m - m_new) * acc + jnp.dot(p, v, preferred_element_type=_F32)
    return m_new, acc


def _flash_init_sumlane(rows):
    return (jnp.full((rows, 1), NEG, _F32), jnp.zeros((rows, LANES), _F32))


def _sumlane_normalize(acc, sum_lane):
    return acc / acc[:, sum_lane:sum_lane + 1]


def _fox_attn_kernel(q_ref, k_ref, v_ref, on_ref, out_ref):
    tq = q_ref.shape[2]
    qi = pl.program_id(1)
    qs = [q_ref[0, h] for h in range(N_HEADS)]

    def chunk(start, width, carries, causal=False):
        new = []
        for h in range(N_HEADS):
            s = _mm_nt(qs[h], k_ref[0, h, pl.ds(start, width), :])
            if causal:
                s = jnp.where(_lane((tq, width)) <= _row((tq, width)), s, NEG)
            pair = h // 2
            new.append(_flash_step(s, v_ref[0, pl.ds(start, width), LANES * pair:LANES * (pair + 1)], carries[h]))
        return tuple(new)

    per_wide = TK_FOX // tq
    n_wide = qi // per_wide
    carries = lax.fori_loop(0, n_wide, lambda c, cr: chunk(pl.multiple_of(c * TK_FOX, TK_FOX), TK_FOX, cr),
                            tuple(_flash_init(tq) for _ in range(N_HEADS)))
    carries = lax.fori_loop(n_wide * per_wide, qi, lambda c, cr: chunk(pl.multiple_of(c * tq, tq), tq, cr), carries)
    carries = chunk(pl.multiple_of(qi * tq, tq), tq, carries, causal=True)
    outs = [acc / l for _, l, acc in carries]
    low = _lane((tq, LANES)) < HEAD_DIM
    y = jnp.concatenate([jnp.where(low, outs[2 * p], outs[2 * p + 1]) for p in range(N_HEADS // 2)], axis=1)
    out_ref[0] = (y * _rms_scale(y) * on_ref[...]).astype(out_ref.dtype)


def _fox_attn(qf, kf, vf, on):
    bsz, _, s_len, _ = qf.shape
    tq = min(TQ_FOX, s_len)
    assert s_len % TK_FOX == 0 and TK_FOX % tq == 0
    return pl.pallas_call(
        _fox_attn_kernel,
        grid=(bsz, s_len // tq),
        in_specs=[pl.BlockSpec((1, N_HEADS, tq, LANES), lambda b, i: (b, 0, i, 0)),
                  pl.BlockSpec((1, N_HEADS, s_len, LANES), lambda b, i: (b, 0, 0, 0)),
                  pl.BlockSpec((1, s_len, GROUP), lambda b, i: (b, 0, 0)),
                  pl.BlockSpec((1, GROUP), lambda b, i: (0, 0))],
        out_specs=pl.BlockSpec((1, tq, GROUP), lambda b, i: (b, i, 0)),
        out_shape=jax.ShapeDtypeStruct((bsz, s_len, GROUP), _MXU),
        compiler_params=_params(("parallel", "arbitrary")),
        name="fox_attn",
    )(qf, kf, vf, on)


def _compress_kernel(x_ref, pa_ref, pb_ref, w1a_ref, w1b_ref, w2_ref, g_ref, kc_ref, cv_ref):
    nc = x_ref.shape[1]
    x = x_ref[0]
    ua = _mm(x + pa_ref[...], w1a_ref[...])
    ub = _mm(x + pb_ref[...], w1b_ref[...])
    hid = _gelu(ua + pltpu.roll(ub, nc - 1, axis=0))
    out = _mm(hid, w2_ref[...])
    low = _lane(out.shape) < HEAD_DIM
    kc_ref[0] = jnp.where(low, out * _low_half_rms_scale(out, low) * g_ref[...], 0.0).astype(kc_ref.dtype)
    cv_ref[0] = out.astype(cv_ref.dtype)


def _compress(xc, pa, pb, w1a, w1b, w2, g):
    bsz, nc, width = xc.shape
    const = lambda shape: pl.BlockSpec(shape, lambda b: (0, 0))
    slab = pl.BlockSpec((1, nc, LANES), lambda b: (b, 0, 0))
    return pl.pallas_call(
        _compress_kernel,
        grid=(bsz,),
        in_specs=[pl.BlockSpec((1, nc, width), lambda b: (b, 0, 0)),
                  const((1, width)), const((1, width)), const((width, 2 * CMP_HIDDEN)),
                  const((width, 2 * CMP_HIDDEN)), const((2 * CMP_HIDDEN, LANES)), const((1, LANES))],
        out_specs=[slab, slab],
        out_shape=[jax.ShapeDtypeStruct((bsz, nc, LANES), _MXU)] * 2,
        compiler_params=_params(("parallel",)),
        name="compress",
    )(xc, pa, pb, w1a, w1b, w2, g)


def _bucket_thresholds():
    max_exact = REL_BUCKETS // 2
    d = np.arange(0, REL_MAX_DIST + 1)
    large = max_exact + (np.log(np.maximum(d, 1).astype(np.float32) / max_exact)
                         / math.log(REL_MAX_DIST / max_exact) * (REL_BUCKETS - max_exact)).astype(np.int32)
    bucket = np.where(d < max_exact, d, np.minimum(large, REL_BUCKETS - 1))
    assert bucket[-1] == REL_BUCKETS - 1 and np.all(np.diff(bucket) >= 0)
    return [int(np.argmax(bucket >= k)) for k in range(REL_BUCKETS)]


_BUCKET_THR = _bucket_thresholds()

NSA_R = TQ_NSA // KT_NSA
NSA_D = WINDOW // KT_NSA
NSA_TAIL = NSA_D + NSA_R
WIN_ENTRIES = NSA_TAIL + 1
SEL_ENTRIES = NSA_R + 3


def _win_entry(e):
    return jnp.minimum(e, NSA_TAIL)


def _sel_entry(e):
    return jnp.clip(e - (NSA_D - 2), 0, NSA_R + 2)


def _rel_bias(dist, rb_ref, h):
    far = rb_ref[REL_BUCKETS - 1, h]
    val = jnp.full(dist.shape, LOG2E * (rb_ref[0, h] - far), _F32)
    for k in range(1, REL_BUCKETS):
        val = jnp.where(dist >= _BUCKET_THR[k], LOG2E * (rb_ref[k, h] - far), val)
    return val


def _nsa_tables_kernel(rb_ref, win_ref, sel_ref, cmp_ref):
    qi = pl.program_id(0)
    tq = TQ_NSA
    nc = cmp_ref.shape[2]

    @pl.when(qi == 0)
    def _():
        i = _row((tq, KT_NSA))
        c = _lane((tq, KT_NSA))
        for h in range(N_HEADS):
            rs = slice(h * tq, (h + 1) * tq)
            for e in range(NSA_TAIL):
                dist = i + (NSA_D - e) * KT_NSA - c
                ok = jnp.where(dist >= 0, jnp.where(dist < WINDOW, 1, 0), 0) > 0
                tab = jnp.where(ok, _rel_bias(dist, rb_ref, h), NEG)
                win_ref[e, rs, :] = tab
                if e >= NSA_D - 1:
                    sel_ref[e - (NSA_D - 2), rs, :] = tab
            win_ref[NSA_TAIL, rs, :] = jnp.full((tq, KT_NSA), NEG, _F32)
            sel_ref[0, rs, :] = jnp.zeros((tq, KT_NSA), _F32)
            sel_ref[NSA_R + 2, rs, :] = jnp.full((tq, KT_NSA), NEG, _F32)

    t = qi * tq + _row((tq, nc))
    dist = t - (CMP_STRIDE * _lane((tq, nc)) + CMP_LEN - 1)
    for h in range(N_HEADS):
        cmp_ref[0, h * tq:(h + 1) * tq, :] = jnp.where(dist >= 0, _rel_bias(dist, rb_ref, h), NEG)


def _nsa_tables(rel_bias, s_len):
    nq = s_len // TQ_NSA
    nc = s_len // CMP_STRIDE
    rows = N_HEADS * TQ_NSA
    return pl.pallas_call(
        _nsa_tables_kernel,
        grid=(nq,),
        in_specs=[pl.BlockSpec(memory_space=pltpu.SMEM)],
        out_specs=[pl.BlockSpec((WIN_ENTRIES, rows, KT_NSA), lambda i: (0, 0, 0)),
                   pl.BlockSpec((SEL_ENTRIES, rows, KT_NSA), lambda i: (0, 0, 0)),
                   pl.BlockSpec((1, rows, nc), lambda i: (i, 0, 0))],
        out_shape=[jax.ShapeDtypeStruct((WIN_ENTRIES, rows, KT_NSA), _F32),
                   jax.ShapeDtypeStruct((SEL_ENTRIES, rows, KT_NSA), _F32),
                   jax.ShapeDtypeStruct((nq, rows, nc), _F32)],
        compiler_params=_params(("arbitrary",)),
        name="nsa_tables",
    )(rel_bias)


def _topk_penalty(imp_t, qi, tq):
    n_blk = imp_t.shape[0]
    blk = _row((n_blk, tq))
    cur = (qi * tq + _lane((n_blk, tq))) >> 6
    forced = jnp.where(blk == 0, 1, jnp.where(blk == cur, 1, jnp.where(blk == cur - 1, 1, 0))) > 0
    val = jnp.where(blk <= cur, jnp.where(forced, BIG, imp_t), NEG)
    groups = [val[g * SUBLANES:(g + 1) * SUBLANES, :] for g in range(n_blk // SUBLANES)]
    sub = _row((SUBLANES, tq))
    ranks = [jnp.zeros((SUBLANES, tq), _F32) for _ in groups]
    for j in range(n_blk):
        vj = val[j:j + 1, :]
        jg, jr = divmod(j, SUBLANES)
        for g, vg in enumerate(groups):
            ge = jnp.where(vj >= vg, 1.0, 0.0)
            gt = jnp.where(vj > vg, 1.0, 0.0)
            if g > jg:
                ranks[g] = ranks[g] + ge
            elif g < jg:
                ranks[g] = ranks[g] + gt
            else:
                ranks[g] = ranks[g] + jnp.where(sub > jr, ge, gt)
    rank = jnp.concatenate(ranks, axis=0)
    return jnp.where(rank < float(SLC_TOPK), 0.0, NEG)


def _nsa_attn_kernel(q4_ref, kc_ref, cv_ref, ks_ref, vs_ref, kw_ref, vw_ref, gate_ref,
                     win_ref, sel_ref, cmp_ref, ovl_ref, on_ref, out_ref):
    tq = TQ_NSA
    kt = KT_NSA
    rows = N_HEADS * tq
    qi = pl.program_id(1)
    first = qi * NSA_R
    q4 = q4_ref[0].reshape(rows, LANES)
    lane = _lane((tq, LANES))
    low = lane < HEAD_DIM
    tail_w = NSA_TAIL * kt

    def tail_table(tab_ref, entry_of, first_tile, skip_before=0):
        pieces = []
        for u in range(NSA_TAIL):
            tile = first_tile + u
            entry = entry_of(tile - first + NSA_D)
            pieces.append(tab_ref[jnp.where(tile < skip_before, tab_ref.shape[0] - 1, entry)])
        return jnp.concatenate(pieces, axis=1)

    w_tile = jnp.maximum(first - NSA_D, 0)
    w_start = pl.multiple_of(w_tile * kt, kt)
    s = _mm_nt(q4, kw_ref[0, pl.ds(w_start, tail_w), :]) + tail_table(win_ref, _win_entry, w_tile)
    _, acc_w = _flash_step_sumlane(s, vw_ref[0, pl.ds(w_start, tail_w), :], _flash_init_sumlane(rows))
    o_w = _sumlane_normalize(acc_w, SUM_LANE)

    s = _mm_nt(q4, kc_ref[0]) + cmp_ref[0]
    valid = s > 0.5 * NEG
    m = jnp.max(s, axis=-1, keepdims=True)
    p = jnp.where(valid, jnp.exp2(s - m), 0.0)
    l = jnp.sum(p, axis=-1, keepdims=True)
    p = p / jnp.where(l > 0.0, l, 1.0)
    o_c = jnp.dot(p.astype(_MXU), cv_ref[0], preferred_element_type=_F32)

    psum = p[0:tq] + p[tq:2 * tq] + p[2 * tq:3 * tq] + p[3 * tq:4 * tq]
    p_hi = _round_mxu(psum)
    imp = _mm(p_hi, ovl_ref[...]) + _mm(psum - p_hi, ovl_ref[...])

    pen_t = _topk_penalty(imp.T[0:SLC_BLOCK, :], qi, tq)
    pen = jnp.concatenate([jnp.zeros((LANES - SLC_BLOCK, tq), _F32), pen_t], axis=0).T
    pen4 = jnp.concatenate([pen] * N_HEADS, axis=0)
    q_aug = jnp.where(_lane((rows, LANES)) < HEAD_DIM, q4.astype(_F32), pen4).astype(_MXU)

    def sel_far(width):
        def step(c, carry):
            start = pl.multiple_of(c * width, width)
            s = _mm_nt(q_aug, ks_ref[0, pl.ds(start, width), :])
            return _flash_step_sumlane(s, vs_ref[0, pl.ds(start, width), :], carry)
        return step

    far_w = NSA_FAR * kt
    n_far = jnp.maximum(first - 1, 0) // NSA_FAR
    n_dbl = n_far // 2
    carry = lax.fori_loop(0, n_dbl, sel_far(2 * far_w), _flash_init_sumlane(rows))
    carry = lax.fori_loop(2 * n_dbl, n_far, sel_far(far_w), carry)
    done = n_far * NSA_FAR
    t_tile = jnp.minimum(done, ks_ref.shape[1] // kt - NSA_TAIL)
    t_start = pl.multiple_of(t_tile * kt, kt)
    s = _mm_nt(q_aug, ks_ref[0, pl.ds(t_start, tail_w), :]) + tail_table(sel_ref, _sel_entry, t_tile, done)
    _, acc_s = _flash_step_sumlane(s, vs_ref[0, pl.ds(t_start, tail_w), :], carry)
    o_s = _sumlane_normalize(acc_s, SUM_LANE)

    g = gate_ref[0]
    heads = []
    for h in range(N_HEADS):
        rs = slice(h * tq, (h + 1) * tq)
        gc = g[:, SM_GATE_LANE + h:SM_GATE_LANE + h + 1]
        gs = g[:, SM_GATE_LANE + N_HEADS + h:SM_GATE_LANE + N_HEADS + h + 1]
        gw = g[:, SM_GATE_LANE + 2 * N_HEADS + h:SM_GATE_LANE + 2 * N_HEADS + h + 1]
        heads.append(gc * o_c[rs] + gs * o_s[rs] + gw * o_w[rs])
    slabs = [jnp.where(low, pltpu.roll(heads[2 * p], HEAD_DIM, axis=1), heads[2 * p + 1])
             for p in range(N_HEADS // 2)]
    y = jnp.concatenate(slabs, axis=1)
    out_ref[0] = (y * _rms_scale(y) * on_ref[...]).astype(out_ref.dtype)


def _nsa_attn(q4, kc, cv, ks, vs, kw, vw, gates, win_tab, sel_tab, cmp_tab, ovl, on):
    bsz, _, s_len, _ = q4.shape
    tq = TQ_NSA
    nc = kc.shape[1]
    rows = N_HEADS * tq
    assert s_len >= NSA_TAIL * KT_NSA
    full = lambda n: pl.BlockSpec((1, n, LANES), lambda b, i: (b, 0, 0))
    return pl.pallas_call(
        _nsa_attn_kernel,
        grid=(bsz, s_len // tq),
        in_specs=[pl.BlockSpec((1, N_HEADS, tq, LANES), lambda b, i: (b, 0, i, 0)),
                  full(nc), full(nc), full(s_len), full(s_len), full(s_len), full(s_len),
                  pl.BlockSpec((1, tq, LANES), lambda b, i: (b, i, 0)),
                  _resident((WIN_ENTRIES, rows, KT_NSA)), _resident((SEL_ENTRIES, rows, KT_NSA)),
                  pl.BlockSpec((1, rows, nc), lambda b, i: (i, 0, 0)),
                  _resident((nc, LANES)),
                  pl.BlockSpec((1, GROUP), lambda b, i: (0, 0))],
        out_specs=pl.BlockSpec((1, tq, GROUP), lambda b, i: (b, i, 0)),
        out_shape=jax.ShapeDtypeStruct((bsz, s_len, GROUP), _MXU),
        compiler_params=_params(("parallel", "arbitrary")),
        name="nsa_attn",
    )(q4, kc, cv, ks, vs, kw, vw, gates, win_tab, sel_tab, cmp_tab, ovl, on)


def _ffn_kernel(x_ref, zab_ref, mc_ref, md_ref, wo_ref, g_ref, wgu_ref, wd_ref,
                cw_ref, cb_ref, wg_ref, bg_ref, lam_ref, scw_ref, on_ref, out_ref,
                xext_ref, uext_ref, h_ref, mab_ref, *, tiles_per_seq):
    step = pl.program_id(0)

    @pl.when(step % tiles_per_seq == 0)
    def _():
        xext_ref[0:SUBLANES, :] = jnp.zeros((SUBLANES, GROUP), _F32)
        uext_ref[0:SUBLANES, :] = jnp.zeros((SUBLANES, GROUP), _F32)
        h_ref[...] = jnp.zeros_like(h_ref)

    @pl.when(step == 0)
    def _():
        mab_ref[...] = jnp.zeros_like(mab_ref)

    mab = mab_ref[...]
    mab_ref[...] = _mix_ab(zab_ref[...], cw_ref, cb_ref, wg_ref, bg_ref, lam_ref, scw_ref, on_ref,
                           xext_ref, uext_ref, h_ref)
    x1 = (x_ref[...]
          + jnp.dot(mab, wo_ref[0:2 * GROUP, :], preferred_element_type=_F32)
          + jnp.dot(mc_ref[...], wo_ref[2 * GROUP:3 * GROUP, :], preferred_element_type=_F32)
          + jnp.dot(md_ref[...], wo_ref[3 * GROUP:4 * GROUP, :], preferred_element_type=_F32))
    xn = (x1 * _rms_scale(x1) * g_ref[...]).astype(_MXU)
    out_ref[...] = x1
    for j in range(D_FF // TF_FFN):
        cols = slice(j * TF_FFN, (j + 1) * TF_FFN)
        gt = jnp.dot(xn, wgu_ref[:, cols], preferred_element_type=_F32)
        up = jnp.dot(xn, wgu_ref[:, D_FF + j * TF_FFN:D_FF + (j + 1) * TF_FFN], preferred_element_type=_F32)
        hid = (gt * jax.nn.sigmoid(gt)) * up
        out_ref[...] += jnp.dot(hid.astype(_MXU), wd_ref[cols, :], preferred_element_type=_F32)


def _out_ffn(x2d, zab, mc, md, wo, g, wgu, wd, cw, cb, wg, bg, lam, scw, on_ab, s_len):
    t = x2d.shape[0]
    tm = TM_FFN
    assert t % tm == 0 and s_len % tm == 0 and D_FF % TF_FFN == 0
    n = t // tm
    row = lambda width: pl.BlockSpec((tm, width), lambda i: (jnp.maximum(i - 1, 0), 0))
    ahead = lambda width: pl.BlockSpec((tm, width), lambda i: (jnp.minimum(i, n - 1), 0))
    vec = lambda width: pl.BlockSpec((1, width), lambda i: (0, 0))
    return pl.pallas_call(
        functools.partial(_ffn_kernel, tiles_per_seq=s_len // tm),
        grid=(n + 1,),
        in_specs=[row(D_MODEL), ahead(COL_FOX), row(GROUP), row(GROUP),
                  _resident((D_MODEL, D_MODEL)), vec(D_MODEL),
                  _resident((D_MODEL, 2 * D_FF)), _resident((D_FF, D_MODEL)),
                  pl.BlockSpec((LRU_CONV, GROUP), lambda i: (0, 0)), vec(GROUP), _resident((GROUP, 2 * GROUP)),
                  vec(2 * GROUP), vec(GROUP), pl.BlockSpec((SC_CONV, GROUP), lambda i: (0, 0)), vec(2 * GROUP)],
        out_specs=row(D_MODEL),
        out_shape=jax.ShapeDtypeStruct((t, D_MODEL), _F32),
        scratch_shapes=[pltpu.VMEM((tm + SUBLANES, GROUP), _F32), pltpu.VMEM((tm + SUBLANES, GROUP), _F32),
                        pltpu.VMEM((SUBLANES, GROUP), _F32), pltpu.VMEM((tm, 2 * GROUP), _MXU)],
        compiler_params=_params(("arbitrary",)),
        name="out_ffn",
    )(x2d, zab, mc, md, wo, g, wgu, wd, cw, cb, wg, bg, lam, scw, on_ab)


def _lane_vec(values, start):
    v = jnp.zeros((LANES,), _F32).at[start:start + values.shape[0]].set(values.astype(_F32))
    return v[None, :]


def _block_diag(w):
    h, d, _ = w.shape
    eye = jnp.eye(h, dtype=w.dtype)
    return (eye[:, None, :, None] * w[:, :, None, :]).reshape(h * d, h * d)


def _overlap_ext(s_len):
    nc = s_len // CMP_STRIDE
    n_cmp = nc - 1
    n_slc = s_len // SLC_BLOCK
    cs = np.arange(n_cmp)[:, None] * CMP_STRIDE
    ss = np.arange(n_slc)[None, :] * SLC_BLOCK
    ov = np.clip(np.minimum(cs + CMP_LEN, ss + SLC_BLOCK) - np.maximum(cs, ss), 0, CMP_LEN)
    ext = np.zeros((nc, LANES), np.float32)
    ext[:n_cmp, :n_slc] = ov
    return jnp.asarray(ext, _MXU)


def _layer_params(l, w_in, lru_w_gates, lru_b_gates, fox_f_bias, fox_qk_norm, nsa_qk_norm, nsa_cmp_pos,
                  nsa_cmp_w1, nsa_cmp_w2, nsa_gate_bias, out_norm):
    scale = HEAD_DIM ** -0.5 * LOG2E
    w = w_in[l]
    small = jnp.zeros((D_MODEL, LANES), w.dtype)
    small = small.at[:, 0:3 * N_HEADS].set(jnp.repeat(w[:, C_FOX_F:C_FOX_F + N_HEADS], 3, axis=1))
    small = small.at[:, SM_GATE_LANE:SM_GATE_LANE + 3 * N_HEADS].set(w[:, C_NSA_G:C_NSA_G + 3 * N_HEADS])
    w_perm = jnp.concatenate([w[:, :C_FOX_F], w[:, C_NSA_Q:C_NSA_G], small], axis=1).astype(_MXU)
    assert w_perm.shape[1] == N_IN_PAD

    wg = jnp.concatenate([_block_diag(lru_w_gates[l, 0]), _block_diag(lru_w_gates[l, 1])], axis=1).astype(_MXU)
    bg = lru_b_gates[l].reshape(1, 2 * GROUP)

    fb = _lane_vec(jnp.repeat(fox_f_bias[l], 3), 0)
    fox_gq = (jnp.tile(fox_qk_norm[l, 0], 2) * scale)[None, :]
    fox_gk = jnp.tile(fox_qk_norm[l, 1], 2)[None, :]

    nsa_gq = (jnp.tile(nsa_qk_norm[l, 0], 2) * scale)[None, :]
    nsa_gc = _lane_vec(nsa_qk_norm[l, 1], 0)
    nsa_gs = _lane_vec(nsa_qk_norm[l, 2], 0)
    nsa_gw = _lane_vec(nsa_qk_norm[l, 3], 0)
    gb = _lane_vec(nsa_gate_bias[l], SM_GATE_LANE)

    half = CMP_LEN // 2

    def pos_ext(lo):
        return jnp.concatenate([nsa_cmp_pos[l, 0, lo:lo + half], nsa_cmp_pos[l, 1, lo:lo + half]],
                               axis=1).reshape(1, half * LANES)

    def w1_ext(lo):
        ext = jnp.zeros((half, LANES, 2 * CMP_HIDDEN), _F32)
        ext = ext.at[:, :HEAD_DIM, :CMP_HIDDEN].set(nsa_cmp_w1[l, 0, lo:lo + half])
        ext = ext.at[:, HEAD_DIM:, CMP_HIDDEN:].set(nsa_cmp_w1[l, 1, lo:lo + half])
        return ext.reshape(half * LANES, 2 * CMP_HIDDEN).astype(_MXU)

    w2 = jnp.zeros((2 * CMP_HIDDEN, LANES), _F32)
    w2 = w2.at[:CMP_HIDDEN, :HEAD_DIM].set(nsa_cmp_w2[l, 0]).at[CMP_HIDDEN:, HEAD_DIM:].set(nsa_cmp_w2[l, 1])
    return dict(w_perm=w_perm, wg=wg, bg=bg, fb=fb, fox_gq=fox_gq, fox_gk=fox_gk, nsa_gq=nsa_gq, nsa_gc=nsa_gc,
                nsa_gs=nsa_gs, nsa_gw=nsa_gw, gb=gb, pa=pos_ext(0), pb=pos_ext(half), w1a=w1_ext(0),
                w1b=w1_ext(half), w2=w2.astype(_MXU), on=out_norm[l].reshape(1, 4 * GROUP))


def kernel(x, norm_mix, w_in, lru_conv_w, lru_conv_b, lru_w_gates, lru_b_gates, lru_lambda, sc_conv_w, fox_f_bias, fox_qk_norm, nsa_qk_norm, nsa_cmp_pos, nsa_cmp_w1, nsa_cmp_w2, nsa_gate_bias, rel_bias, out_norm, w_out, norm_ffn, w_gate_up, w_down):
    bsz, s_len, d_model = x.shape
    depth = w_in.shape[0]
    assert d_model == D_MODEL and s_len % TS_MIX == 0 and s_len % TQ_FOX == 0
    assert SLC_TOPK <= s_len // SLC_BLOCK <= SLC_BLOCK
    t = bsz * s_len
    nc = s_len // CMP_STRIDE

    win_tab, sel_tab, cmp_tab = _nsa_tables(rel_bias, s_len)
    ovl = _overlap_ext(s_len)

    for l in range(depth):
        lp = _layer_params(l, w_in, lru_w_gates, lru_b_gates, fox_f_bias, fox_qk_norm, nsa_qk_norm,
                           nsa_cmp_pos, nsa_cmp_w1, nsa_cmp_w2, nsa_gate_bias, out_norm)
        zab, qf, kf, vf, q4, ks, vs, kw, vw, gates, zcmp = _proj_mix(
            x, norm_mix[l][None, :], lp["w_perm"],
            lp["fb"], lp["fox_gq"], lp["fox_gk"], lp["nsa_gq"], lp["nsa_gs"], lp["nsa_gw"], lp["gb"])
        m_c = _fox_attn(qf, kf, vf, lp["on"][:, 2 * GROUP:3 * GROUP])
        kc, cv = _compress(zcmp.reshape(bsz, nc, CMP_STRIDE * LANES), lp["pa"], lp["pb"], lp["w1a"],
                           lp["w1b"], lp["w2"], lp["nsa_gc"])
        m_d = _nsa_attn(q4, kc, cv, ks, vs, kw, vw, gates, win_tab, sel_tab, cmp_tab, ovl,
                        lp["on"][:, 3 * GROUP:4 * GROUP])
        x = _out_ffn(x.reshape(t, D_MODEL), zab.reshape(t, COL_FOX), m_c.reshape(t, GROUP),
                     m_d.reshape(t, GROUP), w_out[l].astype(_MXU), norm_ffn[l][None, :],
                     w_gate_up[l].astype(_MXU), w_down[l].astype(_MXU),
                     lru_conv_w[l], lru_conv_b[l][None, :], lp["wg"], lp["bg"], lru_lambda[l][None, :],
                     sc_conv_w[l], lp["on"][:, 0:2 * GROUP], s_len).reshape(bsz, s_len, D_MODEL)
    return x
```

```python
import math

import numpy as np
import jax
import jax.numpy as jnp
from jax import lax
from jax.experimental import pallas as pl
from jax.experimental.pallas import tpu as pltpu

D_MODEL = 1024
GROUP = 256
HEAD_DIM = 64
N_HEADS = 4
LRU_CONV = 4
LRU_C = 8.0
SC_CONV = 3
CMP_LEN = 32
CMP_STRIDE = 16
CMP_HIDDEN = 128
SLC_BLOCK = 64
SLC_TOPK = 16
WINDOW = 512
REL_BUCKETS = 32
REL_MAX_DIST = 128
D_FF = 2816
RMS_EPS = 1e-6
NEG = -1e30
BIG = 1e30
LOG2E = math.log2(math.e)

C_FOX_F = 8 * GROUP
C_NSA_Q = C_FOX_F + N_HEADS
C_NSA_G = C_NSA_Q + GROUP + 6 * HEAD_DIM

LANES = 128
SUBLANES = 8
VMEM_LIMIT_BYTES = 56 * 1024 * 1024

TS_MIX = 512
TQ_FOX = 1024
TK_FOX = 1024
KT_NSA = 128
TQ_NSA = 256
NSA_FAR = 4
TM_FFN = 512
TF_FFN = 256

SM_GATE_LANE = 16
SUM_LANE = 0

COL_FOX = 5 * GROUP
COL_NSA_Q = COL_FOX + 3 * GROUP
COL_SLABS = COL_NSA_Q + GROUP
N_IN_PAD = COL_SLABS + 4 * LANES

_MXU = jnp.bfloat16
_F32 = jnp.float32


def _params(sem):
    return pltpu.CompilerParams(dimension_semantics=sem, vmem_limit_bytes=VMEM_LIMIT_BYTES)


def _resident(shape):
    zeros = (0,) * len(shape)
    return pl.BlockSpec(shape, lambda *_: zeros, pipeline_mode=pl.Buffered(1))


def _mm(a, b):
    return jnp.dot(a.astype(_MXU), b.astype(_MXU), preferred_element_type=_F32)


def _mm_nt(a, b):
    return lax.dot_general(a.astype(_MXU), b.astype(_MXU), (((1,), (1,)), ((), ())),
                           preferred_element_type=_F32)


def _lane(shape):
    return lax.broadcasted_iota(jnp.int32, shape, len(shape) - 1)


def _row(shape):
    return lax.broadcasted_iota(jnp.int32, shape, 0)


def _rms_scale(x):
    return lax.rsqrt(jnp.mean(x * x, axis=-1, keepdims=True) + RMS_EPS)


def _gelu(x):
    c = math.sqrt(2.0 / math.pi)
    return x * (0.5 * (1.0 + jnp.tanh(c * (x + 0.044715 * (x * x * x)))))


def _round_mxu(x):
    return x.astype(_MXU).astype(_F32)


def _low_half_rms_scale(x, low):
    s = jnp.sum(jnp.where(low, x * x, 0.0), axis=-1, keepdims=True)
    return lax.rsqrt(s * (1.0 / HEAD_DIM) + RMS_EPS)


def _half_rms_scale(x, low):
    x2 = x * x
    s_lo = jnp.sum(jnp.where(low, x2, 0.0), axis=-1, keepdims=True)
    s_hi = jnp.sum(jnp.where(low, 0.0, x2), axis=-1, keepdims=True)
    return jnp.where(low, lax.rsqrt(s_lo * (1.0 / HEAD_DIM) + RMS_EPS),
                     lax.rsqrt(s_hi * (1.0 / HEAD_DIM) + RMS_EPS))


def _linear_scan_rows(a, b):
    n = a.shape[0]
    row = _row(a.shape)
    d = 1
    while d < n:
        keep = row >= d
        a_prev = jnp.where(keep, pltpu.roll(a, d, axis=0), 1.0)
        b_prev = jnp.where(keep, pltpu.roll(b, d, axis=0), 0.0)
        b = a * b_prev + b
        a = a * a_prev
        d *= 2
    return a, b


def _cumsum_rows(x):
    n = x.shape[0]
    row = _row(x.shape)
    d = 1
    while d < n:
        x = x + jnp.where(row >= d, pltpu.roll(x, d, axis=0), 0.0)
        d *= 2
    return x


def _mix_ab(zab, cw_ref, cb_ref, wg_ref, bg_ref, lam_ref, scw_ref, on_ref, out_ref, xext_ref, uext_ref, h_ref):
    ts = zab.shape[0]
    xr = zab[:, 0:GROUP]
    gate = zab[:, GROUP:2 * GROUP]
    bgt = zab[:, 2 * GROUP:3 * GROUP]
    cgt = zab[:, 3 * GROUP:4 * GROUP]
    xs = zab[:, 4 * GROUP:5 * GROUP]

    xext_ref[SUBLANES:SUBLANES + ts, :] = xr
    xc = cb_ref[...] + cw_ref[LRU_CONV - 1:LRU_CONV, :] * xr
    for k in range(LRU_CONV - 1):
        xc = xc + cw_ref[k:k + 1, :] * xext_ref[pl.ds(SUBLANES - (LRU_CONV - 1) + k, ts), :]
    xext_ref[0:SUBLANES, :] = xr[ts - SUBLANES:ts, :]

    gi = _mm(xc, wg_ref[...]) + bg_ref[...]
    r = jax.nn.sigmoid(gi[:, 0:GROUP])
    ig = jax.nn.sigmoid(gi[:, GROUP:2 * GROUP])
    lam = lam_ref[...]
    softplus_neg = jnp.maximum(-lam, 0.0) + jnp.log1p(jnp.exp(-jnp.abs(lam)))
    log_a = (-LRU_C * softplus_neg) * r
    a = jnp.exp(log_a)
    g = -jnp.tanh(log_a) * (a * a + 1.0)
    b = jnp.where(g > 0.0, g * lax.rsqrt(g), 0.0) * (ig * xc)
    a_cum, h = _linear_scan_rows(a, b)
    h = h + a_cum * h_ref[0:1, :]
    h_ref[...] = jnp.broadcast_to(h[ts - 1:ts, :], h_ref.shape)
    y_a = h * _gelu(gate)

    u = cgt * xs
    uext_ref[SUBLANES:SUBLANES + ts, :] = u
    cv = scw_ref[SC_CONV - 1:SC_CONV, :] * u
    for k in range(SC_CONV - 1):
        cv = cv + scw_ref[k:k + 1, :] * uext_ref[pl.ds(SUBLANES - (SC_CONV - 1) + k, ts), :]
    uext_ref[0:SUBLANES, :] = u[ts - SUBLANES:ts, :]
    y_b = bgt * cv

    out_ref[0, :, 0:GROUP] = (y_a * _rms_scale(y_a) * on_ref[:, 0:GROUP]).astype(out_ref.dtype)
    out_ref[0, :, GROUP:2 * GROUP] = (y_b * _rms_scale(y_b) * on_ref[:, GROUP:2 * GROUP]).astype(out_ref.dtype)


def _fox_prep(zfox, zsm, fb_ref, gq_ref, gk_ref, q_ref, k_ref, v_ref, c_ref):
    ts = zfox.shape[0]
    shape = (ts, LANES)
    lane = _lane(shape)
    low = lane < HEAD_DIM

    f = zsm + fb_ref[...]
    logf = jnp.minimum(f, 0.0) - jnp.log1p(jnp.exp(-jnp.abs(f)))
    c = _cumsum_rows(logf) + c_ref[0:1, :]
    c_ref[...] = jnp.broadcast_to(c[ts - 1:ts, :], c_ref.shape)
    neg_c = -LOG2E * c
    p1 = _round_mxu(neg_c)
    r1 = neg_c - p1
    p2 = _round_mxu(r1)
    p3 = _round_mxu(r1 - p2)
    piece = lane - 3 * ((lane * 11) >> 5)
    csel = jnp.where(piece == 0, p1, jnp.where(piece == 1, p2, p3))

    ones_aug = jnp.where(lane < HEAD_DIM + 3, 1.0, 0.0)
    for p in range(N_HEADS // 2):
        qs = zfox[:, LANES * p:LANES * (p + 1)]
        ks = zfox[:, GROUP + LANES * p:GROUP + LANES * (p + 1)]
        qn = qs * _half_rms_scale(qs, low) * gq_ref[...]
        kn = ks * _half_rms_scale(ks, low) * gk_ref[...]
        for half in range(2):
            h = 2 * p + half
            qh = qn if half == 0 else pltpu.roll(qn, HEAD_DIM, axis=1)
            kh = kn if half == 0 else pltpu.roll(kn, HEAD_DIM, axis=1)
            ch = pltpu.roll(csel, HEAD_DIM - 3 * h, axis=1)
            q_ref[0, h] = jnp.where(low, qh, ones_aug).astype(q_ref.dtype)
            k_ref[0, h] = jnp.where(low, kh, jnp.where(lane < HEAD_DIM + 3, ch, 0.0)).astype(k_ref.dtype)
    v_ref[0] = zfox[:, 2 * GROUP:3 * GROUP].astype(v_ref.dtype)


def _nsa_prep(zq, zsel, zwin, zsm, row0, gq_ref, gs_ref, gw_ref, gb_ref,
              q4_ref, ks_ref, vs_ref, kw_ref, vw_ref, gate_ref):
    ts = zq.shape[0]
    shape = (ts, LANES)
    lane = _lane(shape)
    low = lane < HEAD_DIM
    for p in range(N_HEADS // 2):
        qs = zq[:, LANES * p:LANES * (p + 1)]
        qn = qs * _half_rms_scale(qs, low) * gq_ref[...]
        q4_ref[0, 2 * p] = jnp.where(low, qn, 0.0).astype(q4_ref.dtype)
        q4_ref[0, 2 * p + 1] = jnp.where(low, pltpu.roll(qn, HEAD_DIM, axis=1), 0.0).astype(q4_ref.dtype)

    blk = (row0 + _row(shape)) >> 6
    onehot = jnp.where(lane - HEAD_DIM == blk, 1.0, 0.0)
    ks_ref[0] = jnp.where(low, zsel * _low_half_rms_scale(zsel, low) * gs_ref[...], onehot).astype(ks_ref.dtype)
    vs_ref[0] = jnp.where(lane == SUM_LANE, 1.0, zsel).astype(vs_ref.dtype)
    kw_ref[0] = jnp.where(low, zwin * _low_half_rms_scale(zwin, low) * gw_ref[...], 0.0).astype(kw_ref.dtype)
    vw_ref[0] = jnp.where(lane == SUM_LANE, 1.0, zwin).astype(vw_ref.dtype)
    gate_ref[0] = jax.nn.sigmoid(zsm + gb_ref[...])


def _proj_mix_kernel(x_ref, g_ref, w_ref,
                     cw_ref, cb_ref, wg_ref, bg_ref, lam_ref, scw_ref, on_ref,
                     fb_ref, fgq_ref, fgk_ref, ngq_ref, ngs_ref, ngw_ref, gb_ref,
                     mab_ref, qf_ref, kf_ref, vf_ref, q4_ref, ks_ref, vs_ref, kw_ref, vw_ref, gate_ref, zcmp_ref,
                     xext_ref, uext_ref, h_ref, c_ref, zc_ref):
    ts = x_ref.shape[1]
    ti = pl.program_id(1)

    @pl.when(ti == 0)
    def _():
        xext_ref[0:SUBLANES, :] = jnp.zeros((SUBLANES, GROUP), _F32)
        uext_ref[0:SUBLANES, :] = jnp.zeros((SUBLANES, GROUP), _F32)
        h_ref[...] = jnp.zeros_like(h_ref)
        c_ref[...] = jnp.zeros_like(c_ref)

    x = x_ref[0]
    xn = (x * _rms_scale(x) * g_ref[...]).astype(_MXU)
    zab = jnp.dot(xn, w_ref[:, 0:COL_FOX], preferred_element_type=_F32)
    _mix_ab(zab, cw_ref, cb_ref, wg_ref, bg_ref, lam_ref, scw_ref, on_ref, mab_ref, xext_ref, uext_ref, h_ref)
    slabs = jnp.dot(xn, w_ref[:, COL_SLABS:N_IN_PAD], preferred_element_type=_F32)
    zc_ref[...] = slabs[:, 0:LANES]
    for tok in range(CMP_STRIDE):
        zcmp_ref[0, :, tok * LANES:(tok + 1) * LANES] = zc_ref[pl.ds(tok, ts // CMP_STRIDE, stride=CMP_STRIDE), :]
    zsel = slabs[:, LANES:2 * LANES]
    zwin = slabs[:, 2 * LANES:3 * LANES]
    zsm = slabs[:, 3 * LANES:4 * LANES]
    zfox = jnp.dot(xn, w_ref[:, COL_FOX:COL_NSA_Q], preferred_element_type=_F32)
    _fox_prep(zfox, zsm, fb_ref, fgq_ref, fgk_ref, qf_ref, kf_ref, vf_ref, c_ref)
    zq = jnp.dot(xn, w_ref[:, COL_NSA_Q:COL_SLABS], preferred_element_type=_F32)
    _nsa_prep(zq, zsel, zwin, zsm, ti * ts, ngq_ref, ngs_ref, ngw_ref, gb_ref,
              q4_ref, ks_ref, vs_ref, kw_ref, vw_ref, gate_ref)


def _proj_mix(x, gain, w, cw, cb, wg, bg, lam, scw, on_ab, fb, fgq, fgk, ngq, ngs, ngw, gb):
    bsz, s_len, _ = x.shape
    ts = TS_MIX
    vec = lambda width: pl.BlockSpec((1, width), lambda b, t: (0, 0))
    rows = lambda width: pl.BlockSpec((1, ts, width), lambda b, t: (b, t, 0))
    heads = pl.BlockSpec((1, N_HEADS, ts, LANES), lambda b, t: (b, 0, t, 0))
    sds = jax.ShapeDtypeStruct
    return pl.pallas_call(
        _proj_mix_kernel,
        grid=(bsz, s_len // ts),
        in_specs=[rows(D_MODEL), vec(D_MODEL), _resident((D_MODEL, N_IN_PAD)),
                  pl.BlockSpec((LRU_CONV, GROUP), lambda b, t: (0, 0)), vec(GROUP), _resident((GROUP, 2 * GROUP)),
                  vec(2 * GROUP), vec(GROUP), pl.BlockSpec((SC_CONV, GROUP), lambda b, t: (0, 0)), vec(2 * GROUP),
                  vec(LANES), vec(LANES), vec(LANES), vec(LANES), vec(LANES), vec(LANES), vec(LANES)],
        out_specs=[rows(2 * GROUP), heads, heads, rows(GROUP), heads, rows(LANES), rows(LANES),
                   rows(LANES), rows(LANES), rows(LANES),
                   pl.BlockSpec((1, ts // CMP_STRIDE, CMP_STRIDE * LANES), lambda b, t: (b, t, 0))],
        out_shape=[sds((bsz, s_len, 2 * GROUP), _MXU),
                   sds((bsz, N_HEADS, s_len, LANES), _MXU), sds((bsz, N_HEADS, s_len, LANES), _MXU),
                   sds((bsz, s_len, GROUP), _MXU),
                   sds((bsz, N_HEADS, s_len, LANES), _MXU),
                   sds((bsz, s_len, LANES), _MXU), sds((bsz, s_len, LANES), _MXU),
                   sds((bsz, s_len, LANES), _MXU), sds((bsz, s_len, LANES), _MXU),
                   sds((bsz, s_len, LANES), _F32), sds((bsz, s_len // CMP_STRIDE, CMP_STRIDE * LANES), _F32)],
        scratch_shapes=[pltpu.VMEM((ts + SUBLANES, GROUP), _F32), pltpu.VMEM((ts + SUBLANES, GROUP), _F32),
                        pltpu.VMEM((SUBLANES, GROUP), _F32), pltpu.VMEM((SUBLANES, LANES), _F32),
                        pltpu.VMEM((ts, LANES), _F32)],
        compiler_params=_params(("parallel", "arbitrary")),
        name="proj_mix",
    )(x, gain, w, cw, cb, wg, bg, lam, scw, on_ab, fb, fgq, fgk, ngq, ngs, ngw, gb)


def _flash_step(s, v, carry):
    m, l, acc = carry
    m_new = jnp.maximum(m, jnp.max(s, axis=-1, keepdims=True))
    alpha = jnp.exp2(m - m_new)
    p = jnp.exp2(s - m_new)
    l = alpha * l + jnp.sum(p, axis=-1, keepdims=True)
    acc = alpha * acc + jnp.dot(p.astype(_MXU), v, preferred_element_type=_F32)
    return m_new, l, acc


def _flash_init(rows):
    return (jnp.full((rows, 1), NEG, _F32), jnp.zeros((rows, 1), _F32), jnp.zeros((rows, LANES), _F32))


def _flash_step_sumlane(s, v, carry):
    m, acc = carry
    m_new = jnp.maximum(m, jnp.max(s, axis=-1, keepdims=True))
    p = jnp.exp2(s - m_new).astype(_MXU)
    acc = jnp.exp2(m - m_new) * acc + jnp.dot(p, v, preferred_element_type=_F32)
    return m_new, acc


def _flash_init_sumlane(rows):
    return (jnp.full((rows, 1), NEG, _F32), jnp.zeros((rows, LANES), _F32))


def _sumlane_normalize(acc, sum_lane):
    return acc / acc[:, sum_lane:sum_lane + 1]


def _fox_attn_kernel(q_ref, k_ref, v_ref, on_ref, out_ref):
    tq = q_ref.shape[2]
    qi = pl.program_id(1)
    qs = [q_ref[0, h] for h in range(N_HEADS)]

    def chunk(start, width, carries, causal=False):
        new = []
        for h in range(N_HEADS):
            s = _mm_nt(qs[h], k_ref[0, h, pl.ds(start, width), :])
            if causal:
                s = jnp.where(_lane((tq, width)) <= _row((tq, width)), s, NEG)
            pair = h // 2
            new.append(_flash_step(s, v_ref[0, pl.ds(start, width), LANES * pair:LANES * (pair + 1)], carries[h]))
        return tuple(new)

    per_wide = TK_FOX // tq
    n_wide = qi // per_wide
    carries = lax.fori_loop(0, n_wide, lambda c, cr: chunk(pl.multiple_of(c * TK_FOX, TK_FOX), TK_FOX, cr),
                            tuple(_flash_init(tq) for _ in range(N_HEADS)))
    carries = lax.fori_loop(n_wide * per_wide, qi, lambda c, cr: chunk(pl.multiple_of(c * tq, tq), tq, cr), carries)
    carries = chunk(pl.multiple_of(qi * tq, tq), tq, carries, causal=True)
    outs = [acc / l for _, l, acc in carries]
    low = _lane((tq, LANES)) < HEAD_DIM
    y = jnp.concatenate([jnp.where(low, outs[2 * p], outs[2 * p + 1]) for p in range(N_HEADS // 2)], axis=1)
    out_ref[0] = (y * _rms_scale(y) * on_ref[...]).astype(out_ref.dtype)


def _fox_attn(qf, kf, vf, on):
    bsz, _, s_len, _ = qf.shape
    tq = min(TQ_FOX, s_len)
    assert s_len % TK_FOX == 0 and TK_FOX % tq == 0
    return pl.pallas_call(
        _fox_attn_kernel,
        grid=(bsz, s_len // tq),
        in_specs=[pl.BlockSpec((1, N_HEADS, tq, LANES), lambda b, i: (b, 0, i, 0)),
                  pl.BlockSpec((1, N_HEADS, s_len, LANES), lambda b, i: (b, 0, 0, 0)),
                  pl.BlockSpec((1, s_len, GROUP), lambda b, i: (b, 0, 0)),
                  pl.BlockSpec((1, GROUP), lambda b, i: (0, 0))],
        out_specs=pl.BlockSpec((1, tq, GROUP), lambda b, i: (b, i, 0)),
        out_shape=jax.ShapeDtypeStruct((bsz, s_len, GROUP), _MXU),
        compiler_params=_params(("parallel", "arbitrary")),
        name="fox_attn",
    )(qf, kf, vf, on)


def _compress_kernel(x_ref, pa_ref, pb_ref, w1a_ref, w1b_ref, w2_ref, g_ref, kc_ref, cv_ref):
    nc = x_ref.shape[1]
    x = x_ref[0]
    ua = _mm(x + pa_ref[...], w1a_ref[...])
    ub = _mm(x + pb_ref[...], w1b_ref[...])
    hid = _gelu(ua + pltpu.roll(ub, nc - 1, axis=0))
    out = _mm(hid, w2_ref[...])
    low = _lane(out.shape) < HEAD_DIM
    kc_ref[0] = jnp.where(low, out * _low_half_rms_scale(out, low) * g_ref[...], 0.0).astype(kc_ref.dtype)
    cv_ref[0] = out.astype(cv_ref.dtype)


def _compress(xc, pa, pb, w1a, w1b, w2, g):
    bsz, nc, width = xc.shape
    const = lambda shape: pl.BlockSpec(shape, lambda b: (0, 0))
    slab = pl.BlockSpec((1, nc, LANES), lambda b: (b, 0, 0))
    return pl.pallas_call(
        _compress_kernel,
        grid=(bsz,),
        in_specs=[pl.BlockSpec((1, nc, width), lambda b: (b, 0, 0)),
                  const((1, width)), const((1, width)), const((width, 2 * CMP_HIDDEN)),
                  const((width, 2 * CMP_HIDDEN)), const((2 * CMP_HIDDEN, LANES)), const((1, LANES))],
        out_specs=[slab, slab],
        out_shape=[jax.ShapeDtypeStruct((bsz, nc, LANES), _MXU)] * 2,
        compiler_params=_params(("parallel",)),
        name="compress",
    )(xc, pa, pb, w1a, w1b, w2, g)


def _bucket_thresholds():
    max_exact = REL_BUCKETS // 2
    d = np.arange(0, REL_MAX_DIST + 1)
    large = max_exact + (np.log(np.maximum(d, 1).astype(np.float32) / max_exact)
                         / math.log(REL_MAX_DIST / max_exact) * (REL_BUCKETS - max_exact)).astype(np.int32)
    bucket = np.where(d < max_exact, d, np.minimum(large, REL_BUCKETS - 1))
    assert bucket[-1] == REL_BUCKETS - 1 and np.all(np.diff(bucket) >= 0)
    return [int(np.argmax(bucket >= k)) for k in range(REL_BUCKETS)]


_BUCKET_THR = _bucket_thresholds()

NSA_R = TQ_NSA // KT_NSA
NSA_D = WINDOW // KT_NSA
NSA_TAIL = NSA_D + NSA_R
WIN_ENTRIES = NSA_TAIL + 1
SEL_ENTRIES = NSA_R + 3


def _win_entry(e):
    return jnp.minimum(e, NSA_TAIL)


def _sel_entry(e):
    return jnp.clip(e - (NSA_D - 2), 0, NSA_R + 2)


def _rel_bias(dist, rb_ref, h):
    far = rb_ref[REL_BUCKETS - 1, h]
    val = jnp.full(dist.shape, LOG2E * (rb_ref[0, h] - far), _F32)
    for k in range(1, REL_BUCKETS):
        val = jnp.where(dist >= _BUCKET_THR[k], LOG2E * (rb_ref[k, h] - far), val)
    return val


def _nsa_tables_kernel(rb_ref, win_ref, sel_ref, cmp_ref):
    qi = pl.program_id(0)
    tq = TQ_NSA
    nc = cmp_ref.shape[2]

    @pl.when(qi == 0)
    def _():
        i = _row((tq, KT_NSA))
        c = _lane((tq, KT_NSA))
        for h in range(N_HEADS):
            rs = slice(h * tq, (h + 1) * tq)
            for e in range(NSA_TAIL):
                dist = i + (NSA_D - e) * KT_NSA - c
                ok = jnp.where(dist >= 0, jnp.where(dist < WINDOW, 1, 0), 0) > 0
                tab = jnp.where(ok, _rel_bias(dist, rb_ref, h), NEG)
                win_ref[e, rs, :] = tab
                if e >= NSA_D - 1:
                    sel_ref[e - (NSA_D - 2), rs, :] = tab
            win_ref[NSA_TAIL, rs, :] = jnp.full((tq, KT_NSA), NEG, _F32)
            sel_ref[0, rs, :] = jnp.zeros((tq, KT_NSA), _F32)
            sel_ref[NSA_R + 2, rs, :] = jnp.full((tq, KT_NSA), NEG, _F32)

    t = qi * tq + _row((tq, nc))
    dist = t - (CMP_STRIDE * _lane((tq, nc)) + CMP_LEN - 1)
    for h in range(N_HEADS):
        cmp_ref[0, h * tq:(h + 1) * tq, :] = jnp.where(dist >= 0, _rel_bias(dist, rb_ref, h), NEG)


def _nsa_tables(rel_bias, s_len):
    nq = s_len // TQ_NSA
    nc = s_len // CMP_STRIDE
    rows = N_HEADS * TQ_NSA
    return pl.pallas_call(
        _nsa_tables_kernel,
        grid=(nq,),
        in_specs=[pl.BlockSpec(memory_space=pltpu.SMEM)],
        out_specs=[pl.BlockSpec((WIN_ENTRIES, rows, KT_NSA), lambda i: (0, 0, 0)),
                   pl.BlockSpec((SEL_ENTRIES, rows, KT_NSA), lambda i: (0, 0, 0)),
                   pl.BlockSpec((1, rows, nc), lambda i: (i, 0, 0))],
        out_shape=[jax.ShapeDtypeStruct((WIN_ENTRIES, rows, KT_NSA), _F32),
                   jax.ShapeDtypeStruct((SEL_ENTRIES, rows, KT_NSA), _F32),
                   jax.ShapeDtypeStruct((nq, rows, nc), _F32)],
        compiler_params=_params(("arbitrary",)),
        name="nsa_tables",
    )(rel_bias)


def _topk_penalty(imp_t, qi, tq):
    n_blk = imp_t.shape[0]
    blk = _row((n_blk, tq))
    cur = (qi * tq + _lane((n_blk, tq))) >> 6
    forced = jnp.where(blk == 0, 1, jnp.where(blk == cur, 1, jnp.where(blk == cur - 1, 1, 0))) > 0
    val = jnp.where(blk <= cur, jnp.where(forced, BIG, imp_t), NEG)
    groups = [val[g * SUBLANES:(g + 1) * SUBLANES, :] for g in range(n_blk // SUBLANES)]
    sub = _row((SUBLANES, tq))
    ranks = [jnp.zeros((SUBLANES, tq), _F32) for _ in groups]
    for j in range(n_blk):
        vj = val[j:j + 1, :]
        jg, jr = divmod(j, SUBLANES)
        for g, vg in enumerate(groups):
            ge = jnp.where(vj >= vg, 1.0, 0.0)
            gt = jnp.where(vj > vg, 1.0, 0.0)
            if g > jg:
                ranks[g] = ranks[g] + ge
            elif g < jg:
                ranks[g] = ranks[g] + gt
            else:
                ranks[g] = ranks[g] + jnp.where(sub > jr, ge, gt)
    rank = jnp.concatenate(ranks, axis=0)
    return jnp.where(rank < float(SLC_TOPK), 0.0, NEG)


def _nsa_attn_kernel(q4_ref, kc_ref, cv_ref, ks_ref, vs_ref, kw_ref, vw_ref, gate_ref,
                     win_ref, sel_ref, cmp_ref, ovl_ref, on_ref, out_ref):
    tq = TQ_NSA
    kt = KT_NSA
    rows = N_HEADS * tq
    qi = pl.program_id(1)
    first = qi * NSA_R
    q4 = q4_ref[0].reshape(rows, LANES)
    lane = _lane((tq, LANES))
    low = lane < HEAD_DIM
    tail_w = NSA_TAIL * kt

    def tail_table(tab_ref, entry_of, first_tile, skip_before=0):
        pieces = []
        for u in range(NSA_TAIL):
            tile = first_tile + u
            entry = entry_of(tile - first + NSA_D)
            pieces.append(tab_ref[jnp.where(tile < skip_before, tab_ref.shape[0] - 1, entry)])
        return jnp.concatenate(pieces, axis=1)

    w_tile = jnp.maximum(first - NSA_D, 0)
    w_start = pl.multiple_of(w_tile * kt, kt)
    s = _mm_nt(q4, kw_ref[0, pl.ds(w_start, tail_w), :]) + tail_table(win_ref, _win_entry, w_tile)
    _, acc_w = _flash_step_sumlane(s, vw_ref[0, pl.ds(w_start, tail_w), :], _flash_init_sumlane(rows))
    o_w = _sumlane_normalize(acc_w, SUM_LANE)

    s = _mm_nt(q4, kc_ref[0]) + cmp_ref[0]
    valid = s > 0.5 * NEG
    m = jnp.max(s, axis=-1, keepdims=True)
    p = jnp.where(valid, jnp.exp2(s - m), 0.0)
    l = jnp.sum(p, axis=-1, keepdims=True)
    p = p / jnp.where(l > 0.0, l, 1.0)
    o_c = jnp.dot(p.astype(_MXU), cv_ref[0], preferred_element_type=_F32)

    psum = p[0:tq] + p[tq:2 * tq] + p[2 * tq:3 * tq] + p[3 * tq:4 * tq]
    p_hi = _round_mxu(psum)
    imp = _mm(p_hi, ovl_ref[...]) + _mm(psum - p_hi, ovl_ref[...])

    pen_t = _topk_penalty(imp.T[0:SLC_BLOCK, :], qi, tq)
    pen = jnp.concatenate([jnp.zeros((LANES - SLC_BLOCK, tq), _F32), pen_t], axis=0).T
    pen4 = jnp.concatenate([pen] * N_HEADS, axis=0)
    q_aug = jnp.where(_lane((rows, LANES)) < HEAD_DIM, q4.astype(_F32), pen4).astype(_MXU)

    def sel_far(width):
        def step(c, carry):
            start = pl.multiple_of(c * width, width)
            s = _mm_nt(q_aug, ks_ref[0, pl.ds(start, width), :])
            return _flash_step_sumlane(s, vs_ref[0, pl.ds(start, width), :], carry)
        return step

    far_w = NSA_FAR * kt
    n_far = jnp.maximum(first - 1, 0) // NSA_FAR
    n_dbl = n_far // 2
    carry = lax.fori_loop(0, n_dbl, sel_far(2 * far_w), _flash_init_sumlane(rows))
    carry = lax.fori_loop(2 * n_dbl, n_far, sel_far(far_w), carry)
    done = n_far * NSA_FAR
    t_tile = jnp.minimum(done, ks_ref.shape[1] // kt - NSA_TAIL)
    t_start = pl.multiple_of(t_tile * kt, kt)
    s = _mm_nt(q_aug, ks_ref[0, pl.ds(t_start, tail_w), :]) + tail_table(sel_ref, _sel_entry, t_tile, done)
    _, acc_s = _flash_step_sumlane(s, vs_ref[0, pl.ds(t_start, tail_w), :], carry)
    o_s = _sumlane_normalize(acc_s, SUM_LANE)

    g = gate_ref[0]
    heads = []
    for h in range(N_HEADS):
        rs = slice(h * tq, (h + 1) * tq)
        gc = g[:, SM_GATE_LANE + h:SM_GATE_LANE + h + 1]
        gs = g[:, SM_GATE_LANE + N_HEADS + h:SM_GATE_LANE + N_HEADS + h + 1]
        gw = g[:, SM_GATE_LANE + 2 * N_HEADS + h:SM_GATE_LANE + 2 * N_HEADS + h + 1]
        heads.append(gc * o_c[rs] + gs * o_s[rs] + gw * o_w[rs])
    slabs = [jnp.where(low, pltpu.roll(heads[2 * p], HEAD_DIM, axis=1), heads[2 * p + 1])
             for p in range(N_HEADS // 2)]
    y = jnp.concatenate(slabs, axis=1)
    out_ref[0] = (y * _rms_scale(y) * on_ref[...]).astype(out_ref.dtype)


def _nsa_attn(q4, kc, cv, ks, vs, kw, vw, gates, win_tab, sel_tab, cmp_tab, ovl, on):
    bsz, _, s_len, _ = q4.shape
    tq = TQ_NSA
    nc = kc.shape[1]
    rows = N_HEADS * tq
    assert s_len >= NSA_TAIL * KT_NSA
    full = lambda n: pl.BlockSpec((1, n, LANES), lambda b, i: (b, 0, 0))
    return pl.pallas_call(
        _nsa_attn_kernel,
        grid=(bsz, s_len // tq),
        in_specs=[pl.BlockSpec((1, N_HEADS, tq, LANES), lambda b, i: (b, 0, i, 0)),
                  full(nc), full(nc), full(s_len), full(s_len), full(s_len), full(s_len),
                  pl.BlockSpec((1, tq, LANES), lambda b, i: (b, i, 0)),
                  _resident((WIN_ENTRIES, rows, KT_NSA)), _resident((SEL_ENTRIES, rows, KT_NSA)),
                  pl.BlockSpec((1, rows, nc), lambda b, i: (i, 0, 0)),
                  _resident((nc, LANES)),
                  pl.BlockSpec((1, GROUP), lambda b, i: (0, 0))],
        out_specs=pl.BlockSpec((1, tq, GROUP), lambda b, i: (b, i, 0)),
        out_shape=jax.ShapeDtypeStruct((bsz, s_len, GROUP), _MXU),
        compiler_params=_params(("parallel", "arbitrary")),
        name="nsa_attn",
    )(q4, kc, cv, ks, vs, kw, vw, gates, win_tab, sel_tab, cmp_tab, ovl, on)


def _ffn_kernel(x_ref, mab_ref, mc_ref, md_ref, wo_ref, g_ref, wgu_ref, wd_ref, out_ref):
    x1 = (x_ref[...]
          + jnp.dot(mab_ref[...], wo_ref[0:2 * GROUP, :], preferred_element_type=_F32)
          + jnp.dot(mc_ref[...], wo_ref[2 * GROUP:3 * GROUP, :], preferred_element_type=_F32)
          + jnp.dot(md_ref[...], wo_ref[3 * GROUP:4 * GROUP, :], preferred_element_type=_F32))
    xn = (x1 * _rms_scale(x1) * g_ref[...]).astype(_MXU)
    out_ref[...] = x1
    for j in range(D_FF // TF_FFN):
        cols = slice(j * TF_FFN, (j + 1) * TF_FFN)
        gt = jnp.dot(xn, wgu_ref[:, cols], preferred_element_type=_F32)
        up = jnp.dot(xn, wgu_ref[:, D_FF + j * TF_FFN:D_FF + (j + 1) * TF_FFN], preferred_element_type=_F32)
        hid = (gt * jax.nn.sigmoid(gt)) * up
        out_ref[...] += jnp.dot(hid.astype(_MXU), wd_ref[cols, :], preferred_element_type=_F32)


def _out_ffn(x2d, mab, mc, md, wo, g, wgu, wd):
    t = x2d.shape[0]
    tm = min(TM_FFN, t)
    assert t % tm == 0 and D_FF % TF_FFN == 0
    row = lambda width: pl.BlockSpec((tm, width), lambda i: (i, 0))
    return pl.pallas_call(
        _ffn_kernel,
        grid=(t // tm,),
        in_specs=[row(D_MODEL), row(2 * GROUP), row(GROUP), row(GROUP),
                  pl.BlockSpec((D_MODEL, D_MODEL), lambda i: (0, 0)), pl.BlockSpec((1, D_MODEL), lambda i: (0, 0)),
                  pl.BlockSpec((D_MODEL, 2 * D_FF), lambda i: (0, 0)), pl.BlockSpec((D_FF, D_MODEL), lambda i: (0, 0))],
        out_specs=row(D_MODEL),
        out_shape=jax.ShapeDtypeStruct((t, D_MODEL), _F32),
        compiler_params=_params(("parallel",)),
        name="out_ffn",
    )(x2d, mab, mc, md, wo, g, wgu, wd)


def _lane_vec(values, start):
    v = jnp.zeros((LANES,), _F32).at[start:start + values.shape[0]].set(values.astype(_F32))
    return v[None, :]


def _block_diag(w):
    h, d, _ = w.shape
    eye = jnp.eye(h, dtype=w.dtype)
    return (eye[:, None, :, None] * w[:, :, None, :]).reshape(h * d, h * d)


def _overlap_ext(s_len):
    nc = s_len // CMP_STRIDE
    n_cmp = nc - 1
    n_slc = s_len // SLC_BLOCK
    cs = np.arange(n_cmp)[:, None] * CMP_STRIDE
    ss = np.arange(n_slc)[None, :] * SLC_BLOCK
    ov = np.clip(np.minimum(cs + CMP_LEN, ss + SLC_BLOCK) - np.maximum(cs, ss), 0, CMP_LEN)
    ext = np.zeros((nc, LANES), np.float32)
    ext[:n_cmp, :n_slc] = ov
    return jnp.asarray(ext, _MXU)


def _layer_params(l, w_in, lru_w_gates, lru_b_gates, fox_f_bias, fox_qk_norm, nsa_qk_norm, nsa_cmp_pos,
                  nsa_cmp_w1, nsa_cmp_w2, nsa_gate_bias, out_norm):
    scale = HEAD_DIM ** -0.5 * LOG2E
    w = w_in[l]
    small = jnp.zeros((D_MODEL, LANES), w.dtype)
    small = small.at[:, 0:3 * N_HEADS].set(jnp.repeat(w[:, C_FOX_F:C_FOX_F + N_HEADS], 3, axis=1))
    small = small.at[:, SM_GATE_LANE:SM_GATE_LANE + 3 * N_HEADS].set(w[:, C_NSA_G:C_NSA_G + 3 * N_HEADS])
    w_perm = jnp.concatenate([w[:, :C_FOX_F], w[:, C_NSA_Q:C_NSA_G], small], axis=1).astype(_MXU)
    assert w_perm.shape[1] == N_IN_PAD

    wg = jnp.concatenate([_block_diag(lru_w_gates[l, 0]), _block_diag(lru_w_gates[l, 1])], axis=1).astype(_MXU)
    bg = lru_b_gates[l].reshape(1, 2 * GROUP)

    fb = _lane_vec(jnp.repeat(fox_f_bias[l], 3), 0)
    fox_gq = (jnp.tile(fox_qk_norm[l, 0], 2) * scale)[None, :]
    fox_gk = jnp.tile(fox_qk_norm[l, 1], 2)[None, :]

    nsa_gq = (jnp.tile(nsa_qk_norm[l, 0], 2) * scale)[None, :]
    nsa_gc = _lane_vec(nsa_qk_norm[l, 1], 0)
    nsa_gs = _lane_vec(nsa_qk_norm[l, 2], 0)
    nsa_gw = _lane_vec(nsa_qk_norm[l, 3], 0)
    gb = _lane_vec(nsa_gate_bias[l], SM_GATE_LANE)

    half = CMP_LEN // 2

    def pos_ext(lo):
        return jnp.concatenate([nsa_cmp_pos[l, 0, lo:lo + half], nsa_cmp_pos[l, 1, lo:lo + half]],
                               axis=1).reshape(1, half * LANES)

    def w1_ext(lo):
        ext = jnp.zeros((half, LANES, 2 * CMP_HIDDEN), _F32)
        ext = ext.at[:, :HEAD_DIM, :CMP_HIDDEN].set(nsa_cmp_w1[l, 0, lo:lo + half])
        ext = ext.at[:, HEAD_DIM:, CMP_HIDDEN:].set(nsa_cmp_w1[l, 1, lo:lo + half])
        return ext.reshape(half * LANES, 2 * CMP_HIDDEN).astype(_MXU)

    w2 = jnp.zeros((2 * CMP_HIDDEN, LANES), _F32)
    w2 = w2.at[:CMP_HIDDEN, :HEAD_DIM].set(nsa_cmp_w2[l, 0]).at[CMP_HIDDEN:, HEAD_DIM:].set(nsa_cmp_w2[l, 1])
    return dict(w_perm=w_perm, wg=wg, bg=bg, fb=fb, fox_gq=fox_gq, fox_gk=fox_gk, nsa_gq=nsa_gq, nsa_gc=nsa_gc,
                nsa_gs=nsa_gs, nsa_gw=nsa_gw, gb=gb, pa=pos_ext(0), pb=pos_ext(half), w1a=w1_ext(0),
                w1b=w1_ext(half), w2=w2.astype(_MXU), on=out_norm[l].reshape(1, 4 * GROUP))


def kernel(x, norm_mix, w_in, lru_conv_w, lru_conv_b, lru_w_gates, lru_b_gates, lru_lambda, sc_conv_w, fox_f_bias, fox_qk_norm, nsa_qk_norm, nsa_cmp_pos, nsa_cmp_w1, nsa_cmp_w2, nsa_gate_bias, rel_bias, out_norm, w_out, norm_ffn, w_gate_up, w_down):
    bsz, s_len, d_model = x.shape
    depth = w_in.shape[0]
    assert d_model == D_MODEL and s_len % TS_MIX == 0 and s_len % TQ_FOX == 0
    assert SLC_TOPK <= s_len // SLC_BLOCK <= SLC_BLOCK
    t = bsz * s_len

    win_tab, sel_tab, cmp_tab = _nsa_tables(rel_bias, s_len)
    ovl = _overlap_ext(s_len)

    for l in range(depth):
        lp = _layer_params(l, w_in, lru_w_gates, lru_b_gates, fox_f_bias, fox_qk_norm, nsa_qk_norm,
                           nsa_cmp_pos, nsa_cmp_w1, nsa_cmp_w2, nsa_gate_bias, out_norm)
        m_ab, qf, kf, vf, q4, ks, vs, kw, vw, gates, zcmp = _proj_mix(
            x, norm_mix[l][None, :], lp["w_perm"], lru_conv_w[l], lru_conv_b[l][None, :], lp["wg"], lp["bg"],
            lru_lambda[l][None, :], sc_conv_w[l], lp["on"][:, 0:2 * GROUP],
            lp["fb"], lp["fox_gq"], lp["fox_gk"], lp["nsa_gq"], lp["nsa_gs"], lp["nsa_gw"], lp["gb"])
        m_c = _fox_attn(qf, kf, vf, lp["on"][:, 2 * GROUP:3 * GROUP])
        kc, cv = _compress(zcmp, lp["pa"], lp["pb"], lp["w1a"], lp["w1b"], lp["w2"], lp["nsa_gc"])
        m_d = _nsa_attn(q4, kc, cv, ks, vs, kw, vw, gates, win_tab, sel_tab, cmp_tab, ovl,
                        lp["on"][:, 3 * GROUP:4 * GROUP])
        x = _out_ffn(x.reshape(t, D_MODEL), m_ab.reshape(t, 2 * GROUP), m_c.reshape(t, GROUP),
                     m_d.reshape(t, GROUP), w_out[l].astype(_MXU), norm_ffn[l][None, :],
                     w_gate_up[l].astype(_MXU), w_down[l].astype(_MXU)).reshape(bsz, s_len, D_MODEL)
    return x
```

```python
import math

import numpy as np
import jax
import jax.numpy as jnp
from jax import lax
from jax.experimental import pallas as pl
from jax.experimental.pallas import tpu as pltpu

D_MODEL = 1024
GROUP = 256
HEAD_DIM = 64
N_HEADS = 4
LRU_CONV = 4
LRU_C = 8.0
SC_CONV = 3
CMP_LEN = 32
CMP_STRIDE = 16
CMP_HIDDEN = 128
SLC_BLOCK = 64
SLC_TOPK = 16
WINDOW = 512
REL_BUCKETS = 32
REL_MAX_DIST = 128
D_FF = 2816
RMS_EPS = 1e-6
NEG = -1e30
BIG = 1e30
LOG2E = math.log2(math.e)

C_FOX_F = 8 * GROUP
C_NSA_Q = C_FOX_F + N_HEADS
C_NSA_G = C_NSA_Q + GROUP + 6 * HEAD_DIM

LANES = 128
SUBLANES = 8
VMEM_LIMIT_BYTES = 56 * 1024 * 1024

TS_MIX = 512
SCAN_SEG = 32
TQ_FOX = 1024
TK_FOX = 1024
KT_NSA = 128
TQ_NSA = 256
NSA_FAR = 4
TM_FFN = 512
TF_FFN = 256

SM_GATE_LANE = 16
SUM_LANE = 0

COL_FOX = 5 * GROUP
COL_NSA_Q = COL_FOX + 3 * GROUP
COL_SLABS = COL_NSA_Q + GROUP
N_IN_PAD = COL_SLABS + 4 * LANES

_MXU = jnp.bfloat16
_F32 = jnp.float32


def _params(sem):
    return pltpu.CompilerParams(dimension_semantics=sem, vmem_limit_bytes=VMEM_LIMIT_BYTES)


def _resident(shape):
    zeros = (0,) * len(shape)
    return pl.BlockSpec(shape, lambda *_: zeros, pipeline_mode=pl.Buffered(1))


def _mm(a, b):
    return jnp.dot(a.astype(_MXU), b.astype(_MXU), preferred_element_type=_F32)


def _mm_nt(a, b):
    return lax.dot_general(a.astype(_MXU), b.astype(_MXU), (((1,), (1,)), ((), ())),
                           preferred_element_type=_F32)


def _lane(shape):
    return lax.broadcasted_iota(jnp.int32, shape, len(shape) - 1)


def _row(shape):
    return lax.broadcasted_iota(jnp.int32, shape, 0)


def _rms_scale(x):
    return lax.rsqrt(jnp.mean(x * x, axis=-1, keepdims=True) + RMS_EPS)


def _gelu(x):
    c = math.sqrt(2.0 / math.pi)
    return x * (0.5 * (1.0 + jnp.tanh(c * (x + 0.044715 * (x * x * x)))))


def _round_mxu(x):
    return x.astype(_MXU).astype(_F32)


def _low_half_rms_scale(x, low):
    s = jnp.sum(jnp.where(low, x * x, 0.0), axis=-1, keepdims=True)
    return lax.rsqrt(s * (1.0 / HEAD_DIM) + RMS_EPS)


def _half_rms_scale(x, low):
    x2 = x * x
    s_lo = jnp.sum(jnp.where(low, x2, 0.0), axis=-1, keepdims=True)
    s_hi = jnp.sum(jnp.where(low, 0.0, x2), axis=-1, keepdims=True)
    return jnp.where(low, lax.rsqrt(s_lo * (1.0 / HEAD_DIM) + RMS_EPS),
                     lax.rsqrt(s_hi * (1.0 / HEAD_DIM) + RMS_EPS))


def _linear_scan_rows(a, b, h_in):
    n = a.shape[0]
    pos = _row(a.shape) & (SCAN_SEG - 1)
    d = 1
    while d < SCAN_SEG:
        keep = pos >= d
        a_prev = jnp.where(keep, pltpu.roll(a, d, axis=0), 1.0)
        b_prev = jnp.where(keep, pltpu.roll(b, d, axis=0), 0.0)
        b = a * b_prev + b
        a = a * a_prev
        d *= 2
    segs = []
    for r0 in range(0, n, SCAN_SEG):
        seg = b[r0:r0 + SCAN_SEG] + a[r0:r0 + SCAN_SEG] * h_in
        h_in = seg[SCAN_SEG - 1:SCAN_SEG]
        segs.append(seg)
    return jnp.concatenate(segs, axis=0)


def _cumsum_rows(x, c_in):
    n = x.shape[0]
    pos = _row(x.shape) & (SCAN_SEG - 1)
    d = 1
    while d < SCAN_SEG:
        x = x + jnp.where(pos >= d, pltpu.roll(x, d, axis=0), 0.0)
        d *= 2
    segs = []
    for r0 in range(0, n, SCAN_SEG):
        seg = x[r0:r0 + SCAN_SEG] + c_in
        c_in = seg[SCAN_SEG - 1:SCAN_SEG]
        segs.append(seg)
    return jnp.concatenate(segs, axis=0)


def _mix_ab(zab, cw_ref, cb_ref, wg_ref, bg_ref, lam_ref, scw_ref, on_ref, out_ref, xext_ref, uext_ref, h_ref):
    ts = zab.shape[0]
    xr = zab[:, 0:GROUP]
    gate = zab[:, GROUP:2 * GROUP]
    bgt = zab[:, 2 * GROUP:3 * GROUP]
    cgt = zab[:, 3 * GROUP:4 * GROUP]
    xs = zab[:, 4 * GROUP:5 * GROUP]

    xext_ref[SUBLANES:SUBLANES + ts, :] = xr
    xc = cb_ref[...] + cw_ref[LRU_CONV - 1:LRU_CONV, :] * xr
    for k in range(LRU_CONV - 1):
        xc = xc + cw_ref[k:k + 1, :] * xext_ref[pl.ds(SUBLANES - (LRU_CONV - 1) + k, ts), :]
    xext_ref[0:SUBLANES, :] = xr[ts - SUBLANES:ts, :]

    gi = _mm(xc, wg_ref[...]) + bg_ref[...]
    r = jax.nn.sigmoid(gi[:, 0:GROUP])
    ig = jax.nn.sigmoid(gi[:, GROUP:2 * GROUP])
    lam = lam_ref[...]
    softplus_neg = jnp.maximum(-lam, 0.0) + jnp.log1p(jnp.exp(-jnp.abs(lam)))
    log_a = (-LRU_C * softplus_neg) * r
    a = jnp.exp(log_a)
    g = -jnp.tanh(log_a) * (a * a + 1.0)
    b = jnp.where(g > 0.0, g * lax.rsqrt(g), 0.0) * (ig * xc)
    h = _linear_scan_rows(a, b, h_ref[0:1, :])
    h_ref[...] = jnp.broadcast_to(h[ts - 1:ts, :], h_ref.shape)
    y_a = h * _gelu(gate)

    u = cgt * xs
    uext_ref[SUBLANES:SUBLANES + ts, :] = u
    cv = scw_ref[SC_CONV - 1:SC_CONV, :] * u
    for k in range(SC_CONV - 1):
        cv = cv + scw_ref[k:k + 1, :] * uext_ref[pl.ds(SUBLANES - (SC_CONV - 1) + k, ts), :]
    uext_ref[0:SUBLANES, :] = u[ts - SUBLANES:ts, :]
    y_b = bgt * cv

    out_ref[0, :, 0:GROUP] = (y_a * _rms_scale(y_a) * on_ref[:, 0:GROUP]).astype(out_ref.dtype)
    out_ref[0, :, GROUP:2 * GROUP] = (y_b * _rms_scale(y_b) * on_ref[:, GROUP:2 * GROUP]).astype(out_ref.dtype)


def _fox_prep(zfox, zsm, fb_ref, gq_ref, gk_ref, q_ref, k_ref, v_ref, c_ref):
    ts = zfox.shape[0]
    shape = (ts, LANES)
    lane = _lane(shape)
    low = lane < HEAD_DIM

    f = zsm + fb_ref[...]
    logf = jnp.minimum(f, 0.0) - jnp.log1p(jnp.exp(-jnp.abs(f)))
    c = _cumsum_rows(logf, c_ref[0:1, :])
    c_ref[...] = jnp.broadcast_to(c[ts - 1:ts, :], c_ref.shape)
    neg_c = -LOG2E * c
    p1 = _round_mxu(neg_c)
    r1 = neg_c - p1
    p2 = _round_mxu(r1)
    p3 = _round_mxu(r1 - p2)
    piece = lane - 3 * ((lane * 11) >> 5)
    csel = jnp.where(piece == 0, p1, jnp.where(piece == 1, p2, p3))

    ones_aug = jnp.where(lane < HEAD_DIM + 3, 1.0, 0.0)
    for p in range(N_HEADS // 2):
        qs = zfox[:, LANES * p:LANES * (p + 1)]
        ks = zfox[:, GROUP + LANES * p:GROUP + LANES * (p + 1)]
        qn = qs * _half_rms_scale(qs, low) * gq_ref[...]
        kn = ks * _half_rms_scale(ks, low) * gk_ref[...]
        for half in range(2):
            h = 2 * p + half
            qh = qn if half == 0 else pltpu.roll(qn, HEAD_DIM, axis=1)
            kh = kn if half == 0 else pltpu.roll(kn, HEAD_DIM, axis=1)
            ch = pltpu.roll(csel, HEAD_DIM - 3 * h, axis=1)
            q_ref[0, h] = jnp.where(low, qh, ones_aug).astype(q_ref.dtype)
            k_ref[0, h] = jnp.where(low, kh, jnp.where(lane < HEAD_DIM + 3, ch, 0.0)).astype(k_ref.dtype)
    v_ref[0] = zfox[:, 2 * GROUP:3 * GROUP].astype(v_ref.dtype)


def _nsa_prep(zq, zsel, zwin, zsm, row0, gq_ref, gs_ref, gw_ref, gb_ref,
              q4_ref, ks_ref, vs_ref, kw_ref, vw_ref, gate_ref):
    ts = zq.shape[0]
    shape = (ts, LANES)
    lane = _lane(shape)
    low = lane < HEAD_DIM
    for p in range(N_HEADS // 2):
        qs = zq[:, LANES * p:LANES * (p + 1)]
        qn = qs * _half_rms_scale(qs, low) * gq_ref[...]
        q4_ref[0, 2 * p] = jnp.where(low, qn, 0.0).astype(q4_ref.dtype)
        q4_ref[0, 2 * p + 1] = jnp.where(low, pltpu.roll(qn, HEAD_DIM, axis=1), 0.0).astype(q4_ref.dtype)

    blk = (row0 + _row(shape)) >> 6
    onehot = jnp.where(lane - HEAD_DIM == blk, 1.0, 0.0)
    ks_ref[0] = jnp.where(low, zsel * _low_half_rms_scale(zsel, low) * gs_ref[...], onehot).astype(ks_ref.dtype)
    vs_ref[0] = jnp.where(lane == SUM_LANE, 1.0, zsel).astype(vs_ref.dtype)
    kw_ref[0] = jnp.where(low, zwin * _low_half_rms_scale(zwin, low) * gw_ref[...], 0.0).astype(kw_ref.dtype)
    vw_ref[0] = jnp.where(lane == SUM_LANE, 1.0, zwin).astype(vw_ref.dtype)
    gate_ref[0] = jax.nn.sigmoid(zsm + gb_ref[...])


def _proj_mix_kernel(x_ref, g_ref, w_ref,
                     cw_ref, cb_ref, wg_ref, bg_ref, lam_ref, scw_ref, on_ref,
                     fb_ref, fgq_ref, fgk_ref, ngq_ref, ngs_ref, ngw_ref, gb_ref,
                     mab_ref, qf_ref, kf_ref, vf_ref, q4_ref, ks_ref, vs_ref, kw_ref, vw_ref, gate_ref, zcmp_ref,
                     xext_ref, uext_ref, h_ref, c_ref, zc_ref):
    ts = x_ref.shape[1]
    ti = pl.program_id(1)

    @pl.when(ti == 0)
    def _():
        xext_ref[0:SUBLANES, :] = jnp.zeros((SUBLANES, GROUP), _F32)
        uext_ref[0:SUBLANES, :] = jnp.zeros((SUBLANES, GROUP), _F32)
        h_ref[...] = jnp.zeros_like(h_ref)
        c_ref[...] = jnp.zeros_like(c_ref)

    x = x_ref[0]
    xn = (x * _rms_scale(x) * g_ref[...]).astype(_MXU)
    zab = jnp.dot(xn, w_ref[:, 0:COL_FOX], preferred_element_type=_F32)
    slabs = jnp.dot(xn, w_ref[:, COL_SLABS:N_IN_PAD], preferred_element_type=_F32)
    zfox = jnp.dot(xn, w_ref[:, COL_FOX:COL_NSA_Q], preferred_element_type=_F32)
    zq = jnp.dot(xn, w_ref[:, COL_NSA_Q:COL_SLABS], preferred_element_type=_F32)
    zc_ref[...] = slabs[:, 0:LANES]
    for tok in range(CMP_STRIDE):
        zcmp_ref[0, :, tok * LANES:(tok + 1) * LANES] = zc_ref[pl.ds(tok, ts // CMP_STRIDE, stride=CMP_STRIDE), :]
    zsel = slabs[:, LANES:2 * LANES]
    zwin = slabs[:, 2 * LANES:3 * LANES]
    zsm = slabs[:, 3 * LANES:4 * LANES]
    _fox_prep(zfox, zsm, fb_ref, fgq_ref, fgk_ref, qf_ref, kf_ref, vf_ref, c_ref)
    _nsa_prep(zq, zsel, zwin, zsm, ti * ts, ngq_ref, ngs_ref, ngw_ref, gb_ref,
              q4_ref, ks_ref, vs_ref, kw_ref, vw_ref, gate_ref)
    _mix_ab(zab, cw_ref, cb_ref, wg_ref, bg_ref, lam_ref, scw_ref, on_ref, mab_ref, xext_ref, uext_ref, h_ref)


def _proj_mix(x, gain, w, cw, cb, wg, bg, lam, scw, on_ab, fb, fgq, fgk, ngq, ngs, ngw, gb):
    bsz, s_len, _ = x.shape
    ts = TS_MIX
    vec = lambda width: pl.BlockSpec((1, width), lambda b, t: (0, 0))
    rows = lambda width: pl.BlockSpec((1, ts, width), lambda b, t: (b, t, 0))
    heads = pl.BlockSpec((1, N_HEADS, ts, LANES), lambda b, t: (b, 0, t, 0))
    sds = jax.ShapeDtypeStruct
    return pl.pallas_call(
        _proj_mix_kernel,
        grid=(bsz, s_len // ts),
        in_specs=[rows(D_MODEL), vec(D_MODEL), _resident((D_MODEL, N_IN_PAD)),
                  pl.BlockSpec((LRU_CONV, GROUP), lambda b, t: (0, 0)), vec(GROUP), _resident((GROUP, 2 * GROUP)),
                  vec(2 * GROUP), vec(GROUP), pl.BlockSpec((SC_CONV, GROUP), lambda b, t: (0, 0)), vec(2 * GROUP),
                  vec(LANES), vec(LANES), vec(LANES), vec(LANES), vec(LANES), vec(LANES), vec(LANES)],
        out_specs=[rows(2 * GROUP), heads, heads, rows(GROUP), heads, rows(LANES), rows(LANES),
                   rows(LANES), rows(LANES), rows(LANES),
                   pl.BlockSpec((1, ts // CMP_STRIDE, CMP_STRIDE * LANES), lambda b, t: (b, t, 0))],
        out_shape=[sds((bsz, s_len, 2 * GROUP), _MXU),
                   sds((bsz, N_HEADS, s_len, LANES), _MXU), sds((bsz, N_HEADS, s_len, LANES), _MXU),
                   sds((bsz, s_len, GROUP), _MXU),
                   sds((bsz, N_HEADS, s_len, LANES), _MXU),
                   sds((bsz, s_len, LANES), _MXU), sds((bsz, s_len, LANES), _MXU),
                   sds((bsz, s_len, LANES), _MXU), sds((bsz, s_len, LANES), _MXU),
                   sds((bsz, s_len, LANES), _F32), sds((bsz, s_len // CMP_STRIDE, CMP_STRIDE * LANES), _F32)],
        scratch_shapes=[pltpu.VMEM((ts + SUBLANES, GROUP), _F32), pltpu.VMEM((ts + SUBLANES, GROUP), _F32),
                        pltpu.VMEM((SUBLANES, GROUP), _F32), pltpu.VMEM((SUBLANES, LANES), _F32),
                        pltpu.VMEM((ts, LANES), _F32)],
        compiler_params=_params(("parallel", "arbitrary")),
        name="proj_mix",
    )(x, gain, w, cw, cb, wg, bg, lam, scw, on_ab, fb, fgq, fgk, ngq, ngs, ngw, gb)


def _flash_step(s, v, carry):
    m, l, acc = carry
    m_new = jnp.maximum(m, jnp.max(s, axis=-1, keepdims=True))
    alpha = jnp.exp2(m - m_new)
    p = jnp.exp2(s - m_new)
    l = alpha * l + jnp.sum(p, axis=-1, keepdims=True)
    acc = alpha * acc + jnp.dot(p.astype(_MXU), v, preferred_element_type=_F32)
    return m_new, l, acc


def _flash_init(rows):
    return (jnp.full((rows, 1), NEG, _F32), jnp.zeros((rows, 1), _F32), jnp.zeros((rows, LANES), _F32))


def _flash_step_sumlane(s, v, carry):
    m, acc = carry
    m_new = jnp.maximum(m, jnp.max(s, axis=-1, keepdims=True))
    p = jnp.exp2(s - m_new).astype(_MXU)
    acc = jnp.exp2(m - m_new) * acc + jnp.dot(p, v, preferred_element_type=_F32)
    return m_new, acc


def _flash_init_sumlane(rows):
    return (jnp.full((rows, 1), NEG, _F32), jnp.zeros((rows, LANES), _F32))


def _sumlane_normalize(acc, sum_lane):
    return acc / acc[:, sum_lane:sum_lane + 1]


def _fox_attn_kernel(q_ref, k_ref, v_ref, on_ref, out_ref):
    tq = q_ref.shape[2]
    qi = pl.program_id(1)
    qs = [q_ref[0, h] for h in range(N_HEADS)]

    def chunk(start, width, carries, causal=False):
        new = []
        for h in range(N_HEADS):
            s = _mm_nt(qs[h], k_ref[0, h, pl.ds(start, width), :])
            if causal:
                s = jnp.where(_lane((tq, width)) <= _row((tq, width)), s, NEG)
            pair = h // 2
            new.append(_flash_step(s, v_ref[0, pl.ds(start, width), LANES * pair:LANES * (pair + 1)], carries[h]))
        return tuple(new)

    per_wide = TK_FOX // tq
    n_wide = qi // per_wide
    carries = lax.fori_loop(0, n_wide, lambda c, cr: chunk(pl.multiple_of(c * TK_FOX, TK_FOX), TK_FOX, cr),
                            tuple(_flash_init(tq) for _ in range(N_HEADS)))
    carries = lax.fori_loop(n_wide * per_wide, qi, lambda c, cr: chunk(pl.multiple_of(c * tq, tq), tq, cr), carries)
    carries = chunk(pl.multiple_of(qi * tq, tq), tq, carries, causal=True)
    outs = [acc / l for _, l, acc in carries]
    low = _lane((tq, LANES)) < HEAD_DIM
    y = jnp.concatenate([jnp.where(low, outs[2 * p], outs[2 * p + 1]) for p in range(N_HEADS // 2)], axis=1)
    out_ref[0] = (y * _rms_scale(y) * on_ref[...]).astype(out_ref.dtype)


def _fox_attn(qf, kf, vf, on):
    bsz, _, s_len, _ = qf.shape
    tq = min(TQ_FOX, s_len)
    assert s_len % TK_FOX == 0 and TK_FOX % tq == 0
    return pl.pallas_call(
        _fox_attn_kernel,
        grid=(bsz, s_len // tq),
        in_specs=[pl.BlockSpec((1, N_HEADS, tq, LANES), lambda b, i: (b, 0, i, 0)),
                  pl.BlockSpec((1, N_HEADS, s_len, LANES), lambda b, i: (b, 0, 0, 0)),
                  pl.BlockSpec((1, s_len, GROUP), lambda b, i: (b, 0, 0)),
                  pl.BlockSpec((1, GROUP), lambda b, i: (0, 0))],
        out_specs=pl.BlockSpec((1, tq, GROUP), lambda b, i: (b, i, 0)),
        out_shape=jax.ShapeDtypeStruct((bsz, s_len, GROUP), _MXU),
        compiler_params=_params(("parallel", "arbitrary")),
        name="fox_attn",
    )(qf, kf, vf, on)


def _compress_kernel(x_ref, pa_ref, pb_ref, w1a_ref, w1b_ref, w2_ref, g_ref, kc_ref, cv_ref):
    nc = x_ref.shape[1]
    x = x_ref[0]
    ua = _mm(x + pa_ref[...], w1a_ref[...])
    ub = _mm(x + pb_ref[...], w1b_ref[...])
    hid = _gelu(ua + pltpu.roll(ub, nc - 1, axis=0))
    out = _mm(hid, w2_ref[...])
    low = _lane(out.shape) < HEAD_DIM
    kc_ref[0] = jnp.where(low, out * _low_half_rms_scale(out, low) * g_ref[...], 0.0).astype(kc_ref.dtype)
    cv_ref[0] = out.astype(cv_ref.dtype)


def _compress(xc, pa, pb, w1a, w1b, w2, g):
    bsz, nc, width = xc.shape
    const = lambda shape: pl.BlockSpec(shape, lambda b: (0, 0))
    slab = pl.BlockSpec((1, nc, LANES), lambda b: (b, 0, 0))
    return pl.pallas_call(
        _compress_kernel,
        grid=(bsz,),
        in_specs=[pl.BlockSpec((1, nc, width), lambda b: (b, 0, 0)),
                  const((1, width)), const((1, width)), const((width, 2 * CMP_HIDDEN)),
                  const((width, 2 * CMP_HIDDEN)), const((2 * CMP_HIDDEN, LANES)), const((1, LANES))],
        out_specs=[slab, slab],
        out_shape=[jax.ShapeDtypeStruct((bsz, nc, LANES), _MXU)] * 2,
        compiler_params=_params(("parallel",)),
        name="compress",
    )(xc, pa, pb, w1a, w1b, w2, g)


def _bucket_thresholds():
    max_exact = REL_BUCKETS // 2
    d = np.arange(0, REL_MAX_DIST + 1)
    large = max_exact + (np.log(np.maximum(d, 1).astype(np.float32) / max_exact)
                         / math.log(REL_MAX_DIST / max_exact) * (REL_BUCKETS - max_exact)).astype(np.int32)
    bucket = np.where(d < max_exact, d, np.minimum(large, REL_BUCKETS - 1))
    assert bucket[-1] == REL_BUCKETS - 1 and np.all(np.diff(bucket) >= 0)
    return [int(np.argmax(bucket >= k)) for k in range(REL_BUCKETS)]


_BUCKET_THR = _bucket_thresholds()

NSA_R = TQ_NSA // KT_NSA
NSA_D = WINDOW // KT_NSA
NSA_TAIL = NSA_D + NSA_R
WIN_ENTRIES = NSA_TAIL + 1
SEL_ENTRIES = NSA_R + 3


def _win_entry(e):
    return jnp.minimum(e, NSA_TAIL)


def _sel_entry(e):
    return jnp.clip(e - (NSA_D - 2), 0, NSA_R + 2)


def _rel_bias(dist, rb_ref, h):
    far = rb_ref[REL_BUCKETS - 1, h]
    val = jnp.full(dist.shape, LOG2E * (rb_ref[0, h] - far), _F32)
    for k in range(1, REL_BUCKETS):
        val = jnp.where(dist >= _BUCKET_THR[k], LOG2E * (rb_ref[k, h] - far), val)
    return val


def _nsa_tables_kernel(rb_ref, win_ref, sel_ref, cmp_ref):
    qi = pl.program_id(0)
    tq = TQ_NSA
    nc = cmp_ref.shape[2]

    @pl.when(qi == 0)
    def _():
        i = _row((tq, KT_NSA))
        c = _lane((tq, KT_NSA))
        for h in range(N_HEADS):
            rs = slice(h * tq, (h + 1) * tq)
            for e in range(NSA_TAIL):
                dist = i + (NSA_D - e) * KT_NSA - c
                ok = jnp.where(dist >= 0, jnp.where(dist < WINDOW, 1, 0), 0) > 0
                tab = jnp.where(ok, _rel_bias(dist, rb_ref, h), NEG)
                win_ref[e, rs, :] = tab
                if e >= NSA_D - 1:
                    sel_ref[e - (NSA_D - 2), rs, :] = tab
            win_ref[NSA_TAIL, rs, :] = jnp.full((tq, KT_NSA), NEG, _F32)
            sel_ref[0, rs, :] = jnp.zeros((tq, KT_NSA), _F32)
            sel_ref[NSA_R + 2, rs, :] = jnp.full((tq, KT_NSA), NEG, _F32)

    t = qi * tq + _row((tq, nc))
    dist = t - (CMP_STRIDE * _lane((tq, nc)) + CMP_LEN - 1)
    for h in range(N_HEADS):
        cmp_ref[0, h * tq:(h + 1) * tq, :] = jnp.where(dist >= 0, _rel_bias(dist, rb_ref, h), NEG)


def _nsa_tables(rel_bias, s_len):
    nq = s_len // TQ_NSA
    nc = s_len // CMP_STRIDE
    rows = N_HEADS * TQ_NSA
    return pl.pallas_call(
        _nsa_tables_kernel,
        grid=(nq,),
        in_specs=[pl.BlockSpec(memory_space=pltpu.SMEM)],
        out_specs=[pl.BlockSpec((WIN_ENTRIES, rows, KT_NSA), lambda i: (0, 0, 0)),
                   pl.BlockSpec((SEL_ENTRIES, rows, KT_NSA), lambda i: (0, 0, 0)),
                   pl.BlockSpec((1, rows, nc), lambda i: (i, 0, 0))],
        out_shape=[jax.ShapeDtypeStruct((WIN_ENTRIES, rows, KT_NSA), _F32),
                   jax.ShapeDtypeStruct((SEL_ENTRIES, rows, KT_NSA), _F32),
                   jax.ShapeDtypeStruct((nq, rows, nc), _F32)],
        compiler_params=_params(("arbitrary",)),
        name="nsa_tables",
    )(rel_bias)


def _topk_penalty(imp_t, qi, tq):
    n_blk = imp_t.shape[0]
    blk = _row((n_blk, tq))
    cur = (qi * tq + _lane((n_blk, tq))) >> 6
    forced = jnp.where(blk == 0, 1, jnp.where(blk == cur, 1, jnp.where(blk == cur - 1, 1, 0))) > 0
    val = jnp.where(blk <= cur, jnp.where(forced, BIG, imp_t), NEG)
    groups = [val[g * SUBLANES:(g + 1) * SUBLANES, :] for g in range(n_blk // SUBLANES)]
    sub = _row((SUBLANES, tq))
    ranks = [jnp.zeros((SUBLANES, tq), _F32) for _ in groups]
    for j in range(n_blk):
        vj = val[j:j + 1, :]
        jg, jr = divmod(j, SUBLANES)
        for g, vg in enumerate(groups):
            ge = jnp.where(vj >= vg, 1.0, 0.0)
            gt = jnp.where(vj > vg, 1.0, 0.0)
            if g > jg:
                ranks[g] = ranks[g] + ge
            elif g < jg:
                ranks[g] = ranks[g] + gt
            else:
                ranks[g] = ranks[g] + jnp.where(sub > jr, ge, gt)
    rank = jnp.concatenate(ranks, axis=0)
    return jnp.where(rank < float(SLC_TOPK), 0.0, NEG)


def _nsa_attn_kernel(q4_ref, kc_ref, cv_ref, ks_ref, vs_ref, kw_ref, vw_ref, gate_ref,
                     win_ref, sel_ref, cmp_ref, ovl_ref, on_ref, out_ref):
    tq = TQ_NSA
    kt = KT_NSA
    rows = N_HEADS * tq
    qi = pl.program_id(1)
    first = qi * NSA_R
    q4 = q4_ref[0].reshape(rows, LANES)
    lane = _lane((tq, LANES))
    low = lane < HEAD_DIM
    tail_w = NSA_TAIL * kt

    def tail_table(tab_ref, entry_of, first_tile, skip_before=0):
        pieces = []
        for u in range(NSA_TAIL):
            tile = first_tile + u
            entry = entry_of(tile - first + NSA_D)
            pieces.append(tab_ref[jnp.where(tile < skip_before, tab_ref.shape[0] - 1, entry)])
        return jnp.concatenate(pieces, axis=1)

    s = _mm_nt(q4, kc_ref[0]) + cmp_ref[0]
    valid = s > 0.5 * NEG
    m = jnp.max(s, axis=-1, keepdims=True)
    p = jnp.where(valid, jnp.exp2(s - m), 0.0)
    l = jnp.sum(p, axis=-1, keepdims=True)
    p = p / jnp.where(l > 0.0, l, 1.0)
    o_c = jnp.dot(p.astype(_MXU), cv_ref[0], preferred_element_type=_F32)

    psum = p[0:tq] + p[tq:2 * tq] + p[2 * tq:3 * tq] + p[3 * tq:4 * tq]
    p_hi = _round_mxu(psum)
    imp = _mm(p_hi, ovl_ref[...]) + _mm(psum - p_hi, ovl_ref[...])

    pen_t = _topk_penalty(imp.T[0:SLC_BLOCK, :], qi, tq)
    pen = jnp.concatenate([jnp.zeros((LANES - SLC_BLOCK, tq), _F32), pen_t], axis=0).T
    pen4 = jnp.concatenate([pen] * N_HEADS, axis=0)
    q_aug = jnp.where(_lane((rows, LANES)) < HEAD_DIM, q4.astype(_F32), pen4).astype(_MXU)

    w_tile = jnp.maximum(first - NSA_D, 0)
    w_start = pl.multiple_of(w_tile * kt, kt)
    s = _mm_nt(q4, kw_ref[0, pl.ds(w_start, tail_w), :]) + tail_table(win_ref, _win_entry, w_tile)
    _, acc_w = _flash_step_sumlane(s, vw_ref[0, pl.ds(w_start, tail_w), :], _flash_init_sumlane(rows))
    o_w = _sumlane_normalize(acc_w, SUM_LANE)

    def sel_far(width):
        def step(c, carry):
            start = pl.multiple_of(c * width, width)
            s = _mm_nt(q_aug, ks_ref[0, pl.ds(start, width), :])
            return _flash_step_sumlane(s, vs_ref[0, pl.ds(start, width), :], carry)
        return step

    far_w = NSA_FAR * kt
    n_far = jnp.maximum(first - 1, 0) // NSA_FAR
    n_dbl = n_far // 2
    carry = lax.fori_loop(0, n_dbl, sel_far(2 * far_w), _flash_init_sumlane(rows))
    carry = lax.fori_loop(2 * n_dbl, n_far, sel_far(far_w), carry)
    done = n_far * NSA_FAR
    t_tile = jnp.minimum(done, ks_ref.shape[1] // kt - NSA_TAIL)
    t_start = pl.multiple_of(t_tile * kt, kt)
    s = _mm_nt(q_aug, ks_ref[0, pl.ds(t_start, tail_w), :]) + tail_table(sel_ref, _sel_entry, t_tile, done)
    _, acc_s = _flash_step_sumlane(s, vs_ref[0, pl.ds(t_start, tail_w), :], carry)
    o_s = _sumlane_normalize(acc_s, SUM_LANE)

    g = gate_ref[0]
    heads = []
    for h in range(N_HEADS):
        rs = slice(h * tq, (h + 1) * tq)
        gc = g[:, SM_GATE_LANE + h:SM_GATE_LANE + h + 1]
        gs = g[:, SM_GATE_LANE + N_HEADS + h:SM_GATE_LANE + N_HEADS + h + 1]
        gw = g[:, SM_GATE_LANE + 2 * N_HEADS + h:SM_GATE_LANE + 2 * N_HEADS + h + 1]
        heads.append(gc * o_c[rs] + gs * o_s[rs] + gw * o_w[rs])
    slabs = [jnp.where(low, pltpu.roll(heads[2 * p], HEAD_DIM, axis=1), heads[2 * p + 1])
             for p in range(N_HEADS // 2)]
    y = jnp.concatenate(slabs, axis=1)
    out_ref[0] = (y * _rms_scale(y) * on_ref[...]).astype(out_ref.dtype)


def _nsa_attn(q4, kc, cv, ks, vs, kw, vw, gates, win_tab, sel_tab, cmp_tab, ovl, on):
    bsz, _, s_len, _ = q4.shape
    tq = TQ_NSA
    nc = kc.shape[1]
    rows = N_HEADS * tq
    assert s_len >= NSA_TAIL * KT_NSA
    full = lambda n: pl.BlockSpec((1, n, LANES), lambda b, i: (b, 0, 0))
    return pl.pallas_call(
        _nsa_attn_kernel,
        grid=(bsz, s_len // tq),
        in_specs=[pl.BlockSpec((1, N_HEADS, tq, LANES), lambda b, i: (b, 0, i, 0)),
                  full(nc), full(nc), full(s_len), full(s_len), full(s_len), full(s_len),
                  pl.BlockSpec((1, tq, LANES), lambda b, i: (b, i, 0)),
                  _resident((WIN_ENTRIES, rows, KT_NSA)), _resident((SEL_ENTRIES, rows, KT_NSA)),
                  pl.BlockSpec((1, rows, nc), lambda b, i: (i, 0, 0)),
                  _resident((nc, LANES)),
                  pl.BlockSpec((1, GROUP), lambda b, i: (0, 0))],
        out_specs=pl.BlockSpec((1, tq, GROUP), lambda b, i: (b, i, 0)),
        out_shape=jax.ShapeDtypeStruct((bsz, s_len, GROUP), _MXU),
        compiler_params=_params(("parallel", "arbitrary")),
        name="nsa_attn",
    )(q4, kc, cv, ks, vs, kw, vw, gates, win_tab, sel_tab, cmp_tab, ovl, on)


def _ffn_kernel(x_ref, mab_ref, mc_ref, md_ref, wo_ref, g_ref, wgu_ref, wd_ref, out_ref):
    x1 = (x_ref[...]
          + jnp.dot(mab_ref[...], wo_ref[0:2 * GROUP, :], preferred_element_type=_F32)
          + jnp.dot(mc_ref[...], wo_ref[2 * GROUP:3 * GROUP, :], preferred_element_type=_F32)
          + jnp.dot(md_ref[...], wo_ref[3 * GROUP:4 * GROUP, :], preferred_element_type=_F32))
    xn = (x1 * _rms_scale(x1) * g_ref[...]).astype(_MXU)
    out_ref[...] = x1
    for j in range(D_FF // TF_FFN):
        cols = slice(j * TF_FFN, (j + 1) * TF_FFN)
        gt = jnp.dot(xn, wgu_ref[:, cols], preferred_element_type=_F32)
        up = jnp.dot(xn, wgu_ref[:, D_FF + j * TF_FFN:D_FF + (j + 1) * TF_FFN], preferred_element_type=_F32)
        hid = (gt * jax.nn.sigmoid(gt)) * up
        out_ref[...] += jnp.dot(hid.astype(_MXU), wd_ref[cols, :], preferred_element_type=_F32)


def _out_ffn(x2d, mab, mc, md, wo, g, wgu, wd):
    t = x2d.shape[0]
    tm = min(TM_FFN, t)
    assert t % tm == 0 and D_FF % TF_FFN == 0
    row = lambda width: pl.BlockSpec((tm, width), lambda i: (i, 0))
    return pl.pallas_call(
        _ffn_kernel,
        grid=(t // tm,),
        in_specs=[row(D_MODEL), row(2 * GROUP), row(GROUP), row(GROUP),
                  pl.BlockSpec((D_MODEL, D_MODEL), lambda i: (0, 0)), pl.BlockSpec((1, D_MODEL), lambda i: (0, 0)),
                  pl.BlockSpec((D_MODEL, 2 * D_FF), lambda i: (0, 0)), pl.BlockSpec((D_FF, D_MODEL), lambda i: (0, 0))],
        out_specs=row(D_MODEL),
        out_shape=jax.ShapeDtypeStruct((t, D_MODEL), _F32),
        compiler_params=_params(("parallel",)),
        name="out_ffn",
    )(x2d, mab, mc, md, wo, g, wgu, wd)


def _lane_vec(values, start):
    v = jnp.zeros((LANES,), _F32).at[start:start + values.shape[0]].set(values.astype(_F32))
    return v[None, :]


def _block_diag(w):
    h, d, _ = w.shape
    eye = jnp.eye(h, dtype=w.dtype)
    return (eye[:, None, :, None] * w[:, :, None, :]).reshape(h * d, h * d)


def _overlap_ext(s_len):
    nc = s_len // CMP_STRIDE
    n_cmp = nc - 1
    n_slc = s_len // SLC_BLOCK
    cs = np.arange(n_cmp)[:, None] * CMP_STRIDE
    ss = np.arange(n_slc)[None, :] * SLC_BLOCK
    ov = np.clip(np.minimum(cs + CMP_LEN, ss + SLC_BLOCK) - np.maximum(cs, ss), 0, CMP_LEN)
    ext = np.zeros((nc, LANES), np.float32)
    ext[:n_cmp, :n_slc] = ov
    return jnp.asarray(ext, _MXU)


def _layer_params(l, w_in, lru_w_gates, lru_b_gates, fox_f_bias, fox_qk_norm, nsa_qk_norm, nsa_cmp_pos,
                  nsa_cmp_w1, nsa_cmp_w2, nsa_gate_bias, out_norm):
    scale = HEAD_DIM ** -0.5 * LOG2E
    w = w_in[l]
    small = jnp.zeros((D_MODEL, LANES), w.dtype)
    small = small.at[:, 0:3 * N_HEADS].set(jnp.repeat(w[:, C_FOX_F:C_FOX_F + N_HEADS], 3, axis=1))
    small = small.at[:, SM_GATE_LANE:SM_GATE_LANE + 3 * N_HEADS].set(w[:, C_NSA_G:C_NSA_G + 3 * N_HEADS])
    w_perm = jnp.concatenate([w[:, :C_FOX_F], w[:, C_NSA_Q:C_NSA_G], small], axis=1).astype(_MXU)
    assert w_perm.shape[1] == N_IN_PAD

    wg = jnp.concatenate([_block_diag(lru_w_gates[l, 0]), _block_diag(lru_w_gates[l, 1])], axis=1).astype(_MXU)
    bg = lru_b_gates[l].reshape(1, 2 * GROUP)

    fb = _lane_vec(jnp.repeat(fox_f_bias[l], 3), 0)
    fox_gq = (jnp.tile(fox_qk_norm[l, 0], 2) * scale)[None, :]
    fox_gk = jnp.tile(fox_qk_norm[l, 1], 2)[None, :]

    nsa_gq = (jnp.tile(nsa_qk_norm[l, 0], 2) * scale)[None, :]
    nsa_gc = _lane_vec(nsa_qk_norm[l, 1], 0)
    nsa_gs = _lane_vec(nsa_qk_norm[l, 2], 0)
    nsa_gw = _lane_vec(nsa_qk_norm[l, 3], 0)
    gb = _lane_vec(nsa_gate_bias[l], SM_GATE_LANE)

    half = CMP_LEN // 2

    def pos_ext(lo):
        return jnp.concatenate([nsa_cmp_pos[l, 0, lo:lo + half], nsa_cmp_pos[l, 1, lo:lo + half]],
                               axis=1).reshape(1, half * LANES)

    def w1_ext(lo):
        ext = jnp.zeros((half, LANES, 2 * CMP_HIDDEN), _F32)
        ext = ext.at[:, :HEAD_DIM, :CMP_HIDDEN].set(nsa_cmp_w1[l, 0, lo:lo + half])
        ext = ext.at[:, HEAD_DIM:, CMP_HIDDEN:].set(nsa_cmp_w1[l, 1, lo:lo + half])
        return ext.reshape(half * LANES, 2 * CMP_HIDDEN).astype(_MXU)

    w2 = jnp.zeros((2 * CMP_HIDDEN, LANES), _F32)
    w2 = w2.at[:CMP_HIDDEN, :HEAD_DIM].set(nsa_cmp_w2[l, 0]).at[CMP_HIDDEN:, HEAD_DIM:].set(nsa_cmp_w2[l, 1])
    return dict(w_perm=w_perm, wg=wg, bg=bg, fb=fb, fox_gq=fox_gq, fox_gk=fox_gk, nsa_gq=nsa_gq, nsa_gc=nsa_gc,
                nsa_gs=nsa_gs, nsa_gw=nsa_gw, gb=gb, pa=pos_ext(0), pb=pos_ext(half), w1a=w1_ext(0),
                w1b=w1_ext(half), w2=w2.astype(_MXU), on=out_norm[l].reshape(1, 4 * GROUP))


def kernel(x, norm_mix, w_in, lru_conv_w, lru_conv_b, lru_w_gates, lru_b_gates, lru_lambda, sc_conv_w, fox_f_bias, fox_qk_norm, nsa_qk_norm, nsa_cmp_pos, nsa_cmp_w1, nsa_cmp_w2, nsa_gate_bias, rel_bias, out_norm, w_out, norm_ffn, w_gate_up, w_down):
    bsz, s_len, d_model = x.shape
    depth = w_in.shape[0]
    assert d_model == D_MODEL and s_len % TS_MIX == 0 and s_len % TQ_FOX == 0
    assert SLC_TOPK <= s_len // SLC_BLOCK <= SLC_BLOCK
    t = bsz * s_len

    win_tab, sel_tab, cmp_tab = _nsa_tables(rel_bias, s_len)
    ovl = _overlap_ext(s_len)

    for l in range(depth):
        lp = _layer_params(l, w_in, lru_w_gates, lru_b_gates, fox_f_bias, fox_qk_norm, nsa_qk_norm,
                           nsa_cmp_pos, nsa_cmp_w1, nsa_cmp_w2, nsa_gate_bias, out_norm)
        m_ab, qf, kf, vf, q4, ks, vs, kw, vw, gates, zcmp = _proj_mix(
            x, norm_mix[l][None, :], lp["w_perm"], lru_conv_w[l], lru_conv_b[l][None, :], lp["wg"], lp["bg"],
            lru_lambda[l][None, :], sc_conv_w[l], lp["on"][:, 0:2 * GROUP],
            lp["fb"], lp["fox_gq"], lp["fox_gk"], lp["nsa_gq"], lp["nsa_gs"], lp["nsa_gw"], lp["gb"])
        m_c = _fox_attn(qf, kf, vf, lp["on"][:, 2 * GROUP:3 * GROUP])
        kc, cv = _compress(zcmp, lp["pa"], lp["pb"], lp["w1a"], lp["w1b"], lp["w2"], lp["nsa_gc"])
        m_d = _nsa_attn(q4, kc, cv, ks, vs, kw, vw, gates, win_tab, sel_tab, cmp_tab, ovl,
                        lp["on"][:, 3 * GROUP:4 * GROUP])
        x = _out_ffn(x.reshape(t, D_MODEL), m_ab.reshape(t, 2 * GROUP), m_c.reshape(t, GROUP),
                     m_d.reshape(t, GROUP), w_out[l].astype(_MXU), norm_ffn[l][None, :],
                     w_gate_up[l].astype(_MXU), w_down[l].astype(_MXU)).reshape(bsz, s_len, D_MODEL)
    return x
```

```python
import math

import numpy as np
import jax
import jax.numpy as jnp
from jax import lax
from jax.experimental import pallas as pl
from jax.experimental.pallas import tpu as pltpu

D_MODEL = 1024
GROUP = 256
HEAD_DIM = 64
N_HEADS = 4
LRU_CONV = 4
LRU_C = 8.0
SC_CONV = 3
CMP_LEN = 32
CMP_STRIDE = 16
CMP_HIDDEN = 128
SLC_BLOCK = 64
SLC_TOPK = 16
WINDOW = 512
REL_BUCKETS = 32
REL_MAX_DIST = 128
D_FF = 2816
RMS_EPS = 1e-6
NEG = -1e30
BIG = 1e30
LOG2E = math.log2(math.e)

C_FOX_F = 8 * GROUP
C_NSA_Q = C_FOX_F + N_HEADS
C_NSA_G = C_NSA_Q + GROUP + 6 * HEAD_DIM

LANES = 128
SUBLANES = 8
VMEM_LIMIT_BYTES = 56 * 1024 * 1024

TS_MIX = 512
SCAN_SEG = 32
TQ_FOX = 1024
TK_FOX = 1024
KT_NSA = 128
TQ_NSA = 256
NSA_FAR = 4
TM_FFN = 512
TF_FFN = 256

SM_GATE_LANE = 16
SUM_LANE = 0
FOX_SUM_LANE = HEAD_DIM

COL_FOX = 5 * GROUP
COL_NSA_Q = COL_FOX + 2 * GROUP + N_HEADS * LANES
COL_SLABS = COL_NSA_Q + GROUP
N_IN_PAD = COL_SLABS + 4 * LANES

_MXU = jnp.bfloat16
_F32 = jnp.float32


def _params(sem):
    return pltpu.CompilerParams(dimension_semantics=sem, vmem_limit_bytes=VMEM_LIMIT_BYTES)


def _resident(shape):
    zeros = (0,) * len(shape)
    return pl.BlockSpec(shape, lambda *_: zeros, pipeline_mode=pl.Buffered(1))


def _mm(a, b):
    return jnp.dot(a.astype(_MXU), b.astype(_MXU), preferred_element_type=_F32)


def _mm_nt(a, b):
    return lax.dot_general(a.astype(_MXU), b.astype(_MXU), (((1,), (1,)), ((), ())),
                           preferred_element_type=_F32)


def _lane(shape):
    return lax.broadcasted_iota(jnp.int32, shape, len(shape) - 1)


def _row(shape):
    return lax.broadcasted_iota(jnp.int32, shape, 0)


def _rms_scale(x):
    return lax.rsqrt(jnp.mean(x * x, axis=-1, keepdims=True) + RMS_EPS)


def _gelu(x):
    c = math.sqrt(2.0 / math.pi)
    return x * (0.5 * (1.0 + jnp.tanh(c * (x + 0.044715 * (x * x * x)))))


def _round_mxu(x):
    return x.astype(_MXU).astype(_F32)


def _low_half_rms_scale(x, low):
    s = jnp.sum(jnp.where(low, x * x, 0.0), axis=-1, keepdims=True)
    return lax.rsqrt(s * (1.0 / HEAD_DIM) + RMS_EPS)


def _half_rms_scale(x, low):
    x2 = x * x
    s_lo = jnp.sum(jnp.where(low, x2, 0.0), axis=-1, keepdims=True)
    s_hi = jnp.sum(jnp.where(low, 0.0, x2), axis=-1, keepdims=True)
    return jnp.where(low, lax.rsqrt(s_lo * (1.0 / HEAD_DIM) + RMS_EPS),
                     lax.rsqrt(s_hi * (1.0 / HEAD_DIM) + RMS_EPS))


def _linear_scan_rows(a, b, h_in):
    n = a.shape[0]
    pos = _row(a.shape) & (SCAN_SEG - 1)
    d = 1
    while d < SCAN_SEG:
        keep = pos >= d
        a_prev = jnp.where(keep, pltpu.roll(a, d, axis=0), 1.0)
        b_prev = jnp.where(keep, pltpu.roll(b, d, axis=0), 0.0)
        b = a * b_prev + b
        a = a * a_prev
        d *= 2
    segs = []
    for r0 in range(0, n, SCAN_SEG):
        seg = b[r0:r0 + SCAN_SEG] + a[r0:r0 + SCAN_SEG] * h_in
        h_in = seg[SCAN_SEG - 1:SCAN_SEG]
        segs.append(seg)
    return jnp.concatenate(segs, axis=0)


def _cumsum_rows(x, c_in):
    n = x.shape[0]
    pos = _row(x.shape) & (SCAN_SEG - 1)
    d = 1
    while d < SCAN_SEG:
        x = x + jnp.where(pos >= d, pltpu.roll(x, d, axis=0), 0.0)
        d *= 2
    segs = []
    for r0 in range(0, n, SCAN_SEG):
        seg = x[r0:r0 + SCAN_SEG] + c_in
        c_in = seg[SCAN_SEG - 1:SCAN_SEG]
        segs.append(seg)
    return jnp.concatenate(segs, axis=0)


def _mix_ab(zab, cw_ref, cb_ref, wg_ref, bg_ref, lam_ref, scw_ref, on_ref, out_ref, xext_ref, uext_ref, h_ref):
    ts = zab.shape[0]
    xr = zab[:, 0:GROUP]
    gate = zab[:, GROUP:2 * GROUP]
    bgt = zab[:, 2 * GROUP:3 * GROUP]
    cgt = zab[:, 3 * GROUP:4 * GROUP]
    xs = zab[:, 4 * GROUP:5 * GROUP]

    xext_ref[SUBLANES:SUBLANES + ts, :] = xr
    xc = cb_ref[...] + cw_ref[LRU_CONV - 1:LRU_CONV, :] * xr
    for k in range(LRU_CONV - 1):
        xc = xc + cw_ref[k:k + 1, :] * xext_ref[pl.ds(SUBLANES - (LRU_CONV - 1) + k, ts), :]
    xext_ref[0:SUBLANES, :] = xr[ts - SUBLANES:ts, :]

    gi = _mm(xc, wg_ref[...]) + bg_ref[...]
    r = jax.nn.sigmoid(gi[:, 0:GROUP])
    ig = jax.nn.sigmoid(gi[:, GROUP:2 * GROUP])
    lam = lam_ref[...]
    softplus_neg = jnp.maximum(-lam, 0.0) + jnp.log1p(jnp.exp(-jnp.abs(lam)))
    log_a = (-LRU_C * softplus_neg) * r
    a = jnp.exp(log_a)
    g = -jnp.tanh(log_a) * (a * a + 1.0)
    b = jnp.where(g > 0.0, g * lax.rsqrt(g), 0.0) * (ig * xc)
    h = _linear_scan_rows(a, b, h_ref[0:1, :])
    h_ref[...] = jnp.broadcast_to(h[ts - 1:ts, :], h_ref.shape)
    y_a = h * _gelu(gate)

    u = cgt * xs
    uext_ref[SUBLANES:SUBLANES + ts, :] = u
    cv = scw_ref[SC_CONV - 1:SC_CONV, :] * u
    for k in range(SC_CONV - 1):
        cv = cv + scw_ref[k:k + 1, :] * uext_ref[pl.ds(SUBLANES - (SC_CONV - 1) + k, ts), :]
    uext_ref[0:SUBLANES, :] = u[ts - SUBLANES:ts, :]
    y_b = bgt * cv

    out_ref[0, :, 0:GROUP] = (y_a * _rms_scale(y_a) * on_ref[:, 0:GROUP]).astype(out_ref.dtype)
    out_ref[0, :, GROUP:2 * GROUP] = (y_b * _rms_scale(y_b) * on_ref[:, GROUP:2 * GROUP]).astype(out_ref.dtype)


def _fox_prep(zfox, zsm, fb_ref, gq_ref, gk_ref, q_ref, k_ref, v_ref, c_ref):
    ts = zfox.shape[0]
    shape = (ts, LANES)
    lane = _lane(shape)
    low = lane < HEAD_DIM

    f = zsm + fb_ref[...]
    logf = jnp.minimum(f, 0.0) - jnp.log1p(jnp.exp(-jnp.abs(f)))
    c = _cumsum_rows(logf, c_ref[0:1, :])
    c_ref[...] = jnp.broadcast_to(c[ts - 1:ts, :], c_ref.shape)
    neg_c = -LOG2E * c
    p1 = _round_mxu(neg_c)
    r1 = neg_c - p1
    p2 = _round_mxu(r1)
    p3 = _round_mxu(r1 - p2)
    piece = lane - 3 * ((lane * 11) >> 5)
    csel = jnp.where(piece == 0, p1, jnp.where(piece == 1, p2, p3))

    ones_aug = jnp.where(lane < HEAD_DIM + 3, 1.0, 0.0)
    for p in range(N_HEADS // 2):
        qs = zfox[:, LANES * p:LANES * (p + 1)]
        ks = zfox[:, GROUP + LANES * p:GROUP + LANES * (p + 1)]
        qn = qs * _half_rms_scale(qs, low) * gq_ref[...]
        kn = ks * _half_rms_scale(ks, low) * gk_ref[...]
        for half in range(2):
            h = 2 * p + half
            qh = qn if half == 0 else pltpu.roll(qn, HEAD_DIM, axis=1)
            kh = kn if half == 0 else pltpu.roll(kn, HEAD_DIM, axis=1)
            ch = pltpu.roll(csel, HEAD_DIM - 3 * h, axis=1)
            q_ref[0, h] = jnp.where(low, qh, ones_aug).astype(q_ref.dtype)
            k_ref[0, h] = jnp.where(low, kh, jnp.where(lane < HEAD_DIM + 3, ch, 0.0)).astype(k_ref.dtype)
    for h in range(N_HEADS):
        vh = zfox[:, 2 * GROUP + LANES * h:2 * GROUP + LANES * (h + 1)]
        v_ref[0, h] = jnp.where(lane == FOX_SUM_LANE, 1.0, vh).astype(v_ref.dtype)


def _nsa_prep(zq, zsel, zwin, zsm, row0, gq_ref, gs_ref, gw_ref, gb_ref,
              q4_ref, ks_ref, vs_ref, kw_ref, vw_ref, gate_ref):
    ts = zq.shape[0]
    shape = (ts, LANES)
    lane = _lane(shape)
    low = lane < HEAD_DIM
    for p in range(N_HEADS // 2):
        qs = zq[:, LANES * p:LANES * (p + 1)]
        qn = qs * _half_rms_scale(qs, low) * gq_ref[...]
        q4_ref[0, 2 * p] = jnp.where(low, qn, 0.0).astype(q4_ref.dtype)
        q4_ref[0, 2 * p + 1] = jnp.where(low, pltpu.roll(qn, HEAD_DIM, axis=1), 0.0).astype(q4_ref.dtype)

    blk = (row0 + _row(shape)) >> 6
    onehot = jnp.where(lane - HEAD_DIM == blk, 1.0, 0.0)
    ks_ref[0] = jnp.where(low, zsel * _low_half_rms_scale(zsel, low) * gs_ref[...], onehot).astype(ks_ref.dtype)
    vs_ref[0] = jnp.where(lane == SUM_LANE, 1.0, zsel).astype(vs_ref.dtype)
    kw_ref[0] = jnp.where(low, zwin * _low_half_rms_scale(zwin, low) * gw_ref[...], 0.0).astype(kw_ref.dtype)
    vw_ref[0] = jnp.where(lane == SUM_LANE, 1.0, zwin).astype(vw_ref.dtype)
    gate_ref[0] = jax.nn.sigmoid(zsm + gb_ref[...])


def _proj_mix_kernel(x_ref, g_ref, w_ref,
                     cw_ref, cb_ref, wg_ref, bg_ref, lam_ref, scw_ref, on_ref,
                     fb_ref, fgq_ref, fgk_ref, ngq_ref, ngs_ref, ngw_ref, gb_ref,
                     mab_ref, qf_ref, kf_ref, vf_ref, q4_ref, ks_ref, vs_ref, kw_ref, vw_ref, gate_ref, zcmp_ref,
                     xext_ref, uext_ref, h_ref, c_ref, zc_ref):
    ts = x_ref.shape[1]
    ti = pl.program_id(1)

    @pl.when(ti == 0)
    def _():
        xext_ref[0:SUBLANES, :] = jnp.zeros((SUBLANES, GROUP), _F32)
        uext_ref[0:SUBLANES, :] = jnp.zeros((SUBLANES, GROUP), _F32)
        h_ref[...] = jnp.zeros_like(h_ref)
        c_ref[...] = jnp.zeros_like(c_ref)

    x = x_ref[0]
    xn = (x * _rms_scale(x) * g_ref[...]).astype(_MXU)
    zab = jnp.dot(xn, w_ref[:, 0:COL_FOX], preferred_element_type=_F32)
    slabs = jnp.dot(xn, w_ref[:, COL_SLABS:N_IN_PAD], preferred_element_type=_F32)
    zfox = jnp.dot(xn, w_ref[:, COL_FOX:COL_NSA_Q], preferred_element_type=_F32)
    zq = jnp.dot(xn, w_ref[:, COL_NSA_Q:COL_SLABS], preferred_element_type=_F32)
    zc_ref[...] = slabs[:, 0:LANES]
    for tok in range(CMP_STRIDE):
        zcmp_ref[0, :, tok * LANES:(tok + 1) * LANES] = zc_ref[pl.ds(tok, ts // CMP_STRIDE, stride=CMP_STRIDE), :]
    zsel = slabs[:, LANES:2 * LANES]
    zwin = slabs[:, 2 * LANES:3 * LANES]
    zsm = slabs[:, 3 * LANES:4 * LANES]
    _fox_prep(zfox, zsm, fb_ref, fgq_ref, fgk_ref, qf_ref, kf_ref, vf_ref, c_ref)
    _nsa_prep(zq, zsel, zwin, zsm, ti * ts, ngq_ref, ngs_ref, ngw_ref, gb_ref,
              q4_ref, ks_ref, vs_ref, kw_ref, vw_ref, gate_ref)
    _mix_ab(zab, cw_ref, cb_ref, wg_ref, bg_ref, lam_ref, scw_ref, on_ref, mab_ref, xext_ref, uext_ref, h_ref)


def _proj_mix(x, gain, w, cw, cb, wg, bg, lam, scw, on_ab, fb, fgq, fgk, ngq, ngs, ngw, gb):
    bsz, s_len, _ = x.shape
    ts = TS_MIX
    vec = lambda width: pl.BlockSpec((1, width), lambda b, t: (0, 0))
    rows = lambda width: pl.BlockSpec((1, ts, width), lambda b, t: (b, t, 0))
    heads = pl.BlockSpec((1, N_HEADS, ts, LANES), lambda b, t: (b, 0, t, 0))
    sds = jax.ShapeDtypeStruct
    return pl.pallas_call(
        _proj_mix_kernel,
        grid=(bsz, s_len // ts),
        in_specs=[rows(D_MODEL), vec(D_MODEL), _resident((D_MODEL, N_IN_PAD)),
                  pl.BlockSpec((LRU_CONV, GROUP), lambda b, t: (0, 0)), vec(GROUP), _resident((GROUP, 2 * GROUP)),
                  vec(2 * GROUP), vec(GROUP), pl.BlockSpec((SC_CONV, GROUP), lambda b, t: (0, 0)), vec(2 * GROUP),
                  vec(LANES), vec(LANES), vec(LANES), vec(LANES), vec(LANES), vec(LANES), vec(LANES)],
        out_specs=[rows(2 * GROUP), heads, heads, heads, heads, rows(LANES), rows(LANES),
                   rows(LANES), rows(LANES), rows(LANES),
                   pl.BlockSpec((1, ts // CMP_STRIDE, CMP_STRIDE * LANES), lambda b, t: (b, t, 0))],
        out_shape=[sds((bsz, s_len, 2 * GROUP), _MXU),
                   sds((bsz, N_HEADS, s_len, LANES), _MXU), sds((bsz, N_HEADS, s_len, LANES), _MXU),
                   sds((bsz, N_HEADS, s_len, LANES), _MXU),
                   sds((bsz, N_HEADS, s_len, LANES), _MXU),
                   sds((bsz, s_len, LANES), _MXU), sds((bsz, s_len, LANES), _MXU),
                   sds((bsz, s_len, LANES), _MXU), sds((bsz, s_len, LANES), _MXU),
                   sds((bsz, s_len, LANES), _F32), sds((bsz, s_len // CMP_STRIDE, CMP_STRIDE * LANES), _F32)],
        scratch_shapes=[pltpu.VMEM((ts + SUBLANES, GROUP), _F32), pltpu.VMEM((ts + SUBLANES, GROUP), _F32),
                        pltpu.VMEM((SUBLANES, GROUP), _F32), pltpu.VMEM((SUBLANES, LANES), _F32),
                        pltpu.VMEM((ts, LANES), _F32)],
        compiler_params=_params(("parallel", "arbitrary")),
        name="proj_mix",
    )(x, gain, w, cw, cb, wg, bg, lam, scw, on_ab, fb, fgq, fgk, ngq, ngs, ngw, gb)


def _flash_step(s, v, carry):
    m, l, acc = carry
    m_new = jnp.maximum(m, jnp.max(s, axis=-1, keepdims=True))
    p = jnp.exp2(s - m_new)
    pv = jnp.dot(p.astype(_MXU), v, preferred_element_type=_F32)
    alpha = jnp.exp2(m - m_new)
    l = alpha * l + jnp.sum(p, axis=-1, keepdims=True)
    return m_new, l, alpha * acc + pv


def _flash_init(rows):
    return (jnp.full((rows, 1), NEG, _F32), jnp.zeros((rows, 1), _F32), jnp.zeros((rows, LANES), _F32))


def _flash_step_sumlane(s, v, carry):
    m, acc = carry
    m_new = jnp.maximum(m, jnp.max(s, axis=-1, keepdims=True))
    p = jnp.exp2(s - m_new).astype(_MXU)
    pv = jnp.dot(p, v, preferred_element_type=_F32)
    return m_new, jnp.exp2(m - m_new) * acc + pv


def _flash_init_sumlane(rows):
    return (jnp.full((rows, 1), NEG, _F32), jnp.zeros((rows, LANES), _F32))


def _sumlane_normalize(acc, sum_lane):
    return acc / acc[:, sum_lane:sum_lane + 1]


def _fox_attn_kernel(q_ref, k_ref, v_ref, on_ref, out_ref):
    tq = q_ref.shape[2]
    qi = pl.program_id(1)
    qs = [q_ref[0, h] for h in range(N_HEADS)]

    def chunk(start, width, carries, causal=False):
        new = []
        for h in range(N_HEADS):
            s = _mm_nt(qs[h], k_ref[0, h, pl.ds(start, width), :])
            if causal:
                s = jnp.where(_lane((tq, width)) <= _row((tq, width)), s, NEG)
            new.append(_flash_step_sumlane(s, v_ref[0, h, pl.ds(start, width), :], carries[h]))
        return tuple(new)

    per_wide = TK_FOX // tq
    n_wide = qi // per_wide
    carries = lax.fori_loop(0, n_wide, lambda c, cr: chunk(pl.multiple_of(c * TK_FOX, TK_FOX), TK_FOX, cr),
                            tuple(_flash_init_sumlane(tq) for _ in range(N_HEADS)))
    carries = lax.fori_loop(n_wide * per_wide, qi, lambda c, cr: chunk(pl.multiple_of(c * tq, tq), tq, cr), carries)
    carries = chunk(pl.multiple_of(qi * tq, tq), tq, carries, causal=True)
    outs = [_sumlane_normalize(acc, FOX_SUM_LANE) for _, acc in carries]
    low = _lane((tq, LANES)) < HEAD_DIM
    y = jnp.concatenate([jnp.where(low, outs[2 * p], pltpu.roll(outs[2 * p + 1], HEAD_DIM, axis=1))
                         for p in range(N_HEADS // 2)], axis=1)
    out_ref[0] = (y * _rms_scale(y) * on_ref[...]).astype(out_ref.dtype)


def _fox_attn(qf, kf, vf, on):
    bsz, _, s_len, _ = qf.shape
    tq = min(TQ_FOX, s_len)
    assert s_len % TK_FOX == 0 and TK_FOX % tq == 0
    return pl.pallas_call(
        _fox_attn_kernel,
        grid=(bsz, s_len // tq),
        in_specs=[pl.BlockSpec((1, N_HEADS, tq, LANES), lambda b, i: (b, 0, i, 0)),
                  pl.BlockSpec((1, N_HEADS, s_len, LANES), lambda b, i: (b, 0, 0, 0)),
                  pl.BlockSpec((1, N_HEADS, s_len, LANES), lambda b, i: (b, 0, 0, 0)),
                  pl.BlockSpec((1, GROUP), lambda b, i: (0, 0))],
        out_specs=pl.BlockSpec((1, tq, GROUP), lambda b, i: (b, i, 0)),
        out_shape=jax.ShapeDtypeStruct((bsz, s_len, GROUP), _MXU),
        compiler_params=_params(("parallel", "arbitrary")),
        name="fox_attn",
    )(qf, kf, vf, on)


def _compress_kernel(x_ref, pa_ref, pb_ref, w1a_ref, w1b_ref, w2_ref, g_ref, kc_ref, cv_ref):
    nc = x_ref.shape[1]
    x = x_ref[0]
    ua = _mm(x + pa_ref[...], w1a_ref[...])
    ub = _mm(x + pb_ref[...], w1b_ref[...])
    hid = _gelu(ua + pltpu.roll(ub, nc - 1, axis=0))
    out = _mm(hid, w2_ref[...])
    low = _lane(out.shape) < HEAD_DIM
    kc_ref[0] = jnp.where(low, out * _low_half_rms_scale(out, low) * g_ref[...], 0.0).astype(kc_ref.dtype)
    cv_ref[0] = out.astype(cv_ref.dtype)


def _compress(xc, pa, pb, w1a, w1b, w2, g):
    bsz, nc, width = xc.shape
    const = lambda shape: pl.BlockSpec(shape, lambda b: (0, 0))
    slab = pl.BlockSpec((1, nc, LANES), lambda b: (b, 0, 0))
    return pl.pallas_call(
        _compress_kernel,
        grid=(bsz,),
        in_specs=[pl.BlockSpec((1, nc, width), lambda b: (b, 0, 0)),
                  const((1, width)), const((1, width)), const((width, 2 * CMP_HIDDEN)),
                  const((width, 2 * CMP_HIDDEN)), const((2 * CMP_HIDDEN, LANES)), const((1, LANES))],
        out_specs=[slab, slab],
        out_shape=[jax.ShapeDtypeStruct((bsz, nc, LANES), _MXU)] * 2,
        compiler_params=_params(("parallel",)),
        name="compress",
    )(xc, pa, pb, w1a, w1b, w2, g)


def _bucket_thresholds():
    max_exact = REL_BUCKETS // 2
    d = np.arange(0, REL_MAX_DIST + 1)
    large = max_exact + (np.log(np.maximum(d, 1).astype(np.float32) / max_exact)
                         / math.log(REL_MAX_DIST / max_exact) * (REL_BUCKETS - max_exact)).astype(np.int32)
    bucket = np.where(d < max_exact, d, np.minimum(large, REL_BUCKETS - 1))
    assert bucket[-1] == REL_BUCKETS - 1 and np.all(np.diff(bucket) >= 0)
    return [int(np.argmax(bucket >= k)) for k in range(REL_BUCKETS)]


_BUCKET_THR = _bucket_thresholds()

NSA_R = TQ_NSA // KT_NSA
NSA_D = WINDOW // KT_NSA
NSA_TAIL = NSA_D + NSA_R
WIN_ENTRIES = NSA_TAIL + 1
SEL_ENTRIES = NSA_R + 3


def _win_entry(e):
    return jnp.minimum(e, NSA_TAIL)


def _sel_entry(e):
    return jnp.clip(e - (NSA_D - 2), 0, NSA_R + 2)


def _rel_bias(dist, rb_ref, h):
    far = rb_ref[REL_BUCKETS - 1, h]
    val = jnp.full(dist.shape, LOG2E * (rb_ref[0, h] - far), _F32)
    for k in range(1, REL_BUCKETS):
        val = jnp.where(dist >= _BUCKET_THR[k], LOG2E * (rb_ref[k, h] - far), val)
    return val


def _nsa_tables_kernel(rb_ref, win_ref, sel_ref, cmp_ref):
    qi = pl.program_id(0)
    tq = TQ_NSA
    nc = cmp_ref.shape[2]

    @pl.when(qi == 0)
    def _():
        i = _row((tq, KT_NSA))
        c = _lane((tq, KT_NSA))
        for h in range(N_HEADS):
            rs = slice(h * tq, (h + 1) * tq)
            for e in range(NSA_TAIL):
                dist = i + (NSA_D - e) * KT_NSA - c
                ok = jnp.where(dist >= 0, jnp.where(dist < WINDOW, 1, 0), 0) > 0
                tab = jnp.where(ok, _rel_bias(dist, rb_ref, h), NEG)
                win_ref[e, rs, :] = tab
                if e >= NSA_D - 1:
                    sel_ref[e - (NSA_D - 2), rs, :] = tab
            win_ref[NSA_TAIL, rs, :] = jnp.full((tq, KT_NSA), NEG, _F32)
            sel_ref[0, rs, :] = jnp.zeros((tq, KT_NSA), _F32)
            sel_ref[NSA_R + 2, rs, :] = jnp.full((tq, KT_NSA), NEG, _F32)

    t = qi * tq + _row((tq, nc))
    dist = t - (CMP_STRIDE * _lane((tq, nc)) + CMP_LEN - 1)
    for h in range(N_HEADS):
        cmp_ref[0, h * tq:(h + 1) * tq, :] = jnp.where(dist >= 0, _rel_bias(dist, rb_ref, h), NEG)


def _nsa_tables(rel_bias, s_len):
    nq = s_len // TQ_NSA
    nc = s_len // CMP_STRIDE
    rows = N_HEADS * TQ_NSA
    return pl.pallas_call(
        _nsa_tables_kernel,
        grid=(nq,),
        in_specs=[pl.BlockSpec(memory_space=pltpu.SMEM)],
        out_specs=[pl.BlockSpec((WIN_ENTRIES, rows, KT_NSA), lambda i: (0, 0, 0)),
                   pl.BlockSpec((SEL_ENTRIES, rows, KT_NSA), lambda i: (0, 0, 0)),
                   pl.BlockSpec((1, rows, nc), lambda i: (i, 0, 0))],
        out_shape=[jax.ShapeDtypeStruct((WIN_ENTRIES, rows, KT_NSA), _F32),
                   jax.ShapeDtypeStruct((SEL_ENTRIES, rows, KT_NSA), _F32),
                   jax.ShapeDtypeStruct((nq, rows, nc), _F32)],
        compiler_params=_params(("arbitrary",)),
        name="nsa_tables",
    )(rel_bias)


def _topk_penalty(imp_t, qi, tq):
    n_blk = imp_t.shape[0]
    blk = _row((n_blk, tq))
    cur = (qi * tq + _lane((n_blk, tq))) >> 6
    forced = jnp.where(blk == 0, 1, jnp.where(blk == cur, 1, jnp.where(blk == cur - 1, 1, 0))) > 0
    val = jnp.where(blk <= cur, jnp.where(forced, BIG, imp_t), NEG)
    groups = [val[g * SUBLANES:(g + 1) * SUBLANES, :] for g in range(n_blk // SUBLANES)]
    sub = _row((SUBLANES, tq))
    ranks = [jnp.zeros((SUBLANES, tq), _F32) for _ in groups]
    for j in range(n_blk):
        vj = val[j:j + 1, :]
        jg, jr = divmod(j, SUBLANES)
        for g, vg in enumerate(groups):
            ge = jnp.where(vj >= vg, 1.0, 0.0)
            gt = jnp.where(vj > vg, 1.0, 0.0)
            if g > jg:
                ranks[g] = ranks[g] + ge
            elif g < jg:
                ranks[g] = ranks[g] + gt
            else:
                ranks[g] = ranks[g] + jnp.where(sub > jr, ge, gt)
    rank = jnp.concatenate(ranks, axis=0)
    return jnp.where(rank < float(SLC_TOPK), 0.0, NEG)


def _nsa_attn_kernel(q4_ref, kc_ref, cv_ref, ks_ref, vs_ref, kw_ref, vw_ref, gate_ref,
                     win_ref, sel_ref, cmp_ref, ovl_ref, on_ref, out_ref):
    tq = TQ_NSA
    kt = KT_NSA
    rows = N_HEADS * tq
    qi = pl.program_id(1)
    first = qi * NSA_R
    q4 = q4_ref[0].reshape(rows, LANES)
    lane = _lane((tq, LANES))
    low = lane < HEAD_DIM
    tail_w = NSA_TAIL * kt

    def tail_table(tab_ref, entry_of, first_tile, skip_before=0):
        pieces = []
        for u in range(NSA_TAIL):
            tile = first_tile + u
            entry = entry_of(tile - first + NSA_D)
            pieces.append(tab_ref[jnp.where(tile < skip_before, tab_ref.shape[0] - 1, entry)])
        return jnp.concatenate(pieces, axis=1)

    s = _mm_nt(q4, kc_ref[0]) + cmp_ref[0]
    valid = s > 0.5 * NEG
    m = jnp.max(s, axis=-1, keepdims=True)
    p = jnp.where(valid, jnp.exp2(s - m), 0.0)
    l = jnp.sum(p, axis=-1, keepdims=True)
    p = p / jnp.where(l > 0.0, l, 1.0)
    o_c = jnp.dot(p.astype(_MXU), cv_ref[0], preferred_element_type=_F32)

    psum = p[0:tq] + p[tq:2 * tq] + p[2 * tq:3 * tq] + p[3 * tq:4 * tq]
    p_hi = _round_mxu(psum)
    imp = _mm(p_hi, ovl_ref[...]) + _mm(psum - p_hi, ovl_ref[...])

    pen_t = _topk_penalty(imp.T[0:SLC_BLOCK, :], qi, tq)
    pen = jnp.concatenate([jnp.zeros((LANES - SLC_BLOCK, tq), _F32), pen_t], axis=0).T
    pen4 = jnp.concatenate([pen] * N_HEADS, axis=0)
    q_aug = jnp.where(_lane((rows, LANES)) < HEAD_DIM, q4.astype(_F32), pen4).astype(_MXU)

    w_tile = jnp.maximum(first - NSA_D, 0)
    w_start = pl.multiple_of(w_tile * kt, kt)
    s = _mm_nt(q4, kw_ref[0, pl.ds(w_start, tail_w), :]) + tail_table(win_ref, _win_entry, w_tile)
    _, acc_w = _flash_step_sumlane(s, vw_ref[0, pl.ds(w_start, tail_w), :], _flash_init_sumlane(rows))
    o_w = _sumlane_normalize(acc_w, SUM_LANE)

    def sel_far(width):
        def step(c, carry):
            start = pl.multiple_of(c * width, width)
            s = _mm_nt(q_aug, ks_ref[0, pl.ds(start, width), :])
            return _flash_step_sumlane(s, vs_ref[0, pl.ds(start, width), :], carry)
        return step

    far_w = NSA_FAR * kt
    n_far = jnp.maximum(first - 1, 0) // NSA_FAR
    n_dbl = n_far // 2
    carry = lax.fori_loop(0, n_dbl, sel_far(2 * far_w), _flash_init_sumlane(rows))
    carry = lax.fori_loop(2 * n_dbl, n_far, sel_far(far_w), carry)
    done = n_far * NSA_FAR
    t_tile = jnp.minimum(done, ks_ref.shape[1] // kt - NSA_TAIL)
    t_start = pl.multiple_of(t_tile * kt, kt)
    s = _mm_nt(q_aug, ks_ref[0, pl.ds(t_start, tail_w), :]) + tail_table(sel_ref, _sel_entry, t_tile, done)
    _, acc_s = _flash_step_sumlane(s, vs_ref[0, pl.ds(t_start, tail_w), :], carry)
    o_s = _sumlane_normalize(acc_s, SUM_LANE)

    g = gate_ref[0]
    heads = []
    for h in range(N_HEADS):
        rs = slice(h * tq, (h + 1) * tq)
        gc = g[:, SM_GATE_LANE + h:SM_GATE_LANE + h + 1]
        gs = g[:, SM_GATE_LANE + N_HEADS + h:SM_GATE_LANE + N_HEADS + h + 1]
        gw = g[:, SM_GATE_LANE + 2 * N_HEADS + h:SM_GATE_LANE + 2 * N_HEADS + h + 1]
        heads.append(gc * o_c[rs] + gs * o_s[rs] + gw * o_w[rs])
    slabs = [jnp.where(low, pltpu.roll(heads[2 * p], HEAD_DIM, axis=1), heads[2 * p + 1])
             for p in range(N_HEADS // 2)]
    y = jnp.concatenate(slabs, axis=1)
    out_ref[0] = (y * _rms_scale(y) * on_ref[...]).astype(out_ref.dtype)


def _nsa_attn(q4, kc, cv, ks, vs, kw, vw, gates, win_tab, sel_tab, cmp_tab, ovl, on):
    bsz, _, s_len, _ = q4.shape
    tq = TQ_NSA
    nc = kc.shape[1]
    rows = N_HEADS * tq
    assert s_len >= NSA_TAIL * KT_NSA
    full = lambda n: pl.BlockSpec((1, n, LANES), lambda b, i: (b, 0, 0))
    return pl.pallas_call(
        _nsa_attn_kernel,
        grid=(bsz, s_len // tq),
        in_specs=[pl.BlockSpec((1, N_HEADS, tq, LANES), lambda b, i: (b, 0, i, 0)),
                  full(nc), full(nc), full(s_len), full(s_len), full(s_len), full(s_len),
                  pl.BlockSpec((1, tq, LANES), lambda b, i: (b, i, 0)),
                  _resident((WIN_ENTRIES, rows, KT_NSA)), _resident((SEL_ENTRIES, rows, KT_NSA)),
                  pl.BlockSpec((1, rows, nc), lambda b, i: (i, 0, 0)),
                  _resident((nc, LANES)),
                  pl.BlockSpec((1, GROUP), lambda b, i: (0, 0))],
        out_specs=pl.BlockSpec((1, tq, GROUP), lambda b, i: (b, i, 0)),
        out_shape=jax.ShapeDtypeStruct((bsz, s_len, GROUP), _MXU),
        compiler_params=_params(("parallel", "arbitrary")),
        name="nsa_attn",
    )(q4, kc, cv, ks, vs, kw, vw, gates, win_tab, sel_tab, cmp_tab, ovl, on)


def _ffn_kernel(x_ref, mab_ref, mc_ref, md_ref, wo_ref, g_ref, wgu_ref, wd_ref, out_ref):
    x1 = (x_ref[...]
          + jnp.dot(mab_ref[...], wo_ref[0:2 * GROUP, :], preferred_element_type=_F32)
          + jnp.dot(mc_ref[...], wo_ref[2 * GROUP:3 * GROUP, :], preferred_element_type=_F32)
          + jnp.dot(md_ref[...], wo_ref[3 * GROUP:4 * GROUP, :], preferred_element_type=_F32))
    xn = (x1 * _rms_scale(x1) * g_ref[...]).astype(_MXU)
    out_ref[...] = x1
    for j in range(D_FF // TF_FFN):
        cols = slice(j * TF_FFN, (j + 1) * TF_FFN)
        gt = jnp.dot(xn, wgu_ref[:, cols], preferred_element_type=_F32)
        up = jnp.dot(xn, wgu_ref[:, D_FF + j * TF_FFN:D_FF + (j + 1) * TF_FFN], preferred_element_type=_F32)
        hid = (gt * jax.nn.sigmoid(gt)) * up
        out_ref[...] += jnp.dot(hid.astype(_MXU), wd_ref[cols, :], preferred_element_type=_F32)


def _out_ffn(x2d, mab, mc, md, wo, g, wgu, wd):
    t = x2d.shape[0]
    tm = min(TM_FFN, t)
    assert t % tm == 0 and D_FF % TF_FFN == 0
    row = lambda width: pl.BlockSpec((tm, width), lambda i: (i, 0))
    return pl.pallas_call(
        _ffn_kernel,
        grid=(t // tm,),
        in_specs=[row(D_MODEL), row(2 * GROUP), row(GROUP), row(GROUP),
                  pl.BlockSpec((D_MODEL, D_MODEL), lambda i: (0, 0)), pl.BlockSpec((1, D_MODEL), lambda i: (0, 0)),
                  pl.BlockSpec((D_MODEL, 2 * D_FF), lambda i: (0, 0)), pl.BlockSpec((D_FF, D_MODEL), lambda i: (0, 0))],
        out_specs=row(D_MODEL),
        out_shape=jax.ShapeDtypeStruct((t, D_MODEL), _F32),
        compiler_params=_params(("parallel",)),
        name="out_ffn",
    )(x2d, mab, mc, md, wo, g, wgu, wd)


def _lane_vec(values, start):
    v = jnp.zeros((LANES,), _F32).at[start:start + values.shape[0]].set(values.astype(_F32))
    return v[None, :]


def _block_diag(w):
    h, d, _ = w.shape
    eye = jnp.eye(h, dtype=w.dtype)
    return (eye[:, None, :, None] * w[:, :, None, :]).reshape(h * d, h * d)


def _overlap_ext(s_len):
    nc = s_len // CMP_STRIDE
    n_cmp = nc - 1
    n_slc = s_len // SLC_BLOCK
    cs = np.arange(n_cmp)[:, None] * CMP_STRIDE
    ss = np.arange(n_slc)[None, :] * SLC_BLOCK
    ov = np.clip(np.minimum(cs + CMP_LEN, ss + SLC_BLOCK) - np.maximum(cs, ss), 0, CMP_LEN)
    ext = np.zeros((nc, LANES), np.float32)
    ext[:n_cmp, :n_slc] = ov
    return jnp.asarray(ext, _MXU)


def _layer_params(l, w_in, lru_w_gates, lru_b_gates, fox_f_bias, fox_qk_norm, nsa_qk_norm, nsa_cmp_pos,
                  nsa_cmp_w1, nsa_cmp_w2, nsa_gate_bias, out_norm):
    scale = HEAD_DIM ** -0.5 * LOG2E
    w = w_in[l]
    small = jnp.zeros((D_MODEL, LANES), w.dtype)
    small = small.at[:, 0:3 * N_HEADS].set(jnp.repeat(w[:, C_FOX_F:C_FOX_F + N_HEADS], 3, axis=1))
    small = small.at[:, SM_GATE_LANE:SM_GATE_LANE + 3 * N_HEADS].set(w[:, C_NSA_G:C_NSA_G + 3 * N_HEADS])
    c_fox_v = C_FOX_F - GROUP
    fox_v = jnp.pad(w[:, c_fox_v:C_FOX_F].reshape(D_MODEL, N_HEADS, HEAD_DIM),
                    ((0, 0), (0, 0), (0, LANES - HEAD_DIM))).reshape(D_MODEL, N_HEADS * LANES)
    w_perm = jnp.concatenate([w[:, :c_fox_v], fox_v, w[:, C_NSA_Q:C_NSA_G], small], axis=1).astype(_MXU)
    assert w_perm.shape[1] == N_IN_PAD

    wg = jnp.concatenate([_block_diag(lru_w_gates[l, 0]), _block_diag(lru_w_gates[l, 1])], axis=1).astype(_MXU)
    bg = lru_b_gates[l].reshape(1, 2 * GROUP)

    fb = _lane_vec(jnp.repeat(fox_f_bias[l], 3), 0)
    fox_gq = (jnp.tile(fox_qk_norm[l, 0], 2) * scale)[None, :]
    fox_gk = jnp.tile(fox_qk_norm[l, 1], 2)[None, :]

    nsa_gq = (jnp.tile(nsa_qk_norm[l, 0], 2) * scale)[None, :]
    nsa_gc = _lane_vec(nsa_qk_norm[l, 1], 0)
    nsa_gs = _lane_vec(nsa_qk_norm[l, 2], 0)
    nsa_gw = _lane_vec(nsa_qk_norm[l, 3], 0)
    gb = _lane_vec(nsa_gate_bias[l], SM_GATE_LANE)

    half = CMP_LEN // 2

    def pos_ext(lo):
        return jnp.concatenate([nsa_cmp_pos[l, 0, lo:lo + half], nsa_cmp_pos[l, 1, lo:lo + half]],
                               axis=1).reshape(1, half * LANES)

    def w1_ext(lo):
        ext = jnp.zeros((half, LANES, 2 * CMP_HIDDEN), _F32)
        ext = ext.at[:, :HEAD_DIM, :CMP_HIDDEN].set(nsa_cmp_w1[l, 0, lo:lo + half])
        ext = ext.at[:, HEAD_DIM:, CMP_HIDDEN:].set(nsa_cmp_w1[l, 1, lo:lo + half])
        return ext.reshape(half * LANES, 2 * CMP_HIDDEN).astype(_MXU)

    w2 = jnp.zeros((2 * CMP_HIDDEN, LANES), _F32)
    w2 = w2.at[:CMP_HIDDEN, :HEAD_DIM].set(nsa_cmp_w2[l, 0]).at[CMP_HIDDEN:, HEAD_DIM:].set(nsa_cmp_w2[l, 1])
    return dict(w_perm=w_perm, wg=wg, bg=bg, fb=fb, fox_gq=fox_gq, fox_gk=fox_gk, nsa_gq=nsa_gq, nsa_gc=nsa_gc,
                nsa_gs=nsa_gs, nsa_gw=nsa_gw, gb=gb, pa=pos_ext(0), pb=pos_ext(half), w1a=w1_ext(0),
                w1b=w1_ext(half), w2=w2.astype(_MXU), on=out_norm[l].reshape(1, 4 * GROUP))


def kernel(x, norm_mix, w_in, lru_conv_w, lru_conv_b, lru_w_gates, lru_b_gates, lru_lambda, sc_conv_w, fox_f_bias, fox_qk_norm, nsa_qk_norm, nsa_cmp_pos, nsa_cmp_w1, nsa_cmp_w2, nsa_gate_bias, rel_bias, out_norm, w_out, norm_ffn, w_gate_up, w_down):
    bsz, s_len, d_model = x.shape
    depth = w_in.shape[0]
    assert d_model == D_MODEL and s_len % TS_MIX == 0 and s_len % TQ_FOX == 0
    assert SLC_TOPK <= s_len // SLC_BLOCK <= SLC_BLOCK
    t = bsz * s_len

    win_tab, sel_tab, cmp_tab = _nsa_tables(rel_bias, s_len)
    ovl = _overlap_ext(s_len)

    for l in range(depth):
        lp = _layer_params(l, w_in, lru_w_gates, lru_b_gates, fox_f_bias, fox_qk_norm, nsa_qk_norm,
                           nsa_cmp_pos, nsa_cmp_w1, nsa_cmp_w2, nsa_gate_bias, out_norm)
        m_ab, qf, kf, vf, q4, ks, vs, kw, vw, gates, zcmp = _proj_mix(
            x, norm_mix[l][None, :], lp["w_perm"], lru_conv_w[l], lru_conv_b[l][None, :], lp["wg"], lp["bg"],
            lru_lambda[l][None, :], sc_conv_w[l], lp["on"][:, 0:2 * GROUP],
            lp["fb"], lp["fox_gq"], lp["fox_gk"], lp["nsa_gq"], lp["nsa_gs"], lp["nsa_gw"], lp["gb"])
        m_c = _fox_attn(qf, kf, vf, lp["on"][:, 2 * GROUP:3 * GROUP])
        kc, cv = _compress(zcmp, lp["pa"], lp["pb"], lp["w1a"], lp["w1b"], lp["w2"], lp["nsa_gc"])
        m_d = _nsa_attn(q4, kc, cv, ks, vs, kw, vw, gates, win_tab, sel_tab, cmp_tab, ovl,
                        lp["on"][:, 3 * GROUP:4 * GROUP])
        x = _out_ffn(x.reshape(t, D_MODEL), m_ab.reshape(t, 2 * GROUP), m_c.reshape(t, GROUP),
                     m_d.reshape(t, GROUP), w_out[l].astype(_MXU), norm_ffn[l][None, :],
                     w_gate_up[l].astype(_MXU), w_down[l].astype(_MXU)).reshape(bsz, s_len, D_MODEL)
    return x
```

```python
import math

import numpy as np
import jax
import jax.numpy as jnp
from jax import lax
from jax.experimental import pallas as pl
from jax.experimental.pallas import tpu as pltpu

D_MODEL = 1024
GROUP = 256
HEAD_DIM = 64
N_HEADS = 4
LRU_CONV = 4
LRU_C = 8.0
SC_CONV = 3
CMP_LEN = 32
CMP_STRIDE = 16
CMP_HIDDEN = 128
SLC_BLOCK = 64
SLC_SHIFT = SLC_BLOCK.bit_length() - 1
SLC_TOPK = 16
WINDOW = 512
REL_BUCKETS = 32
REL_MAX_DIST = 128
D_FF = 2816
RMS_EPS = 1e-6
NEG = -1e30
BIG = 1e30
LOG2E = math.log2(math.e)

C_FOX_F = 8 * GROUP
C_NSA_Q = C_FOX_F + N_HEADS
C_NSA_G = C_NSA_Q + GROUP + 6 * HEAD_DIM

LANES = 128
SUBLANES = 8
VMEM_LIMIT_BYTES = 56 * 1024 * 1024

TS_MIX = 512
SCAN_SEG = 32
TQ_FOX = 1024
KT_NSA = 128
TQ_NSA = 256
NSA_FAR = 4
TM_FFN = 512
TF_FFN = 256

SM_GATE_LANE = 16
SUM_LANE = 0
FOX_SUM_LANE = HEAD_DIM

COL_FOX = 5 * GROUP
COL_NSA_Q = COL_FOX + 2 * GROUP + N_HEADS * LANES
COL_SLABS = COL_NSA_Q + GROUP
N_IN_PAD = COL_SLABS + 4 * LANES

_MXU = jnp.bfloat16
_F32 = jnp.float32


def _params(sem):
    return pltpu.CompilerParams(dimension_semantics=sem, vmem_limit_bytes=VMEM_LIMIT_BYTES)


def _resident(shape):
    zeros = (0,) * len(shape)
    return pl.BlockSpec(shape, lambda *_: zeros, pipeline_mode=pl.Buffered(1))


def _mm(a, b):
    return jnp.dot(a.astype(_MXU), b.astype(_MXU), preferred_element_type=_F32)


def _mm_nt(a, b):
    return lax.dot_general(a.astype(_MXU), b.astype(_MXU), (((1,), (1,)), ((), ())),
                           preferred_element_type=_F32)


def _lane(shape):
    return lax.broadcasted_iota(jnp.int32, shape, len(shape) - 1)


def _row(shape):
    return lax.broadcasted_iota(jnp.int32, shape, 0)


def _rms_scale(x):
    return lax.rsqrt(jnp.mean(x * x, axis=-1, keepdims=True) + RMS_EPS)


def _gelu(x):
    c = math.sqrt(2.0 / math.pi)
    return x * (0.5 * (1.0 + jnp.tanh(c * (x + 0.044715 * (x * x * x)))))


def _round_mxu(x):
    return x.astype(_MXU).astype(_F32)


def _low_half_rms_scale(x, low):
    s = jnp.sum(jnp.where(low, x * x, 0.0), axis=-1, keepdims=True)
    return lax.rsqrt(s * (1.0 / HEAD_DIM) + RMS_EPS)


def _half_rms_scale(x, low):
    x2 = x * x
    s_lo = jnp.sum(jnp.where(low, x2, 0.0), axis=-1, keepdims=True)
    s_hi = jnp.sum(jnp.where(low, 0.0, x2), axis=-1, keepdims=True)
    return jnp.where(low, lax.rsqrt(s_lo * (1.0 / HEAD_DIM) + RMS_EPS),
                     lax.rsqrt(s_hi * (1.0 / HEAD_DIM) + RMS_EPS))


def _linear_scan_rows(a, b, h_in):
    n = a.shape[0]
    pos = _row(a.shape) & (SCAN_SEG - 1)
    d = 1
    while d < SCAN_SEG:
        keep = pos >= d
        a_prev = jnp.where(keep, pltpu.roll(a, d, axis=0), 1.0)
        b_prev = jnp.where(keep, pltpu.roll(b, d, axis=0), 0.0)
        b = a * b_prev + b
        a = a * a_prev
        d *= 2
    segs = []
    for r0 in range(0, n, SCAN_SEG):
        seg = b[r0:r0 + SCAN_SEG] + a[r0:r0 + SCAN_SEG] * h_in
        h_in = seg[SCAN_SEG - 1:SCAN_SEG]
        segs.append(seg)
    return jnp.concatenate(segs, axis=0)


def _cumsum_rows(x, c_in):
    n = x.shape[0]
    pos = _row(x.shape) & (SCAN_SEG - 1)
    d = 1
    while d < SCAN_SEG:
        x = x + jnp.where(pos >= d, pltpu.roll(x, d, axis=0), 0.0)
        d *= 2
    segs = []
    for r0 in range(0, n, SCAN_SEG):
        seg = x[r0:r0 + SCAN_SEG] + c_in
        c_in = seg[SCAN_SEG - 1:SCAN_SEG]
        segs.append(seg)
    return jnp.concatenate(segs, axis=0)


def _mix_ab(zab, cw_ref, cb_ref, wg_ref, bg_ref, lam_ref, scw_ref, on_ref, out_ref, xext_ref, uext_ref, h_ref):
    ts = zab.shape[0]
    xr = zab[:, 0:GROUP]
    gate = zab[:, GROUP:2 * GROUP]
    bgt = zab[:, 2 * GROUP:3 * GROUP]
    cgt = zab[:, 3 * GROUP:4 * GROUP]
    xs = zab[:, 4 * GROUP:5 * GROUP]

    xext_ref[SUBLANES:SUBLANES + ts, :] = xr
    xc = cb_ref[...] + cw_ref[LRU_CONV - 1:LRU_CONV, :] * xr
    for k in range(LRU_CONV - 1):
        xc = xc + cw_ref[k:k + 1, :] * xext_ref[pl.ds(SUBLANES - (LRU_CONV - 1) + k, ts), :]
    xext_ref[0:SUBLANES, :] = xr[ts - SUBLANES:ts, :]

    gi = _mm(xc, wg_ref[...]) + bg_ref[...]
    r = jax.nn.sigmoid(gi[:, 0:GROUP])
    ig = jax.nn.sigmoid(gi[:, GROUP:2 * GROUP])
    lam = lam_ref[...]
    softplus_neg = jnp.maximum(-lam, 0.0) + jnp.log1p(jnp.exp(-jnp.abs(lam)))
    log_a = (-LRU_C * softplus_neg) * r
    a = jnp.exp(log_a)
    g = -jnp.tanh(log_a) * (a * a + 1.0)
    b = jnp.where(g > 0.0, g * lax.rsqrt(g), 0.0) * (ig * xc)
    h = _linear_scan_rows(a, b, h_ref[0:1, :])
    h_ref[...] = jnp.broadcast_to(h[ts - 1:ts, :], h_ref.shape)
    y_a = h * _gelu(gate)

    u = cgt * xs
    uext_ref[SUBLANES:SUBLANES + ts, :] = u
    cv = scw_ref[SC_CONV - 1:SC_CONV, :] * u
    for k in range(SC_CONV - 1):
        cv = cv + scw_ref[k:k + 1, :] * uext_ref[pl.ds(SUBLANES - (SC_CONV - 1) + k, ts), :]
    uext_ref[0:SUBLANES, :] = u[ts - SUBLANES:ts, :]
    y_b = bgt * cv

    out_ref[0, :, 0:GROUP] = (y_a * _rms_scale(y_a) * on_ref[:, 0:GROUP]).astype(out_ref.dtype)
    out_ref[0, :, GROUP:2 * GROUP] = (y_b * _rms_scale(y_b) * on_ref[:, GROUP:2 * GROUP]).astype(out_ref.dtype)


def _fox_prep(zfox, zsm, fb_ref, gq_ref, gk_ref, q_ref, k_ref, v_ref, c_ref):
    ts = zfox.shape[0]
    shape = (ts, LANES)
    lane = _lane(shape)
    low = lane < HEAD_DIM

    f = zsm + fb_ref[...]
    logf = jnp.minimum(f, 0.0) - jnp.log1p(jnp.exp(-jnp.abs(f)))
    c = _cumsum_rows(logf, c_ref[0:1, :])
    c_ref[...] = jnp.broadcast_to(c[ts - 1:ts, :], c_ref.shape)
    neg_c = -LOG2E * c
    p1 = _round_mxu(neg_c)
    r1 = neg_c - p1
    p2 = _round_mxu(r1)
    p3 = _round_mxu(r1 - p2)
    piece = lane - 3 * ((lane * 11) >> 5)
    csel = jnp.where(piece == 0, p1, jnp.where(piece == 1, p2, p3))

    ones_aug = jnp.where(lane < HEAD_DIM + 3, 1.0, 0.0)
    for p in range(N_HEADS // 2):
        qs = zfox[:, LANES * p:LANES * (p + 1)]
        ks = zfox[:, GROUP + LANES * p:GROUP + LANES * (p + 1)]
        qn = qs * _half_rms_scale(qs, low) * gq_ref[...]
        kn = ks * _half_rms_scale(ks, low) * gk_ref[...]
        for half in range(2):
            h = 2 * p + half
            qh = qn if half == 0 else pltpu.roll(qn, HEAD_DIM, axis=1)
            kh = kn if half == 0 else pltpu.roll(kn, HEAD_DIM, axis=1)
            ch = pltpu.roll(csel, HEAD_DIM - 3 * h, axis=1)
            q_ref[0, h] = jnp.where(low, qh, ones_aug).astype(q_ref.dtype)
            k_ref[0, h] = jnp.where(low, kh, jnp.where(lane < HEAD_DIM + 3, ch, 0.0)).astype(k_ref.dtype)
    for h in range(N_HEADS):
        vh = zfox[:, 2 * GROUP + LANES * h:2 * GROUP + LANES * (h + 1)]
        v_ref[0, h] = jnp.where(lane == FOX_SUM_LANE, 1.0, vh).astype(v_ref.dtype)


def _nsa_prep(zq, zsel, zwin, zsm, row0, gq_ref, gs_ref, gw_ref, gb_ref,
              q4_ref, ks_ref, vs_ref, kw_ref, vw_ref, gate_ref):
    ts = zq.shape[0]
    shape = (ts, LANES)
    lane = _lane(shape)
    low = lane < HEAD_DIM
    for p in range(N_HEADS // 2):
        qs = zq[:, LANES * p:LANES * (p + 1)]
        qn = qs * _half_rms_scale(qs, low) * gq_ref[...]
        q4_ref[0, 2 * p] = jnp.where(low, qn, 0.0).astype(q4_ref.dtype)
        q4_ref[0, 2 * p + 1] = jnp.where(low, pltpu.roll(qn, HEAD_DIM, axis=1), 0.0).astype(q4_ref.dtype)

    blk = (row0 + _row(shape)) >> SLC_SHIFT
    onehot = jnp.where(lane - HEAD_DIM == blk, 1.0, 0.0)
    ks_ref[0] = jnp.where(low, zsel * _low_half_rms_scale(zsel, low) * gs_ref[...], onehot).astype(ks_ref.dtype)
    vs_ref[0] = jnp.where(lane == SUM_LANE, 1.0, zsel).astype(vs_ref.dtype)
    kw_ref[0] = jnp.where(low, zwin * _low_half_rms_scale(zwin, low) * gw_ref[...], 0.0).astype(kw_ref.dtype)
    vw_ref[0] = jnp.where(lane == SUM_LANE, 1.0, zwin).astype(vw_ref.dtype)
    gate_ref[0] = jax.nn.sigmoid(zsm + gb_ref[...])


def _proj_mix_kernel(x_ref, g_ref, w_ref,
                     cw_ref, cb_ref, wg_ref, bg_ref, lam_ref, scw_ref, on_ref,
                     fb_ref, fgq_ref, fgk_ref, ngq_ref, ngs_ref, ngw_ref, gb_ref,
                     mab_ref, qf_ref, kf_ref, vf_ref, q4_ref, ks_ref, vs_ref, kw_ref, vw_ref, gate_ref, zcmp_ref,
                     xext_ref, uext_ref, h_ref, c_ref, zc_ref):
    ts = x_ref.shape[1]
    ti = pl.program_id(1)

    @pl.when(ti == 0)
    def _():
        xext_ref[0:SUBLANES, :] = jnp.zeros((SUBLANES, GROUP), _F32)
        uext_ref[0:SUBLANES, :] = jnp.zeros((SUBLANES, GROUP), _F32)
        h_ref[...] = jnp.zeros_like(h_ref)
        c_ref[...] = jnp.zeros_like(c_ref)

    x = x_ref[0]
    xn = (x * _rms_scale(x) * g_ref[...]).astype(_MXU)
    zab = jnp.dot(xn, w_ref[:, 0:COL_FOX], preferred_element_type=_F32)
    slabs = jnp.dot(xn, w_ref[:, COL_SLABS:N_IN_PAD], preferred_element_type=_F32)
    zfox = jnp.dot(xn, w_ref[:, COL_FOX:COL_NSA_Q], preferred_element_type=_F32)
    zq = jnp.dot(xn, w_ref[:, COL_NSA_Q:COL_SLABS], preferred_element_type=_F32)
    zc_ref[...] = slabs[:, 0:LANES]
    for tok in range(CMP_STRIDE):
        zcmp_ref[0, :, tok * LANES:(tok + 1) * LANES] = zc_ref[pl.ds(tok, ts // CMP_STRIDE, stride=CMP_STRIDE), :]
    zsel = slabs[:, LANES:2 * LANES]
    zwin = slabs[:, 2 * LANES:3 * LANES]
    zsm = slabs[:, 3 * LANES:4 * LANES]
    _fox_prep(zfox, zsm, fb_ref, fgq_ref, fgk_ref, qf_ref, kf_ref, vf_ref, c_ref)
    _nsa_prep(zq, zsel, zwin, zsm, ti * ts, ngq_ref, ngs_ref, ngw_ref, gb_ref,
              q4_ref, ks_ref, vs_ref, kw_ref, vw_ref, gate_ref)
    _mix_ab(zab, cw_ref, cb_ref, wg_ref, bg_ref, lam_ref, scw_ref, on_ref, mab_ref, xext_ref, uext_ref, h_ref)


def _proj_mix(x, gain, w, cw, cb, wg, bg, lam, scw, on_ab, fb, fgq, fgk, ngq, ngs, ngw, gb):
    bsz, s_len, _ = x.shape
    ts = TS_MIX
    vec = lambda width: pl.BlockSpec((1, width), lambda b, t: (0, 0))
    rows = lambda width: pl.BlockSpec((1, ts, width), lambda b, t: (b, t, 0))
    heads = pl.BlockSpec((1, N_HEADS, ts, LANES), lambda b, t: (b, 0, t, 0))
    sds = jax.ShapeDtypeStruct
    return pl.pallas_call(
        _proj_mix_kernel,
        grid=(bsz, s_len // ts),
        in_specs=[rows(D_MODEL), vec(D_MODEL), _resident((D_MODEL, N_IN_PAD)),
                  pl.BlockSpec((LRU_CONV, GROUP), lambda b, t: (0, 0)), vec(GROUP), _resident((GROUP, 2 * GROUP)),
                  vec(2 * GROUP), vec(GROUP), pl.BlockSpec((SC_CONV, GROUP), lambda b, t: (0, 0)), vec(2 * GROUP),
                  vec(LANES), vec(LANES), vec(LANES), vec(LANES), vec(LANES), vec(LANES), vec(LANES)],
        out_specs=[rows(2 * GROUP), heads, heads, heads, heads, rows(LANES), rows(LANES),
                   rows(LANES), rows(LANES), rows(LANES),
                   pl.BlockSpec((1, ts // CMP_STRIDE, CMP_STRIDE * LANES), lambda b, t: (b, t, 0))],
        out_shape=[sds((bsz, s_len, 2 * GROUP), _MXU),
                   sds((bsz, N_HEADS, s_len, LANES), _MXU), sds((bsz, N_HEADS, s_len, LANES), _MXU),
                   sds((bsz, N_HEADS, s_len, LANES), _MXU),
                   sds((bsz, N_HEADS, s_len, LANES), _MXU),
                   sds((bsz, s_len, LANES), _MXU), sds((bsz, s_len, LANES), _MXU),
                   sds((bsz, s_len, LANES), _MXU), sds((bsz, s_len, LANES), _MXU),
                   sds((bsz, s_len, LANES), _F32), sds((bsz, s_len // CMP_STRIDE, CMP_STRIDE * LANES), _F32)],
        scratch_shapes=[pltpu.VMEM((ts + SUBLANES, GROUP), _F32), pltpu.VMEM((ts + SUBLANES, GROUP), _F32),
                        pltpu.VMEM((SUBLANES, GROUP), _F32), pltpu.VMEM((SUBLANES, LANES), _F32),
                        pltpu.VMEM((ts, LANES), _F32)],
        compiler_params=_params(("parallel", "arbitrary")),
        name="proj_mix",
    )(x, gain, w, cw, cb, wg, bg, lam, scw, on_ab, fb, fgq, fgk, ngq, ngs, ngw, gb)


def _flash_step_sumlane(s, v, carry):
    m, acc = carry
    m_new = jnp.maximum(m, jnp.max(s, axis=-1, keepdims=True))
    p = jnp.exp2(s - m_new).astype(_MXU)
    pv = jnp.dot(p, v, preferred_element_type=_F32)
    return m_new, jnp.exp2(m - m_new) * acc + pv


def _flash_init_sumlane(rows):
    return (jnp.full((rows, 1), NEG, _F32), jnp.zeros((rows, LANES), _F32))


def _sumlane_normalize(acc, sum_lane):
    return acc / acc[:, sum_lane:sum_lane + 1]


def _fox_attn_kernel(q_ref, k_ref, v_ref, on_ref, out_ref):
    tq = q_ref.shape[2]
    qi = pl.program_id(1)
    qs = [q_ref[0, h] for h in range(N_HEADS)]

    def step(c, carries, causal=False):
        start = pl.multiple_of(c * tq, tq)
        new = []
        for h in range(N_HEADS):
            s = _mm_nt(qs[h], k_ref[0, h, pl.ds(start, tq), :])
            if causal:
                s = jnp.where(_lane((tq, tq)) <= _row((tq, tq)), s, NEG)
            new.append(_flash_step_sumlane(s, v_ref[0, h, pl.ds(start, tq), :], carries[h]))
        return tuple(new)

    carries = lax.fori_loop(0, qi, step, tuple(_flash_init_sumlane(tq) for _ in range(N_HEADS)))
    carries = step(qi, carries, causal=True)
    outs = [_sumlane_normalize(acc, FOX_SUM_LANE) for _, acc in carries]
    low = _lane((tq, LANES)) < HEAD_DIM
    y = jnp.concatenate([jnp.where(low, outs[2 * p], pltpu.roll(outs[2 * p + 1], HEAD_DIM, axis=1))
                         for p in range(N_HEADS // 2)], axis=1)
    out_ref[0] = (y * _rms_scale(y) * on_ref[...]).astype(out_ref.dtype)


def _fox_attn(qf, kf, vf, on):
    bsz, _, s_len, _ = qf.shape
    tq = min(TQ_FOX, s_len)
    assert s_len % tq == 0
    return pl.pallas_call(
        _fox_attn_kernel,
        grid=(bsz, s_len // tq),
        in_specs=[pl.BlockSpec((1, N_HEADS, tq, LANES), lambda b, i: (b, 0, i, 0)),
                  pl.BlockSpec((1, N_HEADS, s_len, LANES), lambda b, i: (b, 0, 0, 0)),
                  pl.BlockSpec((1, N_HEADS, s_len, LANES), lambda b, i: (b, 0, 0, 0)),
                  pl.BlockSpec((1, GROUP), lambda b, i: (0, 0))],
        out_specs=pl.BlockSpec((1, tq, GROUP), lambda b, i: (b, i, 0)),
        out_shape=jax.ShapeDtypeStruct((bsz, s_len, GROUP), _MXU),
        compiler_params=_params(("parallel", "arbitrary")),
        name="fox_attn",
    )(qf, kf, vf, on)


def _compress_kernel(x_ref, pa_ref, pb_ref, w1a_ref, w1b_ref, w2_ref, g_ref, kc_ref, cv_ref):
    nc = x_ref.shape[1]
    x = x_ref[0]
    ua = _mm(x + pa_ref[...], w1a_ref[...])
    ub = _mm(x + pb_ref[...], w1b_ref[...])
    hid = _gelu(ua + pltpu.roll(ub, nc - 1, axis=0))
    out = _mm(hid, w2_ref[...])
    low = _lane(out.shape) < HEAD_DIM
    kc_ref[0] = jnp.where(low, out * _low_half_rms_scale(out, low) * g_ref[...], 0.0).astype(kc_ref.dtype)
    cv_ref[0] = out.astype(cv_ref.dtype)


def _compress(xc, pa, pb, w1a, w1b, w2, g):
    bsz, nc, width = xc.shape
    const = lambda shape: pl.BlockSpec(shape, lambda b: (0, 0))
    slab = pl.BlockSpec((1, nc, LANES), lambda b: (b, 0, 0))
    return pl.pallas_call(
        _compress_kernel,
        grid=(bsz,),
        in_specs=[pl.BlockSpec((1, nc, width), lambda b: (b, 0, 0)),
                  const((1, width)), const((1, width)), const((width, 2 * CMP_HIDDEN)),
                  const((width, 2 * CMP_HIDDEN)), const((2 * CMP_HIDDEN, LANES)), const((1, LANES))],
        out_specs=[slab, slab],
        out_shape=[jax.ShapeDtypeStruct((bsz, nc, LANES), _MXU)] * 2,
        compiler_params=_params(("parallel",)),
        name="compress",
    )(xc, pa, pb, w1a, w1b, w2, g)


def _bucket_thresholds():
    max_exact = REL_BUCKETS // 2
    d = np.arange(0, REL_MAX_DIST + 1)
    large = max_exact + (np.log(np.maximum(d, 1).astype(np.float32) / max_exact)
                         / math.log(REL_MAX_DIST / max_exact) * (REL_BUCKETS - max_exact)).astype(np.int32)
    bucket = np.where(d < max_exact, d, np.minimum(large, REL_BUCKETS - 1))
    assert bucket[-1] == REL_BUCKETS - 1 and np.all(np.diff(bucket) >= 0)
    return [int(np.argmax(bucket >= k)) for k in range(REL_BUCKETS)]


_BUCKET_THR = _bucket_thresholds()

NSA_R = TQ_NSA // KT_NSA
NSA_D = WINDOW // KT_NSA
NSA_TAIL = NSA_D + NSA_R
WIN_ENTRIES = NSA_TAIL + 1
SEL_ENTRIES = NSA_R + 3


def _win_entry(e):
    return jnp.minimum(e, NSA_TAIL)


def _sel_entry(e):
    return jnp.clip(e - (NSA_D - 2), 0, NSA_R + 2)


def _rel_bias(dist, rb_ref, h):
    far = rb_ref[REL_BUCKETS - 1, h]
    val = jnp.full(dist.shape, LOG2E * (rb_ref[0, h] - far), _F32)
    for k in range(1, REL_BUCKETS):
        val = jnp.where(dist >= _BUCKET_THR[k], LOG2E * (rb_ref[k, h] - far), val)
    return val


def _nsa_tables_kernel(rb_ref, win_ref, sel_ref, cmp_ref):
    qi = pl.program_id(0)
    tq = TQ_NSA
    nc = cmp_ref.shape[2]

    @pl.when(qi == 0)
    def _():
        i = _row((tq, KT_NSA))
        c = _lane((tq, KT_NSA))
        for h in range(N_HEADS):
            rs = slice(h * tq, (h + 1) * tq)
            for e in range(NSA_TAIL):
                dist = i + (NSA_D - e) * KT_NSA - c
                ok = jnp.where(dist >= 0, jnp.where(dist < WINDOW, 1, 0), 0) > 0
                tab = jnp.where(ok, _rel_bias(dist, rb_ref, h), NEG)
                win_ref[e, rs, :] = tab
                if e >= NSA_D - 1:
                    sel_ref[e - (NSA_D - 2), rs, :] = tab
            win_ref[NSA_TAIL, rs, :] = jnp.full((tq, KT_NSA), NEG, _F32)
            sel_ref[0, rs, :] = jnp.zeros((tq, KT_NSA), _F32)
            sel_ref[NSA_R + 2, rs, :] = jnp.full((tq, KT_NSA), NEG, _F32)

    t = qi * tq + _row((tq, nc))
    dist = t - (CMP_STRIDE * _lane((tq, nc)) + CMP_LEN - 1)
    for h in range(N_HEADS):
        cmp_ref[0, h * tq:(h + 1) * tq, :] = jnp.where(dist >= 0, _rel_bias(dist, rb_ref, h), NEG)


def _nsa_tables(rel_bias, s_len):
    nq = s_len // TQ_NSA
    nc = s_len // CMP_STRIDE
    rows = N_HEADS * TQ_NSA
    return pl.pallas_call(
        _nsa_tables_kernel,
        grid=(nq,),
        in_specs=[pl.BlockSpec(memory_space=pltpu.SMEM)],
        out_specs=[pl.BlockSpec((WIN_ENTRIES, rows, KT_NSA), lambda i: (0, 0, 0)),
                   pl.BlockSpec((SEL_ENTRIES, rows, KT_NSA), lambda i: (0, 0, 0)),
                   pl.BlockSpec((1, rows, nc), lambda i: (i, 0, 0))],
        out_shape=[jax.ShapeDtypeStruct((WIN_ENTRIES, rows, KT_NSA), _F32),
                   jax.ShapeDtypeStruct((SEL_ENTRIES, rows, KT_NSA), _F32),
                   jax.ShapeDtypeStruct((nq, rows, nc), _F32)],
        compiler_params=_params(("arbitrary",)),
        name="nsa_tables",
    )(rel_bias)


def _topk_penalty(imp_t, qi, tq):
    n_blk = imp_t.shape[0]
    blk = _row((n_blk, tq))
    cur = (qi * tq + _lane((n_blk, tq))) >> SLC_SHIFT
    forced = jnp.where(blk == 0, 1, jnp.where(blk == cur, 1, jnp.where(blk == cur - 1, 1, 0))) > 0
    val = jnp.where(blk <= cur, jnp.where(forced, BIG, imp_t), NEG)
    groups = [val[g * SUBLANES:(g + 1) * SUBLANES, :] for g in range(n_blk // SUBLANES)]
    sub = _row((SUBLANES, tq))
    ranks = [jnp.zeros((SUBLANES, tq), _F32) for _ in groups]
    for j in range(n_blk):
        vj = val[j:j + 1, :]
        jg, jr = divmod(j, SUBLANES)
        for g, vg in enumerate(groups):
            ge = jnp.where(vj >= vg, 1.0, 0.0)
            gt = jnp.where(vj > vg, 1.0, 0.0)
            if g > jg:
                ranks[g] = ranks[g] + ge
            elif g < jg:
                ranks[g] = ranks[g] + gt
            else:
                ranks[g] = ranks[g] + jnp.where(sub > jr, ge, gt)
    rank = jnp.concatenate(ranks, axis=0)
    return jnp.where(rank < float(SLC_TOPK), 0.0, NEG)


def _nsa_attn_kernel(q4_ref, kc_ref, cv_ref, ks_ref, vs_ref, kw_ref, vw_ref, gate_ref,
                     win_ref, sel_ref, cmp_ref, ovl_ref, on_ref, out_ref):
    tq = TQ_NSA
    kt = KT_NSA
    rows = N_HEADS * tq
    qi = pl.program_id(1)
    first = qi * NSA_R
    q4 = q4_ref[0].reshape(rows, LANES)
    lane = _lane((tq, LANES))
    low = lane < HEAD_DIM
    tail_w = NSA_TAIL * kt

    def tail_table(tab_ref, entry_of, first_tile, skip_before=0):
        pieces = []
        for u in range(NSA_TAIL):
            tile = first_tile + u
            entry = entry_of(tile - first + NSA_D)
            pieces.append(tab_ref[jnp.where(tile < skip_before, tab_ref.shape[0] - 1, entry)])
        return jnp.concatenate(pieces, axis=1)

    s = _mm_nt(q4, kc_ref[0]) + cmp_ref[0]
    valid = s > 0.5 * NEG
    m = jnp.max(s, axis=-1, keepdims=True)
    p = jnp.where(valid, jnp.exp2(s - m), 0.0)
    l = jnp.sum(p, axis=-1, keepdims=True)
    p = p / jnp.where(l > 0.0, l, 1.0)
    o_c = jnp.dot(p.astype(_MXU), cv_ref[0], preferred_element_type=_F32)

    psum = p[0:tq] + p[tq:2 * tq] + p[2 * tq:3 * tq] + p[3 * tq:4 * tq]
    p_hi = _round_mxu(psum)
    imp = _mm(p_hi, ovl_ref[...]) + _mm(psum - p_hi, ovl_ref[...])

    pen_t = _topk_penalty(imp.T[0:SLC_BLOCK, :], qi, tq)
    pen = jnp.concatenate([jnp.zeros((LANES - SLC_BLOCK, tq), _F32), pen_t], axis=0).T
    pen4 = jnp.concatenate([pen] * N_HEADS, axis=0)
    q_aug = jnp.where(_lane((rows, LANES)) < HEAD_DIM, q4.astype(_F32), pen4).astype(_MXU)

    w_tile = jnp.maximum(first - NSA_D, 0)
    w_start = pl.multiple_of(w_tile * kt, kt)
    s = _mm_nt(q4, kw_ref[0, pl.ds(w_start, tail_w), :]) + tail_table(win_ref, _win_entry, w_tile)
    _, acc_w = _flash_step_sumlane(s, vw_ref[0, pl.ds(w_start, tail_w), :], _flash_init_sumlane(rows))
    o_w = _sumlane_normalize(acc_w, SUM_LANE)

    def sel_far(width):
        def step(c, carry):
            start = pl.multiple_of(c * width, width)
            s = _mm_nt(q_aug, ks_ref[0, pl.ds(start, width), :])
            return _flash_step_sumlane(s, vs_ref[0, pl.ds(start, width), :], carry)
        return step

    far_w = NSA_FAR * kt
    n_far = jnp.maximum(first - 1, 0) // NSA_FAR
    n_dbl = n_far // 2
    carry = lax.fori_loop(0, n_dbl, sel_far(2 * far_w), _flash_init_sumlane(rows))
    carry = lax.fori_loop(2 * n_dbl, n_far, sel_far(far_w), carry)
    done = n_far * NSA_FAR
    t_tile = jnp.minimum(done, ks_ref.shape[1] // kt - NSA_TAIL)
    t_start = pl.multiple_of(t_tile * kt, kt)
    s = _mm_nt(q_aug, ks_ref[0, pl.ds(t_start, tail_w), :]) + tail_table(sel_ref, _sel_entry, t_tile, done)
    _, acc_s = _flash_step_sumlane(s, vs_ref[0, pl.ds(t_start, tail_w), :], carry)
    o_s = _sumlane_normalize(acc_s, SUM_LANE)

    g = gate_ref[0]
    heads = []
    for h in range(N_HEADS):
        rs = slice(h * tq, (h + 1) * tq)
        gc = g[:, SM_GATE_LANE + h:SM_GATE_LANE + h + 1]
        gs = g[:, SM_GATE_LANE + N_HEADS + h:SM_GATE_LANE + N_HEADS + h + 1]
        gw = g[:, SM_GATE_LANE + 2 * N_HEADS + h:SM_GATE_LANE + 2 * N_HEADS + h + 1]
        heads.append(gc * o_c[rs] + gs * o_s[rs] + gw * o_w[rs])
    slabs = [jnp.where(low, pltpu.roll(heads[2 * p], HEAD_DIM, axis=1), heads[2 * p + 1])
             for p in range(N_HEADS // 2)]
    y = jnp.concatenate(slabs, axis=1)
    out_ref[0] = (y * _rms_scale(y) * on_ref[...]).astype(out_ref.dtype)


def _nsa_attn(q4, kc, cv, ks, vs, kw, vw, gates, win_tab, sel_tab, cmp_tab, ovl, on):
    bsz, _, s_len, _ = q4.shape
    tq = TQ_NSA
    nc = kc.shape[1]
    rows = N_HEADS * tq
    assert s_len >= NSA_TAIL * KT_NSA
    full = lambda n: pl.BlockSpec((1, n, LANES), lambda b, i: (b, 0, 0))
    return pl.pallas_call(
        _nsa_attn_kernel,
        grid=(bsz, s_len // tq),
        in_specs=[pl.BlockSpec((1, N_HEADS, tq, LANES), lambda b, i: (b, 0, i, 0)),
                  full(nc), full(nc), full(s_len), full(s_len), full(s_len), full(s_len),
                  pl.BlockSpec((1, tq, LANES), lambda b, i: (b, i, 0)),
                  _resident((WIN_ENTRIES, rows, KT_NSA)), _resident((SEL_ENTRIES, rows, KT_NSA)),
                  pl.BlockSpec((1, rows, nc), lambda b, i: (i, 0, 0)),
                  _resident((nc, LANES)),
                  pl.BlockSpec((1, GROUP), lambda b, i: (0, 0))],
        out_specs=pl.BlockSpec((1, tq, GROUP), lambda b, i: (b, i, 0)),
        out_shape=jax.ShapeDtypeStruct((bsz, s_len, GROUP), _MXU),
        compiler_params=_params(("parallel", "arbitrary")),
        name="nsa_attn",
    )(q4, kc, cv, ks, vs, kw, vw, gates, win_tab, sel_tab, cmp_tab, ovl, on)


def _ffn_kernel(x_ref, mab_ref, mc_ref, md_ref, wo_ref, g_ref, wgu_ref, wd_ref, out_ref):
    x1 = (x_ref[...]
          + jnp.dot(mab_ref[...], wo_ref[0:2 * GROUP, :], preferred_element_type=_F32)
          + jnp.dot(mc_ref[...], wo_ref[2 * GROUP:3 * GROUP, :], preferred_element_type=_F32)
          + jnp.dot(md_ref[...], wo_ref[3 * GROUP:4 * GROUP, :], preferred_element_type=_F32))
    xn = (x1 * _rms_scale(x1) * g_ref[...]).astype(_MXU)
    out_ref[...] = x1
    for j in range(D_FF // TF_FFN):
        cols = slice(j * TF_FFN, (j + 1) * TF_FFN)
        gt = jnp.dot(xn, wgu_ref[:, cols], preferred_element_type=_F32)
        up = jnp.dot(xn, wgu_ref[:, D_FF + j * TF_FFN:D_FF + (j + 1) * TF_FFN], preferred_element_type=_F32)
        hid = (gt * jax.nn.sigmoid(gt)) * up
        out_ref[...] += jnp.dot(hid.astype(_MXU), wd_ref[cols, :], preferred_element_type=_F32)


def _out_ffn(x2d, mab, mc, md, wo, g, wgu, wd):
    t = x2d.shape[0]
    tm = min(TM_FFN, t)
    assert t % tm == 0 and D_FF % TF_FFN == 0
    row = lambda width: pl.BlockSpec((tm, width), lambda i: (i, 0))
    return pl.pallas_call(
        _ffn_kernel,
        grid=(t // tm,),
        in_specs=[row(D_MODEL), row(2 * GROUP), row(GROUP), row(GROUP),
                  pl.BlockSpec((D_MODEL, D_MODEL), lambda i: (0, 0)), pl.BlockSpec((1, D_MODEL), lambda i: (0, 0)),
                  pl.BlockSpec((D_MODEL, 2 * D_FF), lambda i: (0, 0)), pl.BlockSpec((D_FF, D_MODEL), lambda i: (0, 0))],
        out_specs=row(D_MODEL),
        out_shape=jax.ShapeDtypeStruct((t, D_MODEL), _F32),
        compiler_params=_params(("parallel",)),
        name="out_ffn",
    )(x2d, mab, mc, md, wo, g, wgu, wd)


def _lane_vec(values, start):
    v = jnp.zeros((LANES,), _F32).at[start:start + values.shape[0]].set(values.astype(_F32))
    return v[None, :]


def _block_diag(w):
    h, d, _ = w.shape
    eye = jnp.eye(h, dtype=w.dtype)
    return (eye[:, None, :, None] * w[:, :, None, :]).reshape(h * d, h * d)


def _overlap_ext(s_len):
    nc = s_len // CMP_STRIDE
    n_cmp = nc - 1
    n_slc = s_len // SLC_BLOCK
    cs = np.arange(n_cmp)[:, None] * CMP_STRIDE
    ss = np.arange(n_slc)[None, :] * SLC_BLOCK
    ov = np.clip(np.minimum(cs + CMP_LEN, ss + SLC_BLOCK) - np.maximum(cs, ss), 0, CMP_LEN)
    ext = np.zeros((nc, LANES), np.float32)
    ext[:n_cmp, :n_slc] = ov
    return jnp.asarray(ext, _MXU)


def _layer_params(l, w_in, lru_w_gates, lru_b_gates, fox_f_bias, fox_qk_norm, nsa_qk_norm, nsa_cmp_pos,
                  nsa_cmp_w1, nsa_cmp_w2, nsa_gate_bias, out_norm):
    scale = HEAD_DIM ** -0.5 * LOG2E
    w = w_in[l]
    small = jnp.zeros((D_MODEL, LANES), w.dtype)
    small = small.at[:, 0:3 * N_HEADS].set(jnp.repeat(w[:, C_FOX_F:C_FOX_F + N_HEADS], 3, axis=1))
    small = small.at[:, SM_GATE_LANE:SM_GATE_LANE + 3 * N_HEADS].set(w[:, C_NSA_G:C_NSA_G + 3 * N_HEADS])
    c_fox_v = C_FOX_F - GROUP
    fox_v = jnp.pad(w[:, c_fox_v:C_FOX_F].reshape(D_MODEL, N_HEADS, HEAD_DIM),
                    ((0, 0), (0, 0), (0, LANES - HEAD_DIM))).reshape(D_MODEL, N_HEADS * LANES)
    w_perm = jnp.concatenate([w[:, :c_fox_v], fox_v, w[:, C_NSA_Q:C_NSA_G], small], axis=1).astype(_MXU)
    assert w_perm.shape[1] == N_IN_PAD

    wg = jnp.concatenate([_block_diag(lru_w_gates[l, 0]), _block_diag(lru_w_gates[l, 1])], axis=1).astype(_MXU)
    bg = lru_b_gates[l].reshape(1, 2 * GROUP)

    fb = _lane_vec(jnp.repeat(fox_f_bias[l], 3), 0)
    fox_gq = (jnp.tile(fox_qk_norm[l, 0], 2) * scale)[None, :]
    fox_gk = jnp.tile(fox_qk_norm[l, 1], 2)[None, :]

    nsa_gq = (jnp.tile(nsa_qk_norm[l, 0], 2) * scale)[None, :]
    nsa_gc = _lane_vec(nsa_qk_norm[l, 1], 0)
    nsa_gs = _lane_vec(nsa_qk_norm[l, 2], 0)
    nsa_gw = _lane_vec(nsa_qk_norm[l, 3], 0)
    gb = _lane_vec(nsa_gate_bias[l], SM_GATE_LANE)

    half = CMP_LEN // 2

    def pos_ext(lo):
        return jnp.concatenate([nsa_cmp_pos[l, 0, lo:lo + half], nsa_cmp_pos[l, 1, lo:lo + half]],
                               axis=1).reshape(1, half * LANES)

    def w1_ext(lo):
        ext = jnp.zeros((half, LANES, 2 * CMP_HIDDEN), _F32)
        ext = ext.at[:, :HEAD_DIM, :CMP_HIDDEN].set(nsa_cmp_w1[l, 0, lo:lo + half])
        ext = ext.at[:, HEAD_DIM:, CMP_HIDDEN:].set(nsa_cmp_w1[l, 1, lo:lo + half])
        return ext.reshape(half * LANES, 2 * CMP_HIDDEN).astype(_MXU)

    w2 = jnp.zeros((2 * CMP_HIDDEN, LANES), _F32)
    w2 = w2.at[:CMP_HIDDEN, :HEAD_DIM].set(nsa_cmp_w2[l, 0]).at[CMP_HIDDEN:, HEAD_DIM:].set(nsa_cmp_w2[l, 1])
    return dict(w_perm=w_perm, wg=wg, bg=bg, fb=fb, fox_gq=fox_gq, fox_gk=fox_gk, nsa_gq=nsa_gq, nsa_gc=nsa_gc,
                nsa_gs=nsa_gs, nsa_gw=nsa_gw, gb=gb, pa=pos_ext(0), pb=pos_ext(half), w1a=w1_ext(0),
                w1b=w1_ext(half), w2=w2.astype(_MXU), on=out_norm[l].reshape(1, 4 * GROUP))


def kernel(x, norm_mix, w_in, lru_conv_w, lru_conv_b, lru_w_gates, lru_b_gates, lru_lambda, sc_conv_w, fox_f_bias, fox_qk_norm, nsa_qk_norm, nsa_cmp_pos, nsa_cmp_w1, nsa_cmp_w2, nsa_gate_bias, rel_bias, out_norm, w_out, norm_ffn, w_gate_up, w_down):
    bsz, s_len, d_model = x.shape
    depth = w_in.shape[0]
    assert d_model == D_MODEL and s_len % TS_MIX == 0 and s_len % TQ_FOX == 0
    assert SLC_TOPK <= s_len // SLC_BLOCK <= SLC_BLOCK
    t = bsz * s_len

    win_tab, sel_tab, cmp_tab = _nsa_tables(rel_bias, s_len)
    ovl = _overlap_ext(s_len)

    for l in range(depth):
        lp = _layer_params(l, w_in, lru_w_gates, lru_b_gates, fox_f_bias, fox_qk_norm, nsa_qk_norm,
                           nsa_cmp_pos, nsa_cmp_w1, nsa_cmp_w2, nsa_gate_bias, out_norm)
        m_ab, qf, kf, vf, q4, ks, vs, kw, vw, gates, zcmp = _proj_mix(
            x, norm_mix[l][None, :], lp["w_perm"], lru_conv_w[l], lru_conv_b[l][None, :], lp["wg"], lp["bg"],
            lru_lambda[l][None, :], sc_conv_w[l], lp["on"][:, 0:2 * GROUP],
            lp["fb"], lp["fox_gq"], lp["fox_gk"], lp["nsa_gq"], lp["nsa_gs"], lp["nsa_gw"], lp["gb"])
        m_c = _fox_attn(qf, kf, vf, lp["on"][:, 2 * GROUP:3 * GROUP])
        kc, cv = _compress(zcmp, lp["pa"], lp["pb"], lp["w1a"], lp["w1b"], lp["w2"], lp["nsa_gc"])
        m_d = _nsa_attn(q4, kc, cv, ks, vs, kw, vw, gates, win_tab, sel_tab, cmp_tab, ovl,
                        lp["on"][:, 3 * GROUP:4 * GROUP])
        x = _out_ffn(x.reshape(t, D_MODEL), m_ab.reshape(t, 2 * GROUP), m_c.reshape(t, GROUP),
                     m_d.reshape(t, GROUP), w_out[l].astype(_MXU), norm_ffn[l][None, :],
                     w_gate_up[l].astype(_MXU), w_down[l].astype(_MXU)).reshape(bsz, s_len, D_MODEL)
    return x
```

```python
import math

import numpy as np
import jax
import jax.numpy as jnp
from jax import lax
from jax.experimental import pallas as pl
from jax.experimental.pallas import tpu as pltpu

D_MODEL = 1024
GROUP = 256
HEAD_DIM = 64
N_HEADS = 4
LRU_CONV = 4
LRU_C = 8.0
SC_CONV = 3
CMP_LEN = 32
CMP_STRIDE = 16
CMP_HIDDEN = 128
SLC_BLOCK = 64
SLC_SHIFT = SLC_BLOCK.bit_length() - 1
SLC_TOPK = 16
WINDOW = 512
REL_BUCKETS = 32
REL_MAX_DIST = 128
D_FF = 2816
RMS_EPS = 1e-6
NEG = -1e30
BIG = 1e30
LOG2E = math.log2(math.e)

C_FOX_F = 8 * GROUP
C_NSA_Q = C_FOX_F + N_HEADS
C_NSA_G = C_NSA_Q + GROUP + 6 * HEAD_DIM

LANES = 128
SUBLANES = 8
VMEM_LIMIT_BYTES = 56 * 1024 * 1024

TS_MIX = 512
SCAN_SEG = 32
TQ_FOX = 1024
KT_NSA = 128
TQ_NSA = 256
NSA_FAR = 4
TM_FFN = 1024
TF_FFN = 256

SM_GATE_LANE = 16
SUM_LANE = 0
FOX_SUM_LANE = HEAD_DIM

COL_FOX = 5 * GROUP
COL_NSA_Q = COL_FOX + 2 * GROUP + N_HEADS * LANES
COL_SLABS = COL_NSA_Q + GROUP
N_IN_PAD = COL_SLABS + 4 * LANES

_MXU = jnp.bfloat16
_F32 = jnp.float32


def _params(sem):
    return pltpu.CompilerParams(dimension_semantics=sem, vmem_limit_bytes=VMEM_LIMIT_BYTES)


def _resident(shape):
    zeros = (0,) * len(shape)
    return pl.BlockSpec(shape, lambda *_: zeros, pipeline_mode=pl.Buffered(1))


def _mm(a, b):
    return jnp.dot(a.astype(_MXU), b.astype(_MXU), preferred_element_type=_F32)


def _mm_nt(a, b):
    return lax.dot_general(a.astype(_MXU), b.astype(_MXU), (((1,), (1,)), ((), ())),
                           preferred_element_type=_F32)


def _lane(shape):
    return lax.broadcasted_iota(jnp.int32, shape, len(shape) - 1)


def _row(shape):
    return lax.broadcasted_iota(jnp.int32, shape, 0)


def _rms_scale(x):
    return lax.rsqrt(jnp.mean(x * x, axis=-1, keepdims=True) + RMS_EPS)


def _gelu(x):
    c = math.sqrt(2.0 / math.pi)
    return x * (0.5 * (1.0 + jnp.tanh(c * (x + 0.044715 * (x * x * x)))))


def _round_mxu(x):
    return x.astype(_MXU).astype(_F32)


def _low_half_rms_scale(x, low):
    s = jnp.sum(jnp.where(low, x * x, 0.0), axis=-1, keepdims=True)
    return lax.rsqrt(s * (1.0 / HEAD_DIM) + RMS_EPS)


def _half_rms_scale(x, low):
    x2 = x * x
    s_lo = jnp.sum(jnp.where(low, x2, 0.0), axis=-1, keepdims=True)
    s_hi = jnp.sum(jnp.where(low, 0.0, x2), axis=-1, keepdims=True)
    return jnp.where(low, lax.rsqrt(s_lo * (1.0 / HEAD_DIM) + RMS_EPS),
                     lax.rsqrt(s_hi * (1.0 / HEAD_DIM) + RMS_EPS))


def _linear_scan_rows(a, b, h_in):
    n = a.shape[0]
    pos = _row(a.shape) & (SCAN_SEG - 1)
    d = 1
    while d < SCAN_SEG:
        keep = pos >= d
        a_prev = jnp.where(keep, pltpu.roll(a, d, axis=0), 1.0)
        b_prev = jnp.where(keep, pltpu.roll(b, d, axis=0), 0.0)
        b = a * b_prev + b
        a = a * a_prev
        d *= 2
    segs = []
    for r0 in range(0, n, SCAN_SEG):
        seg = b[r0:r0 + SCAN_SEG] + a[r0:r0 + SCAN_SEG] * h_in
        h_in = seg[SCAN_SEG - 1:SCAN_SEG]
        segs.append(seg)
    return jnp.concatenate(segs, axis=0)


def _cumsum_rows(x, c_in):
    n = x.shape[0]
    pos = _row(x.shape) & (SCAN_SEG - 1)
    d = 1
    while d < SCAN_SEG:
        x = x + jnp.where(pos >= d, pltpu.roll(x, d, axis=0), 0.0)
        d *= 2
    segs = []
    for r0 in range(0, n, SCAN_SEG):
        seg = x[r0:r0 + SCAN_SEG] + c_in
        c_in = seg[SCAN_SEG - 1:SCAN_SEG]
        segs.append(seg)
    return jnp.concatenate(segs, axis=0)


def _mix_ab(zab, cw_ref, cb_ref, wg_ref, bg_ref, lam_ref, scw_ref, on_ref, out_ref, xext_ref, uext_ref, h_ref):
    ts = zab.shape[0]
    xr = zab[:, 0:GROUP]
    gate = zab[:, GROUP:2 * GROUP]
    bgt = zab[:, 2 * GROUP:3 * GROUP]
    cgt = zab[:, 3 * GROUP:4 * GROUP]
    xs = zab[:, 4 * GROUP:5 * GROUP]

    xext_ref[SUBLANES:SUBLANES + ts, :] = xr
    xc = cb_ref[...] + cw_ref[LRU_CONV - 1:LRU_CONV, :] * xr
    for k in range(LRU_CONV - 1):
        xc = xc + cw_ref[k:k + 1, :] * xext_ref[pl.ds(SUBLANES - (LRU_CONV - 1) + k, ts), :]
    xext_ref[0:SUBLANES, :] = xr[ts - SUBLANES:ts, :]

    gi = _mm(xc, wg_ref[...]) + bg_ref[...]
    r = jax.nn.sigmoid(gi[:, 0:GROUP])
    ig = jax.nn.sigmoid(gi[:, GROUP:2 * GROUP])
    lam = lam_ref[...]
    softplus_neg = jnp.maximum(-lam, 0.0) + jnp.log1p(jnp.exp(-jnp.abs(lam)))
    log_a = (-LRU_C * softplus_neg) * r
    a = jnp.exp(log_a)
    g = -jnp.tanh(log_a) * (a * a + 1.0)
    b = jnp.where(g > 0.0, g * lax.rsqrt(g), 0.0) * (ig * xc)
    h = _linear_scan_rows(a, b, h_ref[0:1, :])
    h_ref[...] = jnp.broadcast_to(h[ts - 1:ts, :], h_ref.shape)
    y_a = h * _gelu(gate)

    u = cgt * xs
    uext_ref[SUBLANES:SUBLANES + ts, :] = u
    cv = scw_ref[SC_CONV - 1:SC_CONV, :] * u
    for k in range(SC_CONV - 1):
        cv = cv + scw_ref[k:k + 1, :] * uext_ref[pl.ds(SUBLANES - (SC_CONV - 1) + k, ts), :]
    uext_ref[0:SUBLANES, :] = u[ts - SUBLANES:ts, :]
    y_b = bgt * cv

    out_ref[0, :, 0:GROUP] = (y_a * _rms_scale(y_a) * on_ref[:, 0:GROUP]).astype(out_ref.dtype)
    out_ref[0, :, GROUP:2 * GROUP] = (y_b * _rms_scale(y_b) * on_ref[:, GROUP:2 * GROUP]).astype(out_ref.dtype)


def _fox_prep(zfox, zsm, fb_ref, gq_ref, gk_ref, q_ref, k_ref, v_ref, c_ref):
    ts = zfox.shape[0]
    shape = (ts, LANES)
    lane = _lane(shape)
    low = lane < HEAD_DIM

    f = zsm + fb_ref[...]
    logf = jnp.minimum(f, 0.0) - jnp.log1p(jnp.exp(-jnp.abs(f)))
    c = _cumsum_rows(logf, c_ref[0:1, :])
    c_ref[...] = jnp.broadcast_to(c[ts - 1:ts, :], c_ref.shape)
    neg_c = -LOG2E * c
    p1 = _round_mxu(neg_c)
    r1 = neg_c - p1
    p2 = _round_mxu(r1)
    p3 = _round_mxu(r1 - p2)
    piece = lane - 3 * ((lane * 11) >> 5)
    csel = jnp.where(piece == 0, p1, jnp.where(piece == 1, p2, p3))

    ones_aug = jnp.where(lane < HEAD_DIM + 3, 1.0, 0.0)
    for p in range(N_HEADS // 2):
        qs = zfox[:, LANES * p:LANES * (p + 1)]
        ks = zfox[:, GROUP + LANES * p:GROUP + LANES * (p + 1)]
        qn = qs * _half_rms_scale(qs, low) * gq_ref[...]
        kn = ks * _half_rms_scale(ks, low) * gk_ref[...]
        for half in range(2):
            h = 2 * p + half
            qh = qn if half == 0 else pltpu.roll(qn, HEAD_DIM, axis=1)
            kh = kn if half == 0 else pltpu.roll(kn, HEAD_DIM, axis=1)
            ch = pltpu.roll(csel, HEAD_DIM - 3 * h, axis=1)
            q_ref[0, h] = jnp.where(low, qh, ones_aug).astype(q_ref.dtype)
            k_ref[0, h] = jnp.where(low, kh, jnp.where(lane < HEAD_DIM + 3, ch, 0.0)).astype(k_ref.dtype)
    for h in range(N_HEADS):
        vh = zfox[:, 2 * GROUP + LANES * h:2 * GROUP + LANES * (h + 1)]
        v_ref[0, h] = jnp.where(lane == FOX_SUM_LANE, 1.0, vh).astype(v_ref.dtype)


def _nsa_prep(zq, zsel, zwin, zsm, row0, gq_ref, gs_ref, gw_ref, gb_ref,
              q4_ref, ks_ref, vs_ref, kw_ref, vw_ref, gate_ref):
    ts = zq.shape[0]
    shape = (ts, LANES)
    lane = _lane(shape)
    low = lane < HEAD_DIM
    for p in range(N_HEADS // 2):
        qs = zq[:, LANES * p:LANES * (p + 1)]
        qn = qs * _half_rms_scale(qs, low) * gq_ref[...]
        q4_ref[0, 2 * p] = jnp.where(low, qn, 0.0).astype(q4_ref.dtype)
        q4_ref[0, 2 * p + 1] = jnp.where(low, pltpu.roll(qn, HEAD_DIM, axis=1), 0.0).astype(q4_ref.dtype)

    blk = (row0 + _row(shape)) >> SLC_SHIFT
    onehot = jnp.where(lane - HEAD_DIM == blk, 1.0, 0.0)
    ks_ref[0] = jnp.where(low, zsel * _low_half_rms_scale(zsel, low) * gs_ref[...], onehot).astype(ks_ref.dtype)
    vs_ref[0] = jnp.where(lane == SUM_LANE, 1.0, zsel).astype(vs_ref.dtype)
    kw_ref[0] = jnp.where(low, zwin * _low_half_rms_scale(zwin, low) * gw_ref[...], 0.0).astype(kw_ref.dtype)
    vw_ref[0] = jnp.where(lane == SUM_LANE, 1.0, zwin).astype(vw_ref.dtype)
    gate_ref[0] = jax.nn.sigmoid(zsm + gb_ref[...])


def _proj_mix_kernel(x_ref, g_ref, w_ref,
                     cw_ref, cb_ref, wg_ref, bg_ref, lam_ref, scw_ref, on_ref,
                     fb_ref, fgq_ref, fgk_ref, ngq_ref, ngs_ref, ngw_ref, gb_ref,
                     mab_ref, qf_ref, kf_ref, vf_ref, q4_ref, ks_ref, vs_ref, kw_ref, vw_ref, gate_ref, zcmp_ref,
                     xext_ref, uext_ref, h_ref, c_ref, zc_ref):
    ts = x_ref.shape[1]
    ti = pl.program_id(1)

    @pl.when(ti == 0)
    def _():
        xext_ref[0:SUBLANES, :] = jnp.zeros((SUBLANES, GROUP), _F32)
        uext_ref[0:SUBLANES, :] = jnp.zeros((SUBLANES, GROUP), _F32)
        h_ref[...] = jnp.zeros_like(h_ref)
        c_ref[...] = jnp.zeros_like(c_ref)

    x = x_ref[0]
    xn = (x * _rms_scale(x) * g_ref[...]).astype(_MXU)
    zab = jnp.dot(xn, w_ref[:, 0:COL_FOX], preferred_element_type=_F32)
    slabs = jnp.dot(xn, w_ref[:, COL_SLABS:N_IN_PAD], preferred_element_type=_F32)
    zfox = jnp.dot(xn, w_ref[:, COL_FOX:COL_NSA_Q], preferred_element_type=_F32)
    zq = jnp.dot(xn, w_ref[:, COL_NSA_Q:COL_SLABS], preferred_element_type=_F32)
    zc_ref[...] = slabs[:, 0:LANES]
    for tok in range(CMP_STRIDE):
        zcmp_ref[0, :, tok * LANES:(tok + 1) * LANES] = zc_ref[pl.ds(tok, ts // CMP_STRIDE, stride=CMP_STRIDE), :]
    zsel = slabs[:, LANES:2 * LANES]
    zwin = slabs[:, 2 * LANES:3 * LANES]
    zsm = slabs[:, 3 * LANES:4 * LANES]
    _fox_prep(zfox, zsm, fb_ref, fgq_ref, fgk_ref, qf_ref, kf_ref, vf_ref, c_ref)
    _nsa_prep(zq, zsel, zwin, zsm, ti * ts, ngq_ref, ngs_ref, ngw_ref, gb_ref,
              q4_ref, ks_ref, vs_ref, kw_ref, vw_ref, gate_ref)
    _mix_ab(zab, cw_ref, cb_ref, wg_ref, bg_ref, lam_ref, scw_ref, on_ref, mab_ref, xext_ref, uext_ref, h_ref)


def _proj_mix(x, gain, w, cw, cb, wg, bg, lam, scw, on_ab, fb, fgq, fgk, ngq, ngs, ngw, gb):
    bsz, s_len, _ = x.shape
    ts = TS_MIX
    vec = lambda width: pl.BlockSpec((1, width), lambda b, t: (0, 0))
    rows = lambda width: pl.BlockSpec((1, ts, width), lambda b, t: (b, t, 0))
    heads = pl.BlockSpec((1, N_HEADS, ts, LANES), lambda b, t: (b, 0, t, 0))
    sds = jax.ShapeDtypeStruct
    return pl.pallas_call(
        _proj_mix_kernel,
        grid=(bsz, s_len // ts),
        in_specs=[rows(D_MODEL), vec(D_MODEL), _resident((D_MODEL, N_IN_PAD)),
                  pl.BlockSpec((LRU_CONV, GROUP), lambda b, t: (0, 0)), vec(GROUP), _resident((GROUP, 2 * GROUP)),
                  vec(2 * GROUP), vec(GROUP), pl.BlockSpec((SC_CONV, GROUP), lambda b, t: (0, 0)), vec(2 * GROUP),
                  vec(LANES), vec(LANES), vec(LANES), vec(LANES), vec(LANES), vec(LANES), vec(LANES)],
        out_specs=[rows(2 * GROUP), heads, heads, heads, heads, rows(LANES), rows(LANES),
                   rows(LANES), rows(LANES), rows(LANES),
                   pl.BlockSpec((1, ts // CMP_STRIDE, CMP_STRIDE * LANES), lambda b, t: (b, t, 0))],
        out_shape=[sds((bsz, s_len, 2 * GROUP), _MXU),
                   sds((bsz, N_HEADS, s_len, LANES), _MXU), sds((bsz, N_HEADS, s_len, LANES), _MXU),
                   sds((bsz, N_HEADS, s_len, LANES), _MXU),
                   sds((bsz, N_HEADS, s_len, LANES), _MXU),
                   sds((bsz, s_len, LANES), _MXU), sds((bsz, s_len, LANES), _MXU),
                   sds((bsz, s_len, LANES), _MXU), sds((bsz, s_len, LANES), _MXU),
                   sds((bsz, s_len, LANES), _F32), sds((bsz, s_len // CMP_STRIDE, CMP_STRIDE * LANES), _F32)],
        scratch_shapes=[pltpu.VMEM((ts + SUBLANES, GROUP), _F32), pltpu.VMEM((ts + SUBLANES, GROUP), _F32),
                        pltpu.VMEM((SUBLANES, GROUP), _F32), pltpu.VMEM((SUBLANES, LANES), _F32),
                        pltpu.VMEM((ts, LANES), _F32)],
        compiler_params=_params(("parallel", "arbitrary")),
        name="proj_mix",
    )(x, gain, w, cw, cb, wg, bg, lam, scw, on_ab, fb, fgq, fgk, ngq, ngs, ngw, gb)


def _flash_step_sumlane(s, v, carry):
    m, acc = carry
    m_new = jnp.maximum(m, jnp.max(s, axis=-1, keepdims=True))
    p = jnp.exp2(s - m_new).astype(_MXU)
    pv = jnp.dot(p, v, preferred_element_type=_F32)
    return m_new, jnp.exp2(m - m_new) * acc + pv


def _flash_init_sumlane(rows):
    return (jnp.full((rows, 1), NEG, _F32), jnp.zeros((rows, LANES), _F32))


def _sumlane_normalize(acc, sum_lane):
    return acc / acc[:, sum_lane:sum_lane + 1]


def _fox_attn_kernel(q_ref, k_ref, v_ref, on_ref, out_ref):
    tq = q_ref.shape[2]
    qi = pl.program_id(1)
    qs = [q_ref[0, h] for h in range(N_HEADS)]

    def step(c, carries, causal=False):
        start = pl.multiple_of(c * tq, tq)
        new = []
        for h in range(N_HEADS):
            s = _mm_nt(qs[h], k_ref[0, h, pl.ds(start, tq), :])
            if causal:
                s = jnp.where(_lane((tq, tq)) <= _row((tq, tq)), s, NEG)
            new.append(_flash_step_sumlane(s, v_ref[0, h, pl.ds(start, tq), :], carries[h]))
        return tuple(new)

    carries = lax.fori_loop(0, qi, step, tuple(_flash_init_sumlane(tq) for _ in range(N_HEADS)))
    carries = step(qi, carries, causal=True)
    outs = [_sumlane_normalize(acc, FOX_SUM_LANE) for _, acc in carries]
    low = _lane((tq, LANES)) < HEAD_DIM
    y = jnp.concatenate([jnp.where(low, outs[2 * p], pltpu.roll(outs[2 * p + 1], HEAD_DIM, axis=1))
                         for p in range(N_HEADS // 2)], axis=1)
    out_ref[0] = (y * _rms_scale(y) * on_ref[...]).astype(out_ref.dtype)


def _fox_attn(qf, kf, vf, on):
    bsz, _, s_len, _ = qf.shape
    tq = min(TQ_FOX, s_len)
    assert s_len % tq == 0
    return pl.pallas_call(
        _fox_attn_kernel,
        grid=(bsz, s_len // tq),
        in_specs=[pl.BlockSpec((1, N_HEADS, tq, LANES), lambda b, i: (b, 0, i, 0)),
                  pl.BlockSpec((1, N_HEADS, s_len, LANES), lambda b, i: (b, 0, 0, 0)),
                  pl.BlockSpec((1, N_HEADS, s_len, LANES), lambda b, i: (b, 0, 0, 0)),
                  pl.BlockSpec((1, GROUP), lambda b, i: (0, 0))],
        out_specs=pl.BlockSpec((1, tq, GROUP), lambda b, i: (b, i, 0)),
        out_shape=jax.ShapeDtypeStruct((bsz, s_len, GROUP), _MXU),
        compiler_params=_params(("parallel", "arbitrary")),
        name="fox_attn",
    )(qf, kf, vf, on)


def _compress_kernel(x_ref, pa_ref, pb_ref, w1a_ref, w1b_ref, w2_ref, g_ref, kc_ref, cv_ref):
    nc = x_ref.shape[1]
    x = x_ref[0]
    ua = _mm(x + pa_ref[...], w1a_ref[...])
    ub = _mm(x + pb_ref[...], w1b_ref[...])
    hid = _gelu(ua + pltpu.roll(ub, nc - 1, axis=0))
    out = _mm(hid, w2_ref[...])
    low = _lane(out.shape) < HEAD_DIM
    kc_ref[0] = jnp.where(low, out * _low_half_rms_scale(out, low) * g_ref[...], 0.0).astype(kc_ref.dtype)
    cv_ref[0] = out.astype(cv_ref.dtype)


def _compress(xc, pa, pb, w1a, w1b, w2, g):
    bsz, nc, width = xc.shape
    const = lambda shape: pl.BlockSpec(shape, lambda b: (0, 0))
    slab = pl.BlockSpec((1, nc, LANES), lambda b: (b, 0, 0))
    return pl.pallas_call(
        _compress_kernel,
        grid=(bsz,),
        in_specs=[pl.BlockSpec((1, nc, width), lambda b: (b, 0, 0)),
                  const((1, width)), const((1, width)), const((width, 2 * CMP_HIDDEN)),
                  const((width, 2 * CMP_HIDDEN)), const((2 * CMP_HIDDEN, LANES)), const((1, LANES))],
        out_specs=[slab, slab],
        out_shape=[jax.ShapeDtypeStruct((bsz, nc, LANES), _MXU)] * 2,
        compiler_params=_params(("parallel",)),
        name="compress",
    )(xc, pa, pb, w1a, w1b, w2, g)


def _bucket_thresholds():
    max_exact = REL_BUCKETS // 2
    d = np.arange(0, REL_MAX_DIST + 1)
    large = max_exact + (np.log(np.maximum(d, 1).astype(np.float32) / max_exact)
                         / math.log(REL_MAX_DIST / max_exact) * (REL_BUCKETS - max_exact)).astype(np.int32)
    bucket = np.where(d < max_exact, d, np.minimum(large, REL_BUCKETS - 1))
    assert bucket[-1] == REL_BUCKETS - 1 and np.all(np.diff(bucket) >= 0)
    return [int(np.argmax(bucket >= k)) for k in range(REL_BUCKETS)]


_BUCKET_THR = _bucket_thresholds()

NSA_R = TQ_NSA // KT_NSA
NSA_D = WINDOW // KT_NSA
NSA_TAIL = NSA_D + NSA_R
WIN_ENTRIES = NSA_TAIL + 1
SEL_ENTRIES = NSA_R + 3


def _win_entry(e):
    return jnp.minimum(e, NSA_TAIL)


def _sel_entry(e):
    return jnp.clip(e - (NSA_D - 2), 0, NSA_R + 2)


def _rel_bias(dist, rb_ref, h):
    far = rb_ref[REL_BUCKETS - 1, h]
    val = jnp.full(dist.shape, LOG2E * (rb_ref[0, h] - far), _F32)
    for k in range(1, REL_BUCKETS):
        val = jnp.where(dist >= _BUCKET_THR[k], LOG2E * (rb_ref[k, h] - far), val)
    return val


def _nsa_tables_kernel(rb_ref, win_ref, sel_ref, cmp_ref):
    qi = pl.program_id(0)
    tq = TQ_NSA
    nc = cmp_ref.shape[2]

    @pl.when(qi == 0)
    def _():
        i = _row((tq, KT_NSA))
        c = _lane((tq, KT_NSA))
        for h in range(N_HEADS):
            rs = slice(h * tq, (h + 1) * tq)
            for e in range(NSA_TAIL):
                dist = i + (NSA_D - e) * KT_NSA - c
                ok = jnp.where(dist >= 0, jnp.where(dist < WINDOW, 1, 0), 0) > 0
                tab = jnp.where(ok, _rel_bias(dist, rb_ref, h), NEG)
                win_ref[e, rs, :] = tab
                if e >= NSA_D - 1:
                    sel_ref[e - (NSA_D - 2), rs, :] = tab
            win_ref[NSA_TAIL, rs, :] = jnp.full((tq, KT_NSA), NEG, _F32)
            sel_ref[0, rs, :] = jnp.zeros((tq, KT_NSA), _F32)
            sel_ref[NSA_R + 2, rs, :] = jnp.full((tq, KT_NSA), NEG, _F32)

    t = qi * tq + _row((tq, nc))
    dist = t - (CMP_STRIDE * _lane((tq, nc)) + CMP_LEN - 1)
    for h in range(N_HEADS):
        cmp_ref[0, h * tq:(h + 1) * tq, :] = jnp.where(dist >= 0, _rel_bias(dist, rb_ref, h), NEG)


def _nsa_tables(rel_bias, s_len):
    nq = s_len // TQ_NSA
    nc = s_len // CMP_STRIDE
    rows = N_HEADS * TQ_NSA
    return pl.pallas_call(
        _nsa_tables_kernel,
        grid=(nq,),
        in_specs=[pl.BlockSpec(memory_space=pltpu.SMEM)],
        out_specs=[pl.BlockSpec((WIN_ENTRIES, rows, KT_NSA), lambda i: (0, 0, 0)),
                   pl.BlockSpec((SEL_ENTRIES, rows, KT_NSA), lambda i: (0, 0, 0)),
                   pl.BlockSpec((1, rows, nc), lambda i: (i, 0, 0))],
        out_shape=[jax.ShapeDtypeStruct((WIN_ENTRIES, rows, KT_NSA), _F32),
                   jax.ShapeDtypeStruct((SEL_ENTRIES, rows, KT_NSA), _F32),
                   jax.ShapeDtypeStruct((nq, rows, nc), _F32)],
        compiler_params=_params(("arbitrary",)),
        name="nsa_tables",
    )(rel_bias)


def _topk_penalty(imp_t, qi, tq):
    n_blk = imp_t.shape[0]
    blk = _row((n_blk, tq))
    cur = (qi * tq + _lane((n_blk, tq))) >> SLC_SHIFT
    forced = jnp.where(blk == 0, 1, jnp.where(blk == cur, 1, jnp.where(blk == cur - 1, 1, 0))) > 0
    val = jnp.where(blk <= cur, jnp.where(forced, BIG, imp_t), NEG)
    groups = [val[g * SUBLANES:(g + 1) * SUBLANES, :] for g in range(n_blk // SUBLANES)]
    sub = _row((SUBLANES, tq))
    ranks = [jnp.zeros((SUBLANES, tq), _F32) for _ in groups]
    for j in range(n_blk):
        vj = val[j:j + 1, :]
        jg, jr = divmod(j, SUBLANES)
        for g, vg in enumerate(groups):
            ge = jnp.where(vj >= vg, 1.0, 0.0)
            gt = jnp.where(vj > vg, 1.0, 0.0)
            if g > jg:
                ranks[g] = ranks[g] + ge
            elif g < jg:
                ranks[g] = ranks[g] + gt
            else:
                ranks[g] = ranks[g] + jnp.where(sub > jr, ge, gt)
    rank = jnp.concatenate(ranks, axis=0)
    return jnp.where(rank < float(SLC_TOPK), 0.0, NEG)


def _nsa_attn_kernel(q4_ref, kc_ref, cv_ref, ks_ref, vs_ref, kw_ref, vw_ref, gate_ref,
                     win_ref, sel_ref, cmp_ref, ovl_ref, on_ref, out_ref):
    tq = TQ_NSA
    kt = KT_NSA
    rows = N_HEADS * tq
    qi = pl.program_id(1)
    first = qi * NSA_R
    q4 = q4_ref[0].reshape(rows, LANES)
    lane = _lane((tq, LANES))
    low = lane < HEAD_DIM
    tail_w = NSA_TAIL * kt

    def tail_table(tab_ref, entry_of, first_tile, skip_before=0):
        pieces = []
        for u in range(NSA_TAIL):
            tile = first_tile + u
            entry = entry_of(tile - first + NSA_D)
            pieces.append(tab_ref[jnp.where(tile < skip_before, tab_ref.shape[0] - 1, entry)])
        return jnp.concatenate(pieces, axis=1)

    s = _mm_nt(q4, kc_ref[0]) + cmp_ref[0]
    valid = s > 0.5 * NEG
    m = jnp.max(s, axis=-1, keepdims=True)
    p = jnp.where(valid, jnp.exp2(s - m), 0.0)
    l = jnp.sum(p, axis=-1, keepdims=True)
    p = p / jnp.where(l > 0.0, l, 1.0)
    o_c = jnp.dot(p.astype(_MXU), cv_ref[0], preferred_element_type=_F32)

    psum = p[0:tq] + p[tq:2 * tq] + p[2 * tq:3 * tq] + p[3 * tq:4 * tq]
    p_hi = _round_mxu(psum)
    imp = _mm(p_hi, ovl_ref[...]) + _mm(psum - p_hi, ovl_ref[...])

    pen_t = _topk_penalty(imp.T[0:SLC_BLOCK, :], qi, tq)
    pen = jnp.concatenate([jnp.zeros((LANES - SLC_BLOCK, tq), _F32), pen_t], axis=0).T
    pen4 = jnp.concatenate([pen] * N_HEADS, axis=0)
    q_aug = jnp.where(_lane((rows, LANES)) < HEAD_DIM, q4.astype(_F32), pen4).astype(_MXU)

    w_tile = jnp.maximum(first - NSA_D, 0)
    w_start = pl.multiple_of(w_tile * kt, kt)
    s = _mm_nt(q4, kw_ref[0, pl.ds(w_start, tail_w), :]) + tail_table(win_ref, _win_entry, w_tile)
    _, acc_w = _flash_step_sumlane(s, vw_ref[0, pl.ds(w_start, tail_w), :], _flash_init_sumlane(rows))
    o_w = _sumlane_normalize(acc_w, SUM_LANE)

    g = gate_ref[0]
    gate = lambda branch, h: g[:, SM_GATE_LANE + branch * N_HEADS + h:SM_GATE_LANE + branch * N_HEADS + h + 1]
    head_rows = [slice(h * tq, (h + 1) * tq) for h in range(N_HEADS)]
    partial = [gate(0, h) * o_c[rs] + gate(2, h) * o_w[rs] for h, rs in enumerate(head_rows)]

    def sel_far(width):
        def step(c, carry):
            start = pl.multiple_of(c * width, width)
            s = _mm_nt(q_aug, ks_ref[0, pl.ds(start, width), :])
            return _flash_step_sumlane(s, vs_ref[0, pl.ds(start, width), :], carry)
        return step

    far_w = NSA_FAR * kt
    n_far = jnp.maximum(first - 1, 0) // NSA_FAR
    n_dbl = n_far // 2
    carry = lax.fori_loop(0, n_dbl, sel_far(2 * far_w), _flash_init_sumlane(rows))
    carry = lax.fori_loop(2 * n_dbl, n_far, sel_far(far_w), carry)
    done = n_far * NSA_FAR
    t_tile = jnp.minimum(done, ks_ref.shape[1] // kt - NSA_TAIL)
    t_start = pl.multiple_of(t_tile * kt, kt)
    s = _mm_nt(q_aug, ks_ref[0, pl.ds(t_start, tail_w), :]) + tail_table(sel_ref, _sel_entry, t_tile, done)
    _, acc_s = _flash_step_sumlane(s, vs_ref[0, pl.ds(t_start, tail_w), :], carry)
    o_s = _sumlane_normalize(acc_s, SUM_LANE)

    heads = [partial[h] + gate(1, h) * o_s[rs] for h, rs in enumerate(head_rows)]
    slabs = [jnp.where(low, pltpu.roll(heads[2 * p], HEAD_DIM, axis=1), heads[2 * p + 1])
             for p in range(N_HEADS // 2)]
    y = jnp.concatenate(slabs, axis=1)
    out_ref[0] = (y * _rms_scale(y) * on_ref[...]).astype(out_ref.dtype)


def _nsa_attn(q4, kc, cv, ks, vs, kw, vw, gates, win_tab, sel_tab, cmp_tab, ovl, on):
    bsz, _, s_len, _ = q4.shape
    tq = TQ_NSA
    nc = kc.shape[1]
    rows = N_HEADS * tq
    assert s_len >= NSA_TAIL * KT_NSA
    full = lambda n: pl.BlockSpec((1, n, LANES), lambda b, i: (b, 0, 0))
    return pl.pallas_call(
        _nsa_attn_kernel,
        grid=(bsz, s_len // tq),
        in_specs=[pl.BlockSpec((1, N_HEADS, tq, LANES), lambda b, i: (b, 0, i, 0)),
                  full(nc), full(nc), full(s_len), full(s_len), full(s_len), full(s_len),
                  pl.BlockSpec((1, tq, LANES), lambda b, i: (b, i, 0)),
                  _resident((WIN_ENTRIES, rows, KT_NSA)), _resident((SEL_ENTRIES, rows, KT_NSA)),
                  pl.BlockSpec((1, rows, nc), lambda b, i: (i, 0, 0)),
                  _resident((nc, LANES)),
                  pl.BlockSpec((1, GROUP), lambda b, i: (0, 0))],
        out_specs=pl.BlockSpec((1, tq, GROUP), lambda b, i: (b, i, 0)),
        out_shape=jax.ShapeDtypeStruct((bsz, s_len, GROUP), _MXU),
        compiler_params=_params(("parallel", "arbitrary")),
        name="nsa_attn",
    )(q4, kc, cv, ks, vs, kw, vw, gates, win_tab, sel_tab, cmp_tab, ovl, on)


def _ffn_kernel(x_ref, mab_ref, mc_ref, md_ref, wo_ref, g_ref, wgu_ref, wd_ref, out_ref):
    x1 = (x_ref[...]
          + jnp.dot(mab_ref[...], wo_ref[0:2 * GROUP, :], preferred_element_type=_F32)
          + jnp.dot(mc_ref[...], wo_ref[2 * GROUP:3 * GROUP, :], preferred_element_type=_F32)
          + jnp.dot(md_ref[...], wo_ref[3 * GROUP:4 * GROUP, :], preferred_element_type=_F32))
    xn = (x1 * _rms_scale(x1) * g_ref[...]).astype(_MXU)
    out_ref[...] = x1
    for j in range(D_FF // TF_FFN):
        cols = slice(j * TF_FFN, (j + 1) * TF_FFN)
        gt = jnp.dot(xn, wgu_ref[:, cols], preferred_element_type=_F32)
        up = jnp.dot(xn, wgu_ref[:, D_FF + j * TF_FFN:D_FF + (j + 1) * TF_FFN], preferred_element_type=_F32)
        hid = (gt * jax.nn.sigmoid(gt)) * up
        out_ref[...] += jnp.dot(hid.astype(_MXU), wd_ref[cols, :], preferred_element_type=_F32)


def _out_ffn(x2d, mab, mc, md, wo, g, wgu, wd):
    t = x2d.shape[0]
    tm = min(TM_FFN, t)
    assert t % tm == 0 and D_FF % TF_FFN == 0
    row = lambda width: pl.BlockSpec((tm, width), lambda i: (i, 0))
    return pl.pallas_call(
        _ffn_kernel,
        grid=(t // tm,),
        in_specs=[row(D_MODEL), row(2 * GROUP), row(GROUP), row(GROUP),
                  _resident((D_MODEL, D_MODEL)), pl.BlockSpec((1, D_MODEL), lambda i: (0, 0)),
                  _resident((D_MODEL, 2 * D_FF)), _resident((D_FF, D_MODEL))],
        out_specs=row(D_MODEL),
        out_shape=jax.ShapeDtypeStruct((t, D_MODEL), _F32),
        compiler_params=_params(("parallel",)),
        name="out_ffn",
    )(x2d, mab, mc, md, wo, g, wgu, wd)


def _lane_vec(values, start):
    v = jnp.zeros((LANES,), _F32).at[start:start + values.shape[0]].set(values.astype(_F32))
    return v[None, :]


def _block_diag(w):
    h, d, _ = w.shape
    eye = jnp.eye(h, dtype=w.dtype)
    return (eye[:, None, :, None] * w[:, :, None, :]).reshape(h * d, h * d)


def _overlap_ext(s_len):
    nc = s_len // CMP_STRIDE
    n_cmp = nc - 1
    n_slc = s_len // SLC_BLOCK
    cs = np.arange(n_cmp)[:, None] * CMP_STRIDE
    ss = np.arange(n_slc)[None, :] * SLC_BLOCK
    ov = np.clip(np.minimum(cs + CMP_LEN, ss + SLC_BLOCK) - np.maximum(cs, ss), 0, CMP_LEN)
    ext = np.zeros((nc, LANES), np.float32)
    ext[:n_cmp, :n_slc] = ov
    return jnp.asarray(ext, _MXU)


def _layer_params(l, w_in, lru_w_gates, lru_b_gates, fox_f_bias, fox_qk_norm, nsa_qk_norm, nsa_cmp_pos,
                  nsa_cmp_w1, nsa_cmp_w2, nsa_gate_bias, out_norm):
    scale = HEAD_DIM ** -0.5 * LOG2E
    w = w_in[l]
    small = jnp.zeros((D_MODEL, LANES), w.dtype)
    small = small.at[:, 0:3 * N_HEADS].set(jnp.repeat(w[:, C_FOX_F:C_FOX_F + N_HEADS], 3, axis=1))
    small = small.at[:, SM_GATE_LANE:SM_GATE_LANE + 3 * N_HEADS].set(w[:, C_NSA_G:C_NSA_G + 3 * N_HEADS])
    c_fox_v = C_FOX_F - GROUP
    fox_v = jnp.pad(w[:, c_fox_v:C_FOX_F].reshape(D_MODEL, N_HEADS, HEAD_DIM),
                    ((0, 0), (0, 0), (0, LANES - HEAD_DIM))).reshape(D_MODEL, N_HEADS * LANES)
    w_perm = jnp.concatenate([w[:, :c_fox_v], fox_v, w[:, C_NSA_Q:C_NSA_G], small], axis=1).astype(_MXU)
    assert w_perm.shape[1] == N_IN_PAD

    wg = jnp.concatenate([_block_diag(lru_w_gates[l, 0]), _block_diag(lru_w_gates[l, 1])], axis=1).astype(_MXU)
    bg = lru_b_gates[l].reshape(1, 2 * GROUP)

    fb = _lane_vec(jnp.repeat(fox_f_bias[l], 3), 0)
    fox_gq = (jnp.tile(fox_qk_norm[l, 0], 2) * scale)[None, :]
    fox_gk = jnp.tile(fox_qk_norm[l, 1], 2)[None, :]

    nsa_gq = (jnp.tile(nsa_qk_norm[l, 0], 2) * scale)[None, :]
    nsa_gc = _lane_vec(nsa_qk_norm[l, 1], 0)
    nsa_gs = _lane_vec(nsa_qk_norm[l, 2], 0)
    nsa_gw = _lane_vec(nsa_qk_norm[l, 3], 0)
    gb = _lane_vec(nsa_gate_bias[l], SM_GATE_LANE)

    half = CMP_LEN // 2

    def pos_ext(lo):
        return jnp.concatenate([nsa_cmp_pos[l, 0, lo:lo + half], nsa_cmp_pos[l, 1, lo:lo + half]],
                               axis=1).reshape(1, half * LANES)

    def w1_ext(lo):
        ext = jnp.zeros((half, LANES, 2 * CMP_HIDDEN), _F32)
        ext = ext.at[:, :HEAD_DIM, :CMP_HIDDEN].set(nsa_cmp_w1[l, 0, lo:lo + half])
        ext = ext.at[:, HEAD_DIM:, CMP_HIDDEN:].set(nsa_cmp_w1[l, 1, lo:lo + half])
        return ext.reshape(half * LANES, 2 * CMP_HIDDEN).astype(_MXU)

    w2 = jnp.zeros((2 * CMP_HIDDEN, LANES), _F32)
    w2 = w2.at[:CMP_HIDDEN, :HEAD_DIM].set(nsa_cmp_w2[l, 0]).at[CMP_HIDDEN:, HEAD_DIM:].set(nsa_cmp_w2[l, 1])
    return dict(w_perm=w_perm, wg=wg, bg=bg, fb=fb, fox_gq=fox_gq, fox_gk=fox_gk, nsa_gq=nsa_gq, nsa_gc=nsa_gc,
                nsa_gs=nsa_gs, nsa_gw=nsa_gw, gb=gb, pa=pos_ext(0), pb=pos_ext(half), w1a=w1_ext(0),
                w1b=w1_ext(half), w2=w2.astype(_MXU), on=out_norm[l].reshape(1, 4 * GROUP))


def kernel(x, norm_mix, w_in, lru_conv_w, lru_conv_b, lru_w_gates, lru_b_gates, lru_lambda, sc_conv_w, fox_f_bias, fox_qk_norm, nsa_qk_norm, nsa_cmp_pos, nsa_cmp_w1, nsa_cmp_w2, nsa_gate_bias, rel_bias, out_norm, w_out, norm_ffn, w_gate_up, w_down):
    bsz, s_len, d_model = x.shape
    depth = w_in.shape[0]
    assert d_model == D_MODEL and s_len % TS_MIX == 0 and s_len % TQ_FOX == 0
    assert SLC_TOPK <= s_len // SLC_BLOCK <= SLC_BLOCK
    t = bsz * s_len

    win_tab, sel_tab, cmp_tab = _nsa_tables(rel_bias, s_len)
    ovl = _overlap_ext(s_len)

    for l in range(depth):
        lp = _layer_params(l, w_in, lru_w_gates, lru_b_gates, fox_f_bias, fox_qk_norm, nsa_qk_norm,
                           nsa_cmp_pos, nsa_cmp_w1, nsa_cmp_w2, nsa_gate_bias, out_norm)
        m_ab, qf, kf, vf, q4, ks, vs, kw, vw, gates, zcmp = _proj_mix(
            x, norm_mix[l][None, :], lp["w_perm"], lru_conv_w[l], lru_conv_b[l][None, :], lp["wg"], lp["bg"],
            lru_lambda[l][None, :], sc_conv_w[l], lp["on"][:, 0:2 * GROUP],
            lp["fb"], lp["fox_gq"], lp["fox_gk"], lp["nsa_gq"], lp["nsa_gs"], lp["nsa_gw"], lp["gb"])
        m_c = _fox_attn(qf, kf, vf, lp["on"][:, 2 * GROUP:3 * GROUP])
        kc, cv = _compress(zcmp, lp["pa"], lp["pb"], lp["w1a"], lp["w1b"], lp["w2"], lp["nsa_gc"])
        m_d = _nsa_attn(q4, kc, cv, ks, vs, kw, vw, gates, win_tab, sel_tab, cmp_tab, ovl,
                        lp["on"][:, 3 * GROUP:4 * GROUP])
        x = _out_ffn(x.reshape(t, D_MODEL), m_ab.reshape(t, 2 * GROUP), m_c.reshape(t, GROUP),
                     m_d.reshape(t, GROUP), w_out[l].astype(_MXU), norm_ffn[l][None, :],
                     w_gate_up[l].astype(_MXU), w_down[l].astype(_MXU)).reshape(bsz, s_len, D_MODEL)
    return x
```

```python
import math

import numpy as np
import jax
import jax.numpy as jnp
from jax import lax
from jax.experimental import pallas as pl
from jax.experimental.pallas import tpu as pltpu

D_MODEL = 1024
GROUP = 256
HEAD_DIM = 64
N_HEADS = 4
LRU_CONV = 4
LRU_C = 8.0
SC_CONV = 3
CMP_LEN = 32
CMP_STRIDE = 16
CMP_HIDDEN = 128
SLC_BLOCK = 64
SLC_SHIFT = SLC_BLOCK.bit_length() - 1
SLC_TOPK = 16
WINDOW = 512
REL_BUCKETS = 32
REL_MAX_DIST = 128
D_FF = 2816
RMS_EPS = 1e-6
NEG = -1e30
BIG = 1e30
LOG2E = math.log2(math.e)

C_FOX_F = 8 * GROUP
C_NSA_Q = C_FOX_F + N_HEADS
C_NSA_G = C_NSA_Q + GROUP + 6 * HEAD_DIM

LANES = 128
SUBLANES = 8
VMEM_LIMIT_BYTES = 56 * 1024 * 1024

TS_MIX = 512
SCAN_SEG = 32
TQ_FOX = 1024
KT_NSA = 128
TQ_NSA = 256
NSA_FAR = 4
TM_FFN = 1024
TF_FFN = 256

SM_GATE_LANE = 16
SUM_LANE = 0
FOX_SUM_LANE = HEAD_DIM

COL_FOX = 5 * GROUP
COL_NSA_Q = COL_FOX + 2 * GROUP + N_HEADS * LANES
COL_SLABS = COL_NSA_Q + GROUP
N_IN_PAD = COL_SLABS + 4 * LANES

_MXU = jnp.bfloat16
_F32 = jnp.float32


def _params(sem):
    return pltpu.CompilerParams(dimension_semantics=sem, vmem_limit_bytes=VMEM_LIMIT_BYTES)


def _resident(shape):
    zeros = (0,) * len(shape)
    return pl.BlockSpec(shape, lambda *_: zeros, pipeline_mode=pl.Buffered(1))


def _mm(a, b):
    return jnp.dot(a.astype(_MXU), b.astype(_MXU), preferred_element_type=_F32)


def _mm_nt(a, b):
    return lax.dot_general(a.astype(_MXU), b.astype(_MXU), (((1,), (1,)), ((), ())),
                           preferred_element_type=_F32)


def _lane(shape):
    return lax.broadcasted_iota(jnp.int32, shape, len(shape) - 1)


def _row(shape):
    return lax.broadcasted_iota(jnp.int32, shape, 0)


def _rms_scale(x):
    return lax.rsqrt(jnp.mean(x * x, axis=-1, keepdims=True) + RMS_EPS)


def _gelu(x):
    c = math.sqrt(2.0 / math.pi)
    return x * (0.5 * (1.0 + jnp.tanh(c * (x + 0.044715 * (x * x * x)))))


def _round_mxu(x):
    return x.astype(_MXU).astype(_F32)


def _low_half_rms_scale(x, low):
    s = jnp.sum(jnp.where(low, x * x, 0.0), axis=-1, keepdims=True)
    return lax.rsqrt(s * (1.0 / HEAD_DIM) + RMS_EPS)


def _half_rms_scale(x, low):
    x2 = x * x
    s_lo = jnp.sum(jnp.where(low, x2, 0.0), axis=-1, keepdims=True)
    s_hi = jnp.sum(jnp.where(low, 0.0, x2), axis=-1, keepdims=True)
    return jnp.where(low, lax.rsqrt(s_lo * (1.0 / HEAD_DIM) + RMS_EPS),
                     lax.rsqrt(s_hi * (1.0 / HEAD_DIM) + RMS_EPS))


def _linear_scan_rows(a, b, h_in):
    n = a.shape[0]
    pos = _row(a.shape) & (SCAN_SEG - 1)
    d = 1
    while d < SCAN_SEG:
        keep = pos >= d
        a_prev = jnp.where(keep, pltpu.roll(a, d, axis=0), 1.0)
        b_prev = jnp.where(keep, pltpu.roll(b, d, axis=0), 0.0)
        b = a * b_prev + b
        a = a * a_prev
        d *= 2
    segs = []
    for r0 in range(0, n, SCAN_SEG):
        seg = b[r0:r0 + SCAN_SEG] + a[r0:r0 + SCAN_SEG] * h_in
        h_in = seg[SCAN_SEG - 1:SCAN_SEG]
        segs.append(seg)
    return jnp.concatenate(segs, axis=0)


def _cumsum_rows(x, c_in):
    n = x.shape[0]
    pos = _row(x.shape) & (SCAN_SEG - 1)
    d = 1
    while d < SCAN_SEG:
        x = x + jnp.where(pos >= d, pltpu.roll(x, d, axis=0), 0.0)
        d *= 2
    segs = []
    for r0 in range(0, n, SCAN_SEG):
        seg = x[r0:r0 + SCAN_SEG] + c_in
        c_in = seg[SCAN_SEG - 1:SCAN_SEG]
        segs.append(seg)
    return jnp.concatenate(segs, axis=0)


def _mix_ab(zab, cw_ref, cb_ref, wg_ref, bg_ref, lam_ref, scw_ref, on_ref, out_ref, xext_ref, uext_ref, h_ref):
    ts = zab.shape[0]
    xr = zab[:, 0:GROUP]
    gate = zab[:, GROUP:2 * GROUP]
    bgt = zab[:, 2 * GROUP:3 * GROUP]
    cgt = zab[:, 3 * GROUP:4 * GROUP]
    xs = zab[:, 4 * GROUP:5 * GROUP]

    xext_ref[SUBLANES:SUBLANES + ts, :] = xr
    xc = cb_ref[...] + cw_ref[LRU_CONV - 1:LRU_CONV, :] * xr
    for k in range(LRU_CONV - 1):
        xc = xc + cw_ref[k:k + 1, :] * xext_ref[pl.ds(SUBLANES - (LRU_CONV - 1) + k, ts), :]
    xext_ref[0:SUBLANES, :] = xr[ts - SUBLANES:ts, :]

    gi = _mm(xc, wg_ref[...]) + bg_ref[...]
    r = jax.nn.sigmoid(gi[:, 0:GROUP])
    ig = jax.nn.sigmoid(gi[:, GROUP:2 * GROUP])
    lam = lam_ref[...]
    softplus_neg = jnp.maximum(-lam, 0.0) + jnp.log1p(jnp.exp(-jnp.abs(lam)))
    log_a = (-LRU_C * softplus_neg) * r
    a = jnp.exp(log_a)
    g = -jnp.tanh(log_a) * (a * a + 1.0)
    b = jnp.where(g > 0.0, g * lax.rsqrt(g), 0.0) * (ig * xc)
    h = _linear_scan_rows(a, b, h_ref[0:1, :])
    h_ref[...] = jnp.broadcast_to(h[ts - 1:ts, :], h_ref.shape)
    y_a = h * _gelu(gate)

    u = cgt * xs
    uext_ref[SUBLANES:SUBLANES + ts, :] = u
    cv = scw_ref[SC_CONV - 1:SC_CONV, :] * u
    for k in range(SC_CONV - 1):
        cv = cv + scw_ref[k:k + 1, :] * uext_ref[pl.ds(SUBLANES - (SC_CONV - 1) + k, ts), :]
    uext_ref[0:SUBLANES, :] = u[ts - SUBLANES:ts, :]
    y_b = bgt * cv

    out_ref[0, :, 0:GROUP] = (y_a * _rms_scale(y_a) * on_ref[:, 0:GROUP]).astype(out_ref.dtype)
    out_ref[0, :, GROUP:2 * GROUP] = (y_b * _rms_scale(y_b) * on_ref[:, GROUP:2 * GROUP]).astype(out_ref.dtype)


def _fox_prep(zfox, zsm, fb_ref, gq_ref, gk_ref, q_ref, k_ref, v_ref, c_ref):
    ts = zfox.shape[0]
    shape = (ts, LANES)
    lane = _lane(shape)
    low = lane < HEAD_DIM

    f = zsm + fb_ref[...]
    logf = jnp.minimum(f, 0.0) - jnp.log1p(jnp.exp(-jnp.abs(f)))
    c = _cumsum_rows(logf, c_ref[0:1, :])
    c_ref[...] = jnp.broadcast_to(c[ts - 1:ts, :], c_ref.shape)
    neg_c = -LOG2E * c
    p1 = _round_mxu(neg_c)
    r1 = neg_c - p1
    p2 = _round_mxu(r1)
    p3 = _round_mxu(r1 - p2)
    piece = lane - 3 * ((lane * 11) >> 5)
    csel = jnp.where(piece == 0, p1, jnp.where(piece == 1, p2, p3))

    ones_aug = jnp.where(lane < HEAD_DIM + 3, 1.0, 0.0)
    for p in range(N_HEADS // 2):
        qs = zfox[:, LANES * p:LANES * (p + 1)]
        ks = zfox[:, GROUP + LANES * p:GROUP + LANES * (p + 1)]
        qn = qs * _half_rms_scale(qs, low) * gq_ref[...]
        kn = ks * _half_rms_scale(ks, low) * gk_ref[...]
        for half in range(2):
            h = 2 * p + half
            qh = qn if half == 0 else pltpu.roll(qn, HEAD_DIM, axis=1)
            kh = kn if half == 0 else pltpu.roll(kn, HEAD_DIM, axis=1)
            ch = pltpu.roll(csel, HEAD_DIM - 3 * h, axis=1)
            q_ref[0, h] = jnp.where(low, qh, ones_aug).astype(q_ref.dtype)
            k_ref[0, h] = jnp.where(low, kh, jnp.where(lane < HEAD_DIM + 3, ch, 0.0)).astype(k_ref.dtype)
    for h in range(N_HEADS):
        vh = zfox[:, 2 * GROUP + LANES * h:2 * GROUP + LANES * (h + 1)]
        v_ref[0, h] = jnp.where(lane == FOX_SUM_LANE, 1.0, vh).astype(v_ref.dtype)


def _nsa_prep(zq, zsel, zwin, zsm, row0, gq_ref, gs_ref, gw_ref, gb_ref,
              q4_ref, ks_ref, vs_ref, kw_ref, vw_ref, gate_ref):
    ts = zq.shape[0]
    shape = (ts, LANES)
    lane = _lane(shape)
    low = lane < HEAD_DIM
    for p in range(N_HEADS // 2):
        qs = zq[:, LANES * p:LANES * (p + 1)]
        qn = qs * _half_rms_scale(qs, low) * gq_ref[...]
        q4_ref[0, 2 * p] = jnp.where(low, qn, 0.0).astype(q4_ref.dtype)
        q4_ref[0, 2 * p + 1] = jnp.where(low, pltpu.roll(qn, HEAD_DIM, axis=1), 0.0).astype(q4_ref.dtype)

    blk = (row0 + _row(shape)) >> SLC_SHIFT
    onehot = jnp.where(lane - HEAD_DIM == blk, 1.0, 0.0)
    ks_ref[0] = jnp.where(low, zsel * _low_half_rms_scale(zsel, low) * gs_ref[...], onehot).astype(ks_ref.dtype)
    vs_ref[0] = jnp.where(lane == SUM_LANE, 1.0, zsel).astype(vs_ref.dtype)
    kw_ref[0] = jnp.where(low, zwin * _low_half_rms_scale(zwin, low) * gw_ref[...], 0.0).astype(kw_ref.dtype)
    vw_ref[0] = jnp.where(lane == SUM_LANE, 1.0, zwin).astype(vw_ref.dtype)
    gate_ref[0] = jax.nn.sigmoid(zsm + gb_ref[...])


def _proj_mix_kernel(x_ref, g_ref, w_ref,
                     cw_ref, cb_ref, wg_ref, bg_ref, lam_ref, scw_ref, on_ref,
                     fb_ref, fgq_ref, fgk_ref, ngq_ref, ngs_ref, ngw_ref, gb_ref,
                     mab_ref, qf_ref, kf_ref, vf_ref, q4_ref, ks_ref, vs_ref, kw_ref, vw_ref, gate_ref, zcmp_ref,
                     xext_ref, uext_ref, h_ref, c_ref, zc_ref):
    ts = x_ref.shape[1]
    ti = pl.program_id(1)

    @pl.when(ti == 0)
    def _():
        xext_ref[0:SUBLANES, :] = jnp.zeros((SUBLANES, GROUP), _F32)
        uext_ref[0:SUBLANES, :] = jnp.zeros((SUBLANES, GROUP), _F32)
        h_ref[...] = jnp.zeros_like(h_ref)
        c_ref[...] = jnp.zeros_like(c_ref)

    x = x_ref[0]
    xn = (x * _rms_scale(x) * g_ref[...]).astype(_MXU)
    zab = jnp.dot(xn, w_ref[:, 0:COL_FOX], preferred_element_type=_F32)
    slabs = jnp.dot(xn, w_ref[:, COL_SLABS:N_IN_PAD], preferred_element_type=_F32)
    zfox = jnp.dot(xn, w_ref[:, COL_FOX:COL_NSA_Q], preferred_element_type=_F32)
    zq = jnp.dot(xn, w_ref[:, COL_NSA_Q:COL_SLABS], preferred_element_type=_F32)
    zc_ref[...] = slabs[:, 0:LANES]
    for tok in range(CMP_STRIDE):
        zcmp_ref[0, :, tok * LANES:(tok + 1) * LANES] = zc_ref[pl.ds(tok, ts // CMP_STRIDE, stride=CMP_STRIDE), :]
    zsel = slabs[:, LANES:2 * LANES]
    zwin = slabs[:, 2 * LANES:3 * LANES]
    zsm = slabs[:, 3 * LANES:4 * LANES]
    _fox_prep(zfox, zsm, fb_ref, fgq_ref, fgk_ref, qf_ref, kf_ref, vf_ref, c_ref)
    _nsa_prep(zq, zsel, zwin, zsm, ti * ts, ngq_ref, ngs_ref, ngw_ref, gb_ref,
              q4_ref, ks_ref, vs_ref, kw_ref, vw_ref, gate_ref)
    _mix_ab(zab, cw_ref, cb_ref, wg_ref, bg_ref, lam_ref, scw_ref, on_ref, mab_ref, xext_ref, uext_ref, h_ref)


def _proj_mix(x, gain, w, cw, cb, wg, bg, lam, scw, on_ab, fb, fgq, fgk, ngq, ngs, ngw, gb):
    bsz, s_len, _ = x.shape
    ts = TS_MIX
    vec = lambda width: pl.BlockSpec((1, width), lambda b, t: (0, 0))
    rows = lambda width: pl.BlockSpec((1, ts, width), lambda b, t: (b, t, 0))
    heads = pl.BlockSpec((1, N_HEADS, ts, LANES), lambda b, t: (b, 0, t, 0))
    sds = jax.ShapeDtypeStruct
    return pl.pallas_call(
        _proj_mix_kernel,
        grid=(bsz, s_len // ts),
        in_specs=[rows(D_MODEL), vec(D_MODEL), _resident((D_MODEL, N_IN_PAD)),
                  pl.BlockSpec((LRU_CONV, GROUP), lambda b, t: (0, 0)), vec(GROUP), _resident((GROUP, 2 * GROUP)),
                  vec(2 * GROUP), vec(GROUP), pl.BlockSpec((SC_CONV, GROUP), lambda b, t: (0, 0)), vec(2 * GROUP),
                  vec(LANES), vec(LANES), vec(LANES), vec(LANES), vec(LANES), vec(LANES), vec(LANES)],
        out_specs=[rows(2 * GROUP), heads, heads, heads, heads, rows(LANES), rows(LANES),
                   rows(LANES), rows(LANES), rows(LANES),
                   pl.BlockSpec((1, ts // CMP_STRIDE, CMP_STRIDE * LANES), lambda b, t: (b, t, 0))],
        out_shape=[sds((bsz, s_len, 2 * GROUP), _MXU),
                   sds((bsz, N_HEADS, s_len, LANES), _MXU), sds((bsz, N_HEADS, s_len, LANES), _MXU),
                   sds((bsz, N_HEADS, s_len, LANES), _MXU),
                   sds((bsz, N_HEADS, s_len, LANES), _MXU),
                   sds((bsz, s_len, LANES), _MXU), sds((bsz, s_len, LANES), _MXU),
                   sds((bsz, s_len, LANES), _MXU), sds((bsz, s_len, LANES), _MXU),
                   sds((bsz, s_len, LANES), _F32), sds((bsz, s_len // CMP_STRIDE, CMP_STRIDE * LANES), _F32)],
        scratch_shapes=[pltpu.VMEM((ts + SUBLANES, GROUP), _F32), pltpu.VMEM((ts + SUBLANES, GROUP), _F32),
                        pltpu.VMEM((SUBLANES, GROUP), _F32), pltpu.VMEM((SUBLANES, LANES), _F32),
                        pltpu.VMEM((ts, LANES), _F32)],
        compiler_params=_params(("parallel", "arbitrary")),
        name="proj_mix",
    )(x, gain, w, cw, cb, wg, bg, lam, scw, on_ab, fb, fgq, fgk, ngq, ngs, ngw, gb)


def _flash_step_sumlane(s, v, carry):
    m, acc = carry
    m_new = jnp.maximum(m, jnp.max(s, axis=-1, keepdims=True))
    p = jnp.exp2(s - m_new).astype(_MXU)
    pv = jnp.dot(p, v, preferred_element_type=_F32)
    return m_new, jnp.exp2(m - m_new) * acc + pv


def _flash_init_sumlane(rows):
    return (jnp.full((rows, 1), NEG, _F32), jnp.zeros((rows, LANES), _F32))


def _sumlane_normalize(acc, sum_lane):
    return acc / acc[:, sum_lane:sum_lane + 1]


def _fox_attn_kernel(q_ref, k_ref, v_ref, on_ref, out_ref):
    tq = q_ref.shape[2]
    qi = pl.program_id(1)
    qs = [q_ref[0, h] for h in range(N_HEADS)]

    def step(c, carries, causal=False):
        start = pl.multiple_of(c * tq, tq)
        new = []
        for h in range(N_HEADS):
            s = _mm_nt(qs[h], k_ref[0, h, pl.ds(start, tq), :])
            if causal:
                s = jnp.where(_lane((tq, tq)) <= _row((tq, tq)), s, NEG)
            new.append(_flash_step_sumlane(s, v_ref[0, h, pl.ds(start, tq), :], carries[h]))
        return tuple(new)

    carries = lax.fori_loop(0, qi, step, tuple(_flash_init_sumlane(tq) for _ in range(N_HEADS)))
    carries = step(qi, carries, causal=True)
    outs = [_sumlane_normalize(acc, FOX_SUM_LANE) for _, acc in carries]
    low = _lane((tq, LANES)) < HEAD_DIM
    y = jnp.concatenate([jnp.where(low, outs[2 * p], pltpu.roll(outs[2 * p + 1], HEAD_DIM, axis=1))
                         for p in range(N_HEADS // 2)], axis=1)
    out_ref[0] = (y * _rms_scale(y) * on_ref[...]).astype(out_ref.dtype)


def _fox_attn(qf, kf, vf, on):
    bsz, _, s_len, _ = qf.shape
    tq = min(TQ_FOX, s_len)
    assert s_len % tq == 0
    return pl.pallas_call(
        _fox_attn_kernel,
        grid=(bsz, s_len // tq),
        in_specs=[pl.BlockSpec((1, N_HEADS, tq, LANES), lambda b, i: (b, 0, i, 0)),
                  pl.BlockSpec((1, N_HEADS, s_len, LANES), lambda b, i: (b, 0, 0, 0)),
                  pl.BlockSpec((1, N_HEADS, s_len, LANES), lambda b, i: (b, 0, 0, 0)),
                  pl.BlockSpec((1, GROUP), lambda b, i: (0, 0))],
        out_specs=pl.BlockSpec((1, tq, GROUP), lambda b, i: (b, i, 0)),
        out_shape=jax.ShapeDtypeStruct((bsz, s_len, GROUP), _MXU),
        compiler_params=_params(("parallel", "arbitrary")),
        name="fox_attn",
    )(qf, kf, vf, on)


def _compress_kernel(x_ref, pa_ref, pb_ref, w1a_ref, w1b_ref, w2_ref, g_ref, kc_ref, cv_ref):
    nc = x_ref.shape[1]
    x = x_ref[0]
    ua = _mm(x + pa_ref[...], w1a_ref[...])
    ub = _mm(x + pb_ref[...], w1b_ref[...])
    hid = _gelu(ua + pltpu.roll(ub, nc - 1, axis=0))
    out = _mm(hid, w2_ref[...])
    low = _lane(out.shape) < HEAD_DIM
    kc_ref[0] = jnp.where(low, out * _low_half_rms_scale(out, low) * g_ref[...], 0.0).astype(kc_ref.dtype)
    cv_ref[0] = out.astype(cv_ref.dtype)


def _compress(xc, pa, pb, w1a, w1b, w2, g):
    bsz, nc, width = xc.shape
    const = lambda shape: pl.BlockSpec(shape, lambda b: (0, 0))
    slab = pl.BlockSpec((1, nc, LANES), lambda b: (b, 0, 0))
    return pl.pallas_call(
        _compress_kernel,
        grid=(bsz,),
        in_specs=[pl.BlockSpec((1, nc, width), lambda b: (b, 0, 0)),
                  const((1, width)), const((1, width)), const((width, 2 * CMP_HIDDEN)),
                  const((width, 2 * CMP_HIDDEN)), const((2 * CMP_HIDDEN, LANES)), const((1, LANES))],
        out_specs=[slab, slab],
        out_shape=[jax.ShapeDtypeStruct((bsz, nc, LANES), _MXU)] * 2,
        compiler_params=_params(("parallel",)),
        name="compress",
    )(xc, pa, pb, w1a, w1b, w2, g)


def _bucket_thresholds():
    max_exact = REL_BUCKETS // 2
    d = np.arange(0, REL_MAX_DIST + 1)
    large = max_exact + (np.log(np.maximum(d, 1).astype(np.float32) / max_exact)
                         / math.log(REL_MAX_DIST / max_exact) * (REL_BUCKETS - max_exact)).astype(np.int32)
    bucket = np.where(d < max_exact, d, np.minimum(large, REL_BUCKETS - 1))
    assert bucket[-1] == REL_BUCKETS - 1 and np.all(np.diff(bucket) >= 0)
    return [int(np.argmax(bucket >= k)) for k in range(REL_BUCKETS)]


_BUCKET_THR = _bucket_thresholds()

NSA_R = TQ_NSA // KT_NSA
NSA_D = WINDOW // KT_NSA
NSA_TAIL = NSA_D + NSA_R
WIN_ENTRIES = NSA_TAIL + 1
SEL_ENTRIES = NSA_R + 3


def _win_entry(e):
    return jnp.minimum(e, NSA_TAIL)


def _sel_entry(e):
    return jnp.clip(e - (NSA_D - 2), 0, NSA_R + 2)


def _rel_bias(dist, rb_ref, h):
    far = rb_ref[REL_BUCKETS - 1, h]
    val = jnp.full(dist.shape, LOG2E * (rb_ref[0, h] - far), _F32)
    for k in range(1, REL_BUCKETS):
        val = jnp.where(dist >= _BUCKET_THR[k], LOG2E * (rb_ref[k, h] - far), val)
    return val


def _nsa_tables_kernel(rb_ref, win_ref, sel_ref, cmp_ref):
    qi = pl.program_id(0)
    tq = TQ_NSA
    nc = cmp_ref.shape[2]

    @pl.when(qi == 0)
    def _():
        i = _row((tq, KT_NSA))
        c = _lane((tq, KT_NSA))
        for h in range(N_HEADS):
            rs = slice(h * tq, (h + 1) * tq)
            for e in range(NSA_TAIL):
                dist = i + (NSA_D - e) * KT_NSA - c
                ok = jnp.where(dist >= 0, jnp.where(dist < WINDOW, 1, 0), 0) > 0
                tab = jnp.where(ok, _rel_bias(dist, rb_ref, h), NEG)
                win_ref[e, rs, :] = tab
                if e >= NSA_D - 1:
                    sel_ref[e - (NSA_D - 2), rs, :] = tab
            win_ref[NSA_TAIL, rs, :] = jnp.full((tq, KT_NSA), NEG, _F32)
            sel_ref[0, rs, :] = jnp.zeros((tq, KT_NSA), _F32)
            sel_ref[NSA_R + 2, rs, :] = jnp.full((tq, KT_NSA), NEG, _F32)

    t = qi * tq + _row((tq, nc))
    dist = t - (CMP_STRIDE * _lane((tq, nc)) + CMP_LEN - 1)
    for h in range(N_HEADS):
        cmp_ref[0, h * tq:(h + 1) * tq, :] = jnp.where(dist >= 0, _rel_bias(dist, rb_ref, h), NEG)


def _nsa_tables(rel_bias, s_len):
    nq = s_len // TQ_NSA
    nc = s_len // CMP_STRIDE
    rows = N_HEADS * TQ_NSA
    return pl.pallas_call(
        _nsa_tables_kernel,
        grid=(nq,),
        in_specs=[pl.BlockSpec(memory_space=pltpu.SMEM)],
        out_specs=[pl.BlockSpec((WIN_ENTRIES, rows, KT_NSA), lambda i: (0, 0, 0)),
                   pl.BlockSpec((SEL_ENTRIES, rows, KT_NSA), lambda i: (0, 0, 0)),
                   pl.BlockSpec((1, rows, nc), lambda i: (i, 0, 0))],
        out_shape=[jax.ShapeDtypeStruct((WIN_ENTRIES, rows, KT_NSA), _F32),
                   jax.ShapeDtypeStruct((SEL_ENTRIES, rows, KT_NSA), _F32),
                   jax.ShapeDtypeStruct((nq, rows, nc), _F32)],
        compiler_params=_params(("arbitrary",)),
        name="nsa_tables",
    )(rel_bias)


def _topk_penalty(imp_t, qi, tq):
    n_blk = imp_t.shape[0]
    blk = _row((n_blk, tq))
    cur = (qi * tq + _lane((n_blk, tq))) >> SLC_SHIFT
    forced = jnp.where(blk == 0, 1, jnp.where(blk == cur, 1, jnp.where(blk == cur - 1, 1, 0))) > 0
    val = jnp.where(blk <= cur, jnp.where(forced, BIG, imp_t), NEG)
    groups = [val[g * SUBLANES:(g + 1) * SUBLANES, :] for g in range(n_blk // SUBLANES)]
    sub = _row((SUBLANES, tq))
    ranks = [jnp.zeros((SUBLANES, tq), _F32) for _ in groups]
    for j in range(n_blk):
        vj = val[j:j + 1, :]
        jg, jr = divmod(j, SUBLANES)
        for g, vg in enumerate(groups):
            ge = jnp.where(vj >= vg, 1.0, 0.0)
            gt = jnp.where(vj > vg, 1.0, 0.0)
            if g > jg:
                ranks[g] = ranks[g] + ge
            elif g < jg:
                ranks[g] = ranks[g] + gt
            else:
                ranks[g] = ranks[g] + jnp.where(sub > jr, ge, gt)
    rank = jnp.concatenate(ranks, axis=0)
    return jnp.where(rank < float(SLC_TOPK), 0.0, NEG)


def _nsa_attn_kernel(q4_ref, kc_ref, cv_ref, ks_ref, vs_ref, kw_ref, vw_ref, gate_ref,
                     win_ref, sel_ref, cmp_ref, ovl_ref, on_ref, out_ref):
    tq = TQ_NSA
    kt = KT_NSA
    rows = N_HEADS * tq
    qi = pl.program_id(1)
    first = qi * NSA_R
    q4 = q4_ref[0].reshape(rows, LANES)
    lane = _lane((tq, LANES))
    low = lane < HEAD_DIM
    tail_w = NSA_TAIL * kt

    def tail_table(tab_ref, entry_of, first_tile, skip_before=0):
        pieces = []
        for u in range(NSA_TAIL):
            tile = first_tile + u
            entry = entry_of(tile - first + NSA_D)
            pieces.append(tab_ref[jnp.where(tile < skip_before, tab_ref.shape[0] - 1, entry)])
        return jnp.concatenate(pieces, axis=1)

    s = _mm_nt(q4, kc_ref[0]) + cmp_ref[0]
    valid = s > 0.5 * NEG
    m = jnp.max(s, axis=-1, keepdims=True)
    p = jnp.where(valid, jnp.exp2(s - m), 0.0)
    l = jnp.sum(p, axis=-1, keepdims=True)
    p = p / jnp.where(l > 0.0, l, 1.0)
    o_c = jnp.dot(p.astype(_MXU), cv_ref[0], preferred_element_type=_F32)

    psum = p[0:tq] + p[tq:2 * tq] + p[2 * tq:3 * tq] + p[3 * tq:4 * tq]
    p_hi = _round_mxu(psum)
    imp = _mm(p_hi, ovl_ref[...]) + _mm(psum - p_hi, ovl_ref[...])

    pen_t = _topk_penalty(imp.T[0:SLC_BLOCK, :], qi, tq)
    pen = jnp.concatenate([jnp.zeros((LANES - SLC_BLOCK, tq), _F32), pen_t], axis=0).T
    pen4 = jnp.concatenate([pen] * N_HEADS, axis=0)
    q_aug = jnp.where(_lane((rows, LANES)) < HEAD_DIM, q4.astype(_F32), pen4).astype(_MXU)

    w_tile = jnp.maximum(first - NSA_D, 0)
    w_start = pl.multiple_of(w_tile * kt, kt)
    s_w = _mm_nt(q4, kw_ref[0, pl.ds(w_start, tail_w), :]) + tail_table(win_ref, _win_entry, w_tile)
    m_w = jnp.max(s_w, axis=-1, keepdims=True)

    def sel_far(width):
        def step(c, carry):
            start = pl.multiple_of(c * width, width)
            s = _mm_nt(q_aug, ks_ref[0, pl.ds(start, width), :])
            return _flash_step_sumlane(s, vs_ref[0, pl.ds(start, width), :], carry)
        return step

    far_w = NSA_FAR * kt
    n_far = jnp.maximum(first - 1, 0) // NSA_FAR
    n_dbl = n_far // 2
    carry = lax.fori_loop(0, n_dbl, sel_far(2 * far_w), _flash_init_sumlane(rows))
    carry = lax.fori_loop(2 * n_dbl, n_far, sel_far(far_w), carry)
    done = n_far * NSA_FAR
    t_tile = jnp.minimum(done, ks_ref.shape[1] // kt - NSA_TAIL)
    t_start = pl.multiple_of(t_tile * kt, kt)
    s = _mm_nt(q_aug, ks_ref[0, pl.ds(t_start, tail_w), :]) + tail_table(sel_ref, _sel_entry, t_tile, done)
    acc_w = jnp.dot(jnp.exp2(s_w - m_w).astype(_MXU), vw_ref[0, pl.ds(w_start, tail_w), :],
                    preferred_element_type=_F32)
    o_w = _sumlane_normalize(acc_w, SUM_LANE)

    g = gate_ref[0]
    gate = lambda branch, h: g[:, SM_GATE_LANE + branch * N_HEADS + h:SM_GATE_LANE + branch * N_HEADS + h + 1]
    head_rows = [slice(h * tq, (h + 1) * tq) for h in range(N_HEADS)]
    partial = [gate(0, h) * o_c[rs] + gate(2, h) * o_w[rs] for h, rs in enumerate(head_rows)]
    _, acc_s = _flash_step_sumlane(s, vs_ref[0, pl.ds(t_start, tail_w), :], carry)
    o_s = _sumlane_normalize(acc_s, SUM_LANE)

    heads = [partial[h] + gate(1, h) * o_s[rs] for h, rs in enumerate(head_rows)]
    slabs = [jnp.where(low, pltpu.roll(heads[2 * p], HEAD_DIM, axis=1), heads[2 * p + 1])
             for p in range(N_HEADS // 2)]
    y = jnp.concatenate(slabs, axis=1)
    out_ref[0] = (y * _rms_scale(y) * on_ref[...]).astype(out_ref.dtype)


def _nsa_attn(q4, kc, cv, ks, vs, kw, vw, gates, win_tab, sel_tab, cmp_tab, ovl, on):
    bsz, _, s_len, _ = q4.shape
    tq = TQ_NSA
    nc = kc.shape[1]
    rows = N_HEADS * tq
    assert s_len >= NSA_TAIL * KT_NSA
    full = lambda n: pl.BlockSpec((1, n, LANES), lambda b, i: (b, 0, 0))
    return pl.pallas_call(
        _nsa_attn_kernel,
        grid=(bsz, s_len // tq),
        in_specs=[pl.BlockSpec((1, N_HEADS, tq, LANES), lambda b, i: (b, 0, i, 0)),
                  full(nc), full(nc), full(s_len), full(s_len), full(s_len), full(s_len),
                  pl.BlockSpec((1, tq, LANES), lambda b, i: (b, i, 0)),
                  _resident((WIN_ENTRIES, rows, KT_NSA)), _resident((SEL_ENTRIES, rows, KT_NSA)),
                  pl.BlockSpec((1, rows, nc), lambda b, i: (i, 0, 0)),
                  _resident((nc, LANES)),
                  pl.BlockSpec((1, GROUP), lambda b, i: (0, 0))],
        out_specs=pl.BlockSpec((1, tq, GROUP), lambda b, i: (b, i, 0)),
        out_shape=jax.ShapeDtypeStruct((bsz, s_len, GROUP), _MXU),
        compiler_params=_params(("parallel", "arbitrary")),
        name="nsa_attn",
    )(q4, kc, cv, ks, vs, kw, vw, gates, win_tab, sel_tab, cmp_tab, ovl, on)


def _ffn_kernel(x_ref, mab_ref, mc_ref, md_ref, wo_ref, g_ref, wgu_ref, wd_ref, out_ref):
    x1 = (x_ref[...]
          + jnp.dot(mab_ref[...], wo_ref[0:2 * GROUP, :], preferred_element_type=_F32)
          + jnp.dot(mc_ref[...], wo_ref[2 * GROUP:3 * GROUP, :], preferred_element_type=_F32)
          + jnp.dot(md_ref[...], wo_ref[3 * GROUP:4 * GROUP, :], preferred_element_type=_F32))
    xn = (x1 * _rms_scale(x1) * g_ref[...]).astype(_MXU)
    out_ref[...] = x1
    for j in range(D_FF // TF_FFN):
        cols = slice(j * TF_FFN, (j + 1) * TF_FFN)
        gt = jnp.dot(xn, wgu_ref[:, cols], preferred_element_type=_F32)
        up = jnp.dot(xn, wgu_ref[:, D_FF + j * TF_FFN:D_FF + (j + 1) * TF_FFN], preferred_element_type=_F32)
        hid = (gt * jax.nn.sigmoid(gt)) * up
        out_ref[...] += jnp.dot(hid.astype(_MXU), wd_ref[cols, :], preferred_element_type=_F32)


def _out_ffn(x2d, mab, mc, md, wo, g, wgu, wd):
    t = x2d.shape[0]
    tm = min(TM_FFN, t)
    assert t % tm == 0 and D_FF % TF_FFN == 0
    row = lambda width: pl.BlockSpec((tm, width), lambda i: (i, 0))
    return pl.pallas_call(
        _ffn_kernel,
        grid=(t // tm,),
        in_specs=[row(D_MODEL), row(2 * GROUP), row(GROUP), row(GROUP),
                  _resident((D_MODEL, D_MODEL)), pl.BlockSpec((1, D_MODEL), lambda i: (0, 0)),
                  _resident((D_MODEL, 2 * D_FF)), _resident((D_FF, D_MODEL))],
        out_specs=row(D_MODEL),
        out_shape=jax.ShapeDtypeStruct((t, D_MODEL), _F32),
        compiler_params=_params(("parallel",)),
        name="out_ffn",
    )(x2d, mab, mc, md, wo, g, wgu, wd)


def _lane_vec(values, start):
    v = jnp.zeros((LANES,), _F32).at[start:start + values.shape[0]].set(values.astype(_F32))
    return v[None, :]


def _block_diag(w):
    h, d, _ = w.shape
    eye = jnp.eye(h, dtype=w.dtype)
    return (eye[:, None, :, None] * w[:, :, None, :]).reshape(h * d, h * d)


def _overlap_ext(s_len):
    nc = s_len // CMP_STRIDE
    n_cmp = nc - 1
    n_slc = s_len // SLC_BLOCK
    cs = np.arange(n_cmp)[:, None] * CMP_STRIDE
    ss = np.arange(n_slc)[None, :] * SLC_BLOCK
    ov = np.clip(np.minimum(cs + CMP_LEN, ss + SLC_BLOCK) - np.maximum(cs, ss), 0, CMP_LEN)
    ext = np.zeros((nc, LANES), np.float32)
    ext[:n_cmp, :n_slc] = ov
    return jnp.asarray(ext, _MXU)


def _layer_params(l, w_in, lru_w_gates, lru_b_gates, fox_f_bias, fox_qk_norm, nsa_qk_norm, nsa_cmp_pos,
                  nsa_cmp_w1, nsa_cmp_w2, nsa_gate_bias, out_norm):
    scale = HEAD_DIM ** -0.5 * LOG2E
    w = w_in[l]
    small = jnp.zeros((D_MODEL, LANES), w.dtype)
    small = small.at[:, 0:3 * N_HEADS].set(jnp.repeat(w[:, C_FOX_F:C_FOX_F + N_HEADS], 3, axis=1))
    small = small.at[:, SM_GATE_LANE:SM_GATE_LANE + 3 * N_HEADS].set(w[:, C_NSA_G:C_NSA_G + 3 * N_HEADS])
    c_fox_v = C_FOX_F - GROUP
    fox_v = jnp.pad(w[:, c_fox_v:C_FOX_F].reshape(D_MODEL, N_HEADS, HEAD_DIM),
                    ((0, 0), (0, 0), (0, LANES - HEAD_DIM))).reshape(D_MODEL, N_HEADS * LANES)
    w_perm = jnp.concatenate([w[:, :c_fox_v], fox_v, w[:, C_NSA_Q:C_NSA_G], small], axis=1).astype(_MXU)
    assert w_perm.shape[1] == N_IN_PAD

    wg = jnp.concatenate([_block_diag(lru_w_gates[l, 0]), _block_diag(lru_w_gates[l, 1])], axis=1).astype(_MXU)
    bg = lru_b_gates[l].reshape(1, 2 * GROUP)

    fb = _lane_vec(jnp.repeat(fox_f_bias[l], 3), 0)
    fox_gq = (jnp.tile(fox_qk_norm[l, 0], 2) * scale)[None, :]
    fox_gk = jnp.tile(fox_qk_norm[l, 1], 2)[None, :]

    nsa_gq = (jnp.tile(nsa_qk_norm[l, 0], 2) * scale)[None, :]
    nsa_gc = _lane_vec(nsa_qk_norm[l, 1], 0)
    nsa_gs = _lane_vec(nsa_qk_norm[l, 2], 0)
    nsa_gw = _lane_vec(nsa_qk_norm[l, 3], 0)
    gb = _lane_vec(nsa_gate_bias[l], SM_GATE_LANE)

    half = CMP_LEN // 2

    def pos_ext(lo):
        return jnp.concatenate([nsa_cmp_pos[l, 0, lo:lo + half], nsa_cmp_pos[l, 1, lo:lo + half]],
                               axis=1).reshape(1, half * LANES)

    def w1_ext(lo):
        ext = jnp.zeros((half, LANES, 2 * CMP_HIDDEN), _F32)
        ext = ext.at[:, :HEAD_DIM, :CMP_HIDDEN].set(nsa_cmp_w1[l, 0, lo:lo + half])
        ext = ext.at[:, HEAD_DIM:, CMP_HIDDEN:].set(nsa_cmp_w1[l, 1, lo:lo + half])
        return ext.reshape(half * LANES, 2 * CMP_HIDDEN).astype(_MXU)

    w2 = jnp.zeros((2 * CMP_HIDDEN, LANES), _F32)
    w2 = w2.at[:CMP_HIDDEN, :HEAD_DIM].set(nsa_cmp_w2[l, 0]).at[CMP_HIDDEN:, HEAD_DIM:].set(nsa_cmp_w2[l, 1])
    return dict(w_perm=w_perm, wg=wg, bg=bg, fb=fb, fox_gq=fox_gq, fox_gk=fox_gk, nsa_gq=nsa_gq, nsa_gc=nsa_gc,
                nsa_gs=nsa_gs, nsa_gw=nsa_gw, gb=gb, pa=pos_ext(0), pb=pos_ext(half), w1a=w1_ext(0),
                w1b=w1_ext(half), w2=w2.astype(_MXU), on=out_norm[l].reshape(1, 4 * GROUP))


def kernel(x, norm_mix, w_in, lru_conv_w, lru_conv_b, lru_w_gates, lru_b_gates, lru_lambda, sc_conv_w, fox_f_bias, fox_qk_norm, nsa_qk_norm, nsa_cmp_pos, nsa_cmp_w1, nsa_cmp_w2, nsa_gate_bias, rel_bias, out_norm, w_out, norm_ffn, w_gate_up, w_down):
    bsz, s_len, d_model = x.shape
    depth = w_in.shape[0]
    assert d_model == D_MODEL and s_len % TS_MIX == 0 and s_len % TQ_FOX == 0
    assert SLC_TOPK <= s_len // SLC_BLOCK <= SLC_BLOCK
    t = bsz * s_len

    win_tab, sel_tab, cmp_tab = _nsa_tables(rel_bias, s_len)
    ovl = _overlap_ext(s_len)

    for l in range(depth):
        lp = _layer_params(l, w_in, lru_w_gates, lru_b_gates, fox_f_bias, fox_qk_norm, nsa_qk_norm,
                           nsa_cmp_pos, nsa_cmp_w1, nsa_cmp_w2, nsa_gate_bias, out_norm)
        m_ab, qf, kf, vf, q4, ks, vs, kw, vw, gates, zcmp = _proj_mix(
            x, norm_mix[l][None, :], lp["w_perm"], lru_conv_w[l], lru_conv_b[l][None, :], lp["wg"], lp["bg"],
            lru_lambda[l][None, :], sc_conv_w[l], lp["on"][:, 0:2 * GROUP],
            lp["fb"], lp["fox_gq"], lp["fox_gk"], lp["nsa_gq"], lp["nsa_gs"], lp["nsa_gw"], lp["gb"])
        m_c = _fox_attn(qf, kf, vf, lp["on"][:, 2 * GROUP:3 * GROUP])
        kc, cv = _compress(zcmp, lp["pa"], lp["pb"], lp["w1a"], lp["w1b"], lp["w2"], lp["nsa_gc"])
        m_d = _nsa_attn(q4, kc, cv, ks, vs, kw, vw, gates, win_tab, sel_tab, cmp_tab, ovl,
                        lp["on"][:, 3 * GROUP:4 * GROUP])
        x = _out_ffn(x.reshape(t, D_MODEL), m_ab.reshape(t, 2 * GROUP), m_c.reshape(t, GROUP),
                     m_d.reshape(t, GROUP), w_out[l].astype(_MXU), norm_ffn[l][None, :],
                     w_gate_up[l].astype(_MXU), w_down[l].astype(_MXU)).reshape(bsz, s_len, D_MODEL)
    return x
```

```python
import math

import numpy as np
import jax
import jax.numpy as jnp
from jax import lax
from jax.experimental import pallas as pl
from jax.experimental.pallas import tpu as pltpu

D_MODEL = 1024
GROUP = 256
HEAD_DIM = 64
N_HEADS = 4
LRU_CONV = 4
LRU_C = 8.0
SC_CONV = 3
CMP_LEN = 32
CMP_STRIDE = 16
CMP_HIDDEN = 128
SLC_BLOCK = 64
SLC_SHIFT = SLC_BLOCK.bit_length() - 1
SLC_TOPK = 16
WINDOW = 512
REL_BUCKETS = 32
REL_MAX_DIST = 128
D_FF = 2816
RMS_EPS = 1e-6
NEG = -1e30
BIG = 1e30
LOG2E = math.log2(math.e)

C_FOX_F = 8 * GROUP
C_NSA_Q = C_FOX_F + N_HEADS
C_NSA_G = C_NSA_Q + GROUP + 6 * HEAD_DIM

LANES = 128
SUBLANES = 8
VMEM_LIMIT_BYTES = 56 * 1024 * 1024

TS_MIX = 512
SCAN_SEG = 32
TQ_FOX = 1024
KT_NSA = 128
TQ_NSA = 256
NSA_FAR = 4
TM_FFN = 1024
TF_FFN = 256

SM_GATE_LANE = 16
SUM_LANE = 0

COL_FOX = 5 * GROUP
COL_NSA_Q = COL_FOX + 2 * GROUP + N_HEADS * LANES
COL_SLABS = COL_NSA_Q + GROUP
N_IN_PAD = COL_SLABS + 4 * LANES

_MXU = jnp.bfloat16
_F32 = jnp.float32


def _params(sem):
    return pltpu.CompilerParams(dimension_semantics=sem, vmem_limit_bytes=VMEM_LIMIT_BYTES)


def _resident(shape):
    zeros = (0,) * len(shape)
    return pl.BlockSpec(shape, lambda *_: zeros, pipeline_mode=pl.Buffered(1))


def _mm(a, b):
    return jnp.dot(a.astype(_MXU), b.astype(_MXU), preferred_element_type=_F32)


def _mm_nt(a, b):
    return lax.dot_general(a.astype(_MXU), b.astype(_MXU), (((1,), (1,)), ((), ())),
                           preferred_element_type=_F32)


def _lane(shape):
    return lax.broadcasted_iota(jnp.int32, shape, len(shape) - 1)


def _row(shape):
    return lax.broadcasted_iota(jnp.int32, shape, 0)


def _rms_scale(x):
    return lax.rsqrt(jnp.mean(x * x, axis=-1, keepdims=True) + RMS_EPS)


def _gelu(x):
    c = math.sqrt(2.0 / math.pi)
    return x * (0.5 * (1.0 + jnp.tanh(c * (x + 0.044715 * (x * x * x)))))


def _round_mxu(x):
    return x.astype(_MXU).astype(_F32)


def _low_half_rms_scale(x, low):
    s = jnp.sum(jnp.where(low, x * x, 0.0), axis=-1, keepdims=True)
    return lax.rsqrt(s * (1.0 / HEAD_DIM) + RMS_EPS)


def _half_rms_scale(x, low):
    x2 = x * x
    s_lo = jnp.sum(jnp.where(low, x2, 0.0), axis=-1, keepdims=True)
    s_hi = jnp.sum(jnp.where(low, 0.0, x2), axis=-1, keepdims=True)
    return jnp.where(low, lax.rsqrt(s_lo * (1.0 / HEAD_DIM) + RMS_EPS),
                     lax.rsqrt(s_hi * (1.0 / HEAD_DIM) + RMS_EPS))


def _linear_scan_rows(a, b, h_in):
    n = a.shape[0]
    pos = _row(a.shape) & (SCAN_SEG - 1)
    d = 1
    while d < SCAN_SEG:
        keep = pos >= d
        a_prev = jnp.where(keep, pltpu.roll(a, d, axis=0), 1.0)
        b_prev = jnp.where(keep, pltpu.roll(b, d, axis=0), 0.0)
        b = a * b_prev + b
        a = a * a_prev
        d *= 2
    segs = []
    for r0 in range(0, n, SCAN_SEG):
        seg = b[r0:r0 + SCAN_SEG] + a[r0:r0 + SCAN_SEG] * h_in
        h_in = seg[SCAN_SEG - 1:SCAN_SEG]
        segs.append(seg)
    return jnp.concatenate(segs, axis=0)


def _cumsum_rows(x, c_in):
    n = x.shape[0]
    pos = _row(x.shape) & (SCAN_SEG - 1)
    d = 1
    while d < SCAN_SEG:
        x = x + jnp.where(pos >= d, pltpu.roll(x, d, axis=0), 0.0)
        d *= 2
    segs = []
    for r0 in range(0, n, SCAN_SEG):
        seg = x[r0:r0 + SCAN_SEG] + c_in
        c_in = seg[SCAN_SEG - 1:SCAN_SEG]
        segs.append(seg)
    return jnp.concatenate(segs, axis=0)


def _mix_ab(zab, cw_ref, cb_ref, wg_ref, bg_ref, lam_ref, scw_ref, on_ref, out_ref, xext_ref, uext_ref, h_ref):
    ts = zab.shape[0]
    xr = zab[:, 0:GROUP]
    gate = zab[:, GROUP:2 * GROUP]
    bgt = zab[:, 2 * GROUP:3 * GROUP]
    cgt = zab[:, 3 * GROUP:4 * GROUP]
    xs = zab[:, 4 * GROUP:5 * GROUP]

    xext_ref[SUBLANES:SUBLANES + ts, :] = xr
    xc = cb_ref[...] + cw_ref[LRU_CONV - 1:LRU_CONV, :] * xr
    for k in range(LRU_CONV - 1):
        xc = xc + cw_ref[k:k + 1, :] * xext_ref[pl.ds(SUBLANES - (LRU_CONV - 1) + k, ts), :]
    xext_ref[0:SUBLANES, :] = xr[ts - SUBLANES:ts, :]

    gi = _mm(xc, wg_ref[...]) + bg_ref[...]
    r = jax.nn.sigmoid(gi[:, 0:GROUP])
    ig = jax.nn.sigmoid(gi[:, GROUP:2 * GROUP])
    lam = lam_ref[...]
    softplus_neg = jnp.maximum(-lam, 0.0) + jnp.log1p(jnp.exp(-jnp.abs(lam)))
    log_a = (-LRU_C * softplus_neg) * r
    a = jnp.exp(log_a)
    g = -jnp.tanh(log_a) * (a * a + 1.0)
    b = jnp.where(g > 0.0, g * lax.rsqrt(g), 0.0) * (ig * xc)
    h = _linear_scan_rows(a, b, h_ref[0:1, :])
    h_ref[...] = jnp.broadcast_to(h[ts - 1:ts, :], h_ref.shape)
    y_a = h * _gelu(gate)

    u = cgt * xs
    uext_ref[SUBLANES:SUBLANES + ts, :] = u
    cv = scw_ref[SC_CONV - 1:SC_CONV, :] * u
    for k in range(SC_CONV - 1):
        cv = cv + scw_ref[k:k + 1, :] * uext_ref[pl.ds(SUBLANES - (SC_CONV - 1) + k, ts), :]
    uext_ref[0:SUBLANES, :] = u[ts - SUBLANES:ts, :]
    y_b = bgt * cv

    out_ref[0, :, 0:GROUP] = (y_a * _rms_scale(y_a) * on_ref[:, 0:GROUP]).astype(out_ref.dtype)
    out_ref[0, :, GROUP:2 * GROUP] = (y_b * _rms_scale(y_b) * on_ref[:, GROUP:2 * GROUP]).astype(out_ref.dtype)


def _fox_prep(zfox, zsm, fb_ref, gq_ref, gk_ref, q_ref, k_ref, v_ref, c_ref):
    ts = zfox.shape[0]
    shape = (ts, LANES)
    lane = _lane(shape)
    low = lane < HEAD_DIM

    f = zsm + fb_ref[...]
    logf = jnp.minimum(f, 0.0) - jnp.log1p(jnp.exp(-jnp.abs(f)))
    c = _cumsum_rows(logf, c_ref[0:1, :])
    c_ref[...] = jnp.broadcast_to(c[ts - 1:ts, :], c_ref.shape)
    neg_c = -LOG2E * c
    p1 = _round_mxu(neg_c)
    r1 = neg_c - p1
    p2 = _round_mxu(r1)
    p3 = _round_mxu(r1 - p2)
    piece = lane - 3 * ((lane * 11) >> 5)
    csel = jnp.where(piece == 0, p1, jnp.where(piece == 1, p2, p3))

    ones_aug = jnp.where(lane < HEAD_DIM + 3, 1.0, 0.0)
    for p in range(N_HEADS // 2):
        qs = zfox[:, LANES * p:LANES * (p + 1)]
        ks = zfox[:, GROUP + LANES * p:GROUP + LANES * (p + 1)]
        qn = qs * _half_rms_scale(qs, low) * gq_ref[...]
        kn = ks * _half_rms_scale(ks, low) * gk_ref[...]
        for half in range(2):
            h = 2 * p + half
            qh = qn if half == 0 else pltpu.roll(qn, HEAD_DIM, axis=1)
            kh = kn if half == 0 else pltpu.roll(kn, HEAD_DIM, axis=1)
            ch = pltpu.roll(csel, HEAD_DIM - 3 * h, axis=1)
            q_ref[0, h] = jnp.where(low, qh, ones_aug).astype(q_ref.dtype)
            k_ref[0, h] = jnp.where(low, kh, jnp.where(lane < HEAD_DIM + 3, ch, 0.0)).astype(k_ref.dtype)
    for h in range(N_HEADS):
        vh = zfox[:, 2 * GROUP + LANES * h:2 * GROUP + LANES * (h + 1)]
        v_ref[0, h] = jnp.where(low, vh, 1.0).astype(v_ref.dtype)


def _nsa_prep(zq, zsel, zwin, zsm, row0, gq_ref, gs_ref, gw_ref, gb_ref,
              q4_ref, ks_ref, vs_ref, kw_ref, vw_ref, gate_ref):
    ts = zq.shape[0]
    shape = (ts, LANES)
    lane = _lane(shape)
    low = lane < HEAD_DIM
    for p in range(N_HEADS // 2):
        qs = zq[:, LANES * p:LANES * (p + 1)]
        qn = qs * _half_rms_scale(qs, low) * gq_ref[...]
        q4_ref[0, 2 * p] = jnp.where(low, qn, 0.0).astype(q4_ref.dtype)
        q4_ref[0, 2 * p + 1] = jnp.where(low, pltpu.roll(qn, HEAD_DIM, axis=1), 0.0).astype(q4_ref.dtype)

    blk = (row0 + _row(shape)) >> SLC_SHIFT
    onehot = jnp.where(lane - HEAD_DIM == blk, 1.0, 0.0)
    ks_ref[0] = jnp.where(low, zsel * _low_half_rms_scale(zsel, low) * gs_ref[...], onehot).astype(ks_ref.dtype)
    vs_ref[0] = jnp.where(lane == SUM_LANE, 1.0, zsel).astype(vs_ref.dtype)
    kw_ref[0] = jnp.where(low, zwin * _low_half_rms_scale(zwin, low) * gw_ref[...], 0.0).astype(kw_ref.dtype)
    vw_ref[0] = jnp.where(lane == SUM_LANE, 1.0, zwin).astype(vw_ref.dtype)
    gate_ref[0] = jax.nn.sigmoid(zsm + gb_ref[...])


def _proj_mix_kernel(x_ref, g_ref, w_ref,
                     cw_ref, cb_ref, wg_ref, bg_ref, lam_ref, scw_ref, on_ref,
                     fb_ref, fgq_ref, fgk_ref, ngq_ref, ngs_ref, ngw_ref, gb_ref,
                     mab_ref, qf_ref, kf_ref, vf_ref, q4_ref, ks_ref, vs_ref, kw_ref, vw_ref, gate_ref, zcmp_ref,
                     xext_ref, uext_ref, h_ref, c_ref, zc_ref):
    ts = x_ref.shape[1]
    ti = pl.program_id(1)

    @pl.when(ti == 0)
    def _():
        xext_ref[0:SUBLANES, :] = jnp.zeros((SUBLANES, GROUP), _F32)
        uext_ref[0:SUBLANES, :] = jnp.zeros((SUBLANES, GROUP), _F32)
        h_ref[...] = jnp.zeros_like(h_ref)
        c_ref[...] = jnp.zeros_like(c_ref)

    x = x_ref[0]
    xn = (x * _rms_scale(x) * g_ref[...]).astype(_MXU)
    zab = jnp.dot(xn, w_ref[:, 0:COL_FOX], preferred_element_type=_F32)
    slabs = jnp.dot(xn, w_ref[:, COL_SLABS:N_IN_PAD], preferred_element_type=_F32)
    zfox = jnp.dot(xn, w_ref[:, COL_FOX:COL_NSA_Q], preferred_element_type=_F32)
    zq = jnp.dot(xn, w_ref[:, COL_NSA_Q:COL_SLABS], preferred_element_type=_F32)
    zc_ref[...] = slabs[:, 0:LANES]
    for tok in range(CMP_STRIDE):
        zcmp_ref[0, :, tok * LANES:(tok + 1) * LANES] = zc_ref[pl.ds(tok, ts // CMP_STRIDE, stride=CMP_STRIDE), :]
    zsel = slabs[:, LANES:2 * LANES]
    zwin = slabs[:, 2 * LANES:3 * LANES]
    zsm = slabs[:, 3 * LANES:4 * LANES]
    _fox_prep(zfox, zsm, fb_ref, fgq_ref, fgk_ref, qf_ref, kf_ref, vf_ref, c_ref)
    _nsa_prep(zq, zsel, zwin, zsm, ti * ts, ngq_ref, ngs_ref, ngw_ref, gb_ref,
              q4_ref, ks_ref, vs_ref, kw_ref, vw_ref, gate_ref)
    _mix_ab(zab, cw_ref, cb_ref, wg_ref, bg_ref, lam_ref, scw_ref, on_ref, mab_ref, xext_ref, uext_ref, h_ref)


def _proj_mix(x, gain, w, cw, cb, wg, bg, lam, scw, on_ab, fb, fgq, fgk, ngq, ngs, ngw, gb):
    bsz, s_len, _ = x.shape
    ts = TS_MIX
    vec = lambda width: pl.BlockSpec((1, width), lambda b, t: (0, 0))
    rows = lambda width: pl.BlockSpec((1, ts, width), lambda b, t: (b, t, 0))
    heads = pl.BlockSpec((1, N_HEADS, ts, LANES), lambda b, t: (b, 0, t, 0))
    sds = jax.ShapeDtypeStruct
    return pl.pallas_call(
        _proj_mix_kernel,
        grid=(bsz, s_len // ts),
        in_specs=[rows(D_MODEL), vec(D_MODEL), _resident((D_MODEL, N_IN_PAD)),
                  pl.BlockSpec((LRU_CONV, GROUP), lambda b, t: (0, 0)), vec(GROUP), _resident((GROUP, 2 * GROUP)),
                  vec(2 * GROUP), vec(GROUP), pl.BlockSpec((SC_CONV, GROUP), lambda b, t: (0, 0)), vec(2 * GROUP),
                  vec(LANES), vec(LANES), vec(LANES), vec(LANES), vec(LANES), vec(LANES), vec(LANES)],
        out_specs=[rows(2 * GROUP), heads, heads, heads, heads, rows(LANES), rows(LANES),
                   rows(LANES), rows(LANES), rows(LANES),
                   pl.BlockSpec((1, ts // CMP_STRIDE, CMP_STRIDE * LANES), lambda b, t: (b, t, 0))],
        out_shape=[sds((bsz, s_len, 2 * GROUP), _MXU),
                   sds((bsz, N_HEADS, s_len, LANES), _MXU), sds((bsz, N_HEADS, s_len, LANES), _MXU),
                   sds((bsz, N_HEADS, s_len, LANES), _MXU),
                   sds((bsz, N_HEADS, s_len, LANES), _MXU),
                   sds((bsz, s_len, LANES), _MXU), sds((bsz, s_len, LANES), _MXU),
                   sds((bsz, s_len, LANES), _MXU), sds((bsz, s_len, LANES), _MXU),
                   sds((bsz, s_len, LANES), _F32), sds((bsz, s_len // CMP_STRIDE, CMP_STRIDE * LANES), _F32)],
        scratch_shapes=[pltpu.VMEM((ts + SUBLANES, GROUP), _F32), pltpu.VMEM((ts + SUBLANES, GROUP), _F32),
                        pltpu.VMEM((SUBLANES, GROUP), _F32), pltpu.VMEM((SUBLANES, LANES), _F32),
                        pltpu.VMEM((ts, LANES), _F32)],
        compiler_params=_params(("parallel", "arbitrary")),
        name="proj_mix",
    )(x, gain, w, cw, cb, wg, bg, lam, scw, on_ab, fb, fgq, fgk, ngq, ngs, ngw, gb)


def _flash_step_sumlane(s, v, carry):
    m, acc = carry
    m_new = jnp.maximum(m, jnp.max(s, axis=-1, keepdims=True))
    p = jnp.exp2(s - m_new).astype(_MXU)
    pv = jnp.dot(p, v, preferred_element_type=_F32)
    return m_new, jnp.exp2(m - m_new) * acc + pv


def _flash_init_sumlane(rows):
    return (jnp.full((rows, 1), NEG, _F32), jnp.zeros((rows, LANES), _F32))


def _sumlane_normalize(acc, sum_lane):
    return acc / acc[:, sum_lane:sum_lane + 1]


def _fox_attn_kernel(q_ref, k_ref, v_ref, on_ref, out_ref):
    tq = q_ref.shape[2]
    qi = pl.program_id(1)
    qs = [q_ref[0, h] for h in range(N_HEADS)]

    def step(c, carries, causal=False):
        start = pl.multiple_of(c * tq, tq)
        new = []
        for h in range(N_HEADS):
            s = _mm_nt(qs[h], k_ref[0, h, pl.ds(start, tq), :])
            if causal:
                s = jnp.where(_lane((tq, tq)) <= _row((tq, tq)), s, NEG)
            new.append(_flash_step_sumlane(s, v_ref[0, h, pl.ds(start, tq), :], carries[h]))
        return tuple(new)

    carries = lax.fori_loop(0, qi, step, tuple(_flash_init_sumlane(tq) for _ in range(N_HEADS)))
    carries = step(qi, carries, causal=True)
    low = _lane((tq, LANES)) < HEAD_DIM
    pairs = []
    for p in range(N_HEADS // 2):
        a_even, a_odd = carries[2 * p][1], carries[2 * p + 1][1]
        r_even, r_odd = pltpu.roll(a_even, HEAD_DIM, axis=1), pltpu.roll(a_odd, HEAD_DIM, axis=1)
        pairs.append(jnp.where(low, a_even, r_odd) / jnp.where(low, r_even, a_odd))
    y = jnp.concatenate(pairs, axis=1)
    out_ref[0] = (y * _rms_scale(y) * on_ref[...]).astype(out_ref.dtype)


def _fox_attn(qf, kf, vf, on):
    bsz, _, s_len, _ = qf.shape
    tq = min(TQ_FOX, s_len)
    assert s_len % tq == 0
    return pl.pallas_call(
        _fox_attn_kernel,
        grid=(bsz, s_len // tq),
        in_specs=[pl.BlockSpec((1, N_HEADS, tq, LANES), lambda b, i: (b, 0, i, 0)),
                  pl.BlockSpec((1, N_HEADS, s_len, LANES), lambda b, i: (b, 0, 0, 0)),
                  pl.BlockSpec((1, N_HEADS, s_len, LANES), lambda b, i: (b, 0, 0, 0)),
                  pl.BlockSpec((1, GROUP), lambda b, i: (0, 0))],
        out_specs=pl.BlockSpec((1, tq, GROUP), lambda b, i: (b, i, 0)),
        out_shape=jax.ShapeDtypeStruct((bsz, s_len, GROUP), _MXU),
        compiler_params=_params(("parallel", "arbitrary")),
        name="fox_attn",
    )(qf, kf, vf, on)


def _compress_kernel(x_ref, pa_ref, pb_ref, w1a_ref, w1b_ref, w2_ref, g_ref, kc_ref, cv_ref):
    nc = x_ref.shape[1]
    x = x_ref[0]
    ua = _mm(x + pa_ref[...], w1a_ref[...])
    ub = _mm(x + pb_ref[...], w1b_ref[...])
    hid = _gelu(ua + pltpu.roll(ub, nc - 1, axis=0))
    out = _mm(hid, w2_ref[...])
    low = _lane(out.shape) < HEAD_DIM
    kc_ref[0] = jnp.where(low, out * _low_half_rms_scale(out, low) * g_ref[...], 0.0).astype(kc_ref.dtype)
    cv_ref[0] = out.astype(cv_ref.dtype)


def _compress(xc, pa, pb, w1a, w1b, w2, g):
    bsz, nc, width = xc.shape
    const = lambda shape: pl.BlockSpec(shape, lambda b: (0, 0))
    slab = pl.BlockSpec((1, nc, LANES), lambda b: (b, 0, 0))
    return pl.pallas_call(
        _compress_kernel,
        grid=(bsz,),
        in_specs=[pl.BlockSpec((1, nc, width), lambda b: (b, 0, 0)),
                  const((1, width)), const((1, width)), const((width, 2 * CMP_HIDDEN)),
                  const((width, 2 * CMP_HIDDEN)), const((2 * CMP_HIDDEN, LANES)), const((1, LANES))],
        out_specs=[slab, slab],
        out_shape=[jax.ShapeDtypeStruct((bsz, nc, LANES), _MXU)] * 2,
        compiler_params=_params(("parallel",)),
        name="compress",
    )(xc, pa, pb, w1a, w1b, w2, g)


def _bucket_thresholds():
    max_exact = REL_BUCKETS // 2
    d = np.arange(0, REL_MAX_DIST + 1)
    large = max_exact + (np.log(np.maximum(d, 1).astype(np.float32) / max_exact)
                         / math.log(REL_MAX_DIST / max_exact) * (REL_BUCKETS - max_exact)).astype(np.int32)
    bucket = np.where(d < max_exact, d, np.minimum(large, REL_BUCKETS - 1))
    assert bucket[-1] == REL_BUCKETS - 1 and np.all(np.diff(bucket) >= 0)
    return [int(np.argmax(bucket >= k)) for k in range(REL_BUCKETS)]


_BUCKET_THR = _bucket_thresholds()

NSA_R = TQ_NSA // KT_NSA
NSA_D = WINDOW // KT_NSA
NSA_TAIL = NSA_D + NSA_R
WIN_ENTRIES = NSA_TAIL + 1
SEL_ENTRIES = NSA_R + 3


def _win_entry(e):
    return jnp.minimum(e, NSA_TAIL)


def _sel_entry(e):
    return jnp.clip(e - (NSA_D - 2), 0, NSA_R + 2)


def _rel_bias(dist, rb_ref, h):
    far = rb_ref[REL_BUCKETS - 1, h]
    val = jnp.full(dist.shape, LOG2E * (rb_ref[0, h] - far), _F32)
    for k in range(1, REL_BUCKETS):
        val = jnp.where(dist >= _BUCKET_THR[k], LOG2E * (rb_ref[k, h] - far), val)
    return val


def _nsa_tables_kernel(rb_ref, win_ref, sel_ref, cmp_ref):
    qi = pl.program_id(0)
    tq = TQ_NSA
    nc = cmp_ref.shape[2]

    @pl.when(qi == 0)
    def _():
        i = _row((tq, KT_NSA))
        c = _lane((tq, KT_NSA))
        for h in range(N_HEADS):
            rs = slice(h * tq, (h + 1) * tq)
            for e in range(NSA_TAIL):
                dist = i + (NSA_D - e) * KT_NSA - c
                ok = jnp.where(dist >= 0, jnp.where(dist < WINDOW, 1, 0), 0) > 0
                tab = jnp.where(ok, _rel_bias(dist, rb_ref, h), NEG)
                win_ref[e, rs, :] = tab
                if e >= NSA_D - 1:
                    sel_ref[e - (NSA_D - 2), rs, :] = tab
            win_ref[NSA_TAIL, rs, :] = jnp.full((tq, KT_NSA), NEG, _F32)
            sel_ref[0, rs, :] = jnp.zeros((tq, KT_NSA), _F32)
            sel_ref[NSA_R + 2, rs, :] = jnp.full((tq, KT_NSA), NEG, _F32)

    t = qi * tq + _row((tq, nc))
    dist = t - (CMP_STRIDE * _lane((tq, nc)) + CMP_LEN - 1)
    for h in range(N_HEADS):
        cmp_ref[0, h * tq:(h + 1) * tq, :] = jnp.where(dist >= 0, _rel_bias(dist, rb_ref, h), NEG)


def _nsa_tables(rel_bias, s_len):
    nq = s_len // TQ_NSA
    nc = s_len // CMP_STRIDE
    rows = N_HEADS * TQ_NSA
    return pl.pallas_call(
        _nsa_tables_kernel,
        grid=(nq,),
        in_specs=[pl.BlockSpec(memory_space=pltpu.SMEM)],
        out_specs=[pl.BlockSpec((WIN_ENTRIES, rows, KT_NSA), lambda i: (0, 0, 0)),
                   pl.BlockSpec((SEL_ENTRIES, rows, KT_NSA), lambda i: (0, 0, 0)),
                   pl.BlockSpec((1, rows, nc), lambda i: (i, 0, 0))],
        out_shape=[jax.ShapeDtypeStruct((WIN_ENTRIES, rows, KT_NSA), _F32),
                   jax.ShapeDtypeStruct((SEL_ENTRIES, rows, KT_NSA), _F32),
                   jax.ShapeDtypeStruct((nq, rows, nc), _F32)],
        compiler_params=_params(("arbitrary",)),
        name="nsa_tables",
    )(rel_bias)


def _topk_penalty(imp_t, qi, tq):
    n_blk = imp_t.shape[0]
    blk = _row((n_blk, tq))
    cur = (qi * tq + _lane((n_blk, tq))) >> SLC_SHIFT
    forced = jnp.where(blk == 0, 1, jnp.where(blk == cur, 1, jnp.where(blk == cur - 1, 1, 0))) > 0
    val = jnp.where(blk <= cur, jnp.where(forced, BIG, imp_t), NEG)
    groups = [val[g * SUBLANES:(g + 1) * SUBLANES, :] for g in range(n_blk // SUBLANES)]
    sub = _row((SUBLANES, tq))
    ranks = [jnp.zeros((SUBLANES, tq), _F32) for _ in groups]
    for j in range(n_blk):
        vj = val[j:j + 1, :]
        jg, jr = divmod(j, SUBLANES)
        for g, vg in enumerate(groups):
            ge = jnp.where(vj >= vg, 1.0, 0.0)
            gt = jnp.where(vj > vg, 1.0, 0.0)
            if g > jg:
                ranks[g] = ranks[g] + ge
            elif g < jg:
                ranks[g] = ranks[g] + gt
            else:
                ranks[g] = ranks[g] + jnp.where(sub > jr, ge, gt)
    rank = jnp.concatenate(ranks, axis=0)
    return jnp.where(rank < float(SLC_TOPK), 0.0, NEG)


def _nsa_attn_kernel(q4_ref, kc_ref, cv_ref, ks_ref, vs_ref, kw_ref, vw_ref, gate_ref,
                     win_ref, sel_ref, cmp_ref, ovl_ref, on_ref, out_ref):
    tq = TQ_NSA
    kt = KT_NSA
    rows = N_HEADS * tq
    qi = pl.program_id(1)
    first = qi * NSA_R
    q4 = q4_ref[0].reshape(rows, LANES)
    lane = _lane((tq, LANES))
    low = lane < HEAD_DIM
    tail_w = NSA_TAIL * kt

    def tail_table(tab_ref, entry_of, first_tile, skip_before=0):
        pieces = []
        for u in range(NSA_TAIL):
            tile = first_tile + u
            entry = entry_of(tile - first + NSA_D)
            pieces.append(tab_ref[jnp.where(tile < skip_before, tab_ref.shape[0] - 1, entry)])
        return jnp.concatenate(pieces, axis=1)

    s = _mm_nt(q4, kc_ref[0]) + cmp_ref[0]
    valid = s > 0.5 * NEG
    m = jnp.max(s, axis=-1, keepdims=True)
    p = jnp.where(valid, jnp.exp2(s - m), 0.0)
    l = jnp.sum(p, axis=-1, keepdims=True)
    p = p / jnp.where(l > 0.0, l, 1.0)
    o_c = jnp.dot(p.astype(_MXU), cv_ref[0], preferred_element_type=_F32)

    psum = p[0:tq] + p[tq:2 * tq] + p[2 * tq:3 * tq] + p[3 * tq:4 * tq]
    p_hi = _round_mxu(psum)
    imp = _mm(p_hi, ovl_ref[...]) + _mm(psum - p_hi, ovl_ref[...])

    pen_t = _topk_penalty(imp.T[0:SLC_BLOCK, :], qi, tq)
    pen = jnp.concatenate([jnp.zeros((LANES - SLC_BLOCK, tq), _F32), pen_t], axis=0).T
    pen4 = jnp.concatenate([pen] * N_HEADS, axis=0)
    q_aug = jnp.where(_lane((rows, LANES)) < HEAD_DIM, q4.astype(_F32), pen4).astype(_MXU)

    w_tile = jnp.maximum(first - NSA_D, 0)
    w_start = pl.multiple_of(w_tile * kt, kt)
    s_w = _mm_nt(q4, kw_ref[0, pl.ds(w_start, tail_w), :]) + tail_table(win_ref, _win_entry, w_tile)
    m_w = jnp.max(s_w, axis=-1, keepdims=True)

    def sel_far(width):
        def step(c, carry):
            start = pl.multiple_of(c * width, width)
            s = _mm_nt(q_aug, ks_ref[0, pl.ds(start, width), :])
            return _flash_step_sumlane(s, vs_ref[0, pl.ds(start, width), :], carry)
        return step

    far_w = NSA_FAR * kt
    n_far = jnp.maximum(first - 1, 0) // NSA_FAR
    n_dbl = n_far // 2
    carry = lax.fori_loop(0, n_dbl, sel_far(2 * far_w), _flash_init_sumlane(rows))
    carry = lax.fori_loop(2 * n_dbl, n_far, sel_far(far_w), carry)
    done = n_far * NSA_FAR
    t_tile = jnp.minimum(done, ks_ref.shape[1] // kt - NSA_TAIL)
    t_start = pl.multiple_of(t_tile * kt, kt)
    s = _mm_nt(q_aug, ks_ref[0, pl.ds(t_start, tail_w), :]) + tail_table(sel_ref, _sel_entry, t_tile, done)
    acc_w = jnp.dot(jnp.exp2(s_w - m_w).astype(_MXU), vw_ref[0, pl.ds(w_start, tail_w), :],
                    preferred_element_type=_F32)
    o_w = _sumlane_normalize(acc_w, SUM_LANE)

    g = gate_ref[0]
    gate = lambda branch, h: g[:, SM_GATE_LANE + branch * N_HEADS + h:SM_GATE_LANE + branch * N_HEADS + h + 1]
    head_rows = [slice(h * tq, (h + 1) * tq) for h in range(N_HEADS)]
    partial = [gate(0, h) * o_c[rs] + gate(2, h) * o_w[rs] for h, rs in enumerate(head_rows)]
    _, acc_s = _flash_step_sumlane(s, vs_ref[0, pl.ds(t_start, tail_w), :], carry)
    o_s = _sumlane_normalize(acc_s, SUM_LANE)

    heads = [partial[h] + gate(1, h) * o_s[rs] for h, rs in enumerate(head_rows)]
    slabs = [jnp.where(low, pltpu.roll(heads[2 * p], HEAD_DIM, axis=1), heads[2 * p + 1])
             for p in range(N_HEADS // 2)]
    y = jnp.concatenate(slabs, axis=1)
    out_ref[0] = (y * _rms_scale(y) * on_ref[...]).astype(out_ref.dtype)


def _nsa_attn(q4, kc, cv, ks, vs, kw, vw, gates, win_tab, sel_tab, cmp_tab, ovl, on):
    bsz, _, s_len, _ = q4.shape
    tq = TQ_NSA
    nc = kc.shape[1]
    rows = N_HEADS * tq
    assert s_len >= NSA_TAIL * KT_NSA
    full = lambda n: pl.BlockSpec((1, n, LANES), lambda b, i: (b, 0, 0))
    return pl.pallas_call(
        _nsa_attn_kernel,
        grid=(bsz, s_len // tq),
        in_specs=[pl.BlockSpec((1, N_HEADS, tq, LANES), lambda b, i: (b, 0, i, 0)),
                  full(nc), full(nc), full(s_len), full(s_len), full(s_len), full(s_len),
                  pl.BlockSpec((1, tq, LANES), lambda b, i: (b, i, 0)),
                  _resident((WIN_ENTRIES, rows, KT_NSA)), _resident((SEL_ENTRIES, rows, KT_NSA)),
                  pl.BlockSpec((1, rows, nc), lambda b, i: (i, 0, 0)),
                  _resident((nc, LANES)),
                  pl.BlockSpec((1, GROUP), lambda b, i: (0, 0))],
        out_specs=pl.BlockSpec((1, tq, GROUP), lambda b, i: (b, i, 0)),
        out_shape=jax.ShapeDtypeStruct((bsz, s_len, GROUP), _MXU),
        compiler_params=_params(("parallel", "arbitrary")),
        name="nsa_attn",
    )(q4, kc, cv, ks, vs, kw, vw, gates, win_tab, sel_tab, cmp_tab, ovl, on)


def _ffn_kernel(x_ref, mab_ref, mc_ref, md_ref, wo_ref, g_ref, wgu_ref, wd_ref, out_ref):
    x1 = (x_ref[...]
          + jnp.dot(mab_ref[...], wo_ref[0:2 * GROUP, :], preferred_element_type=_F32)
          + jnp.dot(mc_ref[...], wo_ref[2 * GROUP:3 * GROUP, :], preferred_element_type=_F32)
          + jnp.dot(md_ref[...], wo_ref[3 * GROUP:4 * GROUP, :], preferred_element_type=_F32))
    xn = (x1 * _rms_scale(x1) * g_ref[...]).astype(_MXU)
    out_ref[...] = x1
    for j in range(D_FF // TF_FFN):
        cols = slice(j * TF_FFN, (j + 1) * TF_FFN)
        gt = jnp.dot(xn, wgu_ref[:, cols], preferred_element_type=_F32)
        up = jnp.dot(xn, wgu_ref[:, D_FF + j * TF_FFN:D_FF + (j + 1) * TF_FFN], preferred_element_type=_F32)
        hid = (gt * jax.nn.sigmoid(gt)) * up
        out_ref[...] += jnp.dot(hid.astype(_MXU), wd_ref[cols, :], preferred_element_type=_F32)


def _out_ffn(x2d, mab, mc, md, wo, g, wgu, wd):
    t = x2d.shape[0]
    tm = min(TM_FFN, t)
    assert t % tm == 0 and D_FF % TF_FFN == 0
    row = lambda width: pl.BlockSpec((tm, width), lambda i: (i, 0))
    return pl.pallas_call(
        _ffn_kernel,
        grid=(t // tm,),
        in_specs=[row(D_MODEL), row(2 * GROUP), row(GROUP), row(GROUP),
                  _resident((D_MODEL, D_MODEL)), pl.BlockSpec((1, D_MODEL), lambda i: (0, 0)),
                  _resident((D_MODEL, 2 * D_FF)), _resident((D_FF, D_MODEL))],
        out_specs=row(D_MODEL),
        out_shape=jax.ShapeDtypeStruct((t, D_MODEL), _F32),
        compiler_params=_params(("parallel",)),
        name="out_ffn",
    )(x2d, mab, mc, md, wo, g, wgu, wd)


def _lane_vec(values, start):
    v = jnp.zeros((LANES,), _F32).at[start:start + values.shape[0]].set(values.astype(_F32))
    return v[None, :]


def _block_diag(w):
    h, d, _ = w.shape
    eye = jnp.eye(h, dtype=w.dtype)
    return (eye[:, None, :, None] * w[:, :, None, :]).reshape(h * d, h * d)


def _overlap_ext(s_len):
    nc = s_len // CMP_STRIDE
    n_cmp = nc - 1
    n_slc = s_len // SLC_BLOCK
    cs = np.arange(n_cmp)[:, None] * CMP_STRIDE
    ss = np.arange(n_slc)[None, :] * SLC_BLOCK
    ov = np.clip(np.minimum(cs + CMP_LEN, ss + SLC_BLOCK) - np.maximum(cs, ss), 0, CMP_LEN)
    ext = np.zeros((nc, LANES), np.float32)
    ext[:n_cmp, :n_slc] = ov
    return jnp.asarray(ext, _MXU)


def _layer_params(l, w_in, lru_w_gates, lru_b_gates, fox_f_bias, fox_qk_norm, nsa_qk_norm, nsa_cmp_pos,
                  nsa_cmp_w1, nsa_cmp_w2, nsa_gate_bias, out_norm):
    scale = HEAD_DIM ** -0.5 * LOG2E
    w = w_in[l]
    small = jnp.zeros((D_MODEL, LANES), w.dtype)
    small = small.at[:, 0:3 * N_HEADS].set(jnp.repeat(w[:, C_FOX_F:C_FOX_F + N_HEADS], 3, axis=1))
    small = small.at[:, SM_GATE_LANE:SM_GATE_LANE + 3 * N_HEADS].set(w[:, C_NSA_G:C_NSA_G + 3 * N_HEADS])
    c_fox_v = C_FOX_F - GROUP
    fox_v = jnp.pad(w[:, c_fox_v:C_FOX_F].reshape(D_MODEL, N_HEADS, HEAD_DIM),
                    ((0, 0), (0, 0), (0, LANES - HEAD_DIM))).reshape(D_MODEL, N_HEADS * LANES)
    w_perm = jnp.concatenate([w[:, :c_fox_v], fox_v, w[:, C_NSA_Q:C_NSA_G], small], axis=1).astype(_MXU)
    assert w_perm.shape[1] == N_IN_PAD

    wg = jnp.concatenate([_block_diag(lru_w_gates[l, 0]), _block_diag(lru_w_gates[l, 1])], axis=1).astype(_MXU)
    bg = lru_b_gates[l].reshape(1, 2 * GROUP)

    fb = _lane_vec(jnp.repeat(fox_f_bias[l], 3), 0)
    fox_gq = (jnp.tile(fox_qk_norm[l, 0], 2) * scale)[None, :]
    fox_gk = jnp.tile(fox_qk_norm[l, 1], 2)[None, :]

    nsa_gq = (jnp.tile(nsa_qk_norm[l, 0], 2) * scale)[None, :]
    nsa_gc = _lane_vec(nsa_qk_norm[l, 1], 0)
    nsa_gs = _lane_vec(nsa_qk_norm[l, 2], 0)
    nsa_gw = _lane_vec(nsa_qk_norm[l, 3], 0)
    gb = _lane_vec(nsa_gate_bias[l], SM_GATE_LANE)

    half = CMP_LEN // 2

    def pos_ext(lo):
        return jnp.concatenate([nsa_cmp_pos[l, 0, lo:lo + half], nsa_cmp_pos[l, 1, lo:lo + half]],
                               axis=1).reshape(1, half * LANES)

    def w1_ext(lo):
        ext = jnp.zeros((half, LANES, 2 * CMP_HIDDEN), _F32)
        ext = ext.at[:, :HEAD_DIM, :CMP_HIDDEN].set(nsa_cmp_w1[l, 0, lo:lo + half])
        ext = ext.at[:, HEAD_DIM:, CMP_HIDDEN:].set(nsa_cmp_w1[l, 1, lo:lo + half])
        return ext.reshape(half * LANES, 2 * CMP_HIDDEN).astype(_MXU)

    w2 = jnp.zeros((2 * CMP_HIDDEN, LANES), _F32)
    w2 = w2.at[:CMP_HIDDEN, :HEAD_DIM].set(nsa_cmp_w2[l, 0]).at[CMP_HIDDEN:, HEAD_DIM:].set(nsa_cmp_w2[l, 1])
    return dict(w_perm=w_perm, wg=wg, bg=bg, fb=fb, fox_gq=fox_gq, fox_gk=fox_gk, nsa_gq=nsa_gq, nsa_gc=nsa_gc,
                nsa_gs=nsa_gs, nsa_gw=nsa_gw, gb=gb, pa=pos_ext(0), pb=pos_ext(half), w1a=w1_ext(0),
                w1b=w1_ext(half), w2=w2.astype(_MXU), on=out_norm[l].reshape(1, 4 * GROUP))


def kernel(x, norm_mix, w_in, lru_conv_w, lru_conv_b, lru_w_gates, lru_b_gates, lru_lambda, sc_conv_w, fox_f_bias, fox_qk_norm, nsa_qk_norm, nsa_cmp_pos, nsa_cmp_w1, nsa_cmp_w2, nsa_gate_bias, rel_bias, out_norm, w_out, norm_ffn, w_gate_up, w_down):
    bsz, s_len, d_model = x.shape
    depth = w_in.shape[0]
    assert d_model == D_MODEL and s_len % TS_MIX == 0 and s_len % TQ_FOX == 0
    assert SLC_TOPK <= s_len // SLC_BLOCK <= SLC_BLOCK
    t = bsz * s_len

    win_tab, sel_tab, cmp_tab = _nsa_tables(rel_bias, s_len)
    ovl = _overlap_ext(s_len)

    for l in range(depth):
        lp = _layer_params(l, w_in, lru_w_gates, lru_b_gates, fox_f_bias, fox_qk_norm, nsa_qk_norm,
                           nsa_cmp_pos, nsa_cmp_w1, nsa_cmp_w2, nsa_gate_bias, out_norm)
        m_ab, qf, kf, vf, q4, ks, vs, kw, vw, gates, zcmp = _proj_mix(
            x, norm_mix[l][None, :], lp["w_perm"], lru_conv_w[l], lru_conv_b[l][None, :], lp["wg"], lp["bg"],
            lru_lambda[l][None, :], sc_conv_w[l], lp["on"][:, 0:2 * GROUP],
            lp["fb"], lp["fox_gq"], lp["fox_gk"], lp["nsa_gq"], lp["nsa_gs"], lp["nsa_gw"], lp["gb"])
        m_c = _fox_attn(qf, kf, vf, lp["on"][:, 2 * GROUP:3 * GROUP])
        kc, cv = _compress(zcmp, lp["pa"], lp["pb"], lp["w1a"], lp["w1b"], lp["w2"], lp["nsa_gc"])
        m_d = _nsa_attn(q4, kc, cv, ks, vs, kw, vw, gates, win_tab, sel_tab, cmp_tab, ovl,
                        lp["on"][:, 3 * GROUP:4 * GROUP])
        x = _out_ffn(x.reshape(t, D_MODEL), m_ab.reshape(t, 2 * GROUP), m_c.reshape(t, GROUP),
                     m_d.reshape(t, GROUP), w_out[l].astype(_MXU), norm_ffn[l][None, :],
                     w_gate_up[l].astype(_MXU), w_down[l].astype(_MXU)).reshape(bsz, s_len, D_MODEL)
    return x
```

```python
import math

import numpy as np
import jax
import jax.numpy as jnp
from jax import lax
from jax.experimental import pallas as pl
from jax.experimental.pallas import tpu as pltpu

D_MODEL = 1024
GROUP = 256
HEAD_DIM = 64
N_HEADS = 4
LRU_CONV = 4
LRU_C = 8.0
SC_CONV = 3
CMP_LEN = 32
CMP_STRIDE = 16
CMP_HIDDEN = 128
SLC_BLOCK = 64
SLC_SHIFT = SLC_BLOCK.bit_length() - 1
SLC_TOPK = 16
WINDOW = 512
REL_BUCKETS = 32
REL_MAX_DIST = 128
D_FF = 2816
RMS_EPS = 1e-6
NEG = -1e30
BIG = 1e30
LOG2E = math.log2(math.e)

C_FOX_F = 8 * GROUP
C_NSA_Q = C_FOX_F + N_HEADS
C_NSA_G = C_NSA_Q + GROUP + 6 * HEAD_DIM

LANES = 128
SUBLANES = 8
VMEM_LIMIT_BYTES = 56 * 1024 * 1024

TS_MIX = 512
SCAN_SEG = 32
TQ_FOX = 1024
KT_NSA = 128
TQ_NSA = 256
NSA_FAR = 4
TM_FFN = 1024
TF_FFN = 256

SM_GATE_LANE = 16
SUM_LANE = 0

COL_FOX = 5 * GROUP
COL_NSA_Q = COL_FOX + 2 * GROUP + N_HEADS * LANES
COL_SLABS = COL_NSA_Q + GROUP
N_IN_PAD = COL_SLABS + 4 * LANES

_MXU = jnp.bfloat16
_F32 = jnp.float32


def _params(sem):
    return pltpu.CompilerParams(dimension_semantics=sem, vmem_limit_bytes=VMEM_LIMIT_BYTES)


def _resident(shape):
    zeros = (0,) * len(shape)
    return pl.BlockSpec(shape, lambda *_: zeros, pipeline_mode=pl.Buffered(1))


def _mm(a, b):
    return jnp.dot(a.astype(_MXU), b.astype(_MXU), preferred_element_type=_F32)


def _mm_nt(a, b):
    return lax.dot_general(a.astype(_MXU), b.astype(_MXU), (((1,), (1,)), ((), ())),
                           preferred_element_type=_F32)


def _lane(shape):
    return lax.broadcasted_iota(jnp.int32, shape, len(shape) - 1)


def _row(shape):
    return lax.broadcasted_iota(jnp.int32, shape, 0)


def _rms_scale(x):
    return lax.rsqrt(jnp.mean(x * x, axis=-1, keepdims=True) + RMS_EPS)


def _gelu(x):
    c = math.sqrt(2.0 / math.pi)
    return x * (0.5 * (1.0 + jnp.tanh(c * (x + 0.044715 * (x * x * x)))))


def _round_mxu(x):
    return x.astype(_MXU).astype(_F32)


def _low_half_rms_scale(x, low):
    s = jnp.sum(jnp.where(low, x * x, 0.0), axis=-1, keepdims=True)
    return lax.rsqrt(s * (1.0 / HEAD_DIM) + RMS_EPS)


def _half_rms_scale(x, low):
    x2 = x * x
    s_lo = jnp.sum(jnp.where(low, x2, 0.0), axis=-1, keepdims=True)
    s_hi = jnp.sum(jnp.where(low, 0.0, x2), axis=-1, keepdims=True)
    return jnp.where(low, lax.rsqrt(s_lo * (1.0 / HEAD_DIM) + RMS_EPS),
                     lax.rsqrt(s_hi * (1.0 / HEAD_DIM) + RMS_EPS))


def _linear_scan_rows(a, b, h_in):
    n = a.shape[0]
    pos = _row(a.shape) & (SCAN_SEG - 1)
    d = 1
    while d < SCAN_SEG:
        keep = pos >= d
        a_prev = jnp.where(keep, pltpu.roll(a, d, axis=0), 1.0)
        b_prev = jnp.where(keep, pltpu.roll(b, d, axis=0), 0.0)
        b = a * b_prev + b
        a = a * a_prev
        d *= 2
    segs = []
    for r0 in range(0, n, SCAN_SEG):
        seg = b[r0:r0 + SCAN_SEG] + a[r0:r0 + SCAN_SEG] * h_in
        h_in = seg[SCAN_SEG - 1:SCAN_SEG]
        segs.append(seg)
    return jnp.concatenate(segs, axis=0)


def _cumsum_rows(x, c_in):
    n = x.shape[0]
    pos = _row(x.shape) & (SCAN_SEG - 1)
    d = 1
    while d < SCAN_SEG:
        x = x + jnp.where(pos >= d, pltpu.roll(x, d, axis=0), 0.0)
        d *= 2
    segs = []
    for r0 in range(0, n, SCAN_SEG):
        seg = x[r0:r0 + SCAN_SEG] + c_in
        c_in = seg[SCAN_SEG - 1:SCAN_SEG]
        segs.append(seg)
    return jnp.concatenate(segs, axis=0)


def _mix_ab(zab, cw_ref, cb_ref, wg_ref, bg_ref, lam_ref, scw_ref, on_ref, out_ref, xext_ref, uext_ref, h_ref):
    ts = zab.shape[0]
    xr = zab[:, 0:GROUP]
    gate = zab[:, GROUP:2 * GROUP]
    bgt = zab[:, 2 * GROUP:3 * GROUP]
    cgt = zab[:, 3 * GROUP:4 * GROUP]
    xs = zab[:, 4 * GROUP:5 * GROUP]

    xext_ref[SUBLANES:SUBLANES + ts, :] = xr
    xc = cb_ref[...] + cw_ref[LRU_CONV - 1:LRU_CONV, :] * xr
    for k in range(LRU_CONV - 1):
        xc = xc + cw_ref[k:k + 1, :] * xext_ref[pl.ds(SUBLANES - (LRU_CONV - 1) + k, ts), :]
    xext_ref[0:SUBLANES, :] = xr[ts - SUBLANES:ts, :]

    gi = _mm(xc, wg_ref[...]) + bg_ref[...]
    r = jax.nn.sigmoid(gi[:, 0:GROUP])
    ig = jax.nn.sigmoid(gi[:, GROUP:2 * GROUP])
    lam = lam_ref[...]
    softplus_neg = jnp.maximum(-lam, 0.0) + jnp.log1p(jnp.exp(-jnp.abs(lam)))
    log_a = (-LRU_C * softplus_neg) * r
    a = jnp.exp(log_a)
    g = -jnp.tanh(log_a) * (a * a + 1.0)
    b = jnp.where(g > 0.0, g * lax.rsqrt(g), 0.0) * (ig * xc)
    h = _linear_scan_rows(a, b, h_ref[0:1, :])
    h_ref[...] = jnp.broadcast_to(h[ts - 1:ts, :], h_ref.shape)
    y_a = h * _gelu(gate)

    u = cgt * xs
    uext_ref[SUBLANES:SUBLANES + ts, :] = u
    cv = scw_ref[SC_CONV - 1:SC_CONV, :] * u
    for k in range(SC_CONV - 1):
        cv = cv + scw_ref[k:k + 1, :] * uext_ref[pl.ds(SUBLANES - (SC_CONV - 1) + k, ts), :]
    uext_ref[0:SUBLANES, :] = u[ts - SUBLANES:ts, :]
    y_b = bgt * cv

    out_ref[0, :, 0:GROUP] = (y_a * _rms_scale(y_a) * on_ref[:, 0:GROUP]).astype(out_ref.dtype)
    out_ref[0, :, GROUP:2 * GROUP] = (y_b * _rms_scale(y_b) * on_ref[:, GROUP:2 * GROUP]).astype(out_ref.dtype)


def _fox_prep(zfox, zsm, fb_ref, gq_ref, gk_ref, q_ref, k_ref, v_ref, c_ref):
    ts = zfox.shape[0]
    shape = (ts, LANES)
    lane = _lane(shape)
    low = lane < HEAD_DIM

    f = zsm + fb_ref[...]
    logf = jnp.minimum(f, 0.0) - jnp.log1p(jnp.exp(-jnp.abs(f)))
    c = _cumsum_rows(logf, c_ref[0:1, :])
    c_ref[...] = jnp.broadcast_to(c[ts - 1:ts, :], c_ref.shape)
    neg_c = -LOG2E * c
    p1 = _round_mxu(neg_c)
    r1 = neg_c - p1
    p2 = _round_mxu(r1)
    p3 = _round_mxu(r1 - p2)
    piece = lane - 3 * ((lane * 11) >> 5)
    csel = jnp.where(piece == 0, p1, jnp.where(piece == 1, p2, p3))

    ones_aug = jnp.where(lane < HEAD_DIM + 3, 1.0, 0.0)
    for p in range(N_HEADS // 2):
        qs = zfox[:, LANES * p:LANES * (p + 1)]
        ks = zfox[:, GROUP + LANES * p:GROUP + LANES * (p + 1)]
        qn = qs * _half_rms_scale(qs, low) * gq_ref[...]
        kn = ks * _half_rms_scale(ks, low) * gk_ref[...]
        for half in range(2):
            h = 2 * p + half
            qh = qn if half == 0 else pltpu.roll(qn, HEAD_DIM, axis=1)
            kh = kn if half == 0 else pltpu.roll(kn, HEAD_DIM, axis=1)
            ch = pltpu.roll(csel, HEAD_DIM - 3 * h, axis=1)
            q_ref[0, h] = jnp.where(low, qh, ones_aug).astype(q_ref.dtype)
            k_ref[0, h] = jnp.where(low, kh, jnp.where(lane < HEAD_DIM + 3, ch, 0.0)).astype(k_ref.dtype)
    for h in range(N_HEADS):
        vh = zfox[:, 2 * GROUP + LANES * h:2 * GROUP + LANES * (h + 1)]
        v_ref[0, h] = jnp.where(low, vh, 1.0).astype(v_ref.dtype)


def _nsa_prep(zq, zsel, zwin, zsm, row0, gq_ref, gs_ref, gw_ref, gb_ref,
              q4_ref, ks_ref, vs_ref, kw_ref, vw_ref, gate_ref):
    ts = zq.shape[0]
    shape = (ts, LANES)
    lane = _lane(shape)
    low = lane < HEAD_DIM
    for p in range(N_HEADS // 2):
        qs = zq[:, LANES * p:LANES * (p + 1)]
        qn = qs * _half_rms_scale(qs, low) * gq_ref[...]
        q4_ref[0, 2 * p] = jnp.where(low, qn, 0.0).astype(q4_ref.dtype)
        q4_ref[0, 2 * p + 1] = jnp.where(low, pltpu.roll(qn, HEAD_DIM, axis=1), 0.0).astype(q4_ref.dtype)

    blk = (row0 + _row(shape)) >> SLC_SHIFT
    onehot = jnp.where(lane - HEAD_DIM == blk, 1.0, 0.0)
    ks_ref[0] = jnp.where(low, zsel * _low_half_rms_scale(zsel, low) * gs_ref[...], onehot).astype(ks_ref.dtype)
    vs_ref[0] = jnp.where(lane == SUM_LANE, 1.0, zsel).astype(vs_ref.dtype)
    kw_ref[0] = jnp.where(low, zwin * _low_half_rms_scale(zwin, low) * gw_ref[...], 0.0).astype(kw_ref.dtype)
    vw_ref[0] = jnp.where(lane == SUM_LANE, 1.0, zwin).astype(vw_ref.dtype)
    gate_ref[0] = jax.nn.sigmoid(zsm + gb_ref[...])


def _proj_mix_kernel(x_ref, g_ref, w_ref,
                     cw_ref, cb_ref, wg_ref, bg_ref, lam_ref, scw_ref, on_ref,
                     fb_ref, fgq_ref, fgk_ref, ngq_ref, ngs_ref, ngw_ref, gb_ref,
                     mab_ref, qf_ref, kf_ref, vf_ref, q4_ref, ks_ref, vs_ref, kw_ref, vw_ref, gate_ref, zcmp_ref,
                     xext_ref, uext_ref, h_ref, c_ref, zc_ref):
    ts = x_ref.shape[1]
    ti = pl.program_id(1)

    @pl.when(ti == 0)
    def _():
        xext_ref[0:SUBLANES, :] = jnp.zeros((SUBLANES, GROUP), _F32)
        uext_ref[0:SUBLANES, :] = jnp.zeros((SUBLANES, GROUP), _F32)
        h_ref[...] = jnp.zeros_like(h_ref)
        c_ref[...] = jnp.zeros_like(c_ref)

    x = x_ref[0]
    xn = (x * _rms_scale(x) * g_ref[...]).astype(_MXU)
    zab = jnp.dot(xn, w_ref[:, 0:COL_FOX], preferred_element_type=_F32)
    slabs = jnp.dot(xn, w_ref[:, COL_SLABS:N_IN_PAD], preferred_element_type=_F32)
    zfox = jnp.dot(xn, w_ref[:, COL_FOX:COL_NSA_Q], preferred_element_type=_F32)
    zq = jnp.dot(xn, w_ref[:, COL_NSA_Q:COL_SLABS], preferred_element_type=_F32)
    zc_ref[...] = slabs[:, 0:LANES]
    for tok in range(CMP_STRIDE):
        zcmp_ref[0, :, tok * LANES:(tok + 1) * LANES] = zc_ref[pl.ds(tok, ts // CMP_STRIDE, stride=CMP_STRIDE), :]
    zsel = slabs[:, LANES:2 * LANES]
    zwin = slabs[:, 2 * LANES:3 * LANES]
    zsm = slabs[:, 3 * LANES:4 * LANES]
    _fox_prep(zfox, zsm, fb_ref, fgq_ref, fgk_ref, qf_ref, kf_ref, vf_ref, c_ref)
    _nsa_prep(zq, zsel, zwin, zsm, ti * ts, ngq_ref, ngs_ref, ngw_ref, gb_ref,
              q4_ref, ks_ref, vs_ref, kw_ref, vw_ref, gate_ref)
    _mix_ab(zab, cw_ref, cb_ref, wg_ref, bg_ref, lam_ref, scw_ref, on_ref, mab_ref, xext_ref, uext_ref, h_ref)


def _proj_mix(x, gain, w, cw, cb, wg, bg, lam, scw, on_ab, fb, fgq, fgk, ngq, ngs, ngw, gb):
    bsz, s_len, _ = x.shape
    ts = TS_MIX
    vec = lambda width: pl.BlockSpec((1, width), lambda b, t: (0, 0))
    rows = lambda width: pl.BlockSpec((1, ts, width), lambda b, t: (b, t, 0))
    heads = pl.BlockSpec((1, N_HEADS, ts, LANES), lambda b, t: (b, 0, t, 0))
    sds = jax.ShapeDtypeStruct
    return pl.pallas_call(
        _proj_mix_kernel,
        grid=(bsz, s_len // ts),
        in_specs=[rows(D_MODEL), vec(D_MODEL), _resident((D_MODEL, N_IN_PAD)),
                  pl.BlockSpec((LRU_CONV, GROUP), lambda b, t: (0, 0)), vec(GROUP), _resident((GROUP, 2 * GROUP)),
                  vec(2 * GROUP), vec(GROUP), pl.BlockSpec((SC_CONV, GROUP), lambda b, t: (0, 0)), vec(2 * GROUP),
                  vec(LANES), vec(LANES), vec(LANES), vec(LANES), vec(LANES), vec(LANES), vec(LANES)],
        out_specs=[rows(2 * GROUP), heads, heads, heads, heads, rows(LANES), rows(LANES),
                   rows(LANES), rows(LANES), rows(LANES),
                   pl.BlockSpec((1, ts // CMP_STRIDE, CMP_STRIDE * LANES), lambda b, t: (b, t, 0))],
        out_shape=[sds((bsz, s_len, 2 * GROUP), _MXU),
                   sds((bsz, N_HEADS, s_len, LANES), _MXU), sds((bsz, N_HEADS, s_len, LANES), _MXU),
                   sds((bsz, N_HEADS, s_len, LANES), _MXU),
                   sds((bsz, N_HEADS, s_len, LANES), _MXU),
                   sds((bsz, s_len, LANES), _MXU), sds((bsz, s_len, LANES), _MXU),
                   sds((bsz, s_len, LANES), _MXU), sds((bsz, s_len, LANES), _MXU),
                   sds((bsz, s_len, LANES), _F32), sds((bsz, s_len // CMP_STRIDE, CMP_STRIDE * LANES), _F32)],
        scratch_shapes=[pltpu.VMEM((ts + SUBLANES, GROUP), _F32), pltpu.VMEM((ts + SUBLANES, GROUP), _F32),
                        pltpu.VMEM((SUBLANES, GROUP), _F32), pltpu.VMEM((SUBLANES, LANES), _F32),
                        pltpu.VMEM((ts, LANES), _F32)],
        compiler_params=_params(("parallel", "arbitrary")),
        name="proj_mix",
    )(x, gain, w, cw, cb, wg, bg, lam, scw, on_ab, fb, fgq, fgk, ngq, ngs, ngw, gb)


def _flash_step_sumlane(s, v, carry):
    m, acc = carry
    m_new = jnp.maximum(m, jnp.max(s, axis=-1, keepdims=True))
    p = jnp.exp2(s - m_new).astype(_MXU)
    pv = jnp.dot(p, v, preferred_element_type=_F32)
    return m_new, jnp.exp2(m - m_new) * acc + pv


def _flash_init_sumlane(rows):
    return (jnp.full((rows, 1), NEG, _F32), jnp.zeros((rows, LANES), _F32))


def _sumlane_normalize(acc, sum_lane):
    return acc / acc[:, sum_lane:sum_lane + 1]


def _fox_attn_kernel(q_ref, k_ref, v_ref, on_ref, out_ref):
    tq = q_ref.shape[2]
    qi = pl.program_id(1)

    def step(c, carries, causal=False):
        start = pl.multiple_of(c * tq, tq)
        new = []
        for h in range(N_HEADS):
            s = _mm_nt(q_ref[0, h], k_ref[0, h, pl.ds(start, tq), :])
            if causal:
                s = jnp.where(_lane((tq, tq)) <= _row((tq, tq)), s, NEG)
            new.append(_flash_step_sumlane(s, v_ref[0, h, pl.ds(start, tq), :], carries[h]))
        return tuple(new)

    carries = lax.fori_loop(0, qi, step, tuple(_flash_init_sumlane(tq) for _ in range(N_HEADS)))
    carries = step(qi, carries, causal=True)
    low = _lane((tq, LANES)) < HEAD_DIM
    pairs = []
    for p in range(N_HEADS // 2):
        a_even, a_odd = carries[2 * p][1], carries[2 * p + 1][1]
        r_even, r_odd = pltpu.roll(a_even, HEAD_DIM, axis=1), pltpu.roll(a_odd, HEAD_DIM, axis=1)
        pairs.append(jnp.where(low, a_even, r_odd) / jnp.where(low, r_even, a_odd))
    y = jnp.concatenate(pairs, axis=1)
    out_ref[0] = (y * _rms_scale(y) * on_ref[...]).astype(out_ref.dtype)


def _fox_attn(qf, kf, vf, on):
    bsz, _, s_len, _ = qf.shape
    tq = min(TQ_FOX, s_len)
    assert s_len % tq == 0
    return pl.pallas_call(
        _fox_attn_kernel,
        grid=(bsz, s_len // tq),
        in_specs=[pl.BlockSpec((1, N_HEADS, tq, LANES), lambda b, i: (b, 0, i, 0)),
                  pl.BlockSpec((1, N_HEADS, s_len, LANES), lambda b, i: (b, 0, 0, 0)),
                  pl.BlockSpec((1, N_HEADS, s_len, LANES), lambda b, i: (b, 0, 0, 0)),
                  pl.BlockSpec((1, GROUP), lambda b, i: (0, 0))],
        out_specs=pl.BlockSpec((1, tq, GROUP), lambda b, i: (b, i, 0)),
        out_shape=jax.ShapeDtypeStruct((bsz, s_len, GROUP), _MXU),
        compiler_params=_params(("parallel", "arbitrary")),
        name="fox_attn",
    )(qf, kf, vf, on)


def _compress_kernel(x_ref, pa_ref, pb_ref, w1a_ref, w1b_ref, w2_ref, g_ref, kc_ref, cv_ref):
    nc = x_ref.shape[1]
    x = x_ref[0]
    ua = _mm(x + pa_ref[...], w1a_ref[...])
    ub = _mm(x + pb_ref[...], w1b_ref[...])
    hid = _gelu(ua + pltpu.roll(ub, nc - 1, axis=0))
    out = _mm(hid, w2_ref[...])
    low = _lane(out.shape) < HEAD_DIM
    kc_ref[0] = jnp.where(low, out * _low_half_rms_scale(out, low) * g_ref[...], 0.0).astype(kc_ref.dtype)
    cv_ref[0] = out.astype(cv_ref.dtype)


def _compress(xc, pa, pb, w1a, w1b, w2, g):
    bsz, nc, width = xc.shape
    const = lambda shape: pl.BlockSpec(shape, lambda b: (0, 0))
    slab = pl.BlockSpec((1, nc, LANES), lambda b: (b, 0, 0))
    return pl.pallas_call(
        _compress_kernel,
        grid=(bsz,),
        in_specs=[pl.BlockSpec((1, nc, width), lambda b: (b, 0, 0)),
                  const((1, width)), const((1, width)), const((width, 2 * CMP_HIDDEN)),
                  const((width, 2 * CMP_HIDDEN)), const((2 * CMP_HIDDEN, LANES)), const((1, LANES))],
        out_specs=[slab, slab],
        out_shape=[jax.ShapeDtypeStruct((bsz, nc, LANES), _MXU)] * 2,
        compiler_params=_params(("parallel",)),
        name="compress",
    )(xc, pa, pb, w1a, w1b, w2, g)


def _bucket_thresholds():
    max_exact = REL_BUCKETS // 2
    d = np.arange(0, REL_MAX_DIST + 1)
    large = max_exact + (np.log(np.maximum(d, 1).astype(np.float32) / max_exact)
                         / math.log(REL_MAX_DIST / max_exact) * (REL_BUCKETS - max_exact)).astype(np.int32)
    bucket = np.where(d < max_exact, d, np.minimum(large, REL_BUCKETS - 1))
    assert bucket[-1] == REL_BUCKETS - 1 and np.all(np.diff(bucket) >= 0)
    return [int(np.argmax(bucket >= k)) for k in range(REL_BUCKETS)]


_BUCKET_THR = _bucket_thresholds()

NSA_R = TQ_NSA // KT_NSA
NSA_D = WINDOW // KT_NSA
NSA_TAIL = NSA_D + NSA_R
WIN_ENTRIES = NSA_TAIL + 1
SEL_ENTRIES = NSA_R + 3


def _win_entry(e):
    return jnp.minimum(e, NSA_TAIL)


def _sel_entry(e):
    return jnp.clip(e - (NSA_D - 2), 0, NSA_R + 2)


def _rel_bias(dist, rb_ref, h):
    far = rb_ref[REL_BUCKETS - 1, h]
    val = jnp.full(dist.shape, LOG2E * (rb_ref[0, h] - far), _F32)
    for k in range(1, REL_BUCKETS):
        val = jnp.where(dist >= _BUCKET_THR[k], LOG2E * (rb_ref[k, h] - far), val)
    return val


def _nsa_tables_kernel(rb_ref, win_ref, sel_ref, cmp_ref):
    qi = pl.program_id(0)
    tq = TQ_NSA
    nc = cmp_ref.shape[2]

    @pl.when(qi == 0)
    def _():
        i = _row((tq, KT_NSA))
        c = _lane((tq, KT_NSA))
        for h in range(N_HEADS):
            rs = slice(h * tq, (h + 1) * tq)
            for e in range(NSA_TAIL):
                dist = i + (NSA_D - e) * KT_NSA - c
                ok = jnp.where(dist >= 0, jnp.where(dist < WINDOW, 1, 0), 0) > 0
                tab = jnp.where(ok, _rel_bias(dist, rb_ref, h), NEG)
                win_ref[e, rs, :] = tab
                if e >= NSA_D - 1:
                    sel_ref[e - (NSA_D - 2), rs, :] = tab
            win_ref[NSA_TAIL, rs, :] = jnp.full((tq, KT_NSA), NEG, _F32)
            sel_ref[0, rs, :] = jnp.zeros((tq, KT_NSA), _F32)
            sel_ref[NSA_R + 2, rs, :] = jnp.full((tq, KT_NSA), NEG, _F32)

    t = qi * tq + _row((tq, nc))
    dist = t - (CMP_STRIDE * _lane((tq, nc)) + CMP_LEN - 1)
    for h in range(N_HEADS):
        cmp_ref[0, h * tq:(h + 1) * tq, :] = jnp.where(dist >= 0, _rel_bias(dist, rb_ref, h), NEG)


def _nsa_tables(rel_bias, s_len):
    nq = s_len // TQ_NSA
    nc = s_len // CMP_STRIDE
    rows = N_HEADS * TQ_NSA
    return pl.pallas_call(
        _nsa_tables_kernel,
        grid=(nq,),
        in_specs=[pl.BlockSpec(memory_space=pltpu.SMEM)],
        out_specs=[pl.BlockSpec((WIN_ENTRIES, rows, KT_NSA), lambda i: (0, 0, 0)),
                   pl.BlockSpec((SEL_ENTRIES, rows, KT_NSA), lambda i: (0, 0, 0)),
                   pl.BlockSpec((1, rows, nc), lambda i: (i, 0, 0))],
        out_shape=[jax.ShapeDtypeStruct((WIN_ENTRIES, rows, KT_NSA), _F32),
                   jax.ShapeDtypeStruct((SEL_ENTRIES, rows, KT_NSA), _F32),
                   jax.ShapeDtypeStruct((nq, rows, nc), _F32)],
        compiler_params=_params(("arbitrary",)),
        name="nsa_tables",
    )(rel_bias)


def _topk_penalty(imp_t, qi, tq):
    n_blk = imp_t.shape[0]
    blk = _row((n_blk, tq))
    cur = (qi * tq + _lane((n_blk, tq))) >> SLC_SHIFT
    forced = jnp.where(blk == 0, 1, jnp.where(blk == cur, 1, jnp.where(blk == cur - 1, 1, 0))) > 0
    val = jnp.where(blk <= cur, jnp.where(forced, BIG, imp_t), NEG)
    groups = [val[g * SUBLANES:(g + 1) * SUBLANES, :] for g in range(n_blk // SUBLANES)]
    sub = _row((SUBLANES, tq))
    ranks = [jnp.zeros((SUBLANES, tq), _F32) for _ in groups]
    for j in range(n_blk):
        vj = val[j:j + 1, :]
        jg, jr = divmod(j, SUBLANES)
        for g, vg in enumerate(groups):
            ge = jnp.where(vj >= vg, 1.0, 0.0)
            gt = jnp.where(vj > vg, 1.0, 0.0)
            if g > jg:
                ranks[g] = ranks[g] + ge
            elif g < jg:
                ranks[g] = ranks[g] + gt
            else:
                ranks[g] = ranks[g] + jnp.where(sub > jr, ge, gt)
    rank = jnp.concatenate(ranks, axis=0)
    return jnp.where(rank < float(SLC_TOPK), 0.0, NEG)


def _nsa_attn_kernel(q4_ref, kc_ref, cv_ref, ks_ref, vs_ref, kw_ref, vw_ref, gate_ref,
                     win_ref, sel_ref, cmp_ref, ovl_ref, on_ref, out_ref):
    tq = TQ_NSA
    kt = KT_NSA
    rows = N_HEADS * tq
    qi = pl.program_id(1)
    first = qi * NSA_R
    q4 = q4_ref[0].reshape(rows, LANES)
    lane = _lane((tq, LANES))
    low = lane < HEAD_DIM
    tail_w = NSA_TAIL * kt

    def tail_table(tab_ref, entry_of, first_tile, skip_before=0):
        pieces = []
        for u in range(NSA_TAIL):
            tile = first_tile + u
            entry = entry_of(tile - first + NSA_D)
            pieces.append(tab_ref[jnp.where(tile < skip_before, tab_ref.shape[0] - 1, entry)])
        return jnp.concatenate(pieces, axis=1)

    s = _mm_nt(q4, kc_ref[0]) + cmp_ref[0]
    valid = s > 0.5 * NEG
    m = jnp.max(s, axis=-1, keepdims=True)
    p = jnp.where(valid, jnp.exp2(s - m), 0.0)
    l = jnp.sum(p, axis=-1, keepdims=True)
    p = p / jnp.where(l > 0.0, l, 1.0)
    o_c = jnp.dot(p.astype(_MXU), cv_ref[0], preferred_element_type=_F32)

    psum = p[0:tq] + p[tq:2 * tq] + p[2 * tq:3 * tq] + p[3 * tq:4 * tq]
    p_hi = _round_mxu(psum)
    imp = _mm(p_hi, ovl_ref[...]) + _mm(psum - p_hi, ovl_ref[...])

    pen_t = _topk_penalty(imp.T[0:SLC_BLOCK, :], qi, tq)
    pen = jnp.concatenate([jnp.zeros((LANES - SLC_BLOCK, tq), _F32), pen_t], axis=0).T
    pen4 = jnp.concatenate([pen] * N_HEADS, axis=0)
    q_aug = jnp.where(_lane((rows, LANES)) < HEAD_DIM, q4.astype(_F32), pen4).astype(_MXU)

    w_tile = jnp.maximum(first - NSA_D, 0)
    w_start = pl.multiple_of(w_tile * kt, kt)
    s_w = _mm_nt(q4, kw_ref[0, pl.ds(w_start, tail_w), :]) + tail_table(win_ref, _win_entry, w_tile)
    m_w = jnp.max(s_w, axis=-1, keepdims=True)

    def sel_far(width):
        def step(c, carry):
            start = pl.multiple_of(c * width, width)
            s = _mm_nt(q_aug, ks_ref[0, pl.ds(start, width), :])
            return _flash_step_sumlane(s, vs_ref[0, pl.ds(start, width), :], carry)
        return step

    far_w = NSA_FAR * kt
    n_far = jnp.maximum(first - 1, 0) // NSA_FAR
    n_dbl = n_far // 2
    carry = lax.fori_loop(0, n_dbl, sel_far(2 * far_w), _flash_init_sumlane(rows))
    carry = lax.fori_loop(2 * n_dbl, n_far, sel_far(far_w), carry)
    done = n_far * NSA_FAR
    t_tile = jnp.minimum(done, ks_ref.shape[1] // kt - NSA_TAIL)
    t_start = pl.multiple_of(t_tile * kt, kt)
    s = _mm_nt(q_aug, ks_ref[0, pl.ds(t_start, tail_w), :]) + tail_table(sel_ref, _sel_entry, t_tile, done)
    acc_w = jnp.dot(jnp.exp2(s_w - m_w).astype(_MXU), vw_ref[0, pl.ds(w_start, tail_w), :],
                    preferred_element_type=_F32)
    o_w = _sumlane_normalize(acc_w, SUM_LANE)

    g = gate_ref[0]
    gate = lambda branch, h: g[:, SM_GATE_LANE + branch * N_HEADS + h:SM_GATE_LANE + branch * N_HEADS + h + 1]
    head_rows = [slice(h * tq, (h + 1) * tq) for h in range(N_HEADS)]
    partial = [gate(0, h) * o_c[rs] + gate(2, h) * o_w[rs] for h, rs in enumerate(head_rows)]
    _, acc_s = _flash_step_sumlane(s, vs_ref[0, pl.ds(t_start, tail_w), :], carry)
    o_s = _sumlane_normalize(acc_s, SUM_LANE)

    heads = [partial[h] + gate(1, h) * o_s[rs] for h, rs in enumerate(head_rows)]
    slabs = [jnp.where(low, pltpu.roll(heads[2 * p], HEAD_DIM, axis=1), heads[2 * p + 1])
             for p in range(N_HEADS // 2)]
    y = jnp.concatenate(slabs, axis=1)
    out_ref[0] = (y * _rms_scale(y) * on_ref[...]).astype(out_ref.dtype)


def _nsa_attn(q4, kc, cv, ks, vs, kw, vw, gates, win_tab, sel_tab, cmp_tab, ovl, on):
    bsz, _, s_len, _ = q4.shape
    tq = TQ_NSA
    nc = kc.shape[1]
    rows = N_HEADS * tq
    assert s_len >= NSA_TAIL * KT_NSA
    full = lambda n: pl.BlockSpec((1, n, LANES), lambda b, i: (b, 0, 0))
    return pl.pallas_call(
        _nsa_attn_kernel,
        grid=(bsz, s_len // tq),
        in_specs=[pl.BlockSpec((1, N_HEADS, tq, LANES), lambda b, i: (b, 0, i, 0)),
                  full(nc), full(nc), full(s_len), full(s_len), full(s_len), full(s_len),
                  pl.BlockSpec((1, tq, LANES), lambda b, i: (b, i, 0)),
                  _resident((WIN_ENTRIES, rows, KT_NSA)), _resident((SEL_ENTRIES, rows, KT_NSA)),
                  pl.BlockSpec((1, rows, nc), lambda b, i: (i, 0, 0)),
                  _resident((nc, LANES)),
                  pl.BlockSpec((1, GROUP), lambda b, i: (0, 0))],
        out_specs=pl.BlockSpec((1, tq, GROUP), lambda b, i: (b, i, 0)),
        out_shape=jax.ShapeDtypeStruct((bsz, s_len, GROUP), _MXU),
        compiler_params=_params(("parallel", "arbitrary")),
        name="nsa_attn",
    )(q4, kc, cv, ks, vs, kw, vw, gates, win_tab, sel_tab, cmp_tab, ovl, on)


def _ffn_kernel(x_ref, mab_ref, mc_ref, md_ref, wo_ref, g_ref, wgu_ref, wd_ref, out_ref):
    x1 = (x_ref[...]
          + jnp.dot(mab_ref[...], wo_ref[0:2 * GROUP, :], preferred_element_type=_F32)
          + jnp.dot(mc_ref[...], wo_ref[2 * GROUP:3 * GROUP, :], preferred_element_type=_F32)
          + jnp.dot(md_ref[...], wo_ref[3 * GROUP:4 * GROUP, :], preferred_element_type=_F32))
    xn = (x1 * _rms_scale(x1) * g_ref[...]).astype(_MXU)
    out_ref[...] = x1
    for j in range(D_FF // TF_FFN):
        cols = slice(j * TF_FFN, (j + 1) * TF_FFN)
        gt = jnp.dot(xn, wgu_ref[:, cols], preferred_element_type=_F32)
        up = jnp.dot(xn, wgu_ref[:, D_FF + j * TF_FFN:D_FF + (j + 1) * TF_FFN], preferred_element_type=_F32)
        hid = (gt * jax.nn.sigmoid(gt)) * up
        out_ref[...] += jnp.dot(hid.astype(_MXU), wd_ref[cols, :], preferred_element_type=_F32)


def _out_ffn(x2d, mab, mc, md, wo, g, wgu, wd):
    t = x2d.shape[0]
    tm = min(TM_FFN, t)
    assert t % tm == 0 and D_FF % TF_FFN == 0
    row = lambda width: pl.BlockSpec((tm, width), lambda i: (i, 0))
    return pl.pallas_call(
        _ffn_kernel,
        grid=(t // tm,),
        in_specs=[row(D_MODEL), row(2 * GROUP), row(GROUP), row(GROUP),
                  _resident((D_MODEL, D_MODEL)), pl.BlockSpec((1, D_MODEL), lambda i: (0, 0)),
                  _resident((D_MODEL, 2 * D_FF)), _resident((D_FF, D_MODEL))],
        out_specs=row(D_MODEL),
        out_shape=jax.ShapeDtypeStruct((t, D_MODEL), _F32),
        compiler_params=_params(("parallel",)),
        name="out_ffn",
    )(x2d, mab, mc, md, wo, g, wgu, wd)


def _lane_vec(values, start):
    v = jnp.zeros((LANES,), _F32).at[start:start + values.shape[0]].set(values.astype(_F32))
    return v[None, :]


def _block_diag(w):
    h, d, _ = w.shape
    eye = jnp.eye(h, dtype=w.dtype)
    return (eye[:, None, :, None] * w[:, :, None, :]).reshape(h * d, h * d)


def _overlap_ext(s_len):
    nc = s_len // CMP_STRIDE
    n_cmp = nc - 1
    n_slc = s_len // SLC_BLOCK
    cs = np.arange(n_cmp)[:, None] * CMP_STRIDE
    ss = np.arange(n_slc)[None, :] * SLC_BLOCK
    ov = np.clip(np.minimum(cs + CMP_LEN, ss + SLC_BLOCK) - np.maximum(cs, ss), 0, CMP_LEN)
    ext = np.zeros((nc, LANES), np.float32)
    ext[:n_cmp, :n_slc] = ov
    return jnp.asarray(ext, _MXU)


def _layer_params(l, w_in, lru_w_gates, lru_b_gates, fox_f_bias, fox_qk_norm, nsa_qk_norm, nsa_cmp_pos,
                  nsa_cmp_w1, nsa_cmp_w2, nsa_gate_bias, out_norm):
    scale = HEAD_DIM ** -0.5 * LOG2E
    w = w_in[l]
    small = jnp.zeros((D_MODEL, LANES), w.dtype)
    small = small.at[:, 0:3 * N_HEADS].set(jnp.repeat(w[:, C_FOX_F:C_FOX_F + N_HEADS], 3, axis=1))
    small = small.at[:, SM_GATE_LANE:SM_GATE_LANE + 3 * N_HEADS].set(w[:, C_NSA_G:C_NSA_G + 3 * N_HEADS])
    c_fox_v = C_FOX_F - GROUP
    fox_v = jnp.pad(w[:, c_fox_v:C_FOX_F].reshape(D_MODEL, N_HEADS, HEAD_DIM),
                    ((0, 0), (0, 0), (0, LANES - HEAD_DIM))).reshape(D_MODEL, N_HEADS * LANES)
    w_perm = jnp.concatenate([w[:, :c_fox_v], fox_v, w[:, C_NSA_Q:C_NSA_G], small], axis=1).astype(_MXU)
    assert w_perm.shape[1] == N_IN_PAD

    wg = jnp.concatenate([_block_diag(lru_w_gates[l, 0]), _block_diag(lru_w_gates[l, 1])], axis=1).astype(_MXU)
    bg = lru_b_gates[l].reshape(1, 2 * GROUP)

    fb = _lane_vec(jnp.repeat(fox_f_bias[l], 3), 0)
    fox_gq = (jnp.tile(fox_qk_norm[l, 0], 2) * scale)[None, :]
    fox_gk = jnp.tile(fox_qk_norm[l, 1], 2)[None, :]

    nsa_gq = (jnp.tile(nsa_qk_norm[l, 0], 2) * scale)[None, :]
    nsa_gc = _lane_vec(nsa_qk_norm[l, 1], 0)
    nsa_gs = _lane_vec(nsa_qk_norm[l, 2], 0)
    nsa_gw = _lane_vec(nsa_qk_norm[l, 3], 0)
    gb = _lane_vec(nsa_gate_bias[l], SM_GATE_LANE)

    half = CMP_LEN // 2

    def pos_ext(lo):
        return jnp.concatenate([nsa_cmp_pos[l, 0, lo:lo + half], nsa_cmp_pos[l, 1, lo:lo + half]],
                               axis=1).reshape(1, half * LANES)

    def w1_ext(lo):
        ext = jnp.zeros((half, LANES, 2 * CMP_HIDDEN), _F32)
        ext = ext.at[:, :HEAD_DIM, :CMP_HIDDEN].set(nsa_cmp_w1[l, 0, lo:lo + half])
        ext = ext.at[:, HEAD_DIM:, CMP_HIDDEN:].set(nsa_cmp_w1[l, 1, lo:lo + half])
        return ext.reshape(half * LANES, 2 * CMP_HIDDEN).astype(_MXU)

    w2 = jnp.zeros((2 * CMP_HIDDEN, LANES), _F32)
    w2 = w2.at[:CMP_HIDDEN, :HEAD_DIM].set(nsa_cmp_w2[l, 0]).at[CMP_HIDDEN:, HEAD_DIM:].set(nsa_cmp_w2[l, 1])
    return dict(w_perm=w_perm, wg=wg, bg=bg, fb=fb, fox_gq=fox_gq, fox_gk=fox_gk, nsa_gq=nsa_gq, nsa_gc=nsa_gc,
                nsa_gs=nsa_gs, nsa_gw=nsa_gw, gb=gb, pa=pos_ext(0), pb=pos_ext(half), w1a=w1_ext(0),
                w1b=w1_ext(half), w2=w2.astype(_MXU), on=out_norm[l].reshape(1, 4 * GROUP))


def kernel(x, norm_mix, w_in, lru_conv_w, lru_conv_b, lru_w_gates, lru_b_gates, lru_lambda, sc_conv_w, fox_f_bias, fox_qk_norm, nsa_qk_norm, nsa_cmp_pos, nsa_cmp_w1, nsa_cmp_w2, nsa_gate_bias, rel_bias, out_norm, w_out, norm_ffn, w_gate_up, w_down):
    bsz, s_len, d_model = x.shape
    depth = w_in.shape[0]
    assert d_model == D_MODEL and s_len % TS_MIX == 0 and s_len % TQ_FOX == 0
    assert SLC_TOPK <= s_len // SLC_BLOCK <= SLC_BLOCK
    t = bsz * s_len

    win_tab, sel_tab, cmp_tab = _nsa_tables(rel_bias, s_len)
    ovl = _overlap_ext(s_len)

    for l in range(depth):
        lp = _layer_params(l, w_in, lru_w_gates, lru_b_gates, fox_f_bias, fox_qk_norm, nsa_qk_norm,
                           nsa_cmp_pos, nsa_cmp_w1, nsa_cmp_w2, nsa_gate_bias, out_norm)
        m_ab, qf, kf, vf, q4, ks, vs, kw, vw, gates, zcmp = _proj_mix(
            x, norm_mix[l][None, :], lp["w_perm"], lru_conv_w[l], lru_conv_b[l][None, :], lp["wg"], lp["bg"],
            lru_lambda[l][None, :], sc_conv_w[l], lp["on"][:, 0:2 * GROUP],
            lp["fb"], lp["fox_gq"], lp["fox_gk"], lp["nsa_gq"], lp["nsa_gs"], lp["nsa_gw"], lp["gb"])
        m_c = _fox_attn(qf, kf, vf, lp["on"][:, 2 * GROUP:3 * GROUP])
        kc, cv = _compress(zcmp, lp["pa"], lp["pb"], lp["w1a"], lp["w1b"], lp["w2"], lp["nsa_gc"])
        m_d = _nsa_attn(q4, kc, cv, ks, vs, kw, vw, gates, win_tab, sel_tab, cmp_tab, ovl,
                        lp["on"][:, 3 * GROUP:4 * GROUP])
        x = _out_ffn(x.reshape(t, D_MODEL), m_ab.reshape(t, 2 * GROUP), m_c.reshape(t, GROUP),
                     m_d.reshape(t, GROUP), w_out[l].astype(_MXU), norm_ffn[l][None, :],
                     w_gate_up[l].astype(_MXU), w_down[l].astype(_MXU)).reshape(bsz, s_len, D_MODEL)
    return x
```
